```python
import math
import jax, jax.numpy as jnp
from jax import lax
import numpy as np

D_MODEL = 1024
BATCH = 8
SEQ = 4096
DEPTH = 1

GRID_W = 64
CTX_LEN = 256
MLA_HEADS = 8
MLA_NOPE = 64
MLA_ROPE = 32
MLA_QK = MLA_NOPE + MLA_ROPE
MLA_V = 64
Q_LORA = 256
KV_LORA = 128
MLA_W = MLA_HEADS * MLA_V
MLA_SCALE = MLA_QK ** -0.5
ROPE_THETA = 10000.0
Q_BLOCK = 128
NA_HEADS = 8
NA_DIM = 64
NA_W = NA_HEADS * NA_DIM
NA_WIN_H = 8
NA_WIN_W = 16
NA_SCALE = NA_DIM ** -0.5
D_IN = Q_LORA + KV_LORA + MLA_ROPE + 3 * NA_W + 2 * D_MODEL
N_EXPERTS = 16
D_EXPERT = 1024
EC_CAPACITY = 2
LN_EPS = 1e-5
RMS_EPS = 1e-6
ALPHA = (2.0 * DEPTH) ** 0.25
BETA = (8.0 * DEPTH) ** -0.25

kernel_name = 'hybrid_mla_natten_ec_dit_block'


def _layer_norm(x, g=None, b=None):
    xf = x.astype(jnp.float32)
    mu = jnp.mean(xf, axis=-1, keepdims=True)
    var = jnp.mean(jnp.square(xf - mu), axis=-1, keepdims=True)
    y = (xf - mu) * lax.rsqrt(var + LN_EPS)
    if g is not None:
        y = y * g.astype(jnp.float32) + b.astype(jnp.float32)
    return y.astype(x.dtype)


def _rms_norm(x, g):
    xf = x.astype(jnp.float32)
    y = xf * lax.rsqrt(jnp.mean(jnp.square(xf), axis=-1, keepdims=True) + RMS_EPS)
    return (y * g.astype(jnp.float32)).astype(x.dtype)


def _axial_rope_tables(n):
    t = jnp.arange(n, dtype=jnp.int32)
    row = (t // GRID_W).astype(jnp.float32)
    col = (t % GRID_W).astype(jnp.float32)
    per_axis = MLA_ROPE // 2
    inv_freq = ROPE_THETA ** (-jnp.arange(0, per_axis, 2, dtype=jnp.float32) / per_axis)
    ang = jnp.concatenate([row[:, None] * inv_freq, col[:, None] * inv_freq], axis=-1)
    return jnp.cos(ang), jnp.sin(ang)


def _rope(x, cos, sin):
    half = x.shape[-1] // 2
    x1 = x[..., :half].astype(jnp.float32)
    x2 = x[..., half:].astype(jnp.float32)
    return jnp.concatenate([x1 * cos - x2 * sin, x1 * sin + x2 * cos], axis=-1).astype(x.dtype)


def _modulation(cvec, w_mod, b_mod):
    m = jax.nn.silu(cvec) @ w_mod + b_mod
    return jnp.split(m, 6, axis=-1)


def _split_in(p):
    o1 = Q_LORA
    o2 = o1 + KV_LORA
    o3 = o2 + MLA_ROPE
    o4 = o3 + 3 * NA_W
    return p[..., :o1], p[..., o1:o2], p[..., o2:o3], p[..., o3:o4], p[..., o4:]


def _mla_q(q_c, q_norm_g, w_uq, rope):
    B, n = q_c.shape[:2]
    q = (_rms_norm(q_c, q_norm_g) @ w_uq).reshape(B, n, MLA_HEADS, MLA_QK)
    q_nope, q_rope = q[..., :MLA_NOPE], q[..., MLA_NOPE:]
    if rope is not None:
        cos, sin = rope
        q_rope = _rope(q_rope, cos[:, None], sin[:, None])
    return jnp.concatenate([q_nope, q_rope], axis=-1)


def _mla_kv(kv_c, k_r, kv_norm_g, w_ukv, rope):
    B, n = kv_c.shape[:2]
    kv = (_rms_norm(kv_c, kv_norm_g) @ w_ukv).reshape(B, n, MLA_HEADS, MLA_NOPE + MLA_V)
    k_nope, v = kv[..., :MLA_NOPE], kv[..., MLA_NOPE:]
    if rope is not None:
        cos, sin = rope
        k_r = _rope(k_r, cos, sin)
    k_rope = jnp.broadcast_to(k_r[:, :, None, :], (B, n, MLA_HEADS, MLA_ROPE))
    return jnp.concatenate([k_nope, k_rope], axis=-1), v


def _na_split(na):
    B, n = na.shape[:2]
    qkv = na.reshape(B, n, 3, NA_HEADS, NA_DIM)
    return qkv[:, :, 0], qkv[:, :, 1], qkv[:, :, 2]


def _attend(q, k, v, scale):
    s = jnp.einsum('bqhd,bkhd->bhqk', q, k).astype(jnp.float32) * scale
    p = jax.nn.softmax(s, axis=-1).astype(v.dtype)
    return jnp.einsum('bhqk,bkhd->bqhd', p, v)


def _mla_latent(q, k_all, v_all):
    B, n, H, dq = q.shape
    nb = n // Q_BLOCK
    qb = q.reshape(B, nb, Q_BLOCK, H, dq).transpose(1, 0, 2, 3, 4)
    out = lax.map(lambda qi: _attend(qi, k_all, v_all, MLA_SCALE), qb)
    return out.transpose(1, 0, 2, 3, 4).reshape(B, n, H * MLA_V)


def _natten_latent(q, k, v, k_ctx, v_ctx, rel_bias):
    B, n, H, d = q.shape
    rows = n // GRID_W
    wh = min(NA_WIN_H, rows)
    ww = NA_WIN_W
    qg = q.reshape(B, rows, GRID_W, H, d)
    kg = k.reshape(B, rows, GRID_W, H, d)
    vg = v.reshape(B, rows, GRID_W, H, d)
    col = np.arange(GRID_W)
    col_start = np.clip(col - ww // 2, 0, GRID_W - ww)
    col_idx = col_start[:, None] + np.arange(ww)[None, :]
    dc = col_idx - col[:, None] + (NA_WIN_W - 1)
    nk = wh * ww

    def row_fn(r):
        r0 = jnp.clip(r - wh // 2, 0, rows - wh)
        q_r = lax.dynamic_index_in_dim(qg, r, axis=1, keepdims=False)
        k_band = lax.dynamic_slice_in_dim(kg, r0, wh, axis=1)
        v_band = lax.dynamic_slice_in_dim(vg, r0, wh, axis=1)
        k_win = k_band[:, :, col_idx].transpose(0, 2, 1, 3, 4, 5).reshape(B, GRID_W, nk, H, d)
        v_win = v_band[:, :, col_idx].transpose(0, 2, 1, 3, 4, 5).reshape(B, GRID_W, nk, H, d)
        dr = r0 + jnp.arange(wh) - r + (NA_WIN_H - 1)
        bias = rel_bias[:, dr[:, None, None], dc[None]]
        bias = bias.transpose(0, 2, 1, 3).reshape(H, GRID_W, nk)
        s_loc = jnp.einsum('bqhd,bqkhd->bhqk', q_r, k_win).astype(jnp.float32) * NA_SCALE + bias[None].astype(jnp.float32)
        s_ctx = jnp.einsum('bqhd,bkhd->bhqk', q_r, k_ctx).astype(jnp.float32) * NA_SCALE
        p = jax.nn.softmax(jnp.concatenate([s_loc, s_ctx], axis=-1), axis=-1).astype(v.dtype)
        return (jnp.einsum('bhqk,bqkhd->bqhd', p[..., :nk], v_win)
                + jnp.einsum('bhqk,bkhd->bqhd', p[..., nk:], v_ctx))

    out = lax.map(row_fn, jnp.arange(rows, dtype=jnp.int32))
    return out.transpose(1, 0, 2, 3, 4).reshape(B, n, H * d)


def _merge(y_mla, y_na, gates, w_proj_mla, w_proj_na, w_out):
    g_mla, g_na = jnp.split(gates, 2, axis=-1)
    return (jax.nn.sigmoid(g_mla) * (y_mla @ w_proj_mla) + jax.nn.sigmoid(g_na) * (y_na @ w_proj_na)) @ w_out


def _expert_choice_ffn(u, w_router, w_exp_gate, w_exp_up, w_exp_down):
    B, n, D = u.shape
    cap = EC_CAPACITY * n // N_EXPERTS
    aff = jax.nn.softmax((u @ w_router).astype(jnp.float32), axis=-1)
    g, idx = lax.top_k(aff.transpose(0, 2, 1), cap)
    xe = jax.vmap(lambda ub, ib: ub[ib])(u, idx)
    h = jax.nn.silu(jnp.einsum('becd,edf->becf', xe, w_exp_gate)) * jnp.einsum('becd,edf->becf', xe, w_exp_up)
    ye = jnp.einsum('becf,efd->becd', h, w_exp_down) * g[..., None].astype(u.dtype)
    return jax.vmap(lambda yb, ib: jnp.zeros((n, D), yb.dtype).at[ib.reshape(-1)].add(yb.reshape(-1, D)))(ye, idx)


def _layer(x, ctx, c, c_ctx, lp, rope, update_ctx):
    B, n, _ = x.shape
    L = ctx.shape[1]
    sh1, sc1, g1, sh2, sc2, g2 = [m[:, None, :] for m in _modulation(c, lp['w_mod'], lp['b_mod'])]
    csh1, csc1, cg1, csh2, csc2, cg2 = _modulation(c_ctx, lp['w_mod'], lp['b_mod'])

    u = _layer_norm(x) * (1 + sc1) + sh1
    uc = _layer_norm(ctx) * (1 + csc1) + csh1
    q_c, kv_c, k_r, na, gates = _split_in(u @ lp['w_in'])
    cq_c, ckv_c, ck_r, cna, cgates = _split_in(uc @ lp['w_in'])

    mq = _mla_q(q_c, lp['q_norm_g'], lp['w_uq'], rope)
    mk, mv = _mla_kv(kv_c, k_r, lp['kv_norm_g'], lp['w_ukv'], rope)
    cmk, cmv = _mla_kv(ckv_c, ck_r, lp['kv_norm_g'], lp['w_ukv'], None)
    y_mla = _mla_latent(mq, jnp.concatenate([cmk, mk], axis=1), jnp.concatenate([cmv, mv], axis=1))

    nq, nk, nv = _na_split(na)
    cnq, cnk, cnv = _na_split(cna)
    y_na = _natten_latent(nq, nk, nv, cnk, cnv, lp['na_rel_bias'])

    mix = _merge(y_mla, y_na, gates, lp['w_proj_mla'], lp['w_proj_na'], lp['w_out'])
    x_new = _layer_norm(ALPHA * x + g1 * mix, lp['ln1_g'], lp['ln1_b'])

    u2 = _layer_norm(x_new) * (1 + sc2) + sh2
    moe = _expert_choice_ffn(u2, lp['w_router'], lp['w_exp_gate'], lp['w_exp_up'], lp['w_exp_down'])
    x_new = _layer_norm(ALPHA * x_new + g2 * moe, lp['ln2_g'], lp['ln2_b'])

    if update_ctx:
        cmq = _mla_q(cq_c, lp['q_norm_g'], lp['w_uq'], None)
        yc_mla = _attend(cmq, cmk, cmv, MLA_SCALE).reshape(B, L, MLA_W)
        yc_na = _attend(cnq, cnk, cnv, NA_SCALE).reshape(B, L, NA_W)
        cmix = _merge(yc_mla, yc_na, cgates, lp['w_proj_mla'], lp['w_proj_na'], lp['w_out'])
        ctx = _layer_norm(ALPHA * ctx + cg1 * cmix, lp['ln1_g'], lp['ln1_b'])
        uc2 = _layer_norm(ctx) * (1 + csc2) + csh2
        cmoe = _expert_choice_ffn(uc2, lp['w_router'], lp['w_exp_gate'], lp['w_exp_up'], lp['w_exp_down'])
        ctx = _layer_norm(ALPHA * ctx + cg2 * cmoe, lp['ln2_g'], lp['ln2_b'])
    return x_new, ctx


def setup_inputs(seed: int = 0) -> dict:
    key = jax.random.key(seed)
    ks = jax.random.split(key, 26)

    def nrm(k, shape, scale):
        return jax.random.normal(k, shape, jnp.float32) * scale

    Ld = DEPTH
    return {
        'x': nrm(ks[0], (BATCH, SEQ, D_MODEL), 1.0),
        'c': nrm(ks[1], (BATCH, D_MODEL), 1.0),
        'ctx': nrm(ks[2], (BATCH, CTX_LEN, D_MODEL), 1.0),
        'c_ctx': nrm(ks[3], (D_MODEL,), 1.0),
        'w_mod': nrm(ks[4], (Ld, D_MODEL, 6 * D_MODEL), 0.5 * D_MODEL ** -0.5),
        'b_mod': nrm(ks[5], (Ld, 6 * D_MODEL), 0.02),
        'w_in': nrm(ks[6], (Ld, D_MODEL, D_IN), D_MODEL ** -0.5),
        'q_norm_g': 1.0 + nrm(ks[7], (Ld, Q_LORA), 0.02),
        'w_uq': nrm(ks[8], (Ld, Q_LORA, MLA_HEADS * MLA_QK), Q_LORA ** -0.5),
        'kv_norm_g': 1.0 + nrm(ks[9], (Ld, KV_LORA), 0.02),
        'w_ukv': nrm(ks[10], (Ld, KV_LORA, MLA_HEADS * (MLA_NOPE + MLA_V)), KV_LORA ** -0.5),
        'na_rel_bias': nrm(ks[11], (Ld, NA_HEADS, 2 * NA_WIN_H - 1, 2 * NA_WIN_W - 1), 0.1),
        'w_proj_mla': nrm(ks[12], (Ld, MLA_W, D_MODEL), MLA_W ** -0.5),
        'w_proj_na': nrm(ks[13], (Ld, NA_W, D_MODEL), NA_W ** -0.5),
        'w_out': nrm(ks[14], (Ld, D_MODEL, D_MODEL), BETA * D_MODEL ** -0.5),
        'ln1_g': 1.0 + nrm(ks[15], (Ld, D_MODEL), 0.02),
        'ln1_b': nrm(ks[16], (Ld, D_MODEL), 0.02),
        'w_router': nrm(ks[17], (Ld, D_MODEL, N_EXPERTS), D_MODEL ** -0.5),
        'w_exp_gate': nrm(ks[18], (Ld, N_EXPERTS, D_MODEL, D_EXPERT), D_MODEL ** -0.5),
        'w_exp_up': nrm(ks[19], (Ld, N_EXPERTS, D_MODEL, D_EXPERT), D_MODEL ** -0.5),
        'w_exp_down': nrm(ks[20], (Ld, N_EXPERTS, D_EXPERT, D_MODEL), BETA * D_EXPERT ** -0.5),
        'ln2_g': 1.0 + nrm(ks[21], (Ld, D_MODEL), 0.02),
        'ln2_b': nrm(ks[22], (Ld, D_MODEL), 0.02),
    }


def reference(x, c, ctx, c_ctx, w_mod, b_mod, w_in, q_norm_g, w_uq, kv_norm_g, w_ukv, na_rel_bias,
              w_proj_mla, w_proj_na, w_out, ln1_g, ln1_b, w_router, w_exp_gate, w_exp_up, w_exp_down,
              ln2_g, ln2_b):
    n = x.shape[1]
    rope = _axial_rope_tables(n)
    for l in range(DEPTH):
        lp = dict(w_mod=w_mod[l], b_mod=b_mod[l], w_in=w_in[l], q_norm_g=q_norm_g[l], w_uq=w_uq[l],
                  kv_norm_g=kv_norm_g[l], w_ukv=w_ukv[l], na_rel_bias=na_rel_bias[l],
                  w_proj_mla=w_proj_mla[l], w_proj_na=w_proj_na[l], w_out=w_out[l],
                  ln1_g=ln1_g[l], ln1_b=ln1_b[l], w_router=w_router[l], w_exp_gate=w_exp_gate[l],
                  w_exp_up=w_exp_up[l], w_exp_down=w_exp_down[l], ln2_g=ln2_g[l], ln2_b=ln2_b[l])
        x, ctx = _layer(x, ctx, c, c_ctx, lp, rope, update_ctx=(l < DEPTH - 1))
    return x
```

```python
import functools
import math

import numpy as np
import jax
import jax.numpy as jnp
from jax import lax
from jax.experimental import pallas as pl
from jax.experimental.pallas import tpu as pltpu

GRID_W = 64
MLA_HEADS = 8
MLA_NOPE = 64
MLA_ROPE = 32
MLA_QK = MLA_NOPE + MLA_ROPE
MLA_V = 64
Q_LORA = 256
KV_LORA = 128
MLA_SCALE = MLA_QK ** -0.5
ROPE_THETA = 10000.0
NA_HEADS = 8
NA_DIM = 64
NA_W = NA_HEADS * NA_DIM
NA_WIN_H = 8
NA_WIN_W = 16
NA_SCALE = NA_DIM ** -0.5
N_EXPERTS = 16
EC_CAPACITY = 2
LN_EPS = 1e-5
RMS_EPS = 1e-6
DEPTH = 1
ALPHA = (2.0 * DEPTH) ** 0.25

LANES = 128
HEAD_PAD = LANES
VMEM_LIMIT = 56 * 1024 * 1024
NEG_BIG = -1e30

NA_QROWS = 4
NA_BAND = NA_QROWS + NA_WIN_H - 1
TOPK_SLOTS = 64

BF16 = jnp.bfloat16
F32 = jnp.float32


def _cparams(sem):
    return pltpu.CompilerParams(dimension_semantics=sem, vmem_limit_bytes=VMEM_LIMIT)


def _ln(x):
    mu = jnp.mean(x, axis=-1, keepdims=True)
    xc = x - mu
    var = jnp.mean(xc * xc, axis=-1, keepdims=True)
    return xc * lax.rsqrt(var + LN_EPS)


def _dot(a, b):
    return jnp.dot(a, b, preferred_element_type=F32)


def _dot_nt(a, b):
    return lax.dot_general(a, b, (((1,), (1,)), ((), ())), preferred_element_type=F32)


def _mod_kernel(c_ref, w_ref, b_ref, o_ref):
    c = c_ref[...]
    s = c * jax.nn.sigmoid(c)
    o_ref[...] = jnp.dot(s, w_ref[...], preferred_element_type=F32,
                         precision=lax.Precision.HIGHEST) + b_ref[...]


def _modulation(cc, w_mod, b_mod):
    rows, d = cc.shape
    n_out = w_mod.shape[1]
    tn = 1024
    return pl.pallas_call(
        _mod_kernel,
        out_shape=jax.ShapeDtypeStruct((rows, n_out), F32),
        grid=(n_out // tn,),
        in_specs=[pl.BlockSpec((rows, d), lambda j: (0, 0)),
                  pl.BlockSpec((d, tn), lambda j: (0, j)),
                  pl.BlockSpec((1, tn), lambda j: (0, j))],
        out_specs=pl.BlockSpec((rows, tn), lambda j: (0, j)),
        compiler_params=_cparams(("arbitrary",)),
        name="modulation",
    )(cc, w_mod, b_mod.reshape(1, n_out))


def _inproj_kernel(latent, x_ref, sc_ref, sh_ref, ck_ref, sk_ref, cq_ref, sq_ref,
                   w_small_ref, w_na_ref, w_g_ref, qg_ref, kvg_ref, wqa_ref, wqb_ref, wk_ref, wv_ref,
                   *out_refs):
    if latent:
        q_ref, k_ref, v_ref, nq_ref, nk_ref, nv_ref, g_ref = out_refs
    else:
        k_ref, v_ref, nk_ref, nv_ref = out_refs
    x = x_ref[0]
    u = (_ln(x) * (1.0 + sc_ref[0]) + sh_ref[0]).astype(BF16)

    small = _dot(u, w_small_ref[...])
    q_c = small[:, :Q_LORA]
    kv_c = small[:, Q_LORA:Q_LORA + KV_LORA]
    kra = small[:, Q_LORA + KV_LORA:Q_LORA + KV_LORA + LANES]
    krb = small[:, Q_LORA + KV_LORA + LANES:]

    kvn = (kv_c * lax.rsqrt(jnp.mean(kv_c * kv_c, axis=-1, keepdims=True) + RMS_EPS) * kvg_ref[...]).astype(BF16)
    kk = _dot(kvn, wk_ref[...])
    kr = kra * ck_ref[...] + krb * sk_ref[...]
    for h in range(MLA_HEADS):
        k_ref[0, h] = (kk[:, h * HEAD_PAD:(h + 1) * HEAD_PAD] + kr).astype(BF16)
    v_ref[0] = _dot(kvn, wv_ref[...]).astype(BF16)

    na = _dot(u, w_na_ref[...])
    if latent:
        qn = (q_c * lax.rsqrt(jnp.mean(q_c * q_c, axis=-1, keepdims=True) + RMS_EPS) * qg_ref[...]).astype(BF16)
        qa = _dot(qn, wqa_ref[...])
        qb = _dot(qn, wqb_ref[...])
        cq = cq_ref[...]
        sq = sq_ref[...]
        for h in range(MLA_HEADS):
            sl = slice(h * HEAD_PAD, (h + 1) * HEAD_PAD)
            q_ref[0, h] = (qa[:, sl] * cq + qb[:, sl] * sq).astype(BF16)
        nq_ref[0] = (na[:, :NA_W] * NA_SCALE).astype(BF16)
        nk_ref[0] = na[:, NA_W:2 * NA_W].astype(BF16)
        nv_ref[0] = na[:, 2 * NA_W:].astype(BF16)
        g_ref[0] = jax.nn.sigmoid(_dot(u, w_g_ref[...])).astype(BF16)
    else:
        nk_ref[0] = na[:, :NA_W].astype(BF16)
        nv_ref[0] = na[:, NA_W:].astype(BF16)


def _inproj(latent, x, sc, sh, ck, sk, cq, sq, w_small, w_na, w_g, qg, kvg, wqa, wqb, wk, wv, tm):
    B, n, D = x.shape
    per_batch = sc.shape[0] > 1
    mod_map = (lambda b, i: (b, 0, 0)) if per_batch else (lambda b, i: (0, 0, 0))
    full = lambda a: pl.BlockSpec(a.shape, lambda b, i: (0,) * a.ndim)
    tab = lambda a: pl.BlockSpec((tm, a.shape[1]), lambda b, i: (i, 0))
    in_specs = [pl.BlockSpec((1, tm, D), lambda b, i: (b, i, 0)),
                pl.BlockSpec((1, 1, D), mod_map), pl.BlockSpec((1, 1, D), mod_map),
                tab(ck), tab(sk), tab(cq), tab(sq),
                full(w_small), full(w_na), full(w_g), full(qg), full(kvg),
                full(wqa), full(wqb), full(wk), full(wv)]
    hk = jax.ShapeDtypeStruct((B, MLA_HEADS, n, HEAD_PAD), BF16)
    hk_spec = pl.BlockSpec((1, MLA_HEADS, tm, HEAD_PAD), lambda b, i: (b, 0, i, 0))
    tok = lambda w: jax.ShapeDtypeStruct((B, n, w), BF16)
    tok_spec = lambda w: pl.BlockSpec((1, tm, w), lambda b, i: (b, i, 0))
    if latent:
        out_shape = [hk, hk, tok(MLA_HEADS * MLA_V), tok(NA_W), tok(NA_W), tok(NA_W), tok(2 * D)]
        out_specs = [hk_spec, hk_spec, tok_spec(MLA_HEADS * MLA_V), tok_spec(NA_W), tok_spec(NA_W),
                     tok_spec(NA_W), tok_spec(2 * D)]
    else:
        out_shape = [hk, tok(MLA_HEADS * MLA_V), tok(NA_W), tok(NA_W)]
        out_specs = [hk_spec, tok_spec(MLA_HEADS * MLA_V), tok_spec(NA_W), tok_spec(NA_W)]
    return pl.pallas_call(
        functools.partial(_inproj_kernel, latent),
        out_shape=out_shape,
        grid=(B, n // tm),
        in_specs=in_specs,
        out_specs=out_specs,
        compiler_params=_cparams(("parallel", "parallel")),
        name="inproj_latent" if latent else "inproj_ctx",
    )(x, sc, sh, ck, sk, cq, sq, w_small, w_na, w_g, qg, kvg, wqa, wqb, wk, wv)


def _mla_kernel(tk, q_ref, kc_ref, kl_ref, vc_ref, vl_ref, o_ref, m_ref, l_ref, acc_ref):
    n = kl_ref.shape[2]
    tq = q_ref.shape[2]
    half = MLA_V

    def step(q, k, v, hh):
        s = _dot_nt(q, k)
        m_old = m_ref[hh]
        m_new = jnp.maximum(m_old, jnp.max(s, axis=-1, keepdims=True))
        a = jnp.exp(m_old - m_new)
        p = jnp.exp(s - m_new)
        l_ref[hh] = a * l_ref[hh] + jnp.sum(p, axis=-1, keepdims=True)
        acc_ref[hh] = a * acc_ref[hh] + _dot(p.astype(BF16), v)
        m_ref[hh] = m_new

    outs = []
    for hh in range(2):
        q = q_ref[0, hh]
        m_ref[hh] = jnp.full((tq, 1), NEG_BIG, F32)
        l_ref[hh] = jnp.zeros((tq, 1), F32)
        acc_ref[hh] = jnp.zeros((tq, LANES), F32)
        step(q, kc_ref[0, hh], vc_ref[0], hh)

        def body(j, carry, q=q, hh=hh):
            off = pl.multiple_of(j * tk, tk)
            step(q, kl_ref[0, hh, pl.ds(off, tk), :], vl_ref[0, pl.ds(off, tk), :], hh)
            return carry

        lax.fori_loop(0, n // tk, body, 0)
        outs.append(acc_ref[hh] / l_ref[hh])
    lane = lax.broadcasted_iota(jnp.int32, (tq, LANES), 1)
    o_ref[0] = jnp.where(lane < half, outs[0], outs[1]).astype(o_ref.dtype)


def _mla_attention(q, k_lat, k_ctx, v_lat, v_ctx, tq, tk):
    B, H, n, _ = q.shape
    L = k_ctx.shape[2]
    return pl.pallas_call(
        functools.partial(_mla_kernel, tk),
        out_shape=jax.ShapeDtypeStruct((B, n, H * MLA_V), BF16),
        grid=(B, H // 2, n // tq),
        in_specs=[pl.BlockSpec((1, 2, tq, HEAD_PAD), lambda b, j, i: (b, j, i, 0)),
                  pl.BlockSpec((1, 2, L, HEAD_PAD), lambda b, j, i: (b, j, 0, 0)),
                  pl.BlockSpec((1, 2, n, HEAD_PAD), lambda b, j, i: (b, j, 0, 0)),
                  pl.BlockSpec((1, L, LANES), lambda b, j, i: (b, 0, j)),
                  pl.BlockSpec((1, n, LANES), lambda b, j, i: (b, 0, j))],
        out_specs=pl.BlockSpec((1, tq, LANES), lambda b, j, i: (b, i, j)),
        scratch_shapes=[pltpu.VMEM((2, tq, 1), F32), pltpu.VMEM((2, tq, 1), F32),
                        pltpu.VMEM((2, tq, LANES), F32)],
        compiler_params=_cparams(("parallel", "parallel", "parallel")),
        name="mla_attention",
    )(q, k_ctx, k_lat, v_ctx, v_lat)


def _na_block_tables(rows):
    nblk = rows // NA_QROWS
    wh = min(NA_WIN_H, rows)
    band0 = np.clip(np.arange(nblk) * NA_QROWS - wh // 2, 0, rows - NA_BAND)
    sigs, types = [], []
    for i in range(nblk):
        r = i * NA_QROWS + np.arange(NA_QROWS)
        r0 = np.clip(r - wh // 2, 0, rows - wh)
        sig = (tuple(r0 - band0[i]), tuple(r - band0[i]))
        if sig not in sigs:
            sigs.append(sig)
        types.append(sigs.index(sig))
    return band0.astype(np.int32), np.asarray(types, np.int32), sigs, wh


def _na_bias_tiles(rel_bias, rows):
    _, _, sigs, wh = _na_block_tables(rows)
    ww = NA_WIN_W
    col = np.arange(GRID_W)
    c0 = np.clip(col - ww // 2, 0, GRID_W - ww)
    tiles = []
    for r0_rel, r_rel in sigs:
        qr = np.repeat(np.arange(NA_QROWS), GRID_W)
        qc = np.tile(col, NA_QROWS)
        kr = np.repeat(np.arange(NA_BAND), GRID_W)
        kc = np.tile(col, NA_BAND)
        r0q = np.asarray(r0_rel)[qr]
        rq = np.asarray(r_rel)[qr]
        row_ok = (kr[None, :] >= r0q[:, None]) & (kr[None, :] < r0q[:, None] + wh)
        col_ok = (kc[None, :] >= c0[qc][:, None]) & (kc[None, :] < c0[qc][:, None] + ww)
        dr = np.clip(kr[None, :] - rq[:, None] + (NA_WIN_H - 1), 0, 2 * NA_WIN_H - 2)
        dc = np.clip(kc[None, :] - qc[:, None] + (NA_WIN_W - 1), 0, 2 * NA_WIN_W - 2)
        vals = rel_bias[:, dr, dc].astype(F32)
        tiles.append(jnp.where((row_ok & col_ok)[None], vals, NEG_BIG))
    return jnp.stack(tiles, axis=0)


def _na_kernel(band_ref, type_ref, q_ref, k_ref, v_ref, kc_ref, vc_ref, bias_ref, o_ref):
    i = pl.program_id(1)
    nq = NA_QROWS * GRID_W
    nk = NA_BAND * GRID_W
    start = pl.multiple_of(band_ref[i] * GRID_W, GRID_W)
    lane = lax.broadcasted_iota(jnp.int32, (nq, LANES), 1)
    low = lane < NA_DIM
    for j in range(NA_HEADS // 2):
        cs = slice(j * LANES, (j + 1) * LANES)
        qp = q_ref[0, :, cs]
        kb = k_ref[0, pl.ds(start, nk), cs]
        vb = v_ref[0, pl.ds(start, nk), cs]
        kc = kc_ref[0, :, cs]
        vc = vc_ref[0, :, cs]
        outs = []
        for hh in range(2):
            qh = jnp.where(low if hh == 0 else ~low, qp, jnp.zeros_like(qp))
            s_loc = _dot_nt(qh, kb) + bias_ref[0, 2 * j + hh]
            s_ctx = _dot_nt(qh, kc)
            m = jnp.maximum(jnp.max(s_loc, axis=-1, keepdims=True), jnp.max(s_ctx, axis=-1, keepdims=True))
            p_loc = jnp.exp(s_loc - m)
            p_ctx = jnp.exp(s_ctx - m)
            l = jnp.sum(p_loc, axis=-1, keepdims=True) + jnp.sum(p_ctx, axis=-1, keepdims=True)
            o = _dot(p_loc.astype(BF16), vb) + _dot(p_ctx.astype(BF16), vc)
            outs.append(o / l)
        o_ref[0, :, cs] = jnp.where(low, outs[0], outs[1]).astype(o_ref.dtype)


def _na_attention(nq, nk, nv, cnk, cnv, bias_tiles):
    B, n, W = nq.shape
    L = cnk.shape[1]
    rows = n // GRID_W
    band0, types, _, _ = _na_block_tables(rows)
    qn = NA_QROWS * GRID_W
    kn = NA_BAND * GRID_W
    grid_spec = pltpu.PrefetchScalarGridSpec(
        num_scalar_prefetch=2,
        grid=(B, rows // NA_QROWS),
        in_specs=[pl.BlockSpec((1, qn, W), lambda b, i, bd, ty: (b, i, 0)),
                  pl.BlockSpec((1, n, W), lambda b, i, bd, ty: (b, 0, 0)),
                  pl.BlockSpec((1, n, W), lambda b, i, bd, ty: (b, 0, 0)),
                  pl.BlockSpec((1, L, W), lambda b, i, bd, ty: (b, 0, 0)),
                  pl.BlockSpec((1, L, W), lambda b, i, bd, ty: (b, 0, 0)),
                  pl.BlockSpec((1, NA_HEADS, qn, kn), lambda b, i, bd, ty: (ty[i], 0, 0, 0))],
        out_specs=pl.BlockSpec((1, qn, W), lambda b, i, bd, ty: (b, i, 0)),
    )
    return pl.pallas_call(
        _na_kernel,
        out_shape=jax.ShapeDtypeStruct((B, n, W), BF16),
        grid_spec=grid_spec,
        compiler_params=_cparams(("parallel", "arbitrary")),
        name="na_attention",
    )(jnp.asarray(band0), jnp.asarray(types), nq, nk, nv, cnk, cnv, bias_tiles)


def _split2(a):
    hi = a.astype(BF16)
    return hi, (a - hi.astype(F32)).astype(BF16)


def _merge_kernel(ym_ref, yn_ref, g_ref, x_ref, g1_ref, sc2_ref, sh2_ref, wpm_ref, wpn_ref, wo_ref,
                  l1g_ref, l1b_ref, wr_ref, xn_ref, up_ref, aff_ref):
    D = x_ref.shape[2]
    g = g_ref[0]
    a = g[:, :D].astype(F32) * _dot(ym_ref[0], wpm_ref[...]) + g[:, D:].astype(F32) * _dot(yn_ref[0], wpn_ref[...])
    mix = _dot(a.astype(BF16), wo_ref[...])
    xn = _ln(ALPHA * x_ref[0] + g1_ref[0] * mix) * l1g_ref[...] + l1b_ref[...]
    xn_ref[0] = xn
    u2 = _ln(xn) * (1.0 + sc2_ref[0]) + sh2_ref[0]

    half = D // 2
    lo = pltpu.bitcast(u2[:, :half].astype(BF16).astype(F32), jnp.uint32)
    hi = pltpu.bitcast(u2[:, half:].astype(BF16).astype(F32), jnp.uint32)
    up_ref[0] = hi | (lo >> 16)

    uh, ul = _split2(u2)
    wh, wl = _split2(wr_ref[...])
    logits = _dot(uh, wh) + (_dot(uh, wl) + _dot(ul, wh))
    e = jnp.exp(logits - jnp.max(logits, axis=-1, keepdims=True))
    aff_ref[0] = e / jnp.sum(e, axis=-1, keepdims=True)


def _merge(y_mla, y_na, gates, x, g1, sc2, sh2, wpm, wpn, wo, l1g, l1b, wr, tm):
    B, n, D = x.shape
    E = wr.shape[1]
    full = lambda a: pl.BlockSpec(a.shape, lambda b, i: (0,) * a.ndim)
    tok = lambda w: pl.BlockSpec((1, tm, w), lambda b, i: (b, i, 0))
    mod = pl.BlockSpec((1, 1, D), lambda b, i: (b, 0, 0))
    return pl.pallas_call(
        _merge_kernel,
        out_shape=[jax.ShapeDtypeStruct((B, n, D), F32),
                   jax.ShapeDtypeStruct((B, n, D // 2), jnp.uint32),
                   jax.ShapeDtypeStruct((B, n, E), F32)],
        grid=(B, n // tm),
        in_specs=[tok(y_mla.shape[2]), tok(y_na.shape[2]), tok(2 * D), tok(D), mod, mod, mod,
                  full(wpm), full(wpn), full(wo), full(l1g), full(l1b), full(wr)],
        out_specs=[tok(D), tok(D // 2), tok(E)],
        compiler_params=_cparams(("parallel", "parallel")),
        name="merge_ln_router",
    )(y_mla, y_na, gates, x, g1, sc2, sh2, wpm, wpn, wo, l1g, l1b, wr)


def _topk_kernel(cap, aff_ref, tri_ref, idx_ref, g_ref, pos_ref):
    aff = aff_ref[0]
    E, n = aff.shape
    bits = pltpu.bitcast(aff, jnp.int32)

    def count(mask):
        return jnp.sum(mask.astype(F32), axis=-1, keepdims=True)

    def search(t, thr):
        cand = thr | (jnp.int32(1) << (30 - t))
        return jnp.where(count(bits >= cand) >= cap, cand, thr)

    thr = lax.fori_loop(0, 31, search, jnp.zeros((E, 1), jnp.int32))
    gt = bits > thr
    eq = bits == thr
    need = cap - count(gt)

    tri = tri_ref[...]

    def prefix(mask):
        mf = mask.astype(F32).astype(BF16)
        parts = []
        off = jnp.zeros((E, 1), F32)
        for c in range(n // LANES):
            blk = mf[:, c * LANES:(c + 1) * LANES]
            parts.append(_dot(blk, tri) + off)
            off = off + jnp.sum(blk.astype(F32), axis=-1, keepdims=True)
        return jnp.concatenate(parts, axis=-1)

    sel = gt | (eq & (prefix(eq) < need))
    pos_ref[...] = jnp.where(sel, prefix(sel), -1.0)

    chunks = cap // TOPK_SLOTS

    def emit(t, carry):
        e = t // chunks
        base = pl.multiple_of((t % chunks) * TOPK_SLOTS, TOPK_SLOTS)
        slot = (lax.broadcasted_iota(jnp.int32, (TOPK_SLOTS, n), 0) + base).astype(F32)
        tok = lax.broadcasted_iota(jnp.int32, (TOPK_SLOTS, n), 1).astype(F32)
        hit = slot == pos_ref[pl.ds(e, 1), :]
        idx = jnp.sum(jnp.where(hit, tok, 0.0), axis=-1, keepdims=True)
        idx_ref[0, e, pl.ds(base, TOPK_SLOTS), :] = idx.astype(jnp.int32)
        g_ref[0, e, pl.ds(base, TOPK_SLOTS), :] = jnp.sum(
            jnp.where(hit, aff_ref[0, pl.ds(e, 1), :], 0.0), axis=-1, keepdims=True)
        return carry

    lax.fori_loop(0, E * chunks, emit, 0)


def _topk(aff_t, cap):
    B, E, n = aff_t.shape
    tri = jnp.asarray(np.triu(np.ones((LANES, LANES), np.float32), k=1), BF16)
    return pl.pallas_call(
        functools.partial(_topk_kernel, cap),
        out_shape=[jax.ShapeDtypeStruct((B, E, cap, 1), jnp.int32),
                   jax.ShapeDtypeStruct((B, E, cap, 1), F32)],
        grid=(B,),
        in_specs=[pl.BlockSpec((1, E, n), lambda b: (b, 0, 0)),
                  pl.BlockSpec((LANES, LANES), lambda b: (0, 0))],
        out_specs=[pl.BlockSpec((1, E, cap, 1), lambda b: (b, 0, 0, 0)),
                   pl.BlockSpec((1, E, cap, 1), lambda b: (b, 0, 0, 0))],
        scratch_shapes=[pltpu.VMEM((E, n), F32)],
        compiler_params=_cparams(("parallel",)),
        name="expert_topk",
    )(aff_t, tri)


def _gather_kernel(idx_ref, u_ref, o_ref):
    cap = o_ref.shape[2]

    def body(c, carry):
        r = idx_ref[0, 0, 0, c]
        o_ref[0, 0, pl.ds(c, 1), :] = u_ref[0, pl.ds(r, 1), :]
        return carry

    lax.fori_loop(0, cap, body, 0, unroll=8)


def _gather(idx, u_packed):
    B, E, _, cap = idx.shape
    _, n, W = u_packed.shape
    return pl.pallas_call(
        _gather_kernel,
        out_shape=jax.ShapeDtypeStruct((B, E, cap, W), u_packed.dtype),
        grid=(B, E),
        in_specs=[pl.BlockSpec((1, 1, 1, cap), lambda b, e: (b, e, 0, 0), memory_space=pltpu.SMEM),
                  pl.BlockSpec((1, n, W), lambda b, e: (b, 0, 0))],
        out_specs=pl.BlockSpec((1, 1, cap, W), lambda b, e: (b, e, 0, 0)),
        compiler_params=_cparams(("parallel", "arbitrary")),
        name="expert_gather",
    )(idx, u_packed)


def _ffn_kernel(xe_ref, g_ref, wg_ref, wu_ref, wd_ref, o_ref, wgb_ref, wub_ref, wdb_ref):
    @pl.when(pl.program_id(1) == 0)
    def _():
        wgb_ref[...] = wg_ref[0].astype(BF16)
        wub_ref[...] = wu_ref[0].astype(BF16)
        wdb_ref[...] = wd_ref[0].astype(BF16)

    w = xe_ref[0, 0]
    lo = pltpu.bitcast(w << 16, F32).astype(BF16)
    hi = pltpu.bitcast(w & jnp.uint32(0xFFFF0000), F32).astype(BF16)
    half = lo.shape[1]
    gate = _dot(lo, wgb_ref[:half, :]) + _dot(hi, wgb_ref[half:, :])
    up = _dot(lo, wub_ref[:half, :]) + _dot(hi, wub_ref[half:, :])
    h = (gate * jax.nn.sigmoid(gate) * up).astype(BF16)
    o_ref[0, 0] = _dot(h, wdb_ref[...]) * g_ref[0, 0]


def _ffn(xe, g, w_gate, w_up, w_down):
    B, E, cap, W = xe.shape
    _, D, F = w_gate.shape
    return pl.pallas_call(
        _ffn_kernel,
        out_shape=jax.ShapeDtypeStruct((B, E, cap, D), F32),
        grid=(E, B),
        in_specs=[pl.BlockSpec((1, 1, cap, W), lambda e, b: (b, e, 0, 0)),
                  pl.BlockSpec((1, 1, cap, 1), lambda e, b: (b, e, 0, 0)),
                  pl.BlockSpec((1, D, F), lambda e, b: (e, 0, 0)),
                  pl.BlockSpec((1, D, F), lambda e, b: (e, 0, 0)),
                  pl.BlockSpec((1, F, D), lambda e, b: (e, 0, 0))],
        out_specs=pl.BlockSpec((1, 1, cap, D), lambda e, b: (b, e, 0, 0)),
        scratch_shapes=[pltpu.VMEM((D, F), BF16), pltpu.VMEM((D, F), BF16), pltpu.VMEM((F, D), BF16)],
        compiler_params=_cparams(("arbitrary", "arbitrary")),
        name="expert_ffn",
    )(xe, g, w_gate, w_up, w_down)


def _scatter_kernel(idx_ref, y_ref, o_ref):
    cap = y_ref.shape[2]

    @pl.when(pl.program_id(2) == 0)
    def _():
        o_ref[...] = jnp.zeros_like(o_ref)

    def body(c, carry):
        r = idx_ref[0, 0, 0, c]
        o_ref[0, pl.ds(r, 1), :] = o_ref[0, pl.ds(r, 1), :] + y_ref[0, 0, pl.ds(c, 1), :]
        return carry

    lax.fori_loop(0, cap, body, 0, unroll=8)


def _scatter(idx, ye, n):
    B, E, cap, D = ye.shape
    halves = 2
    dw = D // halves
    return pl.pallas_call(
        _scatter_kernel,
        out_shape=jax.ShapeDtypeStruct((B, n, D), F32),
        grid=(B, halves, E),
        in_specs=[pl.BlockSpec((1, 1, 1, cap), lambda b, s, e: (b, e, 0, 0), memory_space=pltpu.SMEM),
                  pl.BlockSpec((1, 1, cap, dw), lambda b, s, e: (b, e, 0, s))],
        out_specs=pl.BlockSpec((1, n, dw), lambda b, s, e: (b, 0, s)),
        compiler_params=_cparams(("parallel", "parallel", "arbitrary")),
        name="expert_scatter",
    )(idx, ye)


def _final_kernel(x_ref, moe_ref, g2_ref, lg_ref, lb_ref, o_ref):
    o_ref[0] = _ln(ALPHA * x_ref[0] + g2_ref[0] * moe_ref[0]) * lg_ref[...] + lb_ref[...]


def _final(xn, moe, g2, lg, lb, tm):
    B, n, D = xn.shape
    tok = pl.BlockSpec((1, tm, D), lambda b, i: (b, i, 0))
    vec = pl.BlockSpec((1, D), lambda b, i: (0, 0))
    return pl.pallas_call(
        _final_kernel,
        out_shape=jax.ShapeDtypeStruct((B, n, D), F32),
        grid=(B, n // tm),
        in_specs=[tok, tok, pl.BlockSpec((1, 1, D), lambda b, i: (b, 0, 0)), vec, vec],
        out_specs=tok,
        compiler_params=_cparams(("parallel", "parallel")),
        name="final_ln",
    )(xn, moe, g2, lg, lb)


def _rope_tables(n):
    t = np.arange(n)
    row = (t // GRID_W).astype(np.float32)
    col = (t % GRID_W).astype(np.float32)
    per_axis = MLA_ROPE // 2
    inv_freq = jnp.asarray(ROPE_THETA, F32) ** (-jnp.arange(0, per_axis, 2, dtype=F32) / per_axis)
    ang = jnp.concatenate([jnp.asarray(row)[:, None] * inv_freq, jnp.asarray(col)[:, None] * inv_freq], axis=-1)
    cos, sin = jnp.cos(ang), jnp.sin(ang)
    pad = HEAD_PAD - MLA_QK
    c_tab = jnp.concatenate([jnp.ones((n, MLA_NOPE), F32), cos, cos, jnp.zeros((n, pad), F32)], axis=-1)
    s_tab = jnp.concatenate([jnp.zeros((n, MLA_NOPE), F32), -sin, sin, jnp.zeros((n, pad), F32)], axis=-1)
    return c_tab, s_tab


def _swap_halves(w):
    half = w.shape[-1] // 2
    return jnp.concatenate([w[..., half:], w[..., :half]], axis=-1)


def _prep_weights(w_in, w_uq, w_ukv):
    D = w_in.shape[0]
    o1 = Q_LORA
    o2 = o1 + KV_LORA
    o3 = o2 + MLA_ROPE
    o4 = o3 + 3 * NA_W
    w_kr = w_in[:, o2:o3]
    z = lambda k: jnp.zeros((D, k), w_in.dtype)
    pad = HEAD_PAD - MLA_QK
    kra = jnp.concatenate([z(MLA_NOPE), w_kr, z(pad)], axis=-1)
    krb = jnp.concatenate([z(MLA_NOPE), _swap_halves(w_kr), z(pad)], axis=-1)
    w_small = jnp.concatenate([w_in[:, :o2], kra, krb], axis=-1).astype(BF16)
    w_na = w_in[:, o3:o4].astype(BF16)
    w_g = w_in[:, o4:].astype(BF16)

    uq = w_uq.reshape(Q_LORA, MLA_HEADS, MLA_QK)
    zq = lambda k: jnp.zeros((Q_LORA, MLA_HEADS, k), w_uq.dtype)
    wqa = jnp.concatenate([uq, zq(pad)], axis=-1).reshape(Q_LORA, MLA_HEADS * HEAD_PAD).astype(BF16)
    wqb = jnp.concatenate([zq(MLA_NOPE), _swap_halves(uq[..., MLA_NOPE:]), zq(pad)], axis=-1)
    wqb = wqb.reshape(Q_LORA, MLA_HEADS * HEAD_PAD).astype(BF16)

    ukv = w_ukv.reshape(KV_LORA, MLA_HEADS, MLA_NOPE + MLA_V)
    wk = jnp.concatenate([ukv[..., :MLA_NOPE], jnp.zeros((KV_LORA, MLA_HEADS, HEAD_PAD - MLA_NOPE), w_ukv.dtype)], axis=-1)
    wk = wk.reshape(KV_LORA, MLA_HEADS * HEAD_PAD).astype(BF16)
    wv = ukv[..., MLA_NOPE:].reshape(KV_LORA, MLA_HEADS * MLA_V).astype(BF16)
    return w_small, w_na, w_g, wqa, wqb, wk, wv


def _tile(n, pref):
    t = min(pref, n)
    while n % t:
        t //= 2
    return t


def kernel(x, c, ctx, c_ctx, w_mod, b_mod, w_in, q_norm_g, w_uq, kv_norm_g, w_ukv, na_rel_bias, w_proj_mla,
           w_proj_na, w_out, ln1_g, ln1_b, w_router, w_exp_gate, w_exp_up, w_exp_down, ln2_g, ln2_b):
    B, n, D = x.shape
    L = ctx.shape[1]
    rows = n // GRID_W
    assert n % (GRID_W * NA_QROWS) == 0 and rows >= NA_BAND
    assert w_mod.shape[0] == DEPTH
    cap = EC_CAPACITY * n // N_EXPERTS

    mod_rows = -(-(B + 1) // 8) * 8
    cc = jnp.concatenate([c, c_ctx[None], jnp.zeros((mod_rows - B - 1, D), F32)], axis=0)
    m = _modulation(cc, w_mod[0], b_mod[0])
    sh1, sc1, g1, sh2, sc2, g2 = [m[:B, k * D:(k + 1) * D].reshape(B, 1, D) for k in range(6)]
    csh1 = m[B:B + 1, :D].reshape(1, 1, D)
    csc1 = m[B:B + 1, D:2 * D].reshape(1, 1, D)

    w_small, w_na, w_g, wqa, wqb, wk, wv = _prep_weights(w_in[0], w_uq[0], w_ukv[0])
    qg = q_norm_g[0].reshape(1, Q_LORA)
    kvg = kv_norm_g[0].reshape(1, KV_LORA)
    c_tab, s_tab = _rope_tables(n)
    pad = HEAD_PAD - MLA_QK
    c_ctx_tab = jnp.concatenate([jnp.ones((L, MLA_QK), F32), jnp.zeros((L, pad), F32)], axis=-1)
    s_ctx_tab = jnp.zeros((L, HEAD_PAD), F32)

    tm = _tile(n, 256)
    q, k_lat, v_lat, nq, nk, nv, gates = _inproj(
        True, x, sc1, sh1, c_tab, s_tab, c_tab * MLA_SCALE, s_tab * MLA_SCALE,
        w_small, w_na, w_g, qg, kvg, wqa, wqb, wk, wv, tm)
    k_ctx, v_ctx, cnk, cnv = _inproj(
        False, ctx, csc1, csh1, c_ctx_tab, s_ctx_tab, c_ctx_tab, s_ctx_tab,
        w_small, w_na[:, NA_W:], w_g, qg, kvg, wqa, wqb, wk, wv, _tile(L, 256))

    y_mla = _mla_attention(q, k_lat, k_ctx, v_lat, v_ctx, _tile(n, 512), _tile(n, 512))
    y_na = _na_attention(nq, nk, nv, cnk, cnv, _na_bias_tiles(na_rel_bias[0], rows))

    xn, u_packed, aff = _merge(
        y_mla, y_na, gates, x, g1, sc2, sh2,
        w_proj_mla[0].astype(BF16), w_proj_na[0].astype(BF16), w_out[0].astype(BF16),
        ln1_g[0].reshape(1, D), ln1_b[0].reshape(1, D), w_router[0], tm)

    idx4, g4 = _topk(jnp.swapaxes(aff, 1, 2), cap)
    idx = idx4.reshape(B, N_EXPERTS, 1, cap)
    xe = _gather(idx, u_packed)
    ye = _ffn(xe, g4, w_exp_gate[0], w_exp_up[0], w_exp_down[0])
    moe = _scatter(idx, ye, n)
    return _final(xn, moe, g2, ln2_g[0].reshape(1, D), ln2_b[0].reshape(1, D), tm)
```

```python
import functools
import math

import numpy as np
import jax
import jax.numpy as jnp
from jax import lax
from jax.experimental import pallas as pl
from jax.experimental.pallas import tpu as pltpu

GRID_W = 64
MLA_HEADS = 8
MLA_NOPE = 64
MLA_ROPE = 32
MLA_QK = MLA_NOPE + MLA_ROPE
MLA_V = 64
Q_LORA = 256
KV_LORA = 128
MLA_SCALE = MLA_QK ** -0.5
ROPE_THETA = 10000.0
NA_HEADS = 8
NA_DIM = 64
NA_W = NA_HEADS * NA_DIM
NA_WIN_H = 8
NA_WIN_W = 16
NA_SCALE = NA_DIM ** -0.5
N_EXPERTS = 16
EC_CAPACITY = 2
LN_EPS = 1e-5
RMS_EPS = 1e-6
DEPTH = 1
ALPHA = (2.0 * DEPTH) ** 0.25
LOG2E = math.log2(math.e)

LANES = 128
HEAD_PAD = LANES
VMEM_LIMIT = 56 * 1024 * 1024
NEG_BIG = -1e30

NA_QROWS = 4
NA_BAND = NA_QROWS + NA_WIN_H - 1
TOPK_SLOTS = 64
TOPK_REFINE = 30

BF16 = jnp.bfloat16
F32 = jnp.float32


def _cparams(sem):
    return pltpu.CompilerParams(dimension_semantics=sem, vmem_limit_bytes=VMEM_LIMIT)


def _ln(x):
    mu = jnp.mean(x, axis=-1, keepdims=True)
    xc = x - mu
    var = jnp.mean(xc * xc, axis=-1, keepdims=True)
    return xc * lax.rsqrt(var + LN_EPS)


def _dot(a, b):
    return jnp.dot(a, b, preferred_element_type=F32)


def _dot_nt(a, b):
    return lax.dot_general(a, b, (((1,), (1,)), ((), ())), preferred_element_type=F32)


def _mod_kernel(c_ref, w_ref, b_ref, o_ref):
    c = c_ref[...]
    s = c * jax.nn.sigmoid(c)
    o_ref[...] = jnp.dot(s, w_ref[...], preferred_element_type=F32,
                         precision=lax.Precision.HIGHEST) + b_ref[...]


def _modulation(cc, w_mod, b_mod):
    rows, d = cc.shape
    n_out = w_mod.shape[1]
    tn = 1024
    return pl.pallas_call(
        _mod_kernel,
        out_shape=jax.ShapeDtypeStruct((rows, n_out), F32),
        grid=(n_out // tn,),
        in_specs=[pl.BlockSpec((rows, d), lambda j: (0, 0)),
                  pl.BlockSpec((d, tn), lambda j: (0, j)),
                  pl.BlockSpec((1, tn), lambda j: (0, j))],
        out_specs=pl.BlockSpec((rows, tn), lambda j: (0, j)),
        compiler_params=_cparams(("arbitrary",)),
        name="modulation",
    )(cc, w_mod, b_mod.reshape(1, n_out))


def _inproj_kernel(latent, x_ref, sc_ref, sh_ref, ck_ref, sk_ref, cq_ref, sq_ref,
                   w_small_ref, w_na_ref, w_g_ref, qg_ref, kvg_ref, wqa_ref, wqb_ref, wk_ref, wv_ref,
                   *out_refs):
    if latent:
        q_ref, k_ref, v_ref, nq_ref, nk_ref, nv_ref, g_ref = out_refs
    else:
        k_ref, v_ref, nk_ref, nv_ref = out_refs
    x = x_ref[0]
    u = (_ln(x) * (1.0 + sc_ref[0]) + sh_ref[0]).astype(BF16)

    small = _dot(u, w_small_ref[...])
    q_c = small[:, :Q_LORA]
    kv_c = small[:, Q_LORA:Q_LORA + KV_LORA]
    kra = small[:, Q_LORA + KV_LORA:Q_LORA + KV_LORA + LANES]
    krb = small[:, Q_LORA + KV_LORA + LANES:]

    kvn = (kv_c * lax.rsqrt(jnp.mean(kv_c * kv_c, axis=-1, keepdims=True) + RMS_EPS) * kvg_ref[...]).astype(BF16)
    kk = _dot(kvn, wk_ref[...])
    kr = kra * ck_ref[...] + krb * sk_ref[...]
    for h in range(MLA_HEADS):
        k_ref[0, h] = (kk[:, h * HEAD_PAD:(h + 1) * HEAD_PAD] + kr).astype(BF16)
    v_ref[0] = _dot_nt(wv_ref[...], kvn).astype(BF16)

    na = _dot(u, w_na_ref[...])
    if latent:
        qn = (q_c * lax.rsqrt(jnp.mean(q_c * q_c, axis=-1, keepdims=True) + RMS_EPS) * qg_ref[...]).astype(BF16)
        qa = _dot(qn, wqa_ref[...])
        qb = _dot(qn, wqb_ref[...])
        cq = cq_ref[...]
        sq = sq_ref[...]
        for h in range(MLA_HEADS):
            sl = slice(h * HEAD_PAD, (h + 1) * HEAD_PAD)
            q_ref[0, h] = (qa[:, sl] * cq + qb[:, sl] * sq).astype(BF16)
        nq_ref[0] = (na[:, :NA_W] * NA_SCALE).astype(BF16)
        nk_ref[0] = na[:, NA_W:2 * NA_W].astype(BF16)
        nv_ref[0] = na[:, 2 * NA_W:].astype(BF16)
        g_ref[0] = jax.nn.sigmoid(_dot(u, w_g_ref[...])).astype(BF16)
    else:
        nk_ref[0] = na[:, :NA_W].astype(BF16)
        nv_ref[0] = na[:, NA_W:].astype(BF16)


def _inproj(latent, x, sc, sh, ck, sk, cq, sq, w_small, w_na, w_g, qg, kvg, wqa, wqb, wk, wv, tm):
    B, n, D = x.shape
    per_batch = sc.shape[0] > 1
    mod_map = (lambda b, i: (b, 0, 0)) if per_batch else (lambda b, i: (0, 0, 0))
    full = lambda a: pl.BlockSpec(a.shape, lambda b, i: (0,) * a.ndim)
    tab = lambda a: pl.BlockSpec((tm, a.shape[1]), lambda b, i: (i, 0))
    in_specs = [pl.BlockSpec((1, tm, D), lambda b, i: (b, i, 0)),
                pl.BlockSpec((1, 1, D), mod_map), pl.BlockSpec((1, 1, D), mod_map),
                tab(ck), tab(sk), tab(cq), tab(sq),
                full(w_small), full(w_na), full(w_g), full(qg), full(kvg),
                full(wqa), full(wqb), full(wk), full(wv)]
    hk = jax.ShapeDtypeStruct((B, MLA_HEADS, n, HEAD_PAD), BF16)
    hk_spec = pl.BlockSpec((1, MLA_HEADS, tm, HEAD_PAD), lambda b, i: (b, 0, i, 0))
    tok = lambda w: jax.ShapeDtypeStruct((B, n, w), BF16)
    tok_spec = lambda w: pl.BlockSpec((1, tm, w), lambda b, i: (b, i, 0))
    vt = jax.ShapeDtypeStruct((B, MLA_HEADS * MLA_V, n), BF16)
    vt_spec = pl.BlockSpec((1, MLA_HEADS * MLA_V, tm), lambda b, i: (b, 0, i))
    if latent:
        out_shape = [hk, hk, vt, tok(NA_W), tok(NA_W), tok(NA_W), tok(2 * D)]
        out_specs = [hk_spec, hk_spec, vt_spec, tok_spec(NA_W), tok_spec(NA_W), tok_spec(NA_W), tok_spec(2 * D)]
    else:
        out_shape = [hk, vt, tok(NA_W), tok(NA_W)]
        out_specs = [hk_spec, vt_spec, tok_spec(NA_W), tok_spec(NA_W)]
    return pl.pallas_call(
        functools.partial(_inproj_kernel, latent),
        out_shape=out_shape,
        grid=(B, n // tm),
        in_specs=in_specs,
        out_specs=out_specs,
        compiler_params=_cparams(("parallel", "parallel")),
        name="inproj_latent" if latent else "inproj_ctx",
    )(x, sc, sh, ck, sk, cq, sq, w_small, w_na, w_g, qg, kvg, wqa, wqb, wk, wv)


def _mla_kernel(tk, q_ref, kc_ref, kl_ref, vc_ref, vl_ref, o_ref, m_ref, l_ref, acc_ref):
    n = kl_ref.shape[2]
    tq = q_ref.shape[2]

    def step(k_pair, v_t):
        for hh in range(2):
            s = _dot_nt(k_pair[hh], q_ref[0, hh])
            m_old = m_ref[hh]
            m_new = jnp.maximum(m_old, jnp.max(s, axis=0, keepdims=True))
            a = jnp.exp2(m_old - m_new)
            p = jnp.exp2(s - m_new)
            l_ref[hh] = a * l_ref[hh] + jnp.sum(p, axis=0, keepdims=True)
            acc_ref[hh] = a * acc_ref[hh] + _dot(v_t, p.astype(BF16))
            m_ref[hh] = m_new

    m_ref[...] = jnp.full(m_ref.shape, NEG_BIG, F32)
    l_ref[...] = jnp.zeros(l_ref.shape, F32)
    acc_ref[...] = jnp.zeros(acc_ref.shape, F32)
    step([kc_ref[0, 0], kc_ref[0, 1]], vc_ref[0])

    def body(j, carry):
        off = pl.multiple_of(j * tk, tk)
        step([kl_ref[0, 0, pl.ds(off, tk), :], kl_ref[0, 1, pl.ds(off, tk), :]], vl_ref[0, :, pl.ds(off, tk)])
        return carry

    lax.fori_loop(0, n // tk, body, 0)
    row = lax.broadcasted_iota(jnp.int32, (LANES, tq), 0)
    out_t = jnp.where(row < MLA_V, acc_ref[0] / l_ref[0], acc_ref[1] / l_ref[1])
    o_ref[0] = out_t.T.astype(o_ref.dtype)


def _mla_attention(q, k_lat, k_ctx, v_lat, v_ctx, tq, tk):
    B, H, n, _ = q.shape
    L = k_ctx.shape[2]
    return pl.pallas_call(
        functools.partial(_mla_kernel, tk),
        out_shape=jax.ShapeDtypeStruct((B, n, H * MLA_V), BF16),
        grid=(B, H // 2, n // tq),
        in_specs=[pl.BlockSpec((1, 2, tq, HEAD_PAD), lambda b, j, i: (b, j, i, 0)),
                  pl.BlockSpec((1, 2, L, HEAD_PAD), lambda b, j, i: (b, j, 0, 0)),
                  pl.BlockSpec((1, 2, n, HEAD_PAD), lambda b, j, i: (b, j, 0, 0)),
                  pl.BlockSpec((1, LANES, L), lambda b, j, i: (b, j, 0)),
                  pl.BlockSpec((1, LANES, n), lambda b, j, i: (b, j, 0))],
        out_specs=pl.BlockSpec((1, tq, LANES), lambda b, j, i: (b, i, j)),
        scratch_shapes=[pltpu.VMEM((2, 1, tq), F32), pltpu.VMEM((2, 1, tq), F32),
                        pltpu.VMEM((2, LANES, tq), F32)],
        compiler_params=_cparams(("parallel", "parallel", "parallel")),
        name="mla_attention",
    )(q, k_ctx, k_lat, v_ctx, v_lat)


def _na_block_tables(rows):
    nblk = rows // NA_QROWS
    wh = min(NA_WIN_H, rows)
    band0 = np.clip(np.arange(nblk) * NA_QROWS - wh // 2, 0, rows - NA_BAND)
    sigs, types = [], []
    for i in range(nblk):
        r = i * NA_QROWS + np.arange(NA_QROWS)
        r0 = np.clip(r - wh // 2, 0, rows - wh)
        sig = (tuple(r0 - band0[i]), tuple(r - band0[i]))
        if sig not in sigs:
            sigs.append(sig)
        types.append(sigs.index(sig))
    return band0.astype(np.int32), np.asarray(types, np.int32), sigs, wh


def _na_bias_tiles(rel_bias, rows):
    _, _, sigs, wh = _na_block_tables(rows)
    ww = NA_WIN_W
    n_dr, n_dc = 2 * NA_WIN_H - 1, 2 * NA_WIN_W - 1
    col = np.arange(GRID_W)
    c0 = np.clip(col - ww // 2, 0, GRID_W - ww)
    col_ok = (col[None, :] >= c0[:, None]) & (col[None, :] < c0[:, None] + ww)
    dc = col[None, :] - col[:, None] + (NA_WIN_W - 1)
    pick_c = ((dc[None] == np.arange(n_dc)[:, None, None]) & col_ok[None]).astype(np.float32)
    kr = np.arange(NA_BAND)
    row_ok = np.stack([(kr[None, :] >= np.asarray(r0)[:, None]) & (kr[None, :] < np.asarray(r0)[:, None] + wh)
                       for r0, _ in sigs])
    dr = np.stack([kr[None, :] - np.asarray(r)[:, None] + (NA_WIN_H - 1) for _, r in sigs])
    pick_r = ((dr[..., None] == np.arange(n_dr)) & row_ok[..., None]).astype(np.float32)
    hp = lax.Precision.HIGHEST
    by_col = jnp.einsum('hdj,jxy->hdxy', rel_bias.astype(F32), jnp.asarray(pick_c), precision=hp)
    vals = jnp.einsum('tqkd,hdxy->thqxky', jnp.asarray(pick_r), by_col, precision=hp)
    ok = row_ok[:, None, :, None, :, None] & col_ok[None, None, None, :, None, :]
    tiles = jnp.where(jnp.asarray(ok), vals, NEG_BIG)
    return tiles.reshape(len(sigs), NA_HEADS, NA_QROWS * GRID_W, NA_BAND * GRID_W)


def _na_kernel(band_ref, type_ref, q_ref, k_ref, v_ref, kc_ref, vc_ref, bias_ref, o_ref):
    i = pl.program_id(1)
    nq = NA_QROWS * GRID_W
    nk = NA_BAND * GRID_W
    start = pl.multiple_of(band_ref[i] * GRID_W, GRID_W)
    lane = lax.broadcasted_iota(jnp.int32, (nq, LANES), 1)
    low = lane < NA_DIM
    for j in range(NA_HEADS // 2):
        cs = slice(j * LANES, (j + 1) * LANES)
        qp = q_ref[0, :, cs]
        kb = k_ref[0, pl.ds(start, nk), cs]
        vb = v_ref[0, pl.ds(start, nk), cs]
        kc = kc_ref[0, :, cs]
        vc = vc_ref[0, :, cs]
        outs = []
        for hh in range(2):
            qh = jnp.where(low if hh == 0 else ~low, qp, jnp.zeros_like(qp))
            s_loc = _dot_nt(qh, kb) + bias_ref[0, 2 * j + hh]
            s_ctx = _dot_nt(qh, kc)
            m = jnp.maximum(jnp.max(s_loc, axis=-1, keepdims=True), jnp.max(s_ctx, axis=-1, keepdims=True))
            p_loc = jnp.exp(s_loc - m)
            p_ctx = jnp.exp(s_ctx - m)
            l = jnp.sum(p_loc, axis=-1, keepdims=True) + jnp.sum(p_ctx, axis=-1, keepdims=True)
            o = _dot(p_loc.astype(BF16), vb) + _dot(p_ctx.astype(BF16), vc)
            outs.append(o / l)
        o_ref[0, :, cs] = jnp.where(low, outs[0], outs[1]).astype(o_ref.dtype)


def _na_attention(nq, nk, nv, cnk, cnv, bias_tiles):
    B, n, W = nq.shape
    L = cnk.shape[1]
    rows = n // GRID_W
    band0, types, _, _ = _na_block_tables(rows)
    qn = NA_QROWS * GRID_W
    kn = NA_BAND * GRID_W
    grid_spec = pltpu.PrefetchScalarGridSpec(
        num_scalar_prefetch=2,
        grid=(B, rows // NA_QROWS),
        in_specs=[pl.BlockSpec((1, qn, W), lambda b, i, bd, ty: (b, i, 0)),
                  pl.BlockSpec((1, n, W), lambda b, i, bd, ty: (b, 0, 0)),
                  pl.BlockSpec((1, n, W), lambda b, i, bd, ty: (b, 0, 0)),
                  pl.BlockSpec((1, L, W), lambda b, i, bd, ty: (b, 0, 0)),
                  pl.BlockSpec((1, L, W), lambda b, i, bd, ty: (b, 0, 0)),
                  pl.BlockSpec((1, NA_HEADS, qn, kn), lambda b, i, bd, ty: (ty[i], 0, 0, 0))],
        out_specs=pl.BlockSpec((1, qn, W), lambda b, i, bd, ty: (b, i, 0)),
    )
    return pl.pallas_call(
        _na_kernel,
        out_shape=jax.ShapeDtypeStruct((B, n, W), BF16),
        grid_spec=grid_spec,
        compiler_params=_cparams(("parallel", "arbitrary")),
        name="na_attention",
    )(jnp.asarray(band0), jnp.asarray(types), nq, nk, nv, cnk, cnv, bias_tiles)


def _split2(a):
    hi = a.astype(BF16)
    return hi, (a - hi.astype(F32)).astype(BF16)


def _merge_kernel(ym_ref, yn_ref, g_ref, x_ref, g1_ref, sc2_ref, sh2_ref, wpm_ref, wpn_ref, wo_ref,
                  l1g_ref, l1b_ref, wr_ref, xn_ref, up_ref, aff_ref):
    D = x_ref.shape[2]
    g = g_ref[0]
    a = g[:, :D].astype(F32) * _dot(ym_ref[0], wpm_ref[...]) + g[:, D:].astype(F32) * _dot(yn_ref[0], wpn_ref[...])
    mix = _dot(a.astype(BF16), wo_ref[...])
    xn = _ln(ALPHA * x_ref[0] + g1_ref[0] * mix) * l1g_ref[...] + l1b_ref[...]
    xn_ref[0] = xn
    u2 = _ln(xn) * (1.0 + sc2_ref[0]) + sh2_ref[0]

    up_ref[0] = u2

    uh, ul = _split2(u2)
    wh, wl = _split2(wr_ref[...])
    logits = _dot(uh, wh) + (_dot(uh, wl) + _dot(ul, wh))
    e = jnp.exp(logits - jnp.max(logits, axis=-1, keepdims=True))
    aff_ref[0] = e / jnp.sum(e, axis=-1, keepdims=True)


def _merge(y_mla, y_na, gates, x, g1, sc2, sh2, wpm, wpn, wo, l1g, l1b, wr, tm):
    B, n, D = x.shape
    E = wr.shape[1]
    full = lambda a: pl.BlockSpec(a.shape, lambda b, i: (0,) * a.ndim)
    tok = lambda w: pl.BlockSpec((1, tm, w), lambda b, i: (b, i, 0))
    mod = pl.BlockSpec((1, 1, D), lambda b, i: (b, 0, 0))
    return pl.pallas_call(
        _merge_kernel,
        out_shape=[jax.ShapeDtypeStruct((B, n, D), F32),
                   jax.ShapeDtypeStruct((B, n, D), F32),
                   jax.ShapeDtypeStruct((B, n, E), F32)],
        grid=(B, n // tm),
        in_specs=[tok(y_mla.shape[2]), tok(y_na.shape[2]), tok(2 * D), tok(D), mod, mod, mod,
                  full(wpm), full(wpn), full(wo), full(l1g), full(l1b), full(wr)],
        out_specs=[tok(D), tok(D), tok(E)],
        compiler_params=_cparams(("parallel", "parallel")),
        name="merge_ln_router",
    )(y_mla, y_na, gates, x, g1, sc2, sh2, wpm, wpn, wo, l1g, l1b, wr)


def _topk_kernel(cap, aff_ref, tri_ref, idx_ref, g_ref, pos_ref):
    aff = aff_ref[0]
    E, n = aff.shape

    def count(mask):
        return jnp.sum(mask.astype(F32), axis=-1, keepdims=True)

    def search(t, thr):
        cand = thr | (jnp.int32(1) << (30 - t))
        return jnp.where(count(aff >= pltpu.bitcast(cand, F32)) >= cap, cand, thr)

    thr = lax.fori_loop(0, 31, search, jnp.zeros((E, 1), jnp.int32))
    lo = pltpu.bitcast(thr, F32)
    hi = pltpu.bitcast(thr + 1, F32)

    def refine(t, lh):
        lo, hi = lh
        mid = lo + (hi - lo) * 0.5
        ok = count(aff >= mid) >= cap
        return jnp.where(ok, mid, lo), jnp.where(ok, hi, mid)

    lo, hi = lax.fori_loop(0, TOPK_REFINE, refine, (lo, hi))
    gt = aff >= hi
    eq = (aff >= lo) & ~gt
    need = cap - count(gt)

    tri = tri_ref[...]

    def prefix(mask):
        mf = mask.astype(F32).astype(BF16)
        parts = []
        off = jnp.zeros((E, 1), F32)
        for c in range(n // LANES):
            blk = mf[:, c * LANES:(c + 1) * LANES]
            parts.append(_dot(blk, tri) + off)
            off = off + jnp.sum(blk.astype(F32), axis=-1, keepdims=True)
        return jnp.concatenate(parts, axis=-1)

    sel = gt | (eq & (prefix(eq) < need))
    pos_ref[...] = jnp.where(sel, prefix(sel), -1.0)

    chunks = cap // TOPK_SLOTS

    def emit(t, carry):
        e = t // chunks
        base = pl.multiple_of((t % chunks) * TOPK_SLOTS, TOPK_SLOTS)
        slot = (lax.broadcasted_iota(jnp.int32, (TOPK_SLOTS, n), 0) + base).astype(F32)
        tok = lax.broadcasted_iota(jnp.int32, (TOPK_SLOTS, n), 1).astype(F32)
        hit = slot == pos_ref[pl.ds(e, 1), :]
        idx = jnp.sum(jnp.where(hit, tok, 0.0), axis=-1, keepdims=True)
        idx_ref[0, e, pl.ds(base, TOPK_SLOTS), :] = idx.astype(jnp.int32)
        g_ref[0, e, pl.ds(base, TOPK_SLOTS), :] = jnp.sum(
            jnp.where(hit, aff_ref[0, pl.ds(e, 1), :], 0.0), axis=-1, keepdims=True)
        return carry

    lax.fori_loop(0, E * chunks, emit, 0)


def _topk(aff_t, cap):
    B, E, n = aff_t.shape
    tri = jnp.asarray(np.triu(np.ones((LANES, LANES), np.float32), k=1), BF16)
    return pl.pallas_call(
        functools.partial(_topk_kernel, cap),
        out_shape=[jax.ShapeDtypeStruct((B, E, cap, 1), jnp.int32),
                   jax.ShapeDtypeStruct((B, E, cap, 1), F32)],
        grid=(B,),
        in_specs=[pl.BlockSpec((1, E, n), lambda b: (b, 0, 0)),
                  pl.BlockSpec((LANES, LANES), lambda b: (0, 0))],
        out_specs=[pl.BlockSpec((1, E, cap, 1), lambda b: (b, 0, 0, 0)),
                   pl.BlockSpec((1, E, cap, 1), lambda b: (b, 0, 0, 0))],
        scratch_shapes=[pltpu.VMEM((E, n), F32)],
        compiler_params=_cparams(("parallel",)),
        name="expert_topk",
    )(aff_t, tri)


def _gather_kernel(idx_ref, u_ref, o_ref):
    cap = o_ref.shape[2]

    def body(c, carry):
        r = idx_ref[0, 0, 0, c]
        o_ref[0, 0, pl.ds(c, 1), :] = u_ref[0, pl.ds(r, 1), :]
        return carry

    lax.fori_loop(0, cap, body, 0, unroll=8)


def _gather(idx, u2):
    B, E, _, cap = idx.shape
    _, n, W = u2.shape
    return pl.pallas_call(
        _gather_kernel,
        out_shape=jax.ShapeDtypeStruct((B, E, cap, W), u2.dtype),
        grid=(B, E),
        in_specs=[pl.BlockSpec((1, 1, 1, cap), lambda b, e: (b, e, 0, 0), memory_space=pltpu.SMEM),
                  pl.BlockSpec((1, n, W), lambda b, e: (b, 0, 0))],
        out_specs=pl.BlockSpec((1, 1, cap, W), lambda b, e: (b, e, 0, 0)),
        compiler_params=_cparams(("parallel", "arbitrary")),
        name="expert_gather",
    )(idx, u2)


def _ffn_kernel(xe_ref, g_ref, wg_ref, wu_ref, wd_ref, o_ref, wgb_ref, wub_ref, wdb_ref):
    @pl.when(pl.program_id(1) == 0)
    def _():
        wgb_ref[...] = wg_ref[0].astype(BF16)
        wub_ref[...] = wu_ref[0].astype(BF16)
        wdb_ref[...] = wd_ref[0].astype(BF16)

    xe = xe_ref[0, 0].astype(BF16)
    gate = _dot(xe, wgb_ref[...])
    up = _dot(xe, wub_ref[...])
    h = (gate * jax.nn.sigmoid(gate) * up).astype(BF16)
    o_ref[0, 0] = _dot(h, wdb_ref[...]) * g_ref[0, 0]


def _ffn(xe, g, w_gate, w_up, w_down):
    B, E, cap, W = xe.shape
    _, D, F = w_gate.shape
    return pl.pallas_call(
        _ffn_kernel,
        out_shape=jax.ShapeDtypeStruct((B, E, cap, D), F32),
        grid=(E, B),
        in_specs=[pl.BlockSpec((1, 1, cap, W), lambda e, b: (b, e, 0, 0)),
                  pl.BlockSpec((1, 1, cap, 1), lambda e, b: (b, e, 0, 0)),
                  pl.BlockSpec((1, D, F), lambda e, b: (e, 0, 0)),
                  pl.BlockSpec((1, D, F), lambda e, b: (e, 0, 0)),
                  pl.BlockSpec((1, F, D), lambda e, b: (e, 0, 0))],
        out_specs=pl.BlockSpec((1, 1, cap, D), lambda e, b: (b, e, 0, 0)),
        scratch_shapes=[pltpu.VMEM((D, F), BF16), pltpu.VMEM((D, F), BF16), pltpu.VMEM((F, D), BF16)],
        compiler_params=_cparams(("arbitrary", "arbitrary")),
        name="expert_ffn",
    )(xe, g, w_gate, w_up, w_down)


def _scatter_kernel(idx_ref, y_ref, o_ref):
    cap = y_ref.shape[2]

    @pl.when(pl.program_id(2) == 0)
    def _():
        o_ref[...] = jnp.zeros_like(o_ref)

    def body(c, carry):
        r = idx_ref[0, 0, 0, c]
        o_ref[0, pl.ds(r, 1), :] = o_ref[0, pl.ds(r, 1), :] + y_ref[0, 0, pl.ds(c, 1), :]
        return carry

    lax.fori_loop(0, cap, body, 0, unroll=8)


def _scatter(idx, ye, n):
    B, E, cap, D = ye.shape
    halves = 2
    dw = D // halves
    return pl.pallas_call(
        _scatter_kernel,
        out_shape=jax.ShapeDtypeStruct((B, n, D), F32),
        grid=(B, halves, E),
        in_specs=[pl.BlockSpec((1, 1, 1, cap), lambda b, s, e: (b, e, 0, 0), memory_space=pltpu.SMEM),
                  pl.BlockSpec((1, 1, cap, dw), lambda b, s, e: (b, e, 0, s))],
        out_specs=pl.BlockSpec((1, n, dw), lambda b, s, e: (b, 0, s)),
        compiler_params=_cparams(("parallel", "parallel", "arbitrary")),
        name="expert_scatter",
    )(idx, ye)


def _final_kernel(x_ref, moe_ref, g2_ref, lg_ref, lb_ref, o_ref):
    o_ref[0] = _ln(ALPHA * x_ref[0] + g2_ref[0] * moe_ref[0]) * lg_ref[...] + lb_ref[...]


def _final(xn, moe, g2, lg, lb, tm):
    B, n, D = xn.shape
    tok = pl.BlockSpec((1, tm, D), lambda b, i: (b, i, 0))
    vec = pl.BlockSpec((1, D), lambda b, i: (0, 0))
    return pl.pallas_call(
        _final_kernel,
        out_shape=jax.ShapeDtypeStruct((B, n, D), F32),
        grid=(B, n // tm),
        in_specs=[tok, tok, pl.BlockSpec((1, 1, D), lambda b, i: (b, 0, 0)), vec, vec],
        out_specs=tok,
        compiler_params=_cparams(("parallel", "parallel")),
        name="final_ln",
    )(xn, moe, g2, lg, lb)


def _rope_tables(n):
    t = np.arange(n)
    row = (t // GRID_W).astype(np.float32)
    col = (t % GRID_W).astype(np.float32)
    per_axis = MLA_ROPE // 2
    inv_freq = jnp.asarray(ROPE_THETA, F32) ** (-jnp.arange(0, per_axis, 2, dtype=F32) / per_axis)
    ang = jnp.concatenate([jnp.asarray(row)[:, None] * inv_freq, jnp.asarray(col)[:, None] * inv_freq], axis=-1)
    cos, sin = jnp.cos(ang), jnp.sin(ang)
    pad = HEAD_PAD - MLA_QK
    c_tab = jnp.concatenate([jnp.ones((n, MLA_NOPE), F32), cos, cos, jnp.zeros((n, pad), F32)], axis=-1)
    s_tab = jnp.concatenate([jnp.zeros((n, MLA_NOPE), F32), -sin, sin, jnp.zeros((n, pad), F32)], axis=-1)
    return c_tab, s_tab


def _swap_halves(w):
    half = w.shape[-1] // 2
    return jnp.concatenate([w[..., half:], w[..., :half]], axis=-1)


def _prep_weights(w_in, w_uq, w_ukv):
    D = w_in.shape[0]
    o1 = Q_LORA
    o2 = o1 + KV_LORA
    o3 = o2 + MLA_ROPE
    o4 = o3 + 3 * NA_W
    w_kr = w_in[:, o2:o3]
    z = lambda k: jnp.zeros((D, k), w_in.dtype)
    pad = HEAD_PAD - MLA_QK
    kra = jnp.concatenate([z(MLA_NOPE), w_kr, z(pad)], axis=-1)
    krb = jnp.concatenate([z(MLA_NOPE), _swap_halves(w_kr), z(pad)], axis=-1)
    w_small = jnp.concatenate([w_in[:, :o2], kra, krb], axis=-1).astype(BF16)
    w_na = w_in[:, o3:o4].astype(BF16)
    w_g = w_in[:, o4:].astype(BF16)

    uq = w_uq.reshape(Q_LORA, MLA_HEADS, MLA_QK)
    zq = lambda k: jnp.zeros((Q_LORA, MLA_HEADS, k), w_uq.dtype)
    wqa = jnp.concatenate([uq, zq(pad)], axis=-1).reshape(Q_LORA, MLA_HEADS * HEAD_PAD).astype(BF16)
    wqb = jnp.concatenate([zq(MLA_NOPE), _swap_halves(uq[..., MLA_NOPE:]), zq(pad)], axis=-1)
    wqb = wqb.reshape(Q_LORA, MLA_HEADS * HEAD_PAD).astype(BF16)

    ukv = w_ukv.reshape(KV_LORA, MLA_HEADS, MLA_NOPE + MLA_V)
    wk = jnp.concatenate([ukv[..., :MLA_NOPE], jnp.zeros((KV_LORA, MLA_HEADS, HEAD_PAD - MLA_NOPE), w_ukv.dtype)], axis=-1)
    wk = wk.reshape(KV_LORA, MLA_HEADS * HEAD_PAD).astype(BF16)
    wv = ukv[..., MLA_NOPE:].reshape(KV_LORA, MLA_HEADS * MLA_V).T.astype(BF16)
    return w_small, w_na, w_g, wqa, wqb, wk, wv


def _tile(n, pref):
    t = min(pref, n)
    while n % t:
        t //= 2
    return t


def kernel(x, c, ctx, c_ctx, w_mod, b_mod, w_in, q_norm_g, w_uq, kv_norm_g, w_ukv, na_rel_bias, w_proj_mla,
           w_proj_na, w_out, ln1_g, ln1_b, w_router, w_exp_gate, w_exp_up, w_exp_down, ln2_g, ln2_b):
    B, n, D = x.shape
    L = ctx.shape[1]
    rows = n // GRID_W
    assert n % (GRID_W * NA_QROWS) == 0 and rows >= NA_BAND
    assert w_mod.shape[0] == DEPTH
    cap = EC_CAPACITY * n // N_EXPERTS

    mod_rows = -(-(B + 1) // 8) * 8
    cc = jnp.concatenate([c, c_ctx[None], jnp.zeros((mod_rows - B - 1, D), F32)], axis=0)
    m = _modulation(cc, w_mod[0], b_mod[0])
    sh1, sc1, g1, sh2, sc2, g2 = [m[:B, k * D:(k + 1) * D].reshape(B, 1, D) for k in range(6)]
    csh1 = m[B:B + 1, :D].reshape(1, 1, D)
    csc1 = m[B:B + 1, D:2 * D].reshape(1, 1, D)

    w_small, w_na, w_g, wqa, wqb, wk, wv = _prep_weights(w_in[0], w_uq[0], w_ukv[0])
    qg = q_norm_g[0].reshape(1, Q_LORA)
    kvg = kv_norm_g[0].reshape(1, KV_LORA)
    c_tab, s_tab = _rope_tables(n)
    pad = HEAD_PAD - MLA_QK
    c_ctx_tab = jnp.concatenate([jnp.ones((L, MLA_QK), F32), jnp.zeros((L, pad), F32)], axis=-1)
    s_ctx_tab = jnp.zeros((L, HEAD_PAD), F32)

    tm = _tile(n, 256)
    q, k_lat, v_lat, nq, nk, nv, gates = _inproj(
        True, x, sc1, sh1, c_tab, s_tab, c_tab * (MLA_SCALE * LOG2E), s_tab * (MLA_SCALE * LOG2E),
        w_small, w_na, w_g, qg, kvg, wqa, wqb, wk, wv, tm)
    k_ctx, v_ctx, cnk, cnv = _inproj(
        False, ctx, csc1, csh1, c_ctx_tab, s_ctx_tab, c_ctx_tab, s_ctx_tab,
        w_small, w_na[:, NA_W:], w_g, qg, kvg, wqa, wqb, wk, wv, _tile(L, 256))

    y_mla = _mla_attention(q, k_lat, k_ctx, v_lat, v_ctx, _tile(n, 512), _tile(n, 512))
    y_na = _na_attention(nq, nk, nv, cnk, cnv, _na_bias_tiles(na_rel_bias[0], rows))

    xn, u2, aff = _merge(
        y_mla, y_na, gates, x, g1, sc2, sh2,
        w_proj_mla[0].astype(BF16), w_proj_na[0].astype(BF16), w_out[0].astype(BF16),
        ln1_g[0].reshape(1, D), ln1_b[0].reshape(1, D), w_router[0], tm)

    idx4, g4 = _topk(jnp.swapaxes(aff, 1, 2), cap)
    idx = idx4.reshape(B, N_EXPERTS, 1, cap)
    xe = _gather(idx, u2)
    ye = _ffn(xe, g4, w_exp_gate[0], w_exp_up[0], w_exp_down[0])
    moe = _scatter(idx, ye, n)
    return _final(xn, moe, g2, ln2_g[0].reshape(1, D), ln2_b[0].reshape(1, D), tm)
```

```python
import functools
import math

import numpy as np
import jax
import jax.numpy as jnp
from jax import lax
from jax.experimental import pallas as pl
from jax.experimental.pallas import tpu as pltpu

GRID_W = 64
MLA_HEADS = 8
MLA_NOPE = 64
MLA_ROPE = 32
MLA_QK = MLA_NOPE + MLA_ROPE
MLA_V = 64
Q_LORA = 256
KV_LORA = 128
MLA_SCALE = MLA_QK ** -0.5
ROPE_THETA = 10000.0
NA_HEADS = 8
NA_DIM = 64
NA_W = NA_HEADS * NA_DIM
NA_WIN_H = 8
NA_WIN_W = 16
NA_SCALE = NA_DIM ** -0.5
N_EXPERTS = 16
EC_CAPACITY = 2
LN_EPS = 1e-5
RMS_EPS = 1e-6
DEPTH = 1
ALPHA = (2.0 * DEPTH) ** 0.25
LOG2E = math.log2(math.e)

LANES = 128
HEAD_PAD = LANES
VMEM_LIMIT = 56 * 1024 * 1024
NEG_BIG = -1e30

NA_QROWS = 4
NA_BAND = 12
TOPK_SLOTS = 64
TOPK_REFINE = 30

BF16 = jnp.bfloat16
F32 = jnp.float32


def _cparams(sem):
    return pltpu.CompilerParams(dimension_semantics=sem, vmem_limit_bytes=VMEM_LIMIT)


def _ln(x):
    mu = jnp.mean(x, axis=-1, keepdims=True)
    xc = x - mu
    var = jnp.mean(xc * xc, axis=-1, keepdims=True)
    return xc * lax.rsqrt(var + LN_EPS)


def _dot(a, b):
    return jnp.dot(a, b, preferred_element_type=F32)


def _dot_nt(a, b):
    return lax.dot_general(a, b, (((1,), (1,)), ((), ())), preferred_element_type=F32)


def _mod_kernel(c_ref, w_ref, b_ref, o_ref):
    c = c_ref[...]
    s = c * jax.nn.sigmoid(c)
    o_ref[...] = jnp.dot(s, w_ref[...], preferred_element_type=F32,
                         precision=lax.Precision.HIGHEST) + b_ref[...]


def _modulation(cc, w_mod, b_mod):
    rows, d = cc.shape
    n_out = w_mod.shape[1]
    tn = 1024
    return pl.pallas_call(
        _mod_kernel,
        out_shape=jax.ShapeDtypeStruct((rows, n_out), F32),
        grid=(n_out // tn,),
        in_specs=[pl.BlockSpec((rows, d), lambda j: (0, 0)),
                  pl.BlockSpec((d, tn), lambda j: (0, j)),
                  pl.BlockSpec((1, tn), lambda j: (0, j))],
        out_specs=pl.BlockSpec((rows, tn), lambda j: (0, j)),
        compiler_params=_cparams(("arbitrary",)),
        name="modulation",
    )(cc, w_mod, b_mod.reshape(1, n_out))


def _inproj_kernel(latent, x_ref, sc_ref, sh_ref, ck_ref, sk_ref, cq_ref, sq_ref,
                   w_small_ref, w_na_ref, w_nvt_ref, w_g_ref, qg_ref, kvg_ref, wqa_ref, wqb_ref, wk_ref, wv_ref,
                   *out_refs):
    if latent:
        q_ref, k_ref, v_ref, nq_ref, nk_ref, nv_ref, g_ref = out_refs
    else:
        k_ref, v_ref, nk_ref, nv_ref = out_refs
    x = x_ref[0]
    u = (_ln(x) * (1.0 + sc_ref[0]) + sh_ref[0]).astype(BF16)

    small = _dot(u, w_small_ref[...])
    q_c = small[:, :Q_LORA]
    kv_c = small[:, Q_LORA:Q_LORA + KV_LORA]
    kra = small[:, Q_LORA + KV_LORA:Q_LORA + KV_LORA + LANES]
    krb = small[:, Q_LORA + KV_LORA + LANES:]

    kvn = (kv_c * lax.rsqrt(jnp.mean(kv_c * kv_c, axis=-1, keepdims=True) + RMS_EPS) * kvg_ref[...]).astype(BF16)
    kk = _dot(kvn, wk_ref[...])
    kr = kra * ck_ref[...] + krb * sk_ref[...]
    for h in range(MLA_HEADS):
        k_ref[0, h] = (kk[:, h * HEAD_PAD:(h + 1) * HEAD_PAD] + kr).astype(BF16)
    v_ref[0] = _dot_nt(wv_ref[...], kvn).astype(BF16)

    na = _dot(u, w_na_ref[...])
    nv_ref[0] = _dot_nt(w_nvt_ref[...], u).astype(BF16)
    if latent:
        qn = (q_c * lax.rsqrt(jnp.mean(q_c * q_c, axis=-1, keepdims=True) + RMS_EPS) * qg_ref[...]).astype(BF16)
        qa = _dot(qn, wqa_ref[...])
        qb = _dot(qn, wqb_ref[...])
        cq = cq_ref[...]
        sq = sq_ref[...]
        for h in range(MLA_HEADS):
            sl = slice(h * HEAD_PAD, (h + 1) * HEAD_PAD)
            q_ref[0, h] = (qa[:, sl] * cq + qb[:, sl] * sq).astype(BF16)
        nq_ref[0] = (na[:, :NA_W] * (NA_SCALE * LOG2E)).astype(BF16)
        nk_ref[0] = na[:, NA_W:].astype(BF16)
        g_ref[0] = jax.nn.sigmoid(_dot(u, w_g_ref[...])).astype(BF16)
    else:
        nk_ref[0] = na.astype(BF16)


def _inproj(latent, x, sc, sh, ck, sk, cq, sq, w_small, w_na, w_nvt, w_g, qg, kvg, wqa, wqb, wk, wv, tm):
    B, n, D = x.shape
    per_batch = sc.shape[0] > 1
    mod_map = (lambda b, i: (b, 0, 0)) if per_batch else (lambda b, i: (0, 0, 0))
    full = lambda a: pl.BlockSpec(a.shape, lambda b, i: (0,) * a.ndim)
    tab = lambda a: pl.BlockSpec((tm, a.shape[1]), lambda b, i: (i, 0))
    in_specs = [pl.BlockSpec((1, tm, D), lambda b, i: (b, i, 0)),
                pl.BlockSpec((1, 1, D), mod_map), pl.BlockSpec((1, 1, D), mod_map),
                tab(ck), tab(sk), tab(cq), tab(sq),
                full(w_small), full(w_na), full(w_nvt), full(w_g), full(qg), full(kvg),
                full(wqa), full(wqb), full(wk), full(wv)]
    hk = jax.ShapeDtypeStruct((B, MLA_HEADS, n, HEAD_PAD), BF16)
    hk_spec = pl.BlockSpec((1, MLA_HEADS, tm, HEAD_PAD), lambda b, i: (b, 0, i, 0))
    tok = lambda w: jax.ShapeDtypeStruct((B, n, w), BF16)
    tok_spec = lambda w: pl.BlockSpec((1, tm, w), lambda b, i: (b, i, 0))
    vt = jax.ShapeDtypeStruct((B, MLA_HEADS * MLA_V, n), BF16)
    vt_spec = pl.BlockSpec((1, MLA_HEADS * MLA_V, tm), lambda b, i: (b, 0, i))
    if latent:
        out_shape = [hk, hk, vt, tok(NA_W), tok(NA_W), vt, tok(2 * D)]
        out_specs = [hk_spec, hk_spec, vt_spec, tok_spec(NA_W), tok_spec(NA_W), vt_spec, tok_spec(2 * D)]
    else:
        out_shape = [hk, vt, tok(NA_W), vt]
        out_specs = [hk_spec, vt_spec, tok_spec(NA_W), vt_spec]
    return pl.pallas_call(
        functools.partial(_inproj_kernel, latent),
        out_shape=out_shape,
        grid=(B, n // tm),
        in_specs=in_specs,
        out_specs=out_specs,
        compiler_params=_cparams(("parallel", "parallel")),
        name="inproj_latent" if latent else "inproj_ctx",
    )(x, sc, sh, ck, sk, cq, sq, w_small, w_na, w_nvt, w_g, qg, kvg, wqa, wqb, wk, wv)


def _mla_kernel(tk, q_ref, kc_ref, kl_ref, vc_ref, vl_ref, o_ref, m_ref, l_ref, acc_ref, sa_ref, sb_ref):
    n = kl_ref.shape[2]
    tq = q_ref.shape[2]
    nt = n // tk

    def scores(hh, k):
        return _dot_nt(k, q_ref[0, hh])

    def absorb(hh, s, v_t):
        m_old = m_ref[hh]
        m_new = jnp.maximum(m_old, jnp.max(s, axis=0, keepdims=True))
        a = jnp.exp2(m_old - m_new)
        p = jnp.exp2(s - m_new)
        l_ref[hh] = a * l_ref[hh] + jnp.sum(p, axis=0, keepdims=True)
        acc_ref[hh] = a * acc_ref[hh] + _dot(v_t, p.astype(BF16))
        m_ref[hh] = m_new

    def k_tile(hh, j):
        return kl_ref[0, hh, pl.ds(pl.multiple_of(j * tk, tk), tk), :]

    def v_tile(j):
        return vl_ref[0, :, pl.ds(pl.multiple_of(j * tk, tk), tk)]

    m_ref[...] = jnp.full(m_ref.shape, NEG_BIG, F32)
    l_ref[...] = jnp.zeros(l_ref.shape, F32)
    acc_ref[...] = jnp.zeros(acc_ref.shape, F32)
    s_ctx = [scores(hh, kc_ref[0, hh]) for hh in range(2)]
    for hh in range(2):
        sa_ref[hh] = scores(hh, k_tile(hh, 0))
        absorb(hh, s_ctx[hh], vc_ref[0])

    def advance(j, cur_ref, next_ref):
        v_t = v_tile(j)
        for hh in range(2):
            next_ref[hh] = scores(hh, k_tile(hh, j + 1))
            absorb(hh, cur_ref[hh], v_t)

    def body(jj, carry):
        advance(2 * jj, sa_ref, sb_ref)
        advance(2 * jj + 1, sb_ref, sa_ref)
        return carry

    lax.fori_loop(0, (nt - 1) // 2, body, 0)
    last_ref = sa_ref
    if (nt - 1) % 2:
        advance(nt - 2, sa_ref, sb_ref)
        last_ref = sb_ref
    v_t = v_tile(nt - 1)
    for hh in range(2):
        absorb(hh, last_ref[hh], v_t)
    row = lax.broadcasted_iota(jnp.int32, (LANES, tq), 0)
    out_t = jnp.where(row < MLA_V, acc_ref[0] / l_ref[0], acc_ref[1] / l_ref[1])
    o_ref[0] = out_t.T.astype(o_ref.dtype)


def _mla_attention(q, k_lat, k_ctx, v_lat, v_ctx, tq, tk):
    B, H, n, _ = q.shape
    L = k_ctx.shape[2]
    return pl.pallas_call(
        functools.partial(_mla_kernel, tk),
        out_shape=jax.ShapeDtypeStruct((B, n, H * MLA_V), BF16),
        grid=(B, H // 2, n // tq),
        in_specs=[pl.BlockSpec((1, 2, tq, HEAD_PAD), lambda b, j, i: (b, j, i, 0)),
                  pl.BlockSpec((1, 2, L, HEAD_PAD), lambda b, j, i: (b, j, 0, 0)),
                  pl.BlockSpec((1, 2, n, HEAD_PAD), lambda b, j, i: (b, j, 0, 0)),
                  pl.BlockSpec((1, LANES, L), lambda b, j, i: (b, j, 0)),
                  pl.BlockSpec((1, LANES, n), lambda b, j, i: (b, j, 0))],
        out_specs=pl.BlockSpec((1, tq, LANES), lambda b, j, i: (b, i, j)),
        scratch_shapes=[pltpu.VMEM((2, 1, tq), F32), pltpu.VMEM((2, 1, tq), F32),
                        pltpu.VMEM((2, LANES, tq), F32),
                        pltpu.VMEM((2, tk, tq), F32), pltpu.VMEM((2, tk, tq), F32)],
        compiler_params=_cparams(("parallel", "parallel", "parallel")),
        name="mla_attention",
    )(q, k_ctx, k_lat, v_ctx, v_lat)


def _na_block_tables(rows):
    nblk = rows // NA_QROWS
    wh = min(NA_WIN_H, rows)
    band0 = np.clip(np.arange(nblk) * NA_QROWS - wh // 2, 0, rows - NA_BAND)
    sigs, types = [], []
    for i in range(nblk):
        r = i * NA_QROWS + np.arange(NA_QROWS)
        r0 = np.clip(r - wh // 2, 0, rows - wh)
        sig = (tuple(r0 - band0[i]), tuple(r - band0[i]))
        if sig not in sigs:
            sigs.append(sig)
        types.append(sigs.index(sig))
    return band0.astype(np.int32), np.asarray(types, np.int32), sigs, wh


def _na_bias_tiles(rel_bias, rows):
    _, _, sigs, wh = _na_block_tables(rows)
    ww = NA_WIN_W
    n_dr, n_dc = 2 * NA_WIN_H - 1, 2 * NA_WIN_W - 1
    col = np.arange(GRID_W)
    c0 = np.clip(col - ww // 2, 0, GRID_W - ww)
    col_ok = (col[None, :] >= c0[:, None]) & (col[None, :] < c0[:, None] + ww)
    dc = col[None, :] - col[:, None] + (NA_WIN_W - 1)
    pick_c = ((dc[None] == np.arange(n_dc)[:, None, None]) & col_ok[None]).astype(np.float32)
    kr = np.arange(NA_BAND)
    row_ok = np.stack([(kr[None, :] >= np.asarray(r0)[:, None]) & (kr[None, :] < np.asarray(r0)[:, None] + wh)
                       for r0, _ in sigs])
    dr = np.stack([kr[None, :] - np.asarray(r)[:, None] + (NA_WIN_H - 1) for _, r in sigs])
    pick_r = ((dr[..., None] == np.arange(n_dr)) & row_ok[..., None]).astype(np.float32)
    hp = lax.Precision.HIGHEST
    by_col = jnp.einsum('hdj,jxy->hdxy', rel_bias.astype(F32), jnp.asarray(pick_c), precision=hp)
    vals = jnp.einsum('tqkd,hdxy->thkyqx', jnp.asarray(pick_r), by_col, precision=hp)
    ok = np.transpose(row_ok, (0, 2, 1))[:, None, :, None, :, None] & col_ok.T[None, None, None, :, None, :]
    tiles = jnp.where(jnp.asarray(ok), vals * LOG2E, NEG_BIG)
    return tiles.reshape(len(sigs), NA_HEADS, NA_BAND * GRID_W, NA_QROWS * GRID_W)


def _na_kernel(band_ref, type_ref, q_ref, k_ref, vt_ref, kc_ref, vct_ref, bias_ref, o_ref):
    i = pl.program_id(1)
    nq = NA_QROWS * GRID_W
    nk = NA_BAND * GRID_W
    start = pl.multiple_of(band_ref[i] * GRID_W, NA_QROWS * GRID_W)
    lane = lax.broadcasted_iota(jnp.int32, (nq, LANES), 1)
    row = lax.broadcasted_iota(jnp.int32, (LANES, nq), 0)

    def scores(h):
        cs = slice((h // 2) * LANES, (h // 2 + 1) * LANES)
        qp = q_ref[0, :, cs]
        own = (lane < NA_DIM) if h % 2 == 0 else (lane >= NA_DIM)
        qh = jnp.where(own, qp, jnp.zeros_like(qp))
        s_loc = _dot_nt(k_ref[0, pl.ds(start, nk), cs], qh) + bias_ref[0, h]
        s_ctx = _dot_nt(kc_ref[0, :, cs], qh)
        return s_loc, s_ctx

    def attend(h, s):
        s_loc, s_ctx = s
        cs = slice((h // 2) * LANES, (h // 2 + 1) * LANES)
        m = jnp.maximum(jnp.max(s_loc, axis=0, keepdims=True), jnp.max(s_ctx, axis=0, keepdims=True))
        p_loc = jnp.exp2(s_loc - m)
        p_ctx = jnp.exp2(s_ctx - m)
        l = jnp.sum(p_loc, axis=0, keepdims=True) + jnp.sum(p_ctx, axis=0, keepdims=True)
        o = _dot(vt_ref[0, cs, pl.ds(start, nk)], p_loc.astype(BF16)) + _dot(vct_ref[0, cs, :], p_ctx.astype(BF16))
        return o / l

    s = scores(0)
    outs = []
    for h in range(NA_HEADS):
        s_next = scores(h + 1) if h + 1 < NA_HEADS else None
        outs.append(attend(h, s))
        s = s_next
        if h % 2:
            pair_t = jnp.where(row < NA_DIM, outs[h - 1], outs[h])
            o_ref[0, :, (h // 2) * LANES:(h // 2 + 1) * LANES] = pair_t.T.astype(o_ref.dtype)


def _na_attention(nq, nk, nvt, cnk, cnvt, bias_tiles):
    B, n, W = nq.shape
    L = cnk.shape[1]
    rows = n // GRID_W
    band0, types, _, _ = _na_block_tables(rows)
    qn = NA_QROWS * GRID_W
    kn = NA_BAND * GRID_W
    grid_spec = pltpu.PrefetchScalarGridSpec(
        num_scalar_prefetch=2,
        grid=(B, rows // NA_QROWS),
        in_specs=[pl.BlockSpec((1, qn, W), lambda b, i, bd, ty: (b, i, 0)),
                  pl.BlockSpec((1, n, W), lambda b, i, bd, ty: (b, 0, 0)),
                  pl.BlockSpec((1, W, n), lambda b, i, bd, ty: (b, 0, 0)),
                  pl.BlockSpec((1, L, W), lambda b, i, bd, ty: (b, 0, 0)),
                  pl.BlockSpec((1, W, L), lambda b, i, bd, ty: (b, 0, 0)),
                  pl.BlockSpec((1, NA_HEADS, kn, qn), lambda b, i, bd, ty: (ty[i], 0, 0, 0))],
        out_specs=pl.BlockSpec((1, qn, W), lambda b, i, bd, ty: (b, i, 0)),
    )
    return pl.pallas_call(
        _na_kernel,
        out_shape=jax.ShapeDtypeStruct((B, n, W), BF16),
        grid_spec=grid_spec,
        compiler_params=_cparams(("parallel", "arbitrary")),
        name="na_attention",
    )(jnp.asarray(band0), jnp.asarray(types), nq, nk, nvt, cnk, cnvt, bias_tiles)


def _split2(a):
    hi = a.astype(BF16)
    return hi, (a - hi.astype(F32)).astype(BF16)


def _merge_kernel(ym_ref, yn_ref, g_ref, x_ref, g1_ref, sc2_ref, sh2_ref, wpm_ref, wpn_ref, wo_ref,
                  l1g_ref, l1b_ref, wr_ref, xn_ref, up_ref, aff_ref):
    D = x_ref.shape[2]
    g = g_ref[0]
    a = g[:, :D].astype(F32) * _dot(ym_ref[0], wpm_ref[...]) + g[:, D:].astype(F32) * _dot(yn_ref[0], wpn_ref[...])
    mix = _dot(a.astype(BF16), wo_ref[...])
    xn = _ln(ALPHA * x_ref[0] + g1_ref[0] * mix) * l1g_ref[...] + l1b_ref[...]
    xn_ref[0] = xn
    u2 = _ln(xn) * (1.0 + sc2_ref[0]) + sh2_ref[0]

    up_ref[0] = u2

    uh, ul = _split2(u2)
    wh, wl = _split2(wr_ref[...])
    logits = _dot(uh, wh) + (_dot(uh, wl) + _dot(ul, wh))
    e = jnp.exp(logits - jnp.max(logits, axis=-1, keepdims=True))
    aff_ref[0] = e / jnp.sum(e, axis=-1, keepdims=True)


def _merge(y_mla, y_na, gates, x, g1, sc2, sh2, wpm, wpn, wo, l1g, l1b, wr, tm):
    B, n, D = x.shape
    E = wr.shape[1]
    full = lambda a: pl.BlockSpec(a.shape, lambda b, i: (0,) * a.ndim)
    tok = lambda w: pl.BlockSpec((1, tm, w), lambda b, i: (b, i, 0))
    mod = pl.BlockSpec((1, 1, D), lambda b, i: (b, 0, 0))
    return pl.pallas_call(
        _merge_kernel,
        out_shape=[jax.ShapeDtypeStruct((B, n, D), F32),
                   jax.ShapeDtypeStruct((B, n, D), F32),
                   jax.ShapeDtypeStruct((B, n, E), F32)],
        grid=(B, n // tm),
        in_specs=[tok(y_mla.shape[2]), tok(y_na.shape[2]), tok(2 * D), tok(D), mod, mod, mod,
                  full(wpm), full(wpn), full(wo), full(l1g), full(l1b), full(wr)],
        out_specs=[tok(D), tok(D), tok(E)],
        compiler_params=_cparams(("parallel", "parallel")),
        name="merge_ln_router",
    )(y_mla, y_na, gates, x, g1, sc2, sh2, wpm, wpn, wo, l1g, l1b, wr)


def _topk_kernel(cap, aff_ref, tri_ref, idx_ref, g_ref, pos_ref):
    aff = aff_ref[0]
    E, n = aff.shape

    def count(mask):
        return jnp.sum(mask.astype(F32), axis=-1, keepdims=True)

    def search(t, thr):
        cand = thr | (jnp.int32(1) << (30 - t))
        return jnp.where(count(aff >= pltpu.bitcast(cand, F32)) >= cap, cand, thr)

    thr = lax.fori_loop(0, 31, search, jnp.zeros((E, 1), jnp.int32))
    lo = pltpu.bitcast(thr, F32)
    hi = pltpu.bitcast(thr + 1, F32)

    def refine(t, lh):
        lo, hi = lh
        mid = lo + (hi - lo) * 0.5
        ok = count(aff >= mid) >= cap
        return jnp.where(ok, mid, lo), jnp.where(ok, hi, mid)

    lo, hi = lax.fori_loop(0, TOPK_REFINE, refine, (lo, hi))
    gt = aff >= hi
    eq = (aff >= lo) & ~gt
    need = cap - count(gt)

    tri = tri_ref[...]

    def prefix(mask):
        mf = mask.astype(F32).astype(BF16)
        parts = []
        off = jnp.zeros((E, 1), F32)
        for c in range(n // LANES):
            blk = mf[:, c * LANES:(c + 1) * LANES]
            parts.append(_dot(blk, tri) + off)
            off = off + jnp.sum(blk.astype(F32), axis=-1, keepdims=True)
        return jnp.concatenate(parts, axis=-1)

    sel = gt | (eq & (prefix(eq) < need))
    pos_ref[...] = jnp.where(sel, prefix(sel), -1.0)

    chunks = cap // TOPK_SLOTS

    def emit(t, carry):
        e = t // chunks
        base = pl.multiple_of((t % chunks) * TOPK_SLOTS, TOPK_SLOTS)
        slot = (lax.broadcasted_iota(jnp.int32, (TOPK_SLOTS, n), 0) + base).astype(F32)
        tok = lax.broadcasted_iota(jnp.int32, (TOPK_SLOTS, n), 1).astype(F32)
        hit = slot == pos_ref[pl.ds(e, 1), :]
        idx = jnp.sum(jnp.where(hit, tok, 0.0), axis=-1, keepdims=True)
        idx_ref[0, e, pl.ds(base, TOPK_SLOTS), :] = idx.astype(jnp.int32)
        g_ref[0, e, pl.ds(base, TOPK_SLOTS), :] = jnp.sum(
            jnp.where(hit, aff_ref[0, pl.ds(e, 1), :], 0.0), axis=-1, keepdims=True)
        return carry

    lax.fori_loop(0, E * chunks, emit, 0)


def _topk(aff_t, cap):
    B, E, n = aff_t.shape
    tri = jnp.asarray(np.triu(np.ones((LANES, LANES), np.float32), k=1), BF16)
    return pl.pallas_call(
        functools.partial(_topk_kernel, cap),
        out_shape=[jax.ShapeDtypeStruct((B, E, cap, 1), jnp.int32),
                   jax.ShapeDtypeStruct((B, E, cap, 1), F32)],
        grid=(B,),
        in_specs=[pl.BlockSpec((1, E, n), lambda b: (b, 0, 0)),
                  pl.BlockSpec((LANES, LANES), lambda b: (0, 0))],
        out_specs=[pl.BlockSpec((1, E, cap, 1), lambda b: (b, 0, 0, 0)),
                   pl.BlockSpec((1, E, cap, 1), lambda b: (b, 0, 0, 0))],
        scratch_shapes=[pltpu.VMEM((E, n), F32)],
        compiler_params=_cparams(("parallel",)),
        name="expert_topk",
    )(aff_t, tri)


def _gather_kernel(idx_ref, u_ref, o_ref):
    cap = o_ref.shape[2]

    def body(c, carry):
        r = idx_ref[0, 0, 0, c]
        o_ref[0, 0, pl.ds(c, 1), :] = u_ref[0, pl.ds(r, 1), :]
        return carry

    lax.fori_loop(0, cap, body, 0, unroll=8)


def _gather(idx, u2):
    B, E, _, cap = idx.shape
    _, n, W = u2.shape
    return pl.pallas_call(
        _gather_kernel,
        out_shape=jax.ShapeDtypeStruct((B, E, cap, W), u2.dtype),
        grid=(B, E),
        in_specs=[pl.BlockSpec((1, 1, 1, cap), lambda b, e: (b, e, 0, 0), memory_space=pltpu.SMEM),
                  pl.BlockSpec((1, n, W), lambda b, e: (b, 0, 0))],
        out_specs=pl.BlockSpec((1, 1, cap, W), lambda b, e: (b, e, 0, 0)),
        compiler_params=_cparams(("parallel", "arbitrary")),
        name="expert_gather",
    )(idx, u2)


def _ffn_kernel(xe_ref, g_ref, wg_ref, wu_ref, wd_ref, o_ref, wgb_ref, wub_ref, wdb_ref):
    @pl.when(pl.program_id(1) == 0)
    def _():
        wgb_ref[...] = wg_ref[0].astype(BF16)
        wub_ref[...] = wu_ref[0].astype(BF16)
        wdb_ref[...] = wd_ref[0].astype(BF16)

    xe = xe_ref[0, 0].astype(BF16)
    gate = _dot(xe, wgb_ref[...])
    up = _dot(xe, wub_ref[...])
    h = (gate * jax.nn.sigmoid(gate) * up).astype(BF16)
    o_ref[0, 0] = _dot(h, wdb_ref[...]) * g_ref[0, 0]


def _ffn(xe, g, w_gate, w_up, w_down):
    B, E, cap, W = xe.shape
    _, D, F = w_gate.shape
    return pl.pallas_call(
        _ffn_kernel,
        out_shape=jax.ShapeDtypeStruct((B, E, cap, D), F32),
        grid=(E, B),
        in_specs=[pl.BlockSpec((1, 1, cap, W), lambda e, b: (b, e, 0, 0)),
                  pl.BlockSpec((1, 1, cap, 1), lambda e, b: (b, e, 0, 0)),
                  pl.BlockSpec((1, D, F), lambda e, b: (e, 0, 0)),
                  pl.BlockSpec((1, D, F), lambda e, b: (e, 0, 0)),
                  pl.BlockSpec((1, F, D), lambda e, b: (e, 0, 0))],
        out_specs=pl.BlockSpec((1, 1, cap, D), lambda e, b: (b, e, 0, 0)),
        scratch_shapes=[pltpu.VMEM((D, F), BF16), pltpu.VMEM((D, F), BF16), pltpu.VMEM((F, D), BF16)],
        compiler_params=_cparams(("arbitrary", "arbitrary")),
        name="expert_ffn",
    )(xe, g, w_gate, w_up, w_down)


def _scatter_kernel(idx_ref, y_ref, o_ref):
    cap = y_ref.shape[2]

    @pl.when(pl.program_id(2) == 0)
    def _():
        o_ref[...] = jnp.zeros_like(o_ref)

    def body(c, carry):
        r = idx_ref[0, 0, 0, c]
        o_ref[0, pl.ds(r, 1), :] = o_ref[0, pl.ds(r, 1), :] + y_ref[0, 0, pl.ds(c, 1), :]
        return carry

    lax.fori_loop(0, cap, body, 0, unroll=8)


def _scatter(idx, ye, n):
    B, E, cap, D = ye.shape
    halves = 2
    dw = D // halves
    return pl.pallas_call(
        _scatter_kernel,
        out_shape=jax.ShapeDtypeStruct((B, n, D), F32),
        grid=(B, halves, E),
        in_specs=[pl.BlockSpec((1, 1, 1, cap), lambda b, s, e: (b, e, 0, 0), memory_space=pltpu.SMEM),
                  pl.BlockSpec((1, 1, cap, dw), lambda b, s, e: (b, e, 0, s))],
        out_specs=pl.BlockSpec((1, n, dw), lambda b, s, e: (b, 0, s)),
        compiler_params=_cparams(("parallel", "parallel", "arbitrary")),
        name="expert_scatter",
    )(idx, ye)


def _final_kernel(x_ref, moe_ref, g2_ref, lg_ref, lb_ref, o_ref):
    o_ref[0] = _ln(ALPHA * x_ref[0] + g2_ref[0] * moe_ref[0]) * lg_ref[...] + lb_ref[...]


def _final(xn, moe, g2, lg, lb, tm):
    B, n, D = xn.shape
    tok = pl.BlockSpec((1, tm, D), lambda b, i: (b, i, 0))
    vec = pl.BlockSpec((1, D), lambda b, i: (0, 0))
    return pl.pallas_call(
        _final_kernel,
        out_shape=jax.ShapeDtypeStruct((B, n, D), F32),
        grid=(B, n // tm),
        in_specs=[tok, tok, pl.BlockSpec((1, 1, D), lambda b, i: (b, 0, 0)), vec, vec],
        out_specs=tok,
        compiler_params=_cparams(("parallel", "parallel")),
        name="final_ln",
    )(xn, moe, g2, lg, lb)


def _rope_tables(n):
    t = np.arange(n)
    row = (t // GRID_W).astype(np.float32)
    col = (t % GRID_W).astype(np.float32)
    per_axis = MLA_ROPE // 2
    inv_freq = jnp.asarray(ROPE_THETA, F32) ** (-jnp.arange(0, per_axis, 2, dtype=F32) / per_axis)
    ang = jnp.concatenate([jnp.asarray(row)[:, None] * inv_freq, jnp.asarray(col)[:, None] * inv_freq], axis=-1)
    cos, sin = jnp.cos(ang), jnp.sin(ang)
    pad = HEAD_PAD - MLA_QK
    c_tab = jnp.concatenate([jnp.ones((n, MLA_NOPE), F32), cos, cos, jnp.zeros((n, pad), F32)], axis=-1)
    s_tab = jnp.concatenate([jnp.zeros((n, MLA_NOPE), F32), -sin, sin, jnp.zeros((n, pad), F32)], axis=-1)
    return c_tab, s_tab


def _swap_halves(w):
    half = w.shape[-1] // 2
    return jnp.concatenate([w[..., half:], w[..., :half]], axis=-1)


def _prep_weights(w_in, w_uq, w_ukv):
    D = w_in.shape[0]
    o1 = Q_LORA
    o2 = o1 + KV_LORA
    o3 = o2 + MLA_ROPE
    o4 = o3 + 3 * NA_W
    w_kr = w_in[:, o2:o3]
    z = lambda k: jnp.zeros((D, k), w_in.dtype)
    pad = HEAD_PAD - MLA_QK
    kra = jnp.concatenate([z(MLA_NOPE), w_kr, z(pad)], axis=-1)
    krb = jnp.concatenate([z(MLA_NOPE), _swap_halves(w_kr), z(pad)], axis=-1)
    w_small = jnp.concatenate([w_in[:, :o2], kra, krb], axis=-1).astype(BF16)
    w_na = w_in[:, o3:o3 + 2 * NA_W].astype(BF16)
    w_nvt = w_in[:, o3 + 2 * NA_W:o4].T.astype(BF16)
    w_g = w_in[:, o4:].astype(BF16)

    uq = w_uq.reshape(Q_LORA, MLA_HEADS, MLA_QK)
    zq = lambda k: jnp.zeros((Q_LORA, MLA_HEADS, k), w_uq.dtype)
    wqa = jnp.concatenate([uq, zq(pad)], axis=-1).reshape(Q_LORA, MLA_HEADS * HEAD_PAD).astype(BF16)
    wqb = jnp.concatenate([zq(MLA_NOPE), _swap_halves(uq[..., MLA_NOPE:]), zq(pad)], axis=-1)
    wqb = wqb.reshape(Q_LORA, MLA_HEADS * HEAD_PAD).astype(BF16)

    ukv = w_ukv.reshape(KV_LORA, MLA_HEADS, MLA_NOPE + MLA_V)
    wk = jnp.concatenate([ukv[..., :MLA_NOPE], jnp.zeros((KV_LORA, MLA_HEADS, HEAD_PAD - MLA_NOPE), w_ukv.dtype)], axis=-1)
    wk = wk.reshape(KV_LORA, MLA_HEADS * HEAD_PAD).astype(BF16)
    wv = ukv[..., MLA_NOPE:].reshape(KV_LORA, MLA_HEADS * MLA_V).T.astype(BF16)
    return w_small, w_na, w_nvt, w_g, wqa, wqb, wk, wv


def _tile(n, pref):
    t = min(pref, n)
    while n % t:
        t //= 2
    return t


def kernel(x, c, ctx, c_ctx, w_mod, b_mod, w_in, q_norm_g, w_uq, kv_norm_g, w_ukv, na_rel_bias, w_proj_mla,
           w_proj_na, w_out, ln1_g, ln1_b, w_router, w_exp_gate, w_exp_up, w_exp_down, ln2_g, ln2_b):
    B, n, D = x.shape
    L = ctx.shape[1]
    rows = n // GRID_W
    assert n % (GRID_W * NA_QROWS) == 0 and rows >= NA_BAND
    assert w_mod.shape[0] == DEPTH
    cap = EC_CAPACITY * n // N_EXPERTS

    mod_rows = -(-(B + 1) // 8) * 8
    cc = jnp.concatenate([c, c_ctx[None], jnp.zeros((mod_rows - B - 1, D), F32)], axis=0)
    m = _modulation(cc, w_mod[0], b_mod[0])
    sh1, sc1, g1, sh2, sc2, g2 = [m[:B, k * D:(k + 1) * D].reshape(B, 1, D) for k in range(6)]
    csh1 = m[B:B + 1, :D].reshape(1, 1, D)
    csc1 = m[B:B + 1, D:2 * D].reshape(1, 1, D)

    w_small, w_na, w_nvt, w_g, wqa, wqb, wk, wv = _prep_weights(w_in[0], w_uq[0], w_ukv[0])
    qg = q_norm_g[0].reshape(1, Q_LORA)
    kvg = kv_norm_g[0].reshape(1, KV_LORA)
    c_tab, s_tab = _rope_tables(n)
    pad = HEAD_PAD - MLA_QK
    c_ctx_tab = jnp.concatenate([jnp.ones((L, MLA_QK), F32), jnp.zeros((L, pad), F32)], axis=-1)
    s_ctx_tab = jnp.zeros((L, HEAD_PAD), F32)

    tm = _tile(n, 256)
    q, k_lat, v_lat, nq, nk, nv, gates = _inproj(
        True, x, sc1, sh1, c_tab, s_tab, c_tab * (MLA_SCALE * LOG2E), s_tab * (MLA_SCALE * LOG2E),
        w_small, w_na, w_nvt, w_g, qg, kvg, wqa, wqb, wk, wv, tm)
    k_ctx, v_ctx, cnk, cnv = _inproj(
        False, ctx, csc1, csh1, c_ctx_tab, s_ctx_tab, c_ctx_tab, s_ctx_tab,
        w_small, w_na[:, NA_W:], w_nvt, w_g, qg, kvg, wqa, wqb, wk, wv, _tile(L, 256))

    y_mla = _mla_attention(q, k_lat, k_ctx, v_lat, v_ctx, _tile(n, 512), _tile(n, 512))
    y_na = _na_attention(nq, nk, nv, cnk, cnv, _na_bias_tiles(na_rel_bias[0], rows))

    xn, u2, aff = _merge(
        y_mla, y_na, gates, x, g1, sc2, sh2,
        w_proj_mla[0].astype(BF16), w_proj_na[0].astype(BF16), w_out[0].astype(BF16),
        ln1_g[0].reshape(1, D), ln1_b[0].reshape(1, D), w_router[0], tm)

    idx4, g4 = _topk(jnp.swapaxes(aff, 1, 2), cap)
    idx = idx4.reshape(B, N_EXPERTS, 1, cap)
    xe = _gather(idx, u2)
    ye = _ffn(xe, g4, w_exp_gate[0], w_exp_up[0], w_exp_down[0])
    moe = _scatter(idx, ye, n)
    return _final(xn, moe, g2, ln2_g[0].reshape(1, D), ln2_b[0].reshape(1, D), tm)
```

```python
import functools
import math

import numpy as np
import jax
import jax.numpy as jnp
from jax import lax
from jax.experimental import pallas as pl
from jax.experimental.pallas import tpu as pltpu

GRID_W = 64
MLA_HEADS = 8
MLA_NOPE = 64
MLA_ROPE = 32
MLA_QK = MLA_NOPE + MLA_ROPE
MLA_V = 64
Q_LORA = 256
KV_LORA = 128
MLA_SCALE = MLA_QK ** -0.5
ROPE_THETA = 10000.0
NA_HEADS = 8
NA_DIM = 64
NA_W = NA_HEADS * NA_DIM
NA_WIN_H = 8
NA_WIN_W = 16
NA_SCALE = NA_DIM ** -0.5
N_EXPERTS = 16
EC_CAPACITY = 2
LN_EPS = 1e-5
RMS_EPS = 1e-6
DEPTH = 1
ALPHA = (2.0 * DEPTH) ** 0.25
LOG2E = math.log2(math.e)

LANES = 128
HEAD_PAD = LANES
VMEM_LIMIT = 56 * 1024 * 1024
NEG_BIG = -1e30

NA_QROWS = 4
NA_BAND = 12
ROW_GROUP = 8
TOPK_SLOTS = 64
TOPK_REFINE = 30

BF16 = jnp.bfloat16
F32 = jnp.float32


def _cparams(sem):
    return pltpu.CompilerParams(dimension_semantics=sem, vmem_limit_bytes=VMEM_LIMIT)


def _ln(x):
    mu = jnp.mean(x, axis=-1, keepdims=True)
    xc = x - mu
    var = jnp.mean(xc * xc, axis=-1, keepdims=True)
    return xc * lax.rsqrt(var + LN_EPS)


def _dot(a, b):
    return jnp.dot(a, b, preferred_element_type=F32)


def _dot_nt(a, b):
    return lax.dot_general(a, b, (((1,), (1,)), ((), ())), preferred_element_type=F32)


def _store_rows_as_tiles(ref, val):
    for j in range(val.shape[1] // LANES):
        ref[:, j, :] = val[:, j * LANES:(j + 1) * LANES]


def _load_tiles_as_rows(ref):
    return jnp.concatenate([ref[:, j, :] for j in range(ref.shape[1])], axis=-1)


def _mod_kernel(c_ref, w_ref, b_ref, o_ref):
    c = c_ref[...]
    s = c * jax.nn.sigmoid(c)
    o_ref[...] = jnp.dot(s, w_ref[...], preferred_element_type=F32,
                         precision=lax.Precision.HIGHEST) + b_ref[...]


def _modulation(cc, w_mod, b_mod):
    rows, d = cc.shape
    n_out = w_mod.shape[1]
    tn = 1024
    return pl.pallas_call(
        _mod_kernel,
        out_shape=jax.ShapeDtypeStruct((rows, n_out), F32),
        grid=(n_out // tn,),
        in_specs=[pl.BlockSpec((rows, d), lambda j: (0, 0)),
                  pl.BlockSpec((d, tn), lambda j: (0, j)),
                  pl.BlockSpec((1, tn), lambda j: (0, j))],
        out_specs=pl.BlockSpec((rows, tn), lambda j: (0, j)),
        compiler_params=_cparams(("arbitrary",)),
        name="modulation",
    )(cc, w_mod, b_mod.reshape(1, n_out))


def _inproj_kernel(latent, x_ref, sc_ref, sh_ref, ck_ref, sk_ref, cq_ref, sq_ref,
                   w_small_ref, w_na_ref, w_nvt_ref, w_g_ref, qg_ref, kvg_ref, wqa_ref, wqb_ref, wk_ref, wv_ref,
                   *out_refs):
    if latent:
        q_ref, k_ref, v_ref, nq_ref, nk_ref, nv_ref, g_ref = out_refs
    else:
        k_ref, v_ref, nk_ref, nv_ref = out_refs
    x = x_ref[0]
    u = (_ln(x) * (1.0 + sc_ref[0]) + sh_ref[0]).astype(BF16)

    small = _dot(u, w_small_ref[...])
    q_c = small[:, :Q_LORA]
    kv_c = small[:, Q_LORA:Q_LORA + KV_LORA]
    kra = small[:, Q_LORA + KV_LORA:Q_LORA + KV_LORA + LANES]
    krb = small[:, Q_LORA + KV_LORA + LANES:]

    kvn = (kv_c * lax.rsqrt(jnp.mean(kv_c * kv_c, axis=-1, keepdims=True) + RMS_EPS) * kvg_ref[...]).astype(BF16)
    kk = _dot(kvn, wk_ref[...])
    kr = kra * ck_ref[...] + krb * sk_ref[...]
    for h in range(MLA_HEADS):
        k_ref[0, h] = (kk[:, h * HEAD_PAD:(h + 1) * HEAD_PAD] + kr).astype(BF16)
    v_ref[0] = _dot_nt(wv_ref[...], kvn).astype(BF16)

    na = _dot(u, w_na_ref[...])
    nv_ref[0] = _dot_nt(w_nvt_ref[...], u).astype(BF16)
    if latent:
        qn = (q_c * lax.rsqrt(jnp.mean(q_c * q_c, axis=-1, keepdims=True) + RMS_EPS) * qg_ref[...]).astype(BF16)
        qa = _dot(qn, wqa_ref[...])
        qb = _dot(qn, wqb_ref[...])
        cq = cq_ref[...]
        sq = sq_ref[...]
        for h in range(MLA_HEADS):
            sl = slice(h * HEAD_PAD, (h + 1) * HEAD_PAD)
            q_ref[0, h] = (qa[:, sl] * cq + qb[:, sl] * sq).astype(BF16)
        nq_ref[0] = (na[:, :NA_W] * (NA_SCALE * LOG2E)).astype(BF16)
        nk_ref[0] = na[:, NA_W:].astype(BF16)
        g_ref[0] = jax.nn.sigmoid(_dot(u, w_g_ref[...])).astype(BF16)
    else:
        nk_ref[0] = na.astype(BF16)


def _inproj(latent, x, sc, sh, ck, sk, cq, sq, w_small, w_na, w_nvt, w_g, qg, kvg, wqa, wqb, wk, wv, tm):
    B, n, D = x.shape
    per_batch = sc.shape[0] > 1
    mod_map = (lambda b, i: (b, 0, 0)) if per_batch else (lambda b, i: (0, 0, 0))
    full = lambda a: pl.BlockSpec(a.shape, lambda b, i: (0,) * a.ndim)
    tab = lambda a: pl.BlockSpec((tm, a.shape[1]), lambda b, i: (i, 0))
    in_specs = [pl.BlockSpec((1, tm, D), lambda b, i: (b, i, 0)),
                pl.BlockSpec((1, 1, D), mod_map), pl.BlockSpec((1, 1, D), mod_map),
                tab(ck), tab(sk), tab(cq), tab(sq),
                full(w_small), full(w_na), full(w_nvt), full(w_g), full(qg), full(kvg),
                full(wqa), full(wqb), full(wk), full(wv)]
    hk = jax.ShapeDtypeStruct((B, MLA_HEADS, n, HEAD_PAD), BF16)
    hk_spec = pl.BlockSpec((1, MLA_HEADS, tm, HEAD_PAD), lambda b, i: (b, 0, i, 0))
    tok = lambda w: jax.ShapeDtypeStruct((B, n, w), BF16)
    tok_spec = lambda w: pl.BlockSpec((1, tm, w), lambda b, i: (b, i, 0))
    vt = jax.ShapeDtypeStruct((B, MLA_HEADS * MLA_V, n), BF16)
    vt_spec = pl.BlockSpec((1, MLA_HEADS * MLA_V, tm), lambda b, i: (b, 0, i))
    if latent:
        out_shape = [hk, hk, vt, tok(NA_W), tok(NA_W), vt, tok(2 * D)]
        out_specs = [hk_spec, hk_spec, vt_spec, tok_spec(NA_W), tok_spec(NA_W), vt_spec, tok_spec(2 * D)]
    else:
        out_shape = [hk, vt, tok(NA_W), vt]
        out_specs = [hk_spec, vt_spec, tok_spec(NA_W), vt_spec]
    return pl.pallas_call(
        functools.partial(_inproj_kernel, latent),
        out_shape=out_shape,
        grid=(B, n // tm),
        in_specs=in_specs,
        out_specs=out_specs,
        compiler_params=_cparams(("parallel", "parallel")),
        name="inproj_latent" if latent else "inproj_ctx",
    )(x, sc, sh, ck, sk, cq, sq, w_small, w_na, w_nvt, w_g, qg, kvg, wqa, wqb, wk, wv)


def _mla_kernel(tk, q_ref, kc_ref, kl_ref, vc_ref, vl_ref, o_ref, m_ref, l_ref, acc_ref, sa_ref, sb_ref):
    n = kl_ref.shape[2]
    tq = q_ref.shape[2]
    nt = n // tk

    def scores(hh, k):
        return _dot_nt(k, q_ref[0, hh])

    def absorb(hh, s, v_t):
        m_old = m_ref[hh]
        m_new = jnp.maximum(m_old, jnp.max(s, axis=0, keepdims=True))
        a = jnp.exp2(m_old - m_new)
        p = jnp.exp2(s - m_new)
        l_ref[hh] = a * l_ref[hh] + jnp.sum(p, axis=0, keepdims=True)
        acc_ref[hh] = a * acc_ref[hh] + _dot(v_t, p.astype(BF16))
        m_ref[hh] = m_new

    def k_tile(hh, j):
        return kl_ref[0, hh, pl.ds(pl.multiple_of(j * tk, tk), tk), :]

    def v_tile(j):
        return vl_ref[0, :, pl.ds(pl.multiple_of(j * tk, tk), tk)]

    m_ref[...] = jnp.full(m_ref.shape, NEG_BIG, F32)
    l_ref[...] = jnp.zeros(l_ref.shape, F32)
    acc_ref[...] = jnp.zeros(acc_ref.shape, F32)
    s_ctx = [scores(hh, kc_ref[0, hh]) for hh in range(2)]
    for hh in range(2):
        sa_ref[hh] = scores(hh, k_tile(hh, 0))
        absorb(hh, s_ctx[hh], vc_ref[0])

    def advance(j, cur_ref, next_ref):
        v_t = v_tile(j)
        for hh in range(2):
            next_ref[hh] = scores(hh, k_tile(hh, j + 1))
            absorb(hh, cur_ref[hh], v_t)

    def body(jj, carry):
        advance(2 * jj, sa_ref, sb_ref)
        advance(2 * jj + 1, sb_ref, sa_ref)
        return carry

    lax.fori_loop(0, (nt - 1) // 2, body, 0)
    last_ref = sa_ref
    if (nt - 1) % 2:
        advance(nt - 2, sa_ref, sb_ref)
        last_ref = sb_ref
    v_t = v_tile(nt - 1)
    for hh in range(2):
        absorb(hh, last_ref[hh], v_t)
    row = lax.broadcasted_iota(jnp.int32, (LANES, tq), 0)
    out_t = jnp.where(row < MLA_V, acc_ref[0] / l_ref[0], acc_ref[1] / l_ref[1])
    o_ref[0] = out_t.T.astype(o_ref.dtype)


def _mla_attention(q, k_lat, k_ctx, v_lat, v_ctx, tq, tk):
    B, H, n, _ = q.shape
    L = k_ctx.shape[2]
    return pl.pallas_call(
        functools.partial(_mla_kernel, tk),
        out_shape=jax.ShapeDtypeStruct((B, n, H * MLA_V), BF16),
        grid=(B, H // 2, n // tq),
        in_specs=[pl.BlockSpec((1, 2, tq, HEAD_PAD), lambda b, j, i: (b, j, i, 0)),
                  pl.BlockSpec((1, 2, L, HEAD_PAD), lambda b, j, i: (b, j, 0, 0)),
                  pl.BlockSpec((1, 2, n, HEAD_PAD), lambda b, j, i: (b, j, 0, 0)),
                  pl.BlockSpec((1, LANES, L), lambda b, j, i: (b, j, 0)),
                  pl.BlockSpec((1, LANES, n), lambda b, j, i: (b, j, 0))],
        out_specs=pl.BlockSpec((1, tq, LANES), lambda b, j, i: (b, i, j)),
        scratch_shapes=[pltpu.VMEM((2, 1, tq), F32), pltpu.VMEM((2, 1, tq), F32),
                        pltpu.VMEM((2, LANES, tq), F32),
                        pltpu.VMEM((2, tk, tq), F32), pltpu.VMEM((2, tk, tq), F32)],
        compiler_params=_cparams(("parallel", "parallel", "parallel")),
        name="mla_attention",
    )(q, k_ctx, k_lat, v_ctx, v_lat)


def _na_block_tables(rows):
    nblk = rows // NA_QROWS
    wh = min(NA_WIN_H, rows)
    band0 = np.clip(np.arange(nblk) * NA_QROWS - wh // 2, 0, rows - NA_BAND)
    sigs, types = [], []
    for i in range(nblk):
        r = i * NA_QROWS + np.arange(NA_QROWS)
        r0 = np.clip(r - wh // 2, 0, rows - wh)
        sig = (tuple(r0 - band0[i]), tuple(r - band0[i]))
        if sig not in sigs:
            sigs.append(sig)
        types.append(sigs.index(sig))
    return band0.astype(np.int32), np.asarray(types, np.int32), sigs, wh


def _na_bias_tiles(rel_bias, rows):
    _, _, sigs, wh = _na_block_tables(rows)
    ww = NA_WIN_W
    n_dr, n_dc = 2 * NA_WIN_H - 1, 2 * NA_WIN_W - 1
    col = np.arange(GRID_W)
    c0 = np.clip(col - ww // 2, 0, GRID_W - ww)
    col_ok = (col[None, :] >= c0[:, None]) & (col[None, :] < c0[:, None] + ww)
    dc = col[None, :] - col[:, None] + (NA_WIN_W - 1)
    pick_c = ((dc[None] == np.arange(n_dc)[:, None, None]) & col_ok[None]).astype(np.float32)
    kr = np.arange(NA_BAND)
    row_ok = np.stack([(kr[None, :] >= np.asarray(r0)[:, None]) & (kr[None, :] < np.asarray(r0)[:, None] + wh)
                       for r0, _ in sigs])
    dr = np.stack([kr[None, :] - np.asarray(r)[:, None] + (NA_WIN_H - 1) for _, r in sigs])
    pick_r = ((dr[..., None] == np.arange(n_dr)) & row_ok[..., None]).astype(np.float32)
    hp = lax.Precision.HIGHEST
    by_col = jnp.einsum('hdj,jxy->hdxy', rel_bias.astype(F32), jnp.asarray(pick_c), precision=hp)
    vals = jnp.einsum('tqkd,hdxy->thkyqx', jnp.asarray(pick_r), by_col, precision=hp)
    ok = np.transpose(row_ok, (0, 2, 1))[:, None, :, None, :, None] & col_ok.T[None, None, None, :, None, :]
    tiles = jnp.where(jnp.asarray(ok), vals * LOG2E, NEG_BIG)
    return tiles.reshape(len(sigs), NA_HEADS, NA_BAND * GRID_W, NA_QROWS * GRID_W)


def _na_kernel(band_ref, type_ref, q_ref, k_ref, vt_ref, kc_ref, vct_ref, bias_ref, o_ref):
    i = pl.program_id(1)
    nq = NA_QROWS * GRID_W
    nk = NA_BAND * GRID_W
    start = pl.multiple_of(band_ref[i] * GRID_W, NA_QROWS * GRID_W)
    lane = lax.broadcasted_iota(jnp.int32, (nq, LANES), 1)
    row = lax.broadcasted_iota(jnp.int32, (LANES, nq), 0)

    def scores(h):
        cs = slice((h // 2) * LANES, (h // 2 + 1) * LANES)
        qp = q_ref[0, :, cs]
        own = (lane < NA_DIM) if h % 2 == 0 else (lane >= NA_DIM)
        qh = jnp.where(own, qp, jnp.zeros_like(qp))
        s_loc = _dot_nt(k_ref[0, pl.ds(start, nk), cs], qh) + bias_ref[0, h]
        s_ctx = _dot_nt(kc_ref[0, :, cs], qh)
        return s_loc, s_ctx

    def attend(h, s):
        s_loc, s_ctx = s
        cs = slice((h // 2) * LANES, (h // 2 + 1) * LANES)
        m = jnp.maximum(jnp.max(s_loc, axis=0, keepdims=True), jnp.max(s_ctx, axis=0, keepdims=True))
        p_loc = jnp.exp2(s_loc - m)
        p_ctx = jnp.exp2(s_ctx - m)
        l = jnp.sum(p_loc, axis=0, keepdims=True) + jnp.sum(p_ctx, axis=0, keepdims=True)
        o = _dot(vt_ref[0, cs, pl.ds(start, nk)], p_loc.astype(BF16)) + _dot(vct_ref[0, cs, :], p_ctx.astype(BF16))
        return o / l

    s = scores(0)
    outs = []
    for h in range(NA_HEADS):
        s_next = scores(h + 1) if h + 1 < NA_HEADS else None
        outs.append(attend(h, s))
        s = s_next
        if h % 2:
            pair_t = jnp.where(row < NA_DIM, outs[h - 1], outs[h])
            o_ref[0, :, (h // 2) * LANES:(h // 2 + 1) * LANES] = pair_t.T.astype(o_ref.dtype)


def _na_attention(nq, nk, nvt, cnk, cnvt, bias_tiles):
    B, n, W = nq.shape
    L = cnk.shape[1]
    rows = n // GRID_W
    band0, types, _, _ = _na_block_tables(rows)
    qn = NA_QROWS * GRID_W
    kn = NA_BAND * GRID_W
    grid_spec = pltpu.PrefetchScalarGridSpec(
        num_scalar_prefetch=2,
        grid=(B, rows // NA_QROWS),
        in_specs=[pl.BlockSpec((1, qn, W), lambda b, i, bd, ty: (b, i, 0)),
                  pl.BlockSpec((1, n, W), lambda b, i, bd, ty: (b, 0, 0)),
                  pl.BlockSpec((1, W, n), lambda b, i, bd, ty: (b, 0, 0)),
                  pl.BlockSpec((1, L, W), lambda b, i, bd, ty: (b, 0, 0)),
                  pl.BlockSpec((1, W, L), lambda b, i, bd, ty: (b, 0, 0)),
                  pl.BlockSpec((1, NA_HEADS, kn, qn), lambda b, i, bd, ty: (ty[i], 0, 0, 0))],
        out_specs=pl.BlockSpec((1, qn, W), lambda b, i, bd, ty: (b, i, 0)),
    )
    return pl.pallas_call(
        _na_kernel,
        out_shape=jax.ShapeDtypeStruct((B, n, W), BF16),
        grid_spec=grid_spec,
        compiler_params=_cparams(("parallel", "arbitrary")),
        name="na_attention",
    )(jnp.asarray(band0), jnp.asarray(types), nq, nk, nvt, cnk, cnvt, bias_tiles)


def _split2(a):
    hi = a.astype(BF16)
    return hi, (a - hi.astype(F32)).astype(BF16)


def _merge_kernel(ym_ref, yn_ref, g_ref, x_ref, g1_ref, sc2_ref, sh2_ref, wpm_ref, wpn_ref, wo_ref,
                  l1g_ref, l1b_ref, wr_ref, xn_ref, up_ref, aff_ref):
    D = x_ref.shape[2]
    g = g_ref[0]
    a = g[:, :D].astype(F32) * _dot(ym_ref[0], wpm_ref[...]) + g[:, D:].astype(F32) * _dot(yn_ref[0], wpn_ref[...])
    mix = _dot(a.astype(BF16), wo_ref[...])
    xn = _ln(ALPHA * x_ref[0] + g1_ref[0] * mix) * l1g_ref[...] + l1b_ref[...]
    xn_ref[0] = xn
    u2 = _ln(xn) * (1.0 + sc2_ref[0]) + sh2_ref[0]

    _store_rows_as_tiles(up_ref.at[0], u2)

    uh, ul = _split2(u2)
    wh, wl = _split2(wr_ref[...])
    logits = _dot(uh, wh) + (_dot(uh, wl) + _dot(ul, wh))
    e = jnp.exp(logits - jnp.max(logits, axis=-1, keepdims=True))
    aff_ref[0] = e / jnp.sum(e, axis=-1, keepdims=True)


def _merge(y_mla, y_na, gates, x, g1, sc2, sh2, wpm, wpn, wo, l1g, l1b, wr, tm):
    B, n, D = x.shape
    E = wr.shape[1]
    full = lambda a: pl.BlockSpec(a.shape, lambda b, i: (0,) * a.ndim)
    tok = lambda w: pl.BlockSpec((1, tm, w), lambda b, i: (b, i, 0))
    mod = pl.BlockSpec((1, 1, D), lambda b, i: (b, 0, 0))
    return pl.pallas_call(
        _merge_kernel,
        out_shape=[jax.ShapeDtypeStruct((B, n, D), F32),
                   jax.ShapeDtypeStruct((B, n, D // LANES, LANES), F32),
                   jax.ShapeDtypeStruct((B, n, E), F32)],
        grid=(B, n // tm),
        in_specs=[tok(y_mla.shape[2]), tok(y_na.shape[2]), tok(2 * D), tok(D), mod, mod, mod,
                  full(wpm), full(wpn), full(wo), full(l1g), full(l1b), full(wr)],
        out_specs=[tok(D), pl.BlockSpec((1, tm, D // LANES, LANES), lambda b, i: (b, i, 0, 0)), tok(E)],
        compiler_params=_cparams(("parallel", "parallel")),
        name="merge_ln_router",
    )(y_mla, y_na, gates, x, g1, sc2, sh2, wpm, wpn, wo, l1g, l1b, wr)


def _topk_kernel(cap, aff_ref, tri_ref, idx_ref, g_ref, pos_ref):
    aff = aff_ref[0]
    E, n = aff.shape

    def count(mask):
        return jnp.sum(mask.astype(F32), axis=-1, keepdims=True)

    def search(t, thr):
        cand = thr | (jnp.int32(1) << (30 - t))
        return jnp.where(count(aff >= pltpu.bitcast(cand, F32)) >= cap, cand, thr)

    thr = lax.fori_loop(0, 31, search, jnp.zeros((E, 1), jnp.int32))
    lo = pltpu.bitcast(thr, F32)
    hi = pltpu.bitcast(thr + 1, F32)

    def refine(t, lh):
        lo, hi = lh
        mid = lo + (hi - lo) * 0.5
        ok = count(aff >= mid) >= cap
        return jnp.where(ok, mid, lo), jnp.where(ok, hi, mid)

    lo, hi = lax.fori_loop(0, TOPK_REFINE, refine, (lo, hi))
    gt = aff >= hi
    eq = (aff >= lo) & ~gt
    need = cap - count(gt)

    tri = tri_ref[...]

    def prefix(mask):
        mf = mask.astype(F32).astype(BF16)
        parts = []
        off = jnp.zeros((E, 1), F32)
        for c in range(n // LANES):
            blk = mf[:, c * LANES:(c + 1) * LANES]
            parts.append(_dot(blk, tri) + off)
            off = off + jnp.sum(blk.astype(F32), axis=-1, keepdims=True)
        return jnp.concatenate(parts, axis=-1)

    sel = gt | (eq & (prefix(eq) < need))
    pos_ref[...] = jnp.where(sel, prefix(sel), -1.0)

    chunks = cap // TOPK_SLOTS

    def emit(t, carry):
        e = t // chunks
        base = pl.multiple_of((t % chunks) * TOPK_SLOTS, TOPK_SLOTS)
        slot = (lax.broadcasted_iota(jnp.int32, (TOPK_SLOTS, n), 0) + base).astype(F32)
        tok = lax.broadcasted_iota(jnp.int32, (TOPK_SLOTS, n), 1).astype(F32)
        hit = slot == pos_ref[pl.ds(e, 1), :]
        idx = jnp.sum(jnp.where(hit, tok, 0.0), axis=-1, keepdims=True)
        idx_ref[0, e, pl.ds(base, TOPK_SLOTS), :] = idx.astype(jnp.int32)
        g_ref[0, e, pl.ds(base, TOPK_SLOTS), :] = jnp.sum(
            jnp.where(hit, aff_ref[0, pl.ds(e, 1), :], 0.0), axis=-1, keepdims=True)
        return carry

    lax.fori_loop(0, E * chunks, emit, 0)


def _topk(aff_t, cap):
    B, E, n = aff_t.shape
    tri = jnp.asarray(np.triu(np.ones((LANES, LANES), np.float32), k=1), BF16)
    return pl.pallas_call(
        functools.partial(_topk_kernel, cap),
        out_shape=[jax.ShapeDtypeStruct((B, E, cap, 1), jnp.int32),
                   jax.ShapeDtypeStruct((B, E, cap, 1), F32)],
        grid=(B,),
        in_specs=[pl.BlockSpec((1, E, n), lambda b: (b, 0, 0)),
                  pl.BlockSpec((LANES, LANES), lambda b: (0, 0))],
        out_specs=[pl.BlockSpec((1, E, cap, 1), lambda b: (b, 0, 0, 0)),
                   pl.BlockSpec((1, E, cap, 1), lambda b: (b, 0, 0, 0))],
        scratch_shapes=[pltpu.VMEM((E, n), F32)],
        compiler_params=_cparams(("parallel",)),
        name="expert_topk",
    )(aff_t, tri)


def _gather_kernel(idx_ref, u_ref, o_ref):
    cap = o_ref.shape[2]

    def body(i, carry):
        c0 = pl.multiple_of(i * ROW_GROUP, ROW_GROUP)
        for k in range(ROW_GROUP):
            o_ref[0, 0, c0 + k] = u_ref[0, idx_ref[0, 0, 0, c0 + k]]
        return carry

    lax.fori_loop(0, cap // ROW_GROUP, body, 0)


def _gather(idx, u2):
    B, E, _, cap = idx.shape
    _, n, S, W = u2.shape
    return pl.pallas_call(
        _gather_kernel,
        out_shape=jax.ShapeDtypeStruct((B, E, cap, S, W), u2.dtype),
        grid=(B, E),
        in_specs=[pl.BlockSpec((1, 1, 1, cap), lambda b, e: (b, e, 0, 0), memory_space=pltpu.SMEM),
                  pl.BlockSpec((1, n, S, W), lambda b, e: (b, 0, 0, 0))],
        out_specs=pl.BlockSpec((1, 1, cap, S, W), lambda b, e: (b, e, 0, 0, 0)),
        compiler_params=_cparams(("parallel", "arbitrary")),
        name="expert_gather",
    )(idx, u2)


def _ffn_kernel(xe_ref, g_ref, wg_ref, wu_ref, wd_ref, o_ref, wgb_ref, wub_ref, wdb_ref):
    @pl.when(pl.program_id(1) == 0)
    def _():
        wgb_ref[...] = wg_ref[0].astype(BF16)
        wub_ref[...] = wu_ref[0].astype(BF16)
        wdb_ref[...] = wd_ref[0].astype(BF16)

    xe = _load_tiles_as_rows(xe_ref.at[0, 0]).astype(BF16)
    gate = _dot(xe, wgb_ref[...])
    up = _dot(xe, wub_ref[...])
    h = (gate * jax.nn.sigmoid(gate) * up).astype(BF16)
    _store_rows_as_tiles(o_ref.at[0, 0], _dot(h, wdb_ref[...]) * g_ref[0, 0])


def _ffn(xe, g, w_gate, w_up, w_down):
    B, E, cap, S, W = xe.shape
    _, D, F = w_gate.shape
    return pl.pallas_call(
        _ffn_kernel,
        out_shape=jax.ShapeDtypeStruct((B, E, cap, S, W), F32),
        grid=(E, B),
        in_specs=[pl.BlockSpec((1, 1, cap, S, W), lambda e, b: (b, e, 0, 0, 0)),
                  pl.BlockSpec((1, 1, cap, 1), lambda e, b: (b, e, 0, 0)),
                  pl.BlockSpec((1, D, F), lambda e, b: (e, 0, 0)),
                  pl.BlockSpec((1, D, F), lambda e, b: (e, 0, 0)),
                  pl.BlockSpec((1, F, D), lambda e, b: (e, 0, 0))],
        out_specs=pl.BlockSpec((1, 1, cap, S, W), lambda e, b: (b, e, 0, 0, 0)),
        scratch_shapes=[pltpu.VMEM((D, F), BF16), pltpu.VMEM((D, F), BF16), pltpu.VMEM((F, D), BF16)],
        compiler_params=_cparams(("arbitrary", "arbitrary")),
        name="expert_ffn",
    )(xe, g, w_gate, w_up, w_down)


def _scatter_kernel(idx_ref, y_ref, o_ref):
    cap = y_ref.shape[2]

    @pl.when(pl.program_id(1) == 0)
    def _():
        o_ref[...] = jnp.zeros_like(o_ref)

    def body(i, carry):
        c0 = pl.multiple_of(i * ROW_GROUP, ROW_GROUP)
        rows = [idx_ref[0, 0, 0, c0 + k] for k in range(ROW_GROUP)]
        acc = [o_ref[0, r] for r in rows]
        for k in range(ROW_GROUP):
            o_ref[0, rows[k]] = acc[k] + y_ref[0, 0, c0 + k]
        return carry

    lax.fori_loop(0, cap // ROW_GROUP, body, 0)


def _scatter(idx, ye, n):
    B, E, cap, S, W = ye.shape
    return pl.pallas_call(
        _scatter_kernel,
        out_shape=jax.ShapeDtypeStruct((B, n, S, W), F32),
        grid=(B, E),
        in_specs=[pl.BlockSpec((1, 1, 1, cap), lambda b, e: (b, e, 0, 0), memory_space=pltpu.SMEM),
                  pl.BlockSpec((1, 1, cap, S, W), lambda b, e: (b, e, 0, 0, 0))],
        out_specs=pl.BlockSpec((1, n, S, W), lambda b, e: (b, 0, 0, 0)),
        compiler_params=_cparams(("parallel", "arbitrary")),
        name="expert_scatter",
    )(idx, ye)


def _final_kernel(x_ref, moe_ref, g2_ref, lg_ref, lb_ref, o_ref):
    moe = _load_tiles_as_rows(moe_ref.at[0])
    o_ref[0] = _ln(ALPHA * x_ref[0] + g2_ref[0] * moe) * lg_ref[...] + lb_ref[...]


def _final(xn, moe, g2, lg, lb, tm):
    B, n, D = xn.shape
    tok = pl.BlockSpec((1, tm, D), lambda b, i: (b, i, 0))
    vec = pl.BlockSpec((1, D), lambda b, i: (0, 0))
    return pl.pallas_call(
        _final_kernel,
        out_shape=jax.ShapeDtypeStruct((B, n, D), F32),
        grid=(B, n // tm),
        in_specs=[tok, pl.BlockSpec((1, tm) + moe.shape[2:], lambda b, i: (b, i, 0, 0)),
                  pl.BlockSpec((1, 1, D), lambda b, i: (b, 0, 0)), vec, vec],
        out_specs=tok,
        compiler_params=_cparams(("parallel", "parallel")),
        name="final_ln",
    )(xn, moe, g2, lg, lb)


def _rope_tables(n):
    t = np.arange(n)
    row = (t // GRID_W).astype(np.float32)
    col = (t % GRID_W).astype(np.float32)
    per_axis = MLA_ROPE // 2
    inv_freq = jnp.asarray(ROPE_THETA, F32) ** (-jnp.arange(0, per_axis, 2, dtype=F32) / per_axis)
    ang = jnp.concatenate([jnp.asarray(row)[:, None] * inv_freq, jnp.asarray(col)[:, None] * inv_freq], axis=-1)
    cos, sin = jnp.cos(ang), jnp.sin(ang)
    pad = HEAD_PAD - MLA_QK
    c_tab = jnp.concatenate([jnp.ones((n, MLA_NOPE), F32), cos, cos, jnp.zeros((n, pad), F32)], axis=-1)
    s_tab = jnp.concatenate([jnp.zeros((n, MLA_NOPE), F32), -sin, sin, jnp.zeros((n, pad), F32)], axis=-1)
    return c_tab, s_tab


def _swap_halves(w):
    half = w.shape[-1] // 2
    return jnp.concatenate([w[..., half:], w[..., :half]], axis=-1)


def _prep_weights(w_in, w_uq, w_ukv):
    D = w_in.shape[0]
    o1 = Q_LORA
    o2 = o1 + KV_LORA
    o3 = o2 + MLA_ROPE
    o4 = o3 + 3 * NA_W
    w_kr = w_in[:, o2:o3]
    z = lambda k: jnp.zeros((D, k), w_in.dtype)
    pad = HEAD_PAD - MLA_QK
    kra = jnp.concatenate([z(MLA_NOPE), w_kr, z(pad)], axis=-1)
    krb = jnp.concatenate([z(MLA_NOPE), _swap_halves(w_kr), z(pad)], axis=-1)
    w_small = jnp.concatenate([w_in[:, :o2], kra, krb], axis=-1).astype(BF16)
    w_na = w_in[:, o3:o3 + 2 * NA_W].astype(BF16)
    w_nvt = w_in[:, o3 + 2 * NA_W:o4].T.astype(BF16)
    w_g = w_in[:, o4:].astype(BF16)

    uq = w_uq.reshape(Q_LORA, MLA_HEADS, MLA_QK)
    zq = lambda k: jnp.zeros((Q_LORA, MLA_HEADS, k), w_uq.dtype)
    wqa = jnp.concatenate([uq, zq(pad)], axis=-1).reshape(Q_LORA, MLA_HEADS * HEAD_PAD).astype(BF16)
    wqb = jnp.concatenate([zq(MLA_NOPE), _swap_halves(uq[..., MLA_NOPE:]), zq(pad)], axis=-1)
    wqb = wqb.reshape(Q_LORA, MLA_HEADS * HEAD_PAD).astype(BF16)

    ukv = w_ukv.reshape(KV_LORA, MLA_HEADS, MLA_NOPE + MLA_V)
    wk = jnp.concatenate([ukv[..., :MLA_NOPE], jnp.zeros((KV_LORA, MLA_HEADS, HEAD_PAD - MLA_NOPE), w_ukv.dtype)], axis=-1)
    wk = wk.reshape(KV_LORA, MLA_HEADS * HEAD_PAD).astype(BF16)
    wv = ukv[..., MLA_NOPE:].reshape(KV_LORA, MLA_HEADS * MLA_V).T.astype(BF16)
    return w_small, w_na, w_nvt, w_g, wqa, wqb, wk, wv


def _tile(n, pref):
    t = min(pref, n)
    while n % t:
        t //= 2
    return t


def kernel(x, c, ctx, c_ctx, w_mod, b_mod, w_in, q_norm_g, w_uq, kv_norm_g, w_ukv, na_rel_bias, w_proj_mla,
           w_proj_na, w_out, ln1_g, ln1_b, w_router, w_exp_gate, w_exp_up, w_exp_down, ln2_g, ln2_b):
    B, n, D = x.shape
    L = ctx.shape[1]
    rows = n // GRID_W
    assert n % (GRID_W * NA_QROWS) == 0 and rows >= NA_BAND
    assert w_mod.shape[0] == DEPTH
    cap = EC_CAPACITY * n // N_EXPERTS

    mod_rows = -(-(B + 1) // 8) * 8
    cc = jnp.concatenate([c, c_ctx[None], jnp.zeros((mod_rows - B - 1, D), F32)], axis=0)
    m = _modulation(cc, w_mod[0], b_mod[0])
    sh1, sc1, g1, sh2, sc2, g2 = [m[:B, k * D:(k + 1) * D].reshape(B, 1, D) for k in range(6)]
    csh1 = m[B:B + 1, :D].reshape(1, 1, D)
    csc1 = m[B:B + 1, D:2 * D].reshape(1, 1, D)

    w_small, w_na, w_nvt, w_g, wqa, wqb, wk, wv = _prep_weights(w_in[0], w_uq[0], w_ukv[0])
    qg = q_norm_g[0].reshape(1, Q_LORA)
    kvg = kv_norm_g[0].reshape(1, KV_LORA)
    c_tab, s_tab = _rope_tables(n)
    pad = HEAD_PAD - MLA_QK
    c_ctx_tab = jnp.concatenate([jnp.ones((L, MLA_QK), F32), jnp.zeros((L, pad), F32)], axis=-1)
    s_ctx_tab = jnp.zeros((L, HEAD_PAD), F32)

    tm = _tile(n, 256)
    q, k_lat, v_lat, nq, nk, nv, gates = _inproj(
        True, x, sc1, sh1, c_tab, s_tab, c_tab * (MLA_SCALE * LOG2E), s_tab * (MLA_SCALE * LOG2E),
        w_small, w_na, w_nvt, w_g, qg, kvg, wqa, wqb, wk, wv, tm)
    k_ctx, v_ctx, cnk, cnv = _inproj(
        False, ctx, csc1, csh1, c_ctx_tab, s_ctx_tab, c_ctx_tab, s_ctx_tab,
        w_small, w_na[:, NA_W:], w_nvt, w_g, qg, kvg, wqa, wqb, wk, wv, _tile(L, 256))

    y_mla = _mla_attention(q, k_lat, k_ctx, v_lat, v_ctx, _tile(n, 512), _tile(n, 512))
    y_na = _na_attention(nq, nk, nv, cnk, cnv, _na_bias_tiles(na_rel_bias[0], rows))

    xn, u2, aff = _merge(
        y_mla, y_na, gates, x, g1, sc2, sh2,
        w_proj_mla[0].astype(BF16), w_proj_na[0].astype(BF16), w_out[0].astype(BF16),
        ln1_g[0].reshape(1, D), ln1_b[0].reshape(1, D), w_router[0], tm)

    idx4, g4 = _topk(jnp.swapaxes(aff, 1, 2), cap)
    idx = idx4.reshape(B, N_EXPERTS, 1, cap)
    xe = _gather(idx, u2)
    ye = _ffn(xe, g4, w_exp_gate[0], w_exp_up[0], w_exp_down[0])
    moe = _scatter(idx, ye, n)
    return _final(xn, moe, g2, ln2_g[0].reshape(1, D), ln2_b[0].reshape(1, D), tm)
```

```python
import functools
import math

import numpy as np
import jax
import jax.numpy as jnp
from jax import lax
from jax.experimental import pallas as pl
from jax.experimental.pallas import tpu as pltpu

GRID_W = 64
MLA_HEADS = 8
MLA_NOPE = 64
MLA_ROPE = 32
MLA_QK = MLA_NOPE + MLA_ROPE
MLA_V = 64
Q_LORA = 256
KV_LORA = 128
MLA_SCALE = MLA_QK ** -0.5
ROPE_THETA = 10000.0
NA_HEADS = 8
NA_DIM = 64
NA_W = NA_HEADS * NA_DIM
NA_WIN_H = 8
NA_WIN_W = 16
NA_SCALE = NA_DIM ** -0.5
N_EXPERTS = 16
EC_CAPACITY = 2
LN_EPS = 1e-5
RMS_EPS = 1e-6
DEPTH = 1
ALPHA = (2.0 * DEPTH) ** 0.25
LOG2E = math.log2(math.e)

LANES = 128
HEAD_PAD = LANES
VMEM_LIMIT = 56 * 1024 * 1024
NEG_BIG = -1e30

NA_QROWS = 4
NA_BAND = 12
ROW_GROUP = 8
TOPK_RADIX = 64
TOPK_TABLE_ROWS = 16
TOPK_SLOTS = 128
TOPK_REFINE = 30

BF16 = jnp.bfloat16
F32 = jnp.float32


def _cparams(sem):
    return pltpu.CompilerParams(dimension_semantics=sem, vmem_limit_bytes=VMEM_LIMIT)


def _ln(x):
    mu = jnp.mean(x, axis=-1, keepdims=True)
    xc = x - mu
    var = jnp.mean(xc * xc, axis=-1, keepdims=True)
    return xc * lax.rsqrt(var + LN_EPS)


def _dot(a, b):
    return jnp.dot(a, b, preferred_element_type=F32)


def _dot_nt(a, b):
    return lax.dot_general(a, b, (((1,), (1,)), ((), ())), preferred_element_type=F32)


def _store_rows_as_tiles(ref, val):
    m, d = val.shape
    s = d // LANES
    for j in range(s):
        ref[pl.ds(j, m, stride=s), :] = val[:, j * LANES:(j + 1) * LANES]


def _load_tiles_as_rows(ref, d):
    s = d // LANES
    m = ref.shape[0] // s
    return jnp.concatenate([ref[pl.ds(j, m, stride=s), :] for j in range(s)], axis=-1)


def _mod_kernel(c_ref, w_ref, b_ref, o_ref):
    c = c_ref[...]
    s = c * jax.nn.sigmoid(c)
    o_ref[...] = jnp.dot(s, w_ref[...], preferred_element_type=F32,
                         precision=lax.Precision.HIGHEST) + b_ref[...]


def _modulation(cc, w_mod, b_mod):
    rows, d = cc.shape
    n_out = w_mod.shape[1]
    tn = 1024
    return pl.pallas_call(
        _mod_kernel,
        out_shape=jax.ShapeDtypeStruct((rows, n_out), F32),
        grid=(n_out // tn,),
        in_specs=[pl.BlockSpec((rows, d), lambda j: (0, 0)),
                  pl.BlockSpec((d, tn), lambda j: (0, j)),
                  pl.BlockSpec((1, tn), lambda j: (0, j))],
        out_specs=pl.BlockSpec((rows, tn), lambda j: (0, j)),
        compiler_params=_cparams(("arbitrary",)),
        name="modulation",
    )(cc, w_mod, b_mod.reshape(1, n_out))


def _inproj_kernel(latent, x_ref, sc_ref, sh_ref, ck_ref, sk_ref, cq_ref, sq_ref,
                   w_small_ref, w_na_ref, w_nvt_ref, w_g_ref, qg_ref, kvg_ref, wqa_ref, wqb_ref, wk_ref, wv_ref,
                   *out_refs):
    if latent:
        q_ref, k_ref, v_ref, nq_ref, nk_ref, nv_ref, g_ref = out_refs
    else:
        k_ref, v_ref, nk_ref, nv_ref = out_refs
    x = x_ref[0]
    u = (_ln(x) * (1.0 + sc_ref[0]) + sh_ref[0]).astype(BF16)

    small = _dot(u, w_small_ref[...])
    q_c = small[:, :Q_LORA]
    kv_c = small[:, Q_LORA:Q_LORA + KV_LORA]
    kra = small[:, Q_LORA + KV_LORA:Q_LORA + KV_LORA + LANES]
    krb = small[:, Q_LORA + KV_LORA + LANES:]

    kvn = (kv_c * lax.rsqrt(jnp.mean(kv_c * kv_c, axis=-1, keepdims=True) + RMS_EPS) * kvg_ref[...]).astype(BF16)
    kk = _dot(kvn, wk_ref[...])
    kr = kra * ck_ref[...] + krb * sk_ref[...]
    for h in range(MLA_HEADS):
        k_ref[0, h] = (kk[:, h * HEAD_PAD:(h + 1) * HEAD_PAD] + kr).astype(BF16)
    v_ref[0] = _dot_nt(wv_ref[...], kvn).astype(BF16)

    na = _dot(u, w_na_ref[...])
    nv_ref[0] = _dot_nt(w_nvt_ref[...], u).astype(BF16)
    if latent:
        qn = (q_c * lax.rsqrt(jnp.mean(q_c * q_c, axis=-1, keepdims=True) + RMS_EPS) * qg_ref[...]).astype(BF16)
        qa = _dot(qn, wqa_ref[...])
        qb = _dot(qn, wqb_ref[...])
        cq = cq_ref[...]
        sq = sq_ref[...]
        for h in range(MLA_HEADS):
            sl = slice(h * HEAD_PAD, (h + 1) * HEAD_PAD)
            q_ref[0, h] = (qa[:, sl] * cq + qb[:, sl] * sq).astype(BF16)
        nq_ref[0] = (na[:, :NA_W] * (NA_SCALE * LOG2E)).astype(BF16)
        nk_ref[0] = na[:, NA_W:].astype(BF16)
        g_ref[0] = jax.nn.sigmoid(_dot(u, w_g_ref[...])).astype(BF16)
    else:
        nk_ref[0] = na.astype(BF16)


def _inproj(latent, x, sc, sh, ck, sk, cq, sq, w_small, w_na, w_nvt, w_g, qg, kvg, wqa, wqb, wk, wv, tm):
    B, n, D = x.shape
    per_batch = sc.shape[0] > 1
    mod_map = (lambda b, i: (b, 0, 0)) if per_batch else (lambda b, i: (0, 0, 0))
    full = lambda a: pl.BlockSpec(a.shape, lambda b, i: (0,) * a.ndim)
    tab = lambda a: pl.BlockSpec((tm, a.shape[1]), lambda b, i: (i, 0))
    in_specs = [pl.BlockSpec((1, tm, D), lambda b, i: (b, i, 0)),
                pl.BlockSpec((1, 1, D), mod_map), pl.BlockSpec((1, 1, D), mod_map),
                tab(ck), tab(sk), tab(cq), tab(sq),
                full(w_small), full(w_na), full(w_nvt), full(w_g), full(qg), full(kvg),
                full(wqa), full(wqb), full(wk), full(wv)]
    hk = jax.ShapeDtypeStruct((B, MLA_HEADS, n, HEAD_PAD), BF16)
    hk_spec = pl.BlockSpec((1, MLA_HEADS, tm, HEAD_PAD), lambda b, i: (b, 0, i, 0))
    tok = lambda w: jax.ShapeDtypeStruct((B, n, w), BF16)
    tok_spec = lambda w: pl.BlockSpec((1, tm, w), lambda b, i: (b, i, 0))
    vt = jax.ShapeDtypeStruct((B, MLA_HEADS * MLA_V, n), BF16)
    vt_spec = pl.BlockSpec((1, MLA_HEADS * MLA_V, tm), lambda b, i: (b, 0, i))
    if latent:
        out_shape = [hk, hk, vt, tok(NA_W), tok(NA_W), vt, tok(2 * D)]
        out_specs = [hk_spec, hk_spec, vt_spec, tok_spec(NA_W), tok_spec(NA_W), vt_spec, tok_spec(2 * D)]
    else:
        out_shape = [hk, vt, tok(NA_W), vt]
        out_specs = [hk_spec, vt_spec, tok_spec(NA_W), vt_spec]
    return pl.pallas_call(
        functools.partial(_inproj_kernel, latent),
        out_shape=out_shape,
        grid=(B, n // tm),
        in_specs=in_specs,
        out_specs=out_specs,
        compiler_params=_cparams(("parallel", "parallel")),
        name="inproj_latent" if latent else "inproj_ctx",
    )(x, sc, sh, ck, sk, cq, sq, w_small, w_na, w_nvt, w_g, qg, kvg, wqa, wqb, wk, wv)


def _mla_kernel(tk, q_ref, kc_ref, kl_ref, vc_ref, vl_ref, o_ref, m_ref, l_ref, acc_ref, sa_ref, sb_ref):
    n = kl_ref.shape[2]
    tq = q_ref.shape[2]
    nt = n // tk

    def scores(hh, k):
        return _dot_nt(k, q_ref[0, hh])

    def absorb(hh, s, v_t):
        m_old = m_ref[hh]
        m_new = jnp.maximum(m_old, jnp.max(s, axis=0, keepdims=True))
        a = jnp.exp2(m_old - m_new)
        p = jnp.exp2(s - m_new)
        l_ref[hh] = a * l_ref[hh] + jnp.sum(p, axis=0, keepdims=True)
        acc_ref[hh] = a * acc_ref[hh] + _dot(v_t, p.astype(BF16))
        m_ref[hh] = m_new

    def k_tile(hh, j):
        return kl_ref[0, hh, pl.ds(pl.multiple_of(j * tk, tk), tk), :]

    def v_tile(j):
        return vl_ref[0, :, pl.ds(pl.multiple_of(j * tk, tk), tk)]

    m_ref[...] = jnp.full(m_ref.shape, NEG_BIG, F32)
    l_ref[...] = jnp.zeros(l_ref.shape, F32)
    acc_ref[...] = jnp.zeros(acc_ref.shape, F32)
    s_ctx = [scores(hh, kc_ref[0, hh]) for hh in range(2)]
    for hh in range(2):
        sa_ref[hh] = scores(hh, k_tile(hh, 0))
        absorb(hh, s_ctx[hh], vc_ref[0])

    def advance(j, cur_ref, next_ref):
        v_t = v_tile(j)
        for hh in range(2):
            next_ref[hh] = scores(hh, k_tile(hh, j + 1))
            absorb(hh, cur_ref[hh], v_t)

    def body(jj, carry):
        advance(2 * jj, sa_ref, sb_ref)
        advance(2 * jj + 1, sb_ref, sa_ref)
        return carry

    lax.fori_loop(0, (nt - 1) // 2, body, 0)
    last_ref = sa_ref
    if (nt - 1) % 2:
        advance(nt - 2, sa_ref, sb_ref)
        last_ref = sb_ref
    v_t = v_tile(nt - 1)
    for hh in range(2):
        absorb(hh, last_ref[hh], v_t)
    row = lax.broadcasted_iota(jnp.int32, (LANES, tq), 0)
    out_t = jnp.where(row < MLA_V, acc_ref[0] / l_ref[0], acc_ref[1] / l_ref[1])
    o_ref[0] = out_t.T.astype(o_ref.dtype)


def _mla_attention(q, k_lat, k_ctx, v_lat, v_ctx, tq, tk):
    B, H, n, _ = q.shape
    L = k_ctx.shape[2]
    return pl.pallas_call(
        functools.partial(_mla_kernel, tk),
        out_shape=jax.ShapeDtypeStruct((B, n, H * MLA_V), BF16),
        grid=(B, H // 2, n // tq),
        in_specs=[pl.BlockSpec((1, 2, tq, HEAD_PAD), lambda b, j, i: (b, j, i, 0)),
                  pl.BlockSpec((1, 2, L, HEAD_PAD), lambda b, j, i: (b, j, 0, 0)),
                  pl.BlockSpec((1, 2, n, HEAD_PAD), lambda b, j, i: (b, j, 0, 0)),
                  pl.BlockSpec((1, LANES, L), lambda b, j, i: (b, j, 0)),
                  pl.BlockSpec((1, LANES, n), lambda b, j, i: (b, j, 0))],
        out_specs=pl.BlockSpec((1, tq, LANES), lambda b, j, i: (b, i, j)),
        scratch_shapes=[pltpu.VMEM((2, 1, tq), F32), pltpu.VMEM((2, 1, tq), F32),
                        pltpu.VMEM((2, LANES, tq), F32),
                        pltpu.VMEM((2, tk, tq), F32), pltpu.VMEM((2, tk, tq), F32)],
        compiler_params=_cparams(("parallel", "parallel", "parallel")),
        name="mla_attention",
    )(q, k_ctx, k_lat, v_ctx, v_lat)


def _na_block_tables(rows):
    nblk = rows // NA_QROWS
    wh = min(NA_WIN_H, rows)
    band0 = np.clip(np.arange(nblk) * NA_QROWS - wh // 2, 0, rows - NA_BAND)
    sigs, types = [], []
    for i in range(nblk):
        r = i * NA_QROWS + np.arange(NA_QROWS)
        r0 = np.clip(r - wh // 2, 0, rows - wh)
        sig = (tuple(r0 - band0[i]), tuple(r - band0[i]))
        if sig not in sigs:
            sigs.append(sig)
        types.append(sigs.index(sig))
    return band0.astype(np.int32), np.asarray(types, np.int32), sigs, wh


def _na_bias_tiles(rel_bias, rows):
    _, _, sigs, wh = _na_block_tables(rows)
    ww = NA_WIN_W
    n_dr, n_dc = 2 * NA_WIN_H - 1, 2 * NA_WIN_W - 1
    col = np.arange(GRID_W)
    c0 = np.clip(col - ww // 2, 0, GRID_W - ww)
    col_ok = (col[None, :] >= c0[:, None]) & (col[None, :] < c0[:, None] + ww)
    dc = col[None, :] - col[:, None] + (NA_WIN_W - 1)
    pick_c = ((dc[None] == np.arange(n_dc)[:, None, None]) & col_ok[None]).astype(np.float32)
    kr = np.arange(NA_BAND)
    row_ok = np.stack([(kr[None, :] >= np.asarray(r0)[:, None]) & (kr[None, :] < np.asarray(r0)[:, None] + wh)
                       for r0, _ in sigs])
    dr = np.stack([kr[None, :] - np.asarray(r)[:, None] + (NA_WIN_H - 1) for _, r in sigs])
    pick_r = ((dr[..., None] == np.arange(n_dr)) & row_ok[..., None]).astype(np.float32)
    hp = lax.Precision.HIGHEST
    by_col = jnp.einsum('hdj,jxy->hdxy', rel_bias.astype(F32), jnp.asarray(pick_c), precision=hp)
    vals = jnp.einsum('tqkd,hdxy->thkyqx', jnp.asarray(pick_r), by_col, precision=hp)
    ok = np.transpose(row_ok, (0, 2, 1))[:, None, :, None, :, None] & col_ok.T[None, None, None, :, None, :]
    tiles = jnp.where(jnp.asarray(ok), vals * LOG2E, NEG_BIG)
    return tiles.reshape(len(sigs), NA_HEADS, NA_BAND * GRID_W, NA_QROWS * GRID_W)


def _na_kernel(band_ref, type_ref, q_ref, k_ref, vt_ref, kc_ref, vct_ref, bias_ref, o_ref):
    i = pl.program_id(1)
    nq = NA_QROWS * GRID_W
    nk = NA_BAND * GRID_W
    start = pl.multiple_of(band_ref[i] * GRID_W, NA_QROWS * GRID_W)
    lane = lax.broadcasted_iota(jnp.int32, (nq, LANES), 1)
    row = lax.broadcasted_iota(jnp.int32, (LANES, nq), 0)

    def scores(h):
        cs = slice((h // 2) * LANES, (h // 2 + 1) * LANES)
        qp = q_ref[0, :, cs]
        own = (lane < NA_DIM) if h % 2 == 0 else (lane >= NA_DIM)
        qh = jnp.where(own, qp, jnp.zeros_like(qp))
        s_loc = _dot_nt(k_ref[0, pl.ds(start, nk), cs], qh) + bias_ref[0, h]
        s_ctx = _dot_nt(kc_ref[0, :, cs], qh)
        return s_loc, s_ctx

    def attend(h, s):
        s_loc, s_ctx = s
        cs = slice((h // 2) * LANES, (h // 2 + 1) * LANES)
        m = jnp.maximum(jnp.max(s_loc, axis=0, keepdims=True), jnp.max(s_ctx, axis=0, keepdims=True))
        p_loc = jnp.exp2(s_loc - m)
        p_ctx = jnp.exp2(s_ctx - m)
        l = jnp.sum(p_loc, axis=0, keepdims=True) + jnp.sum(p_ctx, axis=0, keepdims=True)
        o = _dot(vt_ref[0, cs, pl.ds(start, nk)], p_loc.astype(BF16)) + _dot(vct_ref[0, cs, :], p_ctx.astype(BF16))
        return o / l

    s = scores(0)
    outs = []
    for h in range(NA_HEADS):
        s_next = scores(h + 1) if h + 1 < NA_HEADS else None
        outs.append(attend(h, s))
        s = s_next
        if h % 2:
            pair_t = jnp.where(row < NA_DIM, outs[h - 1], outs[h])
            o_ref[0, :, (h // 2) * LANES:(h // 2 + 1) * LANES] = pair_t.T.astype(o_ref.dtype)


def _na_attention(nq, nk, nvt, cnk, cnvt, bias_tiles):
    B, n, W = nq.shape
    L = cnk.shape[1]
    rows = n // GRID_W
    band0, types, _, _ = _na_block_tables(rows)
    qn = NA_QROWS * GRID_W
    kn = NA_BAND * GRID_W
    grid_spec = pltpu.PrefetchScalarGridSpec(
        num_scalar_prefetch=2,
        grid=(B, rows // NA_QROWS),
        in_specs=[pl.BlockSpec((1, qn, W), lambda b, i, bd, ty: (b, i, 0)),
                  pl.BlockSpec((1, n, W), lambda b, i, bd, ty: (b, 0, 0)),
                  pl.BlockSpec((1, W, n), lambda b, i, bd, ty: (b, 0, 0)),
                  pl.BlockSpec((1, L, W), lambda b, i, bd, ty: (b, 0, 0)),
                  pl.BlockSpec((1, W, L), lambda b, i, bd, ty: (b, 0, 0)),
                  pl.BlockSpec((1, NA_HEADS, kn, qn), lambda b, i, bd, ty: (ty[i], 0, 0, 0))],
        out_specs=pl.BlockSpec((1, qn, W), lambda b, i, bd, ty: (b, i, 0)),
    )
    return pl.pallas_call(
        _na_kernel,
        out_shape=jax.ShapeDtypeStruct((B, n, W), BF16),
        grid_spec=grid_spec,
        compiler_params=_cparams(("parallel", "arbitrary")),
        name="na_attention",
    )(jnp.asarray(band0), jnp.asarray(types), nq, nk, nvt, cnk, cnvt, bias_tiles)


def _split2(a):
    hi = a.astype(BF16)
    return hi, (a - hi.astype(F32)).astype(BF16)


def _merge_kernel(ym_ref, yn_ref, g_ref, x_ref, g1_ref, sc2_ref, sh2_ref, wpm_ref, wpn_ref, wo_ref,
                  l1g_ref, l1b_ref, wr_ref, xn_ref, up_ref, aff_ref):
    D = x_ref.shape[2]
    g = g_ref[0]
    a = g[:, :D].astype(F32) * _dot(ym_ref[0], wpm_ref[...]) + g[:, D:].astype(F32) * _dot(yn_ref[0], wpn_ref[...])
    mix = _dot(a.astype(BF16), wo_ref[...])
    xn = _ln(ALPHA * x_ref[0] + g1_ref[0] * mix) * l1g_ref[...] + l1b_ref[...]
    xn_ref[0] = xn
    u2 = _ln(xn) * (1.0 + sc2_ref[0]) + sh2_ref[0]

    _store_rows_as_tiles(up_ref.at[0], u2)

    uh, ul = _split2(u2)
    wh, wl = _split2(wr_ref[...])
    logits = _dot(uh, wh) + (_dot(uh, wl) + _dot(ul, wh))
    e = jnp.exp(logits - jnp.max(logits, axis=-1, keepdims=True))
    aff_ref[0] = e / jnp.sum(e, axis=-1, keepdims=True)


def _merge(y_mla, y_na, gates, x, g1, sc2, sh2, wpm, wpn, wo, l1g, l1b, wr, tm):
    B, n, D = x.shape
    E = wr.shape[1]
    full = lambda a: pl.BlockSpec(a.shape, lambda b, i: (0,) * a.ndim)
    tok = lambda w: pl.BlockSpec((1, tm, w), lambda b, i: (b, i, 0))
    mod = pl.BlockSpec((1, 1, D), lambda b, i: (b, 0, 0))
    return pl.pallas_call(
        _merge_kernel,
        out_shape=[jax.ShapeDtypeStruct((B, n, D), F32),
                   jax.ShapeDtypeStruct((B, n * (D // LANES), LANES), F32),
                   jax.ShapeDtypeStruct((B, n, E), F32)],
        grid=(B, n // tm),
        in_specs=[tok(y_mla.shape[2]), tok(y_na.shape[2]), tok(2 * D), tok(D), mod, mod, mod,
                  full(wpm), full(wpn), full(wo), full(l1g), full(l1b), full(wr)],
        out_specs=[tok(D), pl.BlockSpec((1, tm * (D // LANES), LANES), lambda b, i: (b, i, 0)), tok(E)],
        compiler_params=_cparams(("parallel", "parallel")),
        name="merge_ln_router",
    )(y_mla, y_na, gates, x, g1, sc2, sh2, wpm, wpn, wo, l1g, l1b, wr)


def _topk_kernel(cap, aff_ref, tri_ref, idx_ref, g_ref, pos_ref):
    aff = aff_ref[0]
    E, n = aff.shape

    def count(mask):
        return jnp.sum(mask.astype(F32), axis=-1, keepdims=True)

    def search(t, thr):
        cand = thr | (jnp.int32(1) << (30 - t))
        return jnp.where(count(aff >= pltpu.bitcast(cand, F32)) >= cap, cand, thr)

    thr = lax.fori_loop(0, 31, search, jnp.zeros((E, 1), jnp.int32))
    lo = pltpu.bitcast(thr, F32)
    hi = pltpu.bitcast(thr + 1, F32)

    def refine(t, lh):
        lo, hi = lh
        mid = lo + (hi - lo) * 0.5
        ok = count(aff >= mid) >= cap
        return jnp.where(ok, mid, lo), jnp.where(ok, hi, mid)

    lo, hi = lax.fori_loop(0, TOPK_REFINE, refine, (lo, hi))
    gt = aff >= hi
    eq = (aff >= lo) & ~gt
    need = cap - count(gt)

    tri = tri_ref[...]

    def prefix(mask):
        mf = mask.astype(F32).astype(BF16)
        parts = []
        off = jnp.zeros((E, 1), F32)
        for c in range(n // LANES):
            blk = mf[:, c * LANES:(c + 1) * LANES]
            parts.append(_dot(blk, tri) + off)
            off = off + jnp.sum(blk.astype(F32), axis=-1, keepdims=True)
        return jnp.concatenate(parts, axis=-1)

    sel = gt | (eq & (prefix(eq) < need))
    pos_ref[...] = jnp.where(sel, prefix(sel), -1.0)

    tok = lax.broadcasted_iota(jnp.int32, (1, n), 1)
    digits = [(tok // TOPK_RADIX).astype(F32), (tok % TOPK_RADIX).astype(F32)]
    chunks = cap // TOPK_SLOTS

    def per_expert(e, carry):
        a = aff_ref[0, pl.ds(e, 1), :]
        a1 = a.astype(BF16).astype(F32)
        a2 = (a - a1).astype(BF16).astype(F32)
        a3 = (a - a1) - a2
        rows = digits + [a1, a2, a3]
        table = jnp.concatenate(rows + [jnp.zeros((TOPK_TABLE_ROWS - len(rows), n), F32)], axis=0).astype(BF16)
        pos_e = pos_ref[pl.ds(e, 1), :]

        hits = []
        for ch in range(chunks):
            slot = (lax.broadcasted_iota(jnp.int32, (TOPK_SLOTS, n), 0) + ch * TOPK_SLOTS).astype(F32)
            hits.append(jnp.where(slot == pos_e, 1.0, 0.0).astype(BF16))
        for ch in range(chunks):
            r = _dot_nt(hits[ch], table)
            idx = r[:, 0:1] * TOPK_RADIX + r[:, 1:2]
            idx_ref[0, e, ch * TOPK_SLOTS:(ch + 1) * TOPK_SLOTS, :] = idx.astype(jnp.int32)
            g_ref[0, e, ch * TOPK_SLOTS:(ch + 1) * TOPK_SLOTS, :] = (r[:, 2:3] + r[:, 3:4]) + r[:, 4:5]
        return carry

    lax.fori_loop(0, E, per_expert, 0)


def _topk(aff_t, cap):
    B, E, n = aff_t.shape
    tri = jnp.asarray(np.triu(np.ones((LANES, LANES), np.float32), k=1), BF16)
    return pl.pallas_call(
        functools.partial(_topk_kernel, cap),
        out_shape=[jax.ShapeDtypeStruct((B, E, cap, 1), jnp.int32),
                   jax.ShapeDtypeStruct((B, E, cap, 1), F32)],
        grid=(B,),
        in_specs=[pl.BlockSpec((1, E, n), lambda b: (b, 0, 0)),
                  pl.BlockSpec((LANES, LANES), lambda b: (0, 0))],
        out_specs=[pl.BlockSpec((1, E, cap, 1), lambda b: (b, 0, 0, 0)),
                   pl.BlockSpec((1, E, cap, 1), lambda b: (b, 0, 0, 0))],
        scratch_shapes=[pltpu.VMEM((E, n), F32)],
        compiler_params=_cparams(("parallel",)),
        name="expert_topk",
    )(aff_t, tri)


def _gather_kernel(s, idx_ref, u_ref, o_ref):
    cap = o_ref.shape[2] // s

    def body(i, carry):
        c0 = pl.multiple_of(i * ROW_GROUP, ROW_GROUP)
        for k in range(ROW_GROUP):
            src = pl.multiple_of(idx_ref[0, 0, 0, c0 + k] * s, s)
            o_ref[0, 0, pl.ds((c0 + k) * s, s), :] = u_ref[0, pl.ds(src, s), :]
        return carry

    lax.fori_loop(0, cap // ROW_GROUP, body, 0)


def _gather(idx, u2, n):
    B, E, _, cap = idx.shape
    _, ns, W = u2.shape
    s = ns // n
    return pl.pallas_call(
        functools.partial(_gather_kernel, s),
        out_shape=jax.ShapeDtypeStruct((B, E, cap * s, W), u2.dtype),
        grid=(B, E),
        in_specs=[pl.BlockSpec((1, 1, 1, cap), lambda b, e: (b, e, 0, 0), memory_space=pltpu.SMEM),
                  pl.BlockSpec((1, ns, W), lambda b, e: (b, 0, 0))],
        out_specs=pl.BlockSpec((1, 1, cap * s, W), lambda b, e: (b, e, 0, 0)),
        compiler_params=_cparams(("parallel", "arbitrary")),
        name="expert_gather",
    )(idx, u2)


def _ffn_kernel(xe_ref, g_ref, wg_ref, wu_ref, wd_ref, o_ref, wgb_ref, wub_ref, wdb_ref):
    @pl.when(pl.program_id(1) == 0)
    def _():
        wgb_ref[...] = wg_ref[0].astype(BF16)
        wub_ref[...] = wu_ref[0].astype(BF16)
        wdb_ref[...] = wd_ref[0].astype(BF16)

    xe = _load_tiles_as_rows(xe_ref.at[0, 0], wgb_ref.shape[0]).astype(BF16)
    gate = _dot(xe, wgb_ref[...])
    up = _dot(xe, wub_ref[...])
    h = (gate * jax.nn.sigmoid(gate) * up).astype(BF16)
    _store_rows_as_tiles(o_ref.at[0, 0], _dot(h, wdb_ref[...]) * g_ref[0, 0])


def _ffn(xe, g, w_gate, w_up, w_down):
    B, E, caps, W = xe.shape
    _, D, F = w_gate.shape
    cap = g.shape[2]
    return pl.pallas_call(
        _ffn_kernel,
        out_shape=jax.ShapeDtypeStruct((B, E, caps, W), F32),
        grid=(E, B),
        in_specs=[pl.BlockSpec((1, 1, caps, W), lambda e, b: (b, e, 0, 0)),
                  pl.BlockSpec((1, 1, cap, 1), lambda e, b: (b, e, 0, 0)),
                  pl.BlockSpec((1, D, F), lambda e, b: (e, 0, 0)),
                  pl.BlockSpec((1, D, F), lambda e, b: (e, 0, 0)),
                  pl.BlockSpec((1, F, D), lambda e, b: (e, 0, 0))],
        out_specs=pl.BlockSpec((1, 1, caps, W), lambda e, b: (b, e, 0, 0)),
        scratch_shapes=[pltpu.VMEM((D, F), BF16), pltpu.VMEM((D, F), BF16), pltpu.VMEM((F, D), BF16)],
        compiler_params=_cparams(("arbitrary", "arbitrary")),
        name="expert_ffn",
    )(xe, g, w_gate, w_up, w_down)


def _scatter_kernel(s, idx_ref, y_ref, o_ref):
    cap = y_ref.shape[2] // s

    def tile(r):
        return pl.ds(pl.multiple_of(r * s, s), s)

    @pl.when(pl.program_id(1) == 0)
    def _():
        o_ref[...] = jnp.zeros_like(o_ref)

    def body(i, carry):
        c0 = pl.multiple_of(i * ROW_GROUP, ROW_GROUP)
        rows = [idx_ref[0, 0, 0, c0 + k] for k in range(ROW_GROUP)]
        acc = [o_ref[0, tile(r), :] for r in rows]
        for k in range(ROW_GROUP):
            o_ref[0, tile(rows[k]), :] = acc[k] + y_ref[0, 0, tile(c0 + k), :]
        return carry

    lax.fori_loop(0, cap // ROW_GROUP, body, 0)


def _scatter(idx, ye, n):
    B, E, caps, W = ye.shape
    cap = idx.shape[3]
    s = caps // cap
    return pl.pallas_call(
        functools.partial(_scatter_kernel, s),
        out_shape=jax.ShapeDtypeStruct((B, n * s, W), F32),
        grid=(B, E),
        in_specs=[pl.BlockSpec((1, 1, 1, cap), lambda b, e: (b, e, 0, 0), memory_space=pltpu.SMEM),
                  pl.BlockSpec((1, 1, caps, W), lambda b, e: (b, e, 0, 0))],
        out_specs=pl.BlockSpec((1, n * s, W), lambda b, e: (b, 0, 0)),
        compiler_params=_cparams(("parallel", "arbitrary")),
        name="expert_scatter",
    )(idx, ye)


def _final_kernel(x_ref, moe_ref, g2_ref, lg_ref, lb_ref, o_ref):
    moe = _load_tiles_as_rows(moe_ref.at[0], x_ref.shape[2])
    o_ref[0] = _ln(ALPHA * x_ref[0] + g2_ref[0] * moe) * lg_ref[...] + lb_ref[...]


def _final(xn, moe, g2, lg, lb, tm):
    B, n, D = xn.shape
    tok = pl.BlockSpec((1, tm, D), lambda b, i: (b, i, 0))
    vec = pl.BlockSpec((1, D), lambda b, i: (0, 0))
    return pl.pallas_call(
        _final_kernel,
        out_shape=jax.ShapeDtypeStruct((B, n, D), F32),
        grid=(B, n // tm),
        in_specs=[tok, pl.BlockSpec((1, tm * (D // LANES), LANES), lambda b, i: (b, i, 0)),
                  pl.BlockSpec((1, 1, D), lambda b, i: (b, 0, 0)), vec, vec],
        out_specs=tok,
        compiler_params=_cparams(("parallel", "parallel")),
        name="final_ln",
    )(xn, moe, g2, lg, lb)


def _rope_tables(n):
    t = np.arange(n)
    row = (t // GRID_W).astype(np.float32)
    col = (t % GRID_W).astype(np.float32)
    per_axis = MLA_ROPE // 2
    inv_freq = jnp.asarray(ROPE_THETA, F32) ** (-jnp.arange(0, per_axis, 2, dtype=F32) / per_axis)
    ang = jnp.concatenate([jnp.asarray(row)[:, None] * inv_freq, jnp.asarray(col)[:, None] * inv_freq], axis=-1)
    cos, sin = jnp.cos(ang), jnp.sin(ang)
    pad = HEAD_PAD - MLA_QK
    c_tab = jnp.concatenate([jnp.ones((n, MLA_NOPE), F32), cos, cos, jnp.zeros((n, pad), F32)], axis=-1)
    s_tab = jnp.concatenate([jnp.zeros((n, MLA_NOPE), F32), -sin, sin, jnp.zeros((n, pad), F32)], axis=-1)
    return c_tab, s_tab


def _swap_halves(w):
    half = w.shape[-1] // 2
    return jnp.concatenate([w[..., half:], w[..., :half]], axis=-1)


def _prep_weights(w_in, w_uq, w_ukv):
    D = w_in.shape[0]
    o1 = Q_LORA
    o2 = o1 + KV_LORA
    o3 = o2 + MLA_ROPE
    o4 = o3 + 3 * NA_W
    w_kr = w_in[:, o2:o3]
    z = lambda k: jnp.zeros((D, k), w_in.dtype)
    pad = HEAD_PAD - MLA_QK
    kra = jnp.concatenate([z(MLA_NOPE), w_kr, z(pad)], axis=-1)
    krb = jnp.concatenate([z(MLA_NOPE), _swap_halves(w_kr), z(pad)], axis=-1)
    w_small = jnp.concatenate([w_in[:, :o2], kra, krb], axis=-1).astype(BF16)
    w_na = w_in[:, o3:o3 + 2 * NA_W].astype(BF16)
    w_nvt = w_in[:, o3 + 2 * NA_W:o4].T.astype(BF16)
    w_g = w_in[:, o4:].astype(BF16)

    uq = w_uq.reshape(Q_LORA, MLA_HEADS, MLA_QK)
    zq = lambda k: jnp.zeros((Q_LORA, MLA_HEADS, k), w_uq.dtype)
    wqa = jnp.concatenate([uq, zq(pad)], axis=-1).reshape(Q_LORA, MLA_HEADS * HEAD_PAD).astype(BF16)
    wqb = jnp.concatenate([zq(MLA_NOPE), _swap_halves(uq[..., MLA_NOPE:]), zq(pad)], axis=-1)
    wqb = wqb.reshape(Q_LORA, MLA_HEADS * HEAD_PAD).astype(BF16)

    ukv = w_ukv.reshape(KV_LORA, MLA_HEADS, MLA_NOPE + MLA_V)
    wk = jnp.concatenate([ukv[..., :MLA_NOPE], jnp.zeros((KV_LORA, MLA_HEADS, HEAD_PAD - MLA_NOPE), w_ukv.dtype)], axis=-1)
    wk = wk.reshape(KV_LORA, MLA_HEADS * HEAD_PAD).astype(BF16)
    wv = ukv[..., MLA_NOPE:].reshape(KV_LORA, MLA_HEADS * MLA_V).T.astype(BF16)
    return w_small, w_na, w_nvt, w_g, wqa, wqb, wk, wv


def _tile(n, pref):
    t = min(pref, n)
    while n % t:
        t //= 2
    return t


def kernel(x, c, ctx, c_ctx, w_mod, b_mod, w_in, q_norm_g, w_uq, kv_norm_g, w_ukv, na_rel_bias, w_proj_mla,
           w_proj_na, w_out, ln1_g, ln1_b, w_router, w_exp_gate, w_exp_up, w_exp_down, ln2_g, ln2_b):
    B, n, D = x.shape
    L = ctx.shape[1]
    rows = n // GRID_W
    assert n % (GRID_W * NA_QROWS) == 0 and rows >= NA_BAND
    assert w_mod.shape[0] == DEPTH
    cap = EC_CAPACITY * n // N_EXPERTS

    mod_rows = -(-(B + 1) // 8) * 8
    cc = jnp.concatenate([c, c_ctx[None], jnp.zeros((mod_rows - B - 1, D), F32)], axis=0)
    m = _modulation(cc, w_mod[0], b_mod[0])
    sh1, sc1, g1, sh2, sc2, g2 = [m[:B, k * D:(k + 1) * D].reshape(B, 1, D) for k in range(6)]
    csh1 = m[B:B + 1, :D].reshape(1, 1, D)
    csc1 = m[B:B + 1, D:2 * D].reshape(1, 1, D)

    w_small, w_na, w_nvt, w_g, wqa, wqb, wk, wv = _prep_weights(w_in[0], w_uq[0], w_ukv[0])
    qg = q_norm_g[0].reshape(1, Q_LORA)
    kvg = kv_norm_g[0].reshape(1, KV_LORA)
    c_tab, s_tab = _rope_tables(n)
    pad = HEAD_PAD - MLA_QK
    c_ctx_tab = jnp.concatenate([jnp.ones((L, MLA_QK), F32), jnp.zeros((L, pad), F32)], axis=-1)
    s_ctx_tab = jnp.zeros((L, HEAD_PAD), F32)

    tm = _tile(n, 256)
    q, k_lat, v_lat, nq, nk, nv, gates = _inproj(
        True, x, sc1, sh1, c_tab, s_tab, c_tab * (MLA_SCALE * LOG2E), s_tab * (MLA_SCALE * LOG2E),
        w_small, w_na, w_nvt, w_g, qg, kvg, wqa, wqb, wk, wv, _tile(n, 512))
    k_ctx, v_ctx, cnk, cnv = _inproj(
        False, ctx, csc1, csh1, c_ctx_tab, s_ctx_tab, c_ctx_tab, s_ctx_tab,
        w_small, w_na[:, NA_W:], w_nvt, w_g, qg, kvg, wqa, wqb, wk, wv, _tile(L, 256))

    y_mla = _mla_attention(q, k_lat, k_ctx, v_lat, v_ctx, _tile(n, 1024), _tile(n, 512))
    y_na = _na_attention(nq, nk, nv, cnk, cnv, _na_bias_tiles(na_rel_bias[0], rows))

    xn, u2, aff = _merge(
        y_mla, y_na, gates, x, g1, sc2, sh2,
        w_proj_mla[0].astype(BF16), w_proj_na[0].astype(BF16), w_out[0].astype(BF16),
        ln1_g[0].reshape(1, D), ln1_b[0].reshape(1, D), w_router[0], tm)

    idx4, g4 = _topk(jnp.swapaxes(aff, 1, 2), cap)
    idx = idx4.reshape(B, N_EXPERTS, 1, cap)
    xe = _gather(idx, u2, n)
    ye = _ffn(xe, g4, w_exp_gate[0], w_exp_up[0], w_exp_down[0])
    moe = _scatter(idx, ye, n)
    return _final(xn, moe, g2, ln2_g[0].reshape(1, D), ln2_b[0].reshape(1, D), tm)
```

```python
import functools
import math

import numpy as np
import jax
import jax.numpy as jnp
from jax import lax
from jax.experimental import pallas as pl
from jax.experimental.pallas import tpu as pltpu

GRID_W = 64
MLA_HEADS = 8
MLA_NOPE = 64
MLA_ROPE = 32
MLA_QK = MLA_NOPE + MLA_ROPE
MLA_V = 64
Q_LORA = 256
KV_LORA = 128
MLA_SCALE = MLA_QK ** -0.5
ROPE_THETA = 10000.0
NA_HEADS = 8
NA_DIM = 64
NA_W = NA_HEADS * NA_DIM
NA_WIN_H = 8
NA_WIN_W = 16
NA_SCALE = NA_DIM ** -0.5
N_EXPERTS = 16
EC_CAPACITY = 2
LN_EPS = 1e-5
RMS_EPS = 1e-6
DEPTH = 1
ALPHA = (2.0 * DEPTH) ** 0.25
LOG2E = math.log2(math.e)

LANES = 128
HEAD_PAD = LANES
VMEM_LIMIT = 56 * 1024 * 1024
NEG_BIG = -1e30

NA_QROWS = 4
NA_BAND = 12
ROW_GROUP = 8
MERGE_SUB = 256
TOPK_RADIX = 64
TOPK_TABLE_ROWS = 16
TOPK_SLOTS = 128
TOPK_REFINE = 30

BF16 = jnp.bfloat16
F32 = jnp.float32


def _cparams(sem):
    return pltpu.CompilerParams(dimension_semantics=sem, vmem_limit_bytes=VMEM_LIMIT)


def _ln(x):
    mu = jnp.mean(x, axis=-1, keepdims=True)
    xc = x - mu
    var = jnp.mean(xc * xc, axis=-1, keepdims=True)
    return xc * lax.rsqrt(var + LN_EPS)


def _dot(a, b):
    return jnp.dot(a, b, preferred_element_type=F32)


def _dot_nt(a, b):
    return lax.dot_general(a, b, (((1,), (1,)), ((), ())), preferred_element_type=F32)


def _store_rows_as_tiles(ref, val):
    m, d = val.shape
    s = d // LANES
    for j in range(s):
        ref[pl.ds(j, m, stride=s), :] = val[:, j * LANES:(j + 1) * LANES]


def _load_tiles_as_rows(ref, d):
    s = d // LANES
    m = ref.shape[0] // s
    return jnp.concatenate([ref[pl.ds(j, m, stride=s), :] for j in range(s)], axis=-1)


def _mod_kernel(c_ref, w_ref, b_ref, o_ref):
    c = c_ref[...]
    s = c * jax.nn.sigmoid(c)
    o_ref[...] = jnp.dot(s, w_ref[...], preferred_element_type=F32,
                         precision=lax.Precision.HIGHEST) + b_ref[...]


def _modulation(cc, w_mod, b_mod):
    rows, d = cc.shape
    n_out = w_mod.shape[1]
    tn = 1024
    return pl.pallas_call(
        _mod_kernel,
        out_shape=jax.ShapeDtypeStruct((rows, n_out), F32),
        grid=(n_out // tn,),
        in_specs=[pl.BlockSpec((rows, d), lambda j: (0, 0)),
                  pl.BlockSpec((d, tn), lambda j: (0, j)),
                  pl.BlockSpec((1, tn), lambda j: (0, j))],
        out_specs=pl.BlockSpec((rows, tn), lambda j: (0, j)),
        compiler_params=_cparams(("arbitrary",)),
        name="modulation",
    )(cc, w_mod, b_mod.reshape(1, n_out))


def _inproj_kernel(latent, x_ref, sc_ref, sh_ref, ck_ref, sk_ref, cq_ref, sq_ref,
                   w_small_ref, w_na_ref, w_nvt_ref, w_g_ref, qg_ref, kvg_ref, wqa_ref, wqb_ref, wk_ref, wv_ref,
                   *out_refs):
    if latent:
        q_ref, k_ref, v_ref, nq_ref, nk_ref, nv_ref, g_ref = out_refs
    else:
        k_ref, v_ref, nk_ref, nv_ref = out_refs
    x = x_ref[0]
    u = (_ln(x) * (1.0 + sc_ref[0]) + sh_ref[0]).astype(BF16)

    small = _dot(u, w_small_ref[...])
    q_c = small[:, :Q_LORA]
    kv_c = small[:, Q_LORA:Q_LORA + KV_LORA]
    kra = small[:, Q_LORA + KV_LORA:Q_LORA + KV_LORA + LANES]
    krb = small[:, Q_LORA + KV_LORA + LANES:]

    kvn = (kv_c * lax.rsqrt(jnp.mean(kv_c * kv_c, axis=-1, keepdims=True) + RMS_EPS) * kvg_ref[...]).astype(BF16)
    kk = _dot(kvn, wk_ref[...])
    kr = kra * ck_ref[...] + krb * sk_ref[...]
    for h in range(MLA_HEADS):
        k_ref[0, h] = (kk[:, h * HEAD_PAD:(h + 1) * HEAD_PAD] + kr).astype(BF16)
    v_ref[0] = _dot_nt(wv_ref[...], kvn).astype(BF16)

    na = _dot(u, w_na_ref[...])
    nv_ref[0] = _dot_nt(w_nvt_ref[...], u).astype(BF16)
    if latent:
        qn = (q_c * lax.rsqrt(jnp.mean(q_c * q_c, axis=-1, keepdims=True) + RMS_EPS) * qg_ref[...]).astype(BF16)
        qa = _dot(qn, wqa_ref[...])
        qb = _dot(qn, wqb_ref[...])
        cq = cq_ref[...]
        sq = sq_ref[...]
        for h in range(MLA_HEADS):
            sl = slice(h * HEAD_PAD, (h + 1) * HEAD_PAD)
            q_ref[0, h] = (qa[:, sl] * cq + qb[:, sl] * sq).astype(BF16)
        nq_ref[0] = (na[:, :NA_W] * (NA_SCALE * LOG2E)).astype(BF16)
        nk_ref[0] = na[:, NA_W:].astype(BF16)
        g_ref[0] = jax.nn.sigmoid(_dot(u, w_g_ref[...])).astype(BF16)
    else:
        nk_ref[0] = na.astype(BF16)


def _inproj(latent, x, sc, sh, ck, sk, cq, sq, w_small, w_na, w_nvt, w_g, qg, kvg, wqa, wqb, wk, wv, tm):
    B, n, D = x.shape
    per_batch = sc.shape[0] > 1
    mod_map = (lambda b, i: (b, 0, 0)) if per_batch else (lambda b, i: (0, 0, 0))
    full = lambda a: pl.BlockSpec(a.shape, lambda b, i: (0,) * a.ndim)
    tab = lambda a: pl.BlockSpec((tm, a.shape[1]), lambda b, i: (i, 0))
    in_specs = [pl.BlockSpec((1, tm, D), lambda b, i: (b, i, 0)),
                pl.BlockSpec((1, 1, D), mod_map), pl.BlockSpec((1, 1, D), mod_map),
                tab(ck), tab(sk), tab(cq), tab(sq),
                full(w_small), full(w_na), full(w_nvt), full(w_g), full(qg), full(kvg),
                full(wqa), full(wqb), full(wk), full(wv)]
    hk = jax.ShapeDtypeStruct((B, MLA_HEADS, n, HEAD_PAD), BF16)
    hk_spec = pl.BlockSpec((1, MLA_HEADS, tm, HEAD_PAD), lambda b, i: (b, 0, i, 0))
    tok = lambda w: jax.ShapeDtypeStruct((B, n, w), BF16)
    tok_spec = lambda w: pl.BlockSpec((1, tm, w), lambda b, i: (b, i, 0))
    vt = jax.ShapeDtypeStruct((B, MLA_HEADS * MLA_V, n), BF16)
    vt_spec = pl.BlockSpec((1, MLA_HEADS * MLA_V, tm), lambda b, i: (b, 0, i))
    if latent:
        out_shape = [hk, hk, vt, tok(NA_W), tok(NA_W), vt, tok(2 * D)]
        out_specs = [hk_spec, hk_spec, vt_spec, tok_spec(NA_W), tok_spec(NA_W), vt_spec, tok_spec(2 * D)]
    else:
        out_shape = [hk, vt, tok(NA_W), vt]
        out_specs = [hk_spec, vt_spec, tok_spec(NA_W), vt_spec]
    return pl.pallas_call(
        functools.partial(_inproj_kernel, latent),
        out_shape=out_shape,
        grid=(B, n // tm),
        in_specs=in_specs,
        out_specs=out_specs,
        compiler_params=_cparams(("parallel", "parallel")),
        name="inproj_latent" if latent else "inproj_ctx",
    )(x, sc, sh, ck, sk, cq, sq, w_small, w_na, w_nvt, w_g, qg, kvg, wqa, wqb, wk, wv)


def _mla_kernel(tk, q_ref, kc_ref, kl_ref, vc_ref, vl_ref, o_ref, m_ref, l_ref, acc_ref, sa_ref, sb_ref, sc_ref):
    heads = q_ref.shape[1]
    n = kl_ref.shape[2]
    tq = q_ref.shape[2]
    nt = n // tk
    row = lax.broadcasted_iota(jnp.int32, (LANES, tq), 0)

    def scores(h, k):
        return _dot_nt(k, q_ref[0, h])

    def absorb(hh, s, v_t):
        m_old = m_ref[hh]
        m_new = jnp.maximum(m_old, jnp.max(s, axis=0, keepdims=True))
        a = jnp.exp2(m_old - m_new)
        p = jnp.exp2(s - m_new)
        l_ref[hh] = a * l_ref[hh] + jnp.sum(p, axis=0, keepdims=True)
        acc_ref[hh] = a * acc_ref[hh] + _dot(v_t, p.astype(BF16))
        m_ref[hh] = m_new

    def ctx_scores(h0):
        for hh in range(2):
            sc_ref[hh] = scores(h0 + hh, kc_ref[0, h0 + hh])

    def pair_pass(pair, carry):
        h0 = 2 * pair
        rows = pl.ds(pl.multiple_of(pair * LANES, LANES), LANES)

        def k_tile(hh, j):
            return kl_ref[0, h0 + hh, pl.ds(pl.multiple_of(j * tk, tk), tk), :]

        def v_tile(j):
            return vl_ref[0, rows, pl.ds(pl.multiple_of(j * tk, tk), tk)]

        def advance(j, cur_ref, next_ref):
            v_t = v_tile(j)
            for hh in range(2):
                next_ref[hh] = scores(h0 + hh, k_tile(hh, j + 1))
                absorb(hh, cur_ref[hh], v_t)

        m_ref[...] = jnp.full(m_ref.shape, NEG_BIG, F32)
        l_ref[...] = jnp.zeros(l_ref.shape, F32)
        acc_ref[...] = jnp.zeros(acc_ref.shape, F32)
        v_ctx = vc_ref[0, rows, :]
        for hh in range(2):
            sa_ref[hh] = scores(h0 + hh, k_tile(hh, 0))
            absorb(hh, sc_ref[hh], v_ctx)

        def body(jj, c):
            advance(2 * jj, sa_ref, sb_ref)
            advance(2 * jj + 1, sb_ref, sa_ref)
            return c

        lax.fori_loop(0, (nt - 1) // 2, body, 0)
        last_ref = sa_ref
        if (nt - 1) % 2:
            advance(nt - 2, sa_ref, sb_ref)
            last_ref = sb_ref
        ctx_scores(jnp.minimum(h0 + 2, heads - 2))
        v_t = v_tile(nt - 1)
        for hh in range(2):
            absorb(hh, last_ref[hh], v_t)
        out_t = jnp.where(row < MLA_V, acc_ref[0] / l_ref[0], acc_ref[1] / l_ref[1])
        o_ref[0, :, rows] = out_t.T.astype(o_ref.dtype)
        return carry

    ctx_scores(0)
    lax.fori_loop(0, heads // 2, pair_pass, 0)


def _mla_attention(q, k_lat, k_ctx, v_lat, v_ctx, tq, tk):
    B, H, n, _ = q.shape
    L = k_ctx.shape[2]
    W = H * MLA_V
    return pl.pallas_call(
        functools.partial(_mla_kernel, tk),
        out_shape=jax.ShapeDtypeStruct((B, n, W), BF16),
        grid=(B, n // tq),
        in_specs=[pl.BlockSpec((1, H, tq, HEAD_PAD), lambda b, i: (b, 0, i, 0)),
                  pl.BlockSpec((1, H, L, HEAD_PAD), lambda b, i: (b, 0, 0, 0)),
                  pl.BlockSpec((1, H, n, HEAD_PAD), lambda b, i: (b, 0, 0, 0)),
                  pl.BlockSpec((1, W, L), lambda b, i: (b, 0, 0)),
                  pl.BlockSpec((1, W, n), lambda b, i: (b, 0, 0))],
        out_specs=pl.BlockSpec((1, tq, W), lambda b, i: (b, i, 0)),
        scratch_shapes=[pltpu.VMEM((2, 1, tq), F32), pltpu.VMEM((2, 1, tq), F32),
                        pltpu.VMEM((2, LANES, tq), F32),
                        pltpu.VMEM((2, tk, tq), F32), pltpu.VMEM((2, tk, tq), F32),
                        pltpu.VMEM((2, L, tq), F32)],
        compiler_params=_cparams(("parallel", "parallel")),
        name="mla_attention",
    )(q, k_ctx, k_lat, v_ctx, v_lat)


def _na_block_tables(rows):
    nblk = rows // NA_QROWS
    wh = min(NA_WIN_H, rows)
    band0 = np.clip(np.arange(nblk) * NA_QROWS - wh // 2, 0, rows - NA_BAND)
    sigs, types = [], []
    for i in range(nblk):
        r = i * NA_QROWS + np.arange(NA_QROWS)
        r0 = np.clip(r - wh // 2, 0, rows - wh)
        sig = (tuple(r0 - band0[i]), tuple(r - band0[i]))
        if sig not in sigs:
            sigs.append(sig)
        types.append(sigs.index(sig))
    return band0.astype(np.int32), np.asarray(types, np.int32), sigs, wh


def _na_bias_tiles(rel_bias, rows):
    _, _, sigs, wh = _na_block_tables(rows)
    ww = NA_WIN_W
    n_dr, n_dc = 2 * NA_WIN_H - 1, 2 * NA_WIN_W - 1
    col = np.arange(GRID_W)
    c0 = np.clip(col - ww // 2, 0, GRID_W - ww)
    col_ok = (col[None, :] >= c0[:, None]) & (col[None, :] < c0[:, None] + ww)
    dc = col[None, :] - col[:, None] + (NA_WIN_W - 1)
    pick_c = ((dc[None] == np.arange(n_dc)[:, None, None]) & col_ok[None]).astype(np.float32)
    kr = np.arange(NA_BAND)
    row_ok = np.stack([(kr[None, :] >= np.asarray(r0)[:, None]) & (kr[None, :] < np.asarray(r0)[:, None] + wh)
                       for r0, _ in sigs])
    dr = np.stack([kr[None, :] - np.asarray(r)[:, None] + (NA_WIN_H - 1) for _, r in sigs])
    pick_r = ((dr[..., None] == np.arange(n_dr)) & row_ok[..., None]).astype(np.float32)
    hp = lax.Precision.HIGHEST
    by_col = jnp.einsum('hdj,jxy->hdxy', rel_bias.astype(F32), jnp.asarray(pick_c), precision=hp)
    vals = jnp.einsum('tqkd,hdxy->thkyqx', jnp.asarray(pick_r), by_col, precision=hp)
    ok = np.transpose(row_ok, (0, 2, 1))[:, None, :, None, :, None] & col_ok.T[None, None, None, :, None, :]
    tiles = jnp.where(jnp.asarray(ok), vals * LOG2E, NEG_BIG)
    return tiles.reshape(len(sigs), NA_HEADS, NA_BAND * GRID_W, NA_QROWS * GRID_W)


def _na_kernel(nblk, band_ref, type_ref, q_ref, k_ref, vt_ref, kc_ref, vct_ref, *rest):
    bias_refs, o_ref = rest[:nblk], rest[nblk]
    i = pl.program_id(1)
    nq = NA_QROWS * GRID_W
    nk = NA_BAND * GRID_W
    starts = [pl.multiple_of(band_ref[i * nblk + b] * GRID_W, NA_QROWS * GRID_W) for b in range(nblk)]
    lane = lax.broadcasted_iota(jnp.int32, (nq, LANES), 1)
    row = lax.broadcasted_iota(jnp.int32, (LANES, nq), 0)

    def scores(c):
        blk, h = divmod(c, NA_HEADS)
        cs = slice((h // 2) * LANES, (h // 2 + 1) * LANES)
        qp = q_ref[0, blk * nq:(blk + 1) * nq, cs]
        own = (lane < NA_DIM) if h % 2 == 0 else (lane >= NA_DIM)
        qh = jnp.where(own, qp, jnp.zeros_like(qp))
        s_loc = _dot_nt(k_ref[0, pl.ds(starts[blk], nk), cs], qh) + bias_refs[blk][0, h]
        s_ctx = _dot_nt(kc_ref[0, :, cs], qh)
        return s_loc, s_ctx

    def attend(c, s):
        s_loc, s_ctx = s
        blk, h = divmod(c, NA_HEADS)
        start = starts[blk]
        cs = slice((h // 2) * LANES, (h // 2 + 1) * LANES)
        m = jnp.maximum(jnp.max(s_loc, axis=0, keepdims=True), jnp.max(s_ctx, axis=0, keepdims=True))
        p_loc = jnp.exp2(s_loc - m)
        p_ctx = jnp.exp2(s_ctx - m)
        l = jnp.sum(p_loc, axis=0, keepdims=True) + jnp.sum(p_ctx, axis=0, keepdims=True)
        o = _dot(vt_ref[0, cs, pl.ds(start, nk)], p_loc.astype(BF16)) + _dot(vct_ref[0, cs, :], p_ctx.astype(BF16))
        return o / l

    chain = nblk * NA_HEADS
    s = scores(0)
    outs = []
    for c in range(chain):
        s_next = scores(c + 1) if c + 1 < chain else None
        outs.append(attend(c, s))
        s = s_next
        if c % 2:
            blk, h = divmod(c, NA_HEADS)
            pair_t = jnp.where(row < NA_DIM, outs[c - 1], outs[c])
            o_ref[0, blk * nq:(blk + 1) * nq, (h // 2) * LANES:(h // 2 + 1) * LANES] = pair_t.T.astype(o_ref.dtype)


def _na_attention(nq, nk, nvt, cnk, cnvt, bias_tiles):
    B, n, W = nq.shape
    L = cnk.shape[1]
    rows = n // GRID_W
    band0, types, _, _ = _na_block_tables(rows)
    qn = NA_QROWS * GRID_W
    kn = NA_BAND * GRID_W
    nblocks = rows // NA_QROWS
    nblk = 2 if nblocks % 2 == 0 else 1

    def bias_spec(k):
        return pl.BlockSpec((1, NA_HEADS, kn, qn), lambda b, i, bd, ty: (ty[i * nblk + k], 0, 0, 0))

    grid_spec = pltpu.PrefetchScalarGridSpec(
        num_scalar_prefetch=2,
        grid=(B, nblocks // nblk),
        in_specs=[pl.BlockSpec((1, nblk * qn, W), lambda b, i, bd, ty: (b, i, 0)),
                  pl.BlockSpec((1, n, W), lambda b, i, bd, ty: (b, 0, 0)),
                  pl.BlockSpec((1, W, n), lambda b, i, bd, ty: (b, 0, 0)),
                  pl.BlockSpec((1, L, W), lambda b, i, bd, ty: (b, 0, 0)),
                  pl.BlockSpec((1, W, L), lambda b, i, bd, ty: (b, 0, 0))] + [bias_spec(k) for k in range(nblk)],
        out_specs=pl.BlockSpec((1, nblk * qn, W), lambda b, i, bd, ty: (b, i, 0)),
    )
    return pl.pallas_call(
        functools.partial(_na_kernel, nblk),
        out_shape=jax.ShapeDtypeStruct((B, n, W), BF16),
        grid_spec=grid_spec,
        compiler_params=_cparams(("parallel", "arbitrary")),
        name="na_attention",
    )(jnp.asarray(band0), jnp.asarray(types), nq, nk, nvt, cnk, cnvt, *([bias_tiles] * nblk))


def _split2(a):
    hi = a.astype(BF16)
    return hi, (a - hi.astype(F32)).astype(BF16)


def _merge_kernel(ym_ref, yn_ref, g_ref, x_ref, g1_ref, sc2_ref, sh2_ref, wpm_ref, wpn_ref, wo_ref,
                  l1g_ref, l1b_ref, wr_ref, xn_ref, up_ref, aff_ref):
    tm, D = x_ref.shape[1], x_ref.shape[2]
    s = D // LANES
    sub = min(MERGE_SUB, tm)
    wh, wl = _split2(wr_ref[...])

    def mix_of(r):
        g = g_ref[0, r, :]
        a = (g[:, :D].astype(F32) * _dot(ym_ref[0, r, :], wpm_ref[...])
             + g[:, D:].astype(F32) * _dot(yn_ref[0, r, :], wpn_ref[...]))
        return _dot(a.astype(BF16), wo_ref[...])

    def finish(r, mix):
        xn = _ln(ALPHA * x_ref[0, r, :] + g1_ref[0] * mix) * l1g_ref[...] + l1b_ref[...]
        xn_ref[0, r, :] = xn
        u2 = _ln(xn) * (1.0 + sc2_ref[0]) + sh2_ref[0]
        _store_rows_as_tiles(up_ref.at[0, r.start * s:r.stop * s], u2)
        uh, ul = _split2(u2)
        logits = _dot(uh, wh) + (_dot(uh, wl) + _dot(ul, wh))
        e = jnp.exp(logits - jnp.max(logits, axis=-1, keepdims=True))
        aff_ref[0, r, :] = e / jnp.sum(e, axis=-1, keepdims=True)

    rows = [slice(i, i + sub) for i in range(0, tm, sub)]
    mix = mix_of(rows[0])
    for i, r in enumerate(rows):
        mix_next = mix_of(rows[i + 1]) if i + 1 < len(rows) else None
        finish(r, mix)
        mix = mix_next


def _merge(y_mla, y_na, gates, x, g1, sc2, sh2, wpm, wpn, wo, l1g, l1b, wr, tm):
    B, n, D = x.shape
    E = wr.shape[1]
    full = lambda a: pl.BlockSpec(a.shape, lambda b, i: (0,) * a.ndim)
    tok = lambda w: pl.BlockSpec((1, tm, w), lambda b, i: (b, i, 0))
    mod = pl.BlockSpec((1, 1, D), lambda b, i: (b, 0, 0))
    return pl.pallas_call(
        _merge_kernel,
        out_shape=[jax.ShapeDtypeStruct((B, n, D), F32),
                   jax.ShapeDtypeStruct((B, n * (D // LANES), LANES), F32),
                   jax.ShapeDtypeStruct((B, n, E), F32)],
        grid=(B, n // tm),
        in_specs=[tok(y_mla.shape[2]), tok(y_na.shape[2]), tok(2 * D), tok(D), mod, mod, mod,
                  full(wpm), full(wpn), full(wo), full(l1g), full(l1b), full(wr)],
        out_specs=[tok(D), pl.BlockSpec((1, tm * (D // LANES), LANES), lambda b, i: (b, i, 0)), tok(E)],
        compiler_params=_cparams(("parallel", "parallel")),
        name="merge_ln_router",
    )(y_mla, y_na, gates, x, g1, sc2, sh2, wpm, wpn, wo, l1g, l1b, wr)


def _topk_kernel(cap, aff_ref, tri_ref, idx_ref, g_ref, pos_ref):
    aff = aff_ref[0]
    E, n = aff.shape

    def count(mask):
        return jnp.sum(mask.astype(F32), axis=-1, keepdims=True)

    def search(t, thr):
        cand = thr | (jnp.int32(1) << (30 - t))
        return jnp.where(count(aff >= pltpu.bitcast(cand, F32)) >= cap, cand, thr)

    thr = lax.fori_loop(0, 31, search, jnp.zeros((E, 1), jnp.int32))
    lo = pltpu.bitcast(thr, F32)
    hi = pltpu.bitcast(thr + 1, F32)

    def refine(t, lh):
        lo, hi = lh
        mid = lo + (hi - lo) * 0.5
        ok = count(aff >= mid) >= cap
        return jnp.where(ok, mid, lo), jnp.where(ok, hi, mid)

    lo, hi = lax.fori_loop(0, TOPK_REFINE, refine, (lo, hi))
    gt = aff >= hi
    eq = (aff >= lo) & ~gt
    need = cap - count(gt)

    tri = tri_ref[...]

    def prefix(mask):
        mf = mask.astype(F32).astype(BF16)
        parts = []
        off = jnp.zeros((E, 1), F32)
        for c in range(n // LANES):
            blk = mf[:, c * LANES:(c + 1) * LANES]
            parts.append(_dot(blk, tri) + off)
            off = off + jnp.sum(blk.astype(F32), axis=-1, keepdims=True)
        return jnp.concatenate(parts, axis=-1)

    sel = gt | (eq & (prefix(eq) < need))
    pos_ref[...] = jnp.where(sel, prefix(sel), -1.0)

    tok = lax.broadcasted_iota(jnp.int32, (1, n), 1)
    digits = [(tok // TOPK_RADIX).astype(F32), (tok % TOPK_RADIX).astype(F32)]
    chunks = cap // TOPK_SLOTS

    def per_expert(e, carry):
        a = aff_ref[0, pl.ds(e, 1), :]
        a1 = a.astype(BF16).astype(F32)
        a2 = (a - a1).astype(BF16).astype(F32)
        a3 = (a - a1) - a2
        rows = digits + [a1, a2, a3]
        table = jnp.concatenate(rows + [jnp.zeros((TOPK_TABLE_ROWS - len(rows), n), F32)], axis=0).astype(BF16)
        pos_e = pos_ref[pl.ds(e, 1), :]

        hits = []
        for ch in range(chunks):
            slot = (lax.broadcasted_iota(jnp.int32, (TOPK_SLOTS, n), 0) + ch * TOPK_SLOTS).astype(F32)
            hits.append(jnp.where(slot == pos_e, 1.0, 0.0).astype(BF16))
        for ch in range(chunks):
            r = _dot_nt(hits[ch], table)
            idx = r[:, 0:1] * TOPK_RADIX + r[:, 1:2]
            idx_ref[0, e, ch * TOPK_SLOTS:(ch + 1) * TOPK_SLOTS, :] = idx.astype(jnp.int32)
            g_ref[0, e, ch * TOPK_SLOTS:(ch + 1) * TOPK_SLOTS, :] = (r[:, 2:3] + r[:, 3:4]) + r[:, 4:5]
        return carry

    lax.fori_loop(0, E, per_expert, 0)


def _topk(aff_t, cap):
    B, E, n = aff_t.shape
    tri = jnp.asarray(np.triu(np.ones((LANES, LANES), np.float32), k=1), BF16)
    return pl.pallas_call(
        functools.partial(_topk_kernel, cap),
        out_shape=[jax.ShapeDtypeStruct((B, E, cap, 1), jnp.int32),
                   jax.ShapeDtypeStruct((B, E, cap, 1), F32)],
        grid=(B,),
        in_specs=[pl.BlockSpec((1, E, n), lambda b: (b, 0, 0)),
                  pl.BlockSpec((LANES, LANES), lambda b: (0, 0))],
        out_specs=[pl.BlockSpec((1, E, cap, 1), lambda b: (b, 0, 0, 0)),
                   pl.BlockSpec((1, E, cap, 1), lambda b: (b, 0, 0, 0))],
        scratch_shapes=[pltpu.VMEM((E, n), F32)],
        compiler_params=_cparams(("parallel",)),
        name="expert_topk",
    )(aff_t, tri)


def _gather_kernel(s, idx_ref, u_ref, o_ref):
    cap = o_ref.shape[2] // s

    def body(i, carry):
        c0 = pl.multiple_of(i * ROW_GROUP, ROW_GROUP)
        for k in range(ROW_GROUP):
            src = pl.multiple_of(idx_ref[0, 0, 0, c0 + k] * s, s)
            o_ref[0, 0, pl.ds((c0 + k) * s, s), :] = u_ref[0, pl.ds(src, s), :]
        return carry

    lax.fori_loop(0, cap // ROW_GROUP, body, 0)


def _gather(idx, u2, n):
    B, E, _, cap = idx.shape
    _, ns, W = u2.shape
    s = ns // n
    return pl.pallas_call(
        functools.partial(_gather_kernel, s),
        out_shape=jax.ShapeDtypeStruct((B, E, cap * s, W), u2.dtype),
        grid=(B, E),
        in_specs=[pl.BlockSpec((1, 1, 1, cap), lambda b, e: (b, e, 0, 0), memory_space=pltpu.SMEM),
                  pl.BlockSpec((1, ns, W), lambda b, e: (b, 0, 0))],
        out_specs=pl.BlockSpec((1, 1, cap * s, W), lambda b, e: (b, e, 0, 0)),
        compiler_params=_cparams(("parallel", "arbitrary")),
        name="expert_gather",
    )(idx, u2)


def _ffn_kernel(xe_ref, g_ref, wg_ref, wu_ref, wd_ref, o_ref, wgb_ref, wub_ref, wdb_ref):
    @pl.when(pl.program_id(1) == 0)
    def _():
        wgb_ref[...] = wg_ref[0].astype(BF16)
        wub_ref[...] = wu_ref[0].astype(BF16)
        wdb_ref[...] = wd_ref[0].astype(BF16)

    xe = _load_tiles_as_rows(xe_ref.at[0, 0], wgb_ref.shape[0]).astype(BF16)
    gate = _dot(xe, wgb_ref[...])
    up = _dot(xe, wub_ref[...])
    h = (gate * jax.nn.sigmoid(gate) * up).astype(BF16)
    _store_rows_as_tiles(o_ref.at[0, 0], _dot(h, wdb_ref[...]) * g_ref[0, 0])


def _ffn(xe, g, w_gate, w_up, w_down):
    B, E, caps, W = xe.shape
    _, D, F = w_gate.shape
    cap = g.shape[2]
    return pl.pallas_call(
        _ffn_kernel,
        out_shape=jax.ShapeDtypeStruct((B, E, caps, W), F32),
        grid=(E, B),
        in_specs=[pl.BlockSpec((1, 1, caps, W), lambda e, b: (b, e, 0, 0)),
                  pl.BlockSpec((1, 1, cap, 1), lambda e, b: (b, e, 0, 0)),
                  pl.BlockSpec((1, D, F), lambda e, b: (e, 0, 0)),
                  pl.BlockSpec((1, D, F), lambda e, b: (e, 0, 0)),
                  pl.BlockSpec((1, F, D), lambda e, b: (e, 0, 0))],
        out_specs=pl.BlockSpec((1, 1, caps, W), lambda e, b: (b, e, 0, 0)),
        scratch_shapes=[pltpu.VMEM((D, F), BF16), pltpu.VMEM((D, F), BF16), pltpu.VMEM((F, D), BF16)],
        compiler_params=_cparams(("arbitrary", "arbitrary")),
        name="expert_ffn",
    )(xe, g, w_gate, w_up, w_down)


def _scatter_kernel(s, idx_ref, y_ref, o_ref):
    cap = y_ref.shape[2] // s

    def tile(r):
        return pl.ds(pl.multiple_of(r * s, s), s)

    @pl.when(pl.program_id(1) == 0)
    def _():
        o_ref[...] = jnp.zeros_like(o_ref)

    def body(i, carry):
        c0 = pl.multiple_of(i * ROW_GROUP, ROW_GROUP)
        rows = [idx_ref[0, 0, 0, c0 + k] for k in range(ROW_GROUP)]
        acc = [o_ref[0, tile(r), :] for r in rows]
        for k in range(ROW_GROUP):
            o_ref[0, tile(rows[k]), :] = acc[k] + y_ref[0, 0, tile(c0 + k), :]
        return carry

    lax.fori_loop(0, cap // ROW_GROUP, body, 0)


def _scatter(idx, ye, n):
    B, E, caps, W = ye.shape
    cap = idx.shape[3]
    s = caps // cap
    return pl.pallas_call(
        functools.partial(_scatter_kernel, s),
        out_shape=jax.ShapeDtypeStruct((B, n * s, W), F32),
        grid=(B, E),
        in_specs=[pl.BlockSpec((1, 1, 1, cap), lambda b, e: (b, e, 0, 0), memory_space=pltpu.SMEM),
                  pl.BlockSpec((1, 1, caps, W), lambda b, e: (b, e, 0, 0))],
        out_specs=pl.BlockSpec((1, n * s, W), lambda b, e: (b, 0, 0)),
        compiler_params=_cparams(("parallel", "arbitrary")),
        name="expert_scatter",
    )(idx, ye)


def _final_kernel(x_ref, moe_ref, g2_ref, lg_ref, lb_ref, o_ref):
    moe = _load_tiles_as_rows(moe_ref.at[0], x_ref.shape[2])
    o_ref[0] = _ln(ALPHA * x_ref[0] + g2_ref[0] * moe) * lg_ref[...] + lb_ref[...]


def _final(xn, moe, g2, lg, lb, tm):
    B, n, D = xn.shape
    tok = pl.BlockSpec((1, tm, D), lambda b, i: (b, i, 0))
    vec = pl.BlockSpec((1, D), lambda b, i: (0, 0))
    return pl.pallas_call(
        _final_kernel,
        out_shape=jax.ShapeDtypeStruct((B, n, D), F32),
        grid=(B, n // tm),
        in_specs=[tok, pl.BlockSpec((1, tm * (D // LANES), LANES), lambda b, i: (b, i, 0)),
                  pl.BlockSpec((1, 1, D), lambda b, i: (b, 0, 0)), vec, vec],
        out_specs=tok,
        compiler_params=_cparams(("parallel", "parallel")),
        name="final_ln",
    )(xn, moe, g2, lg, lb)


def _rope_tables(n):
    t = np.arange(n)
    row = (t // GRID_W).astype(np.float32)
    col = (t % GRID_W).astype(np.float32)
    per_axis = MLA_ROPE // 2
    inv_freq = jnp.asarray(ROPE_THETA, F32) ** (-jnp.arange(0, per_axis, 2, dtype=F32) / per_axis)
    ang = jnp.concatenate([jnp.asarray(row)[:, None] * inv_freq, jnp.asarray(col)[:, None] * inv_freq], axis=-1)
    cos, sin = jnp.cos(ang), jnp.sin(ang)
    pad = HEAD_PAD - MLA_QK
    c_tab = jnp.concatenate([jnp.ones((n, MLA_NOPE), F32), cos, cos, jnp.zeros((n, pad), F32)], axis=-1)
    s_tab = jnp.concatenate([jnp.zeros((n, MLA_NOPE), F32), -sin, sin, jnp.zeros((n, pad), F32)], axis=-1)
    return c_tab, s_tab


def _swap_halves(w):
    half = w.shape[-1] // 2
    return jnp.concatenate([w[..., half:], w[..., :half]], axis=-1)


def _prep_weights(w_in, w_uq, w_ukv):
    D = w_in.shape[0]
    o1 = Q_LORA
    o2 = o1 + KV_LORA
    o3 = o2 + MLA_ROPE
    o4 = o3 + 3 * NA_W
    w_kr = w_in[:, o2:o3]
    z = lambda k: jnp.zeros((D, k), w_in.dtype)
    pad = HEAD_PAD - MLA_QK
    kra = jnp.concatenate([z(MLA_NOPE), w_kr, z(pad)], axis=-1)
    krb = jnp.concatenate([z(MLA_NOPE), _swap_halves(w_kr), z(pad)], axis=-1)
    w_small = jnp.concatenate([w_in[:, :o2], kra, krb], axis=-1).astype(BF16)
    w_na = w_in[:, o3:o3 + 2 * NA_W].astype(BF16)
    w_nvt = w_in[:, o3 + 2 * NA_W:o4].T.astype(BF16)
    w_g = w_in[:, o4:].astype(BF16)

    uq = w_uq.reshape(Q_LORA, MLA_HEADS, MLA_QK)
    zq = lambda k: jnp.zeros((Q_LORA, MLA_HEADS, k), w_uq.dtype)
    wqa = jnp.concatenate([uq, zq(pad)], axis=-1).reshape(Q_LORA, MLA_HEADS * HEAD_PAD).astype(BF16)
    wqb = jnp.concatenate([zq(MLA_NOPE), _swap_halves(uq[..., MLA_NOPE:]), zq(pad)], axis=-1)
    wqb = wqb.reshape(Q_LORA, MLA_HEADS * HEAD_PAD).astype(BF16)

    ukv = w_ukv.reshape(KV_LORA, MLA_HEADS, MLA_NOPE + MLA_V)
    wk = jnp.concatenate([ukv[..., :MLA_NOPE], jnp.zeros((KV_LORA, MLA_HEADS, HEAD_PAD - MLA_NOPE), w_ukv.dtype)], axis=-1)
    wk = wk.reshape(KV_LORA, MLA_HEADS * HEAD_PAD).astype(BF16)
    wv = ukv[..., MLA_NOPE:].reshape(KV_LORA, MLA_HEADS * MLA_V).T.astype(BF16)
    return w_small, w_na, w_nvt, w_g, wqa, wqb, wk, wv


def _tile(n, pref):
    t = min(pref, n)
    while n % t:
        t //= 2
    return t


def kernel(x, c, ctx, c_ctx, w_mod, b_mod, w_in, q_norm_g, w_uq, kv_norm_g, w_ukv, na_rel_bias, w_proj_mla,
           w_proj_na, w_out, ln1_g, ln1_b, w_router, w_exp_gate, w_exp_up, w_exp_down, ln2_g, ln2_b):
    B, n, D = x.shape
    L = ctx.shape[1]
    rows = n // GRID_W
    assert n % (GRID_W * NA_QROWS) == 0 and rows >= NA_BAND
    assert w_mod.shape[0] == DEPTH
    cap = EC_CAPACITY * n // N_EXPERTS

    mod_rows = -(-(B + 1) // 8) * 8
    cc = jnp.concatenate([c, c_ctx[None], jnp.zeros((mod_rows - B - 1, D), F32)], axis=0)
    m = _modulation(cc, w_mod[0], b_mod[0])
    sh1, sc1, g1, sh2, sc2, g2 = [m[:B, k * D:(k + 1) * D].reshape(B, 1, D) for k in range(6)]
    csh1 = m[B:B + 1, :D].reshape(1, 1, D)
    csc1 = m[B:B + 1, D:2 * D].reshape(1, 1, D)

    w_small, w_na, w_nvt, w_g, wqa, wqb, wk, wv = _prep_weights(w_in[0], w_uq[0], w_ukv[0])
    qg = q_norm_g[0].reshape(1, Q_LORA)
    kvg = kv_norm_g[0].reshape(1, KV_LORA)
    c_tab, s_tab = _rope_tables(n)
    pad = HEAD_PAD - MLA_QK
    c_ctx_tab = jnp.concatenate([jnp.ones((L, MLA_QK), F32), jnp.zeros((L, pad), F32)], axis=-1)
    s_ctx_tab = jnp.zeros((L, HEAD_PAD), F32)

    tm = _tile(n, 256)
    q, k_lat, v_lat, nq, nk, nv, gates = _inproj(
        True, x, sc1, sh1, c_tab, s_tab, c_tab * (MLA_SCALE * LOG2E), s_tab * (MLA_SCALE * LOG2E),
        w_small, w_na, w_nvt, w_g, qg, kvg, wqa, wqb, wk, wv, _tile(n, 512))
    k_ctx, v_ctx, cnk, cnv = _inproj(
        False, ctx, csc1, csh1, c_ctx_tab, s_ctx_tab, c_ctx_tab, s_ctx_tab,
        w_small, w_na[:, NA_W:], w_nvt, w_g, qg, kvg, wqa, wqb, wk, wv, _tile(L, 256))

    y_mla = _mla_attention(q, k_lat, k_ctx, v_lat, v_ctx, _tile(n, 1024), _tile(n, 512))
    y_na = _na_attention(nq, nk, nv, cnk, cnv, _na_bias_tiles(na_rel_bias[0], rows))

    xn, u2, aff = _merge(
        y_mla, y_na, gates, x, g1, sc2, sh2,
        w_proj_mla[0].astype(BF16), w_proj_na[0].astype(BF16), w_out[0].astype(BF16),
        ln1_g[0].reshape(1, D), ln1_b[0].reshape(1, D), w_router[0], _tile(n, 2 * MERGE_SUB))

    idx4, g4 = _topk(jnp.swapaxes(aff, 1, 2), cap)
    idx = idx4.reshape(B, N_EXPERTS, 1, cap)
    xe = _gather(idx, u2, n)
    ye = _ffn(xe, g4, w_exp_gate[0], w_exp_up[0], w_exp_down[0])
    moe = _scatter(idx, ye, n)
    return _final(xn, moe, g2, ln2_g[0].reshape(1, D), ln2_b[0].reshape(1, D), tm)
```

```python
import functools
import math

import numpy as np
import jax
import jax.numpy as jnp
from jax import lax
from jax.experimental import pallas as pl
from jax.experimental.pallas import tpu as pltpu

GRID_W = 64
MLA_HEADS = 8
MLA_NOPE = 64
MLA_ROPE = 32
MLA_QK = MLA_NOPE + MLA_ROPE
MLA_V = 64
Q_LORA = 256
KV_LORA = 128
MLA_SCALE = MLA_QK ** -0.5
ROPE_THETA = 10000.0
NA_HEADS = 8
NA_DIM = 64
NA_W = NA_HEADS * NA_DIM
NA_WIN_H = 8
NA_WIN_W = 16
NA_SCALE = NA_DIM ** -0.5
N_EXPERTS = 16
EC_CAPACITY = 2
LN_EPS = 1e-5
RMS_EPS = 1e-6
DEPTH = 1
ALPHA = (2.0 * DEPTH) ** 0.25
LOG2E = math.log2(math.e)

LANES = 128
HEAD_PAD = LANES
VMEM_LIMIT = 56 * 1024 * 1024
NEG_BIG = -1e30

NA_QROWS = 4
NA_BAND = 12
ROW_GROUP = 8
NA_LOOKAHEAD = 2
MERGE_SUB = 256
TOPK_RADIX = 64
TOPK_TABLE_ROWS = 16
TOPK_SLOTS = 128
TOPK_REFINE = 30

BF16 = jnp.bfloat16
F32 = jnp.float32


def _cparams(sem):
    return pltpu.CompilerParams(dimension_semantics=sem, vmem_limit_bytes=VMEM_LIMIT)


def _ln(x):
    mu = jnp.mean(x, axis=-1, keepdims=True)
    xc = x - mu
    var = jnp.mean(xc * xc, axis=-1, keepdims=True)
    return xc * lax.rsqrt(var + LN_EPS)


def _dot(a, b):
    return jnp.dot(a, b, preferred_element_type=F32)


def _dot_nt(a, b):
    return lax.dot_general(a, b, (((1,), (1,)), ((), ())), preferred_element_type=F32)


def _store_rows_as_tiles(ref, val):
    m, d = val.shape
    s = d // LANES
    for j in range(s):
        ref[pl.ds(j, m, stride=s), :] = val[:, j * LANES:(j + 1) * LANES]


def _load_tiles_as_rows(ref, d):
    s = d // LANES
    m = ref.shape[0] // s
    return jnp.concatenate([ref[pl.ds(j, m, stride=s), :] for j in range(s)], axis=-1)


def _mod_kernel(c_ref, w_ref, b_ref, o_ref):
    c = c_ref[...]
    s = c * jax.nn.sigmoid(c)
    o_ref[...] = jnp.dot(s, w_ref[...], preferred_element_type=F32,
                         precision=lax.Precision.HIGHEST) + b_ref[...]


def _modulation(cc, w_mod, b_mod):
    rows, d = cc.shape
    n_out = w_mod.shape[1]
    tn = 1024
    return pl.pallas_call(
        _mod_kernel,
        out_shape=jax.ShapeDtypeStruct((rows, n_out), F32),
        grid=(n_out // tn,),
        in_specs=[pl.BlockSpec((rows, d), lambda j: (0, 0)),
                  pl.BlockSpec((d, tn), lambda j: (0, j)),
                  pl.BlockSpec((1, tn), lambda j: (0, j))],
        out_specs=pl.BlockSpec((rows, tn), lambda j: (0, j)),
        compiler_params=_cparams(("arbitrary",)),
        name="modulation",
    )(cc, w_mod, b_mod.reshape(1, n_out))


def _inproj_kernel(latent, x_ref, sc_ref, sh_ref, ck_ref, sk_ref, cq_ref, sq_ref,
                   w_small_ref, w_na_ref, w_nvt_ref, w_g_ref, qg_ref, kvg_ref, wqa_ref, wqb_ref, wk_ref, wv_ref,
                   *out_refs):
    if latent:
        q_ref, k_ref, v_ref, nq_ref, nk_ref, nv_ref, g_ref = out_refs
    else:
        k_ref, v_ref, nk_ref, nv_ref = out_refs
    x = x_ref[0]
    u = (_ln(x) * (1.0 + sc_ref[0]) + sh_ref[0]).astype(BF16)

    small = _dot(u, w_small_ref[...])
    q_c = small[:, :Q_LORA]
    kv_c = small[:, Q_LORA:Q_LORA + KV_LORA]
    kra = small[:, Q_LORA + KV_LORA:Q_LORA + KV_LORA + LANES]
    krb = small[:, Q_LORA + KV_LORA + LANES:]

    kvn = (kv_c * lax.rsqrt(jnp.mean(kv_c * kv_c, axis=-1, keepdims=True) + RMS_EPS) * kvg_ref[...]).astype(BF16)
    kk = _dot(kvn, wk_ref[...])
    kr = kra * ck_ref[...] + krb * sk_ref[...]
    for h in range(MLA_HEADS):
        k_ref[0, h] = (kk[:, h * HEAD_PAD:(h + 1) * HEAD_PAD] + kr).astype(BF16)
    v_ref[0] = _dot_nt(wv_ref[...], kvn).astype(BF16)

    na = _dot(u, w_na_ref[...])
    nv_ref[0] = _dot_nt(w_nvt_ref[...], u).astype(BF16)
    if latent:
        qn = (q_c * lax.rsqrt(jnp.mean(q_c * q_c, axis=-1, keepdims=True) + RMS_EPS) * qg_ref[...]).astype(BF16)
        qa = _dot(qn, wqa_ref[...])
        qb = _dot(qn, wqb_ref[...])
        cq = cq_ref[...]
        sq = sq_ref[...]
        for h in range(MLA_HEADS):
            sl = slice(h * HEAD_PAD, (h + 1) * HEAD_PAD)
            q_ref[0, h] = (qa[:, sl] * cq + qb[:, sl] * sq).astype(BF16)
        nq_ref[0] = (na[:, :NA_W] * (NA_SCALE * LOG2E)).astype(BF16)
        nk_ref[0] = na[:, NA_W:].astype(BF16)
        g_ref[0] = jax.nn.sigmoid(_dot(u, w_g_ref[...])).astype(BF16)
    else:
        nk_ref[0] = na.astype(BF16)


def _inproj(latent, x, sc, sh, ck, sk, cq, sq, w_small, w_na, w_nvt, w_g, qg, kvg, wqa, wqb, wk, wv, tm):
    B, n, D = x.shape
    per_batch = sc.shape[0] > 1
    mod_map = (lambda b, i: (b, 0, 0)) if per_batch else (lambda b, i: (0, 0, 0))
    full = lambda a: pl.BlockSpec(a.shape, lambda b, i: (0,) * a.ndim)
    tab = lambda a: pl.BlockSpec((tm, a.shape[1]), lambda b, i: (i, 0))
    in_specs = [pl.BlockSpec((1, tm, D), lambda b, i: (b, i, 0)),
                pl.BlockSpec((1, 1, D), mod_map), pl.BlockSpec((1, 1, D), mod_map),
                tab(ck), tab(sk), tab(cq), tab(sq),
                full(w_small), full(w_na), full(w_nvt), full(w_g), full(qg), full(kvg),
                full(wqa), full(wqb), full(wk), full(wv)]
    hk = jax.ShapeDtypeStruct((B, MLA_HEADS, n, HEAD_PAD), BF16)
    hk_spec = pl.BlockSpec((1, MLA_HEADS, tm, HEAD_PAD), lambda b, i: (b, 0, i, 0))
    tok = lambda w: jax.ShapeDtypeStruct((B, n, w), BF16)
    tok_spec = lambda w: pl.BlockSpec((1, tm, w), lambda b, i: (b, i, 0))
    vt = jax.ShapeDtypeStruct((B, MLA_HEADS * MLA_V, n), BF16)
    vt_spec = pl.BlockSpec((1, MLA_HEADS * MLA_V, tm), lambda b, i: (b, 0, i))
    if latent:
        out_shape = [hk, hk, vt, tok(NA_W), tok(NA_W), vt, tok(2 * D)]
        out_specs = [hk_spec, hk_spec, vt_spec, tok_spec(NA_W), tok_spec(NA_W), vt_spec, tok_spec(2 * D)]
    else:
        out_shape = [hk, vt, tok(NA_W), vt]
        out_specs = [hk_spec, vt_spec, tok_spec(NA_W), vt_spec]
    return pl.pallas_call(
        functools.partial(_inproj_kernel, latent),
        out_shape=out_shape,
        grid=(B, n // tm),
        in_specs=in_specs,
        out_specs=out_specs,
        compiler_params=_cparams(("parallel", "parallel")),
        name="inproj_latent" if latent else "inproj_ctx",
    )(x, sc, sh, ck, sk, cq, sq, w_small, w_na, w_nvt, w_g, qg, kvg, wqa, wqb, wk, wv)


def _mla_kernel(tk, q_ref, kc_ref, kl_ref, vc_ref, vl_ref, o_ref, m_ref, l_ref, acc_ref, sa_ref, sb_ref, sc_ref):
    heads = q_ref.shape[1]
    n = kl_ref.shape[2]
    tq = q_ref.shape[2]
    nt = n // tk
    row = lax.broadcasted_iota(jnp.int32, (LANES, tq), 0)

    def scores(h, k):
        return _dot_nt(k, q_ref[0, h])

    def absorb(hh, s, v_t):
        m_old = m_ref[hh]
        m_new = jnp.maximum(m_old, jnp.max(s, axis=0, keepdims=True))
        a = jnp.exp2(m_old - m_new)
        p = jnp.exp2(s - m_new)
        l_ref[hh] = a * l_ref[hh] + jnp.sum(p, axis=0, keepdims=True)
        acc_ref[hh] = a * acc_ref[hh] + _dot(v_t, p.astype(BF16))
        m_ref[hh] = m_new

    def ctx_scores(h0):
        for hh in range(2):
            sc_ref[hh] = scores(h0 + hh, kc_ref[0, h0 + hh])

    def pair_pass(pair, carry):
        h0 = 2 * pair
        rows = pl.ds(pl.multiple_of(pair * LANES, LANES), LANES)

        def k_tile(hh, j):
            return kl_ref[0, h0 + hh, pl.ds(pl.multiple_of(j * tk, tk), tk), :]

        def v_tile(j):
            return vl_ref[0, rows, pl.ds(pl.multiple_of(j * tk, tk), tk)]

        def advance(j, cur_ref, next_ref):
            v_t = v_tile(j)
            for hh in range(2):
                next_ref[hh] = scores(h0 + hh, k_tile(hh, j + 1))
                absorb(hh, cur_ref[hh], v_t)

        m_ref[...] = jnp.full(m_ref.shape, NEG_BIG, F32)
        l_ref[...] = jnp.zeros(l_ref.shape, F32)
        acc_ref[...] = jnp.zeros(acc_ref.shape, F32)
        v_ctx = vc_ref[0, rows, :]
        for hh in range(2):
            sa_ref[hh] = scores(h0 + hh, k_tile(hh, 0))
            absorb(hh, sc_ref[hh], v_ctx)

        def body(jj, c):
            advance(2 * jj, sa_ref, sb_ref)
            advance(2 * jj + 1, sb_ref, sa_ref)
            return c

        lax.fori_loop(0, (nt - 1) // 2, body, 0)
        last_ref = sa_ref
        if (nt - 1) % 2:
            advance(nt - 2, sa_ref, sb_ref)
            last_ref = sb_ref
        ctx_scores(jnp.minimum(h0 + 2, heads - 2))
        v_t = v_tile(nt - 1)
        for hh in range(2):
            absorb(hh, last_ref[hh], v_t)
        out_t = jnp.where(row < MLA_V, acc_ref[0] / l_ref[0], acc_ref[1] / l_ref[1])
        o_ref[0, :, rows] = out_t.T.astype(o_ref.dtype)
        return carry

    ctx_scores(0)
    lax.fori_loop(0, heads // 2, pair_pass, 0)


def _mla_attention(q, k_lat, k_ctx, v_lat, v_ctx, tq, tk):
    B, H, n, _ = q.shape
    L = k_ctx.shape[2]
    W = H * MLA_V
    return pl.pallas_call(
        functools.partial(_mla_kernel, tk),
        out_shape=jax.ShapeDtypeStruct((B, n, W), BF16),
        grid=(B, n // tq),
        in_specs=[pl.BlockSpec((1, H, tq, HEAD_PAD), lambda b, i: (b, 0, i, 0)),
                  pl.BlockSpec((1, H, L, HEAD_PAD), lambda b, i: (b, 0, 0, 0)),
                  pl.BlockSpec((1, H, n, HEAD_PAD), lambda b, i: (b, 0, 0, 0)),
                  pl.BlockSpec((1, W, L), lambda b, i: (b, 0, 0)),
                  pl.BlockSpec((1, W, n), lambda b, i: (b, 0, 0))],
        out_specs=pl.BlockSpec((1, tq, W), lambda b, i: (b, i, 0)),
        scratch_shapes=[pltpu.VMEM((2, 1, tq), F32), pltpu.VMEM((2, 1, tq), F32),
                        pltpu.VMEM((2, LANES, tq), F32),
                        pltpu.VMEM((2, tk, tq), F32), pltpu.VMEM((2, tk, tq), F32),
                        pltpu.VMEM((2, L, tq), F32)],
        compiler_params=_cparams(("parallel", "parallel")),
        name="mla_attention",
    )(q, k_ctx, k_lat, v_ctx, v_lat)


def _na_block_tables(rows):
    nblk = rows // NA_QROWS
    wh = min(NA_WIN_H, rows)
    band0 = np.clip(np.arange(nblk) * NA_QROWS - wh // 2, 0, rows - NA_BAND)
    sigs, types = [], []
    for i in range(nblk):
        r = i * NA_QROWS + np.arange(NA_QROWS)
        r0 = np.clip(r - wh // 2, 0, rows - wh)
        sig = (tuple(r0 - band0[i]), tuple(r - band0[i]))
        if sig not in sigs:
            sigs.append(sig)
        types.append(sigs.index(sig))
    return band0.astype(np.int32), np.asarray(types, np.int32), sigs, wh


def _bias_tile_kernel(dr, ok, tbl_ref, o_ref):
    neg = jnp.full((GRID_W, GRID_W), NEG_BIG, F32)
    for t in range(o_ref.shape[0]):
        for k in range(NA_BAND):
            pieces = [tbl_ref[0, int(dr[t, q, k])] if ok[t, q, k] else neg for q in range(NA_QROWS)]
            o_ref[t, 0, k * GRID_W:(k + 1) * GRID_W, :] = jnp.concatenate(pieces, axis=1)


def _na_bias_tiles(rel_bias, rows):
    _, _, sigs, wh = _na_block_tables(rows)
    ww = NA_WIN_W
    n_dr, n_dc = 2 * NA_WIN_H - 1, 2 * NA_WIN_W - 1
    col = np.arange(GRID_W)
    c0 = np.clip(col - ww // 2, 0, GRID_W - ww)
    col_ok = (col[:, None] >= c0[None, :]) & (col[:, None] < c0[None, :] + ww)
    dc = col[:, None] - col[None, :] + (NA_WIN_W - 1)
    pick_c = ((dc[None] == np.arange(n_dc)[:, None, None]) & col_ok[None]).astype(np.float32)
    by_col = jnp.einsum('hdj,jyx->hdyx', rel_bias.astype(F32), jnp.asarray(pick_c), precision=lax.Precision.HIGHEST)
    by_col = jnp.where(jnp.asarray(col_ok), by_col * LOG2E, NEG_BIG)
    kr = np.arange(NA_BAND)
    ok = np.stack([(kr[None, :] >= np.asarray(r0)[:, None]) & (kr[None, :] < np.asarray(r0)[:, None] + wh)
                   for r0, _ in sigs])
    dr = np.stack([kr[None, :] - np.asarray(r)[:, None] + (NA_WIN_H - 1) for _, r in sigs])
    assert ((dr >= 0) & (dr < n_dr))[ok].all()
    T = len(sigs)
    kn, qn = NA_BAND * GRID_W, NA_QROWS * GRID_W
    return pl.pallas_call(
        functools.partial(_bias_tile_kernel, dr, ok),
        out_shape=jax.ShapeDtypeStruct((T, NA_HEADS, kn, qn), F32),
        grid=(NA_HEADS,),
        in_specs=[pl.BlockSpec((1, n_dr, GRID_W, GRID_W), lambda h: (h, 0, 0, 0))],
        out_specs=pl.BlockSpec((T, 1, kn, qn), lambda h: (0, h, 0, 0)),
        compiler_params=_cparams(("parallel",)),
        name="na_bias_tiles",
    )(by_col)


def _na_kernel(nblk, band_ref, type_ref, q_ref, k_ref, vt_ref, kc_ref, vct_ref, *rest):
    bias_refs, o_ref = rest[:nblk], rest[nblk]
    i = pl.program_id(1)
    nq = NA_QROWS * GRID_W
    nk = NA_BAND * GRID_W
    starts = [pl.multiple_of(band_ref[i * nblk + b] * GRID_W, NA_QROWS * GRID_W) for b in range(nblk)]
    lane = lax.broadcasted_iota(jnp.int32, (nq, LANES), 1)
    row = lax.broadcasted_iota(jnp.int32, (LANES, nq), 0)

    def scores(c):
        blk, h = divmod(c, NA_HEADS)
        cs = slice((h // 2) * LANES, (h // 2 + 1) * LANES)
        qp = q_ref[0, blk * nq:(blk + 1) * nq, cs]
        own = (lane < NA_DIM) if h % 2 == 0 else (lane >= NA_DIM)
        qh = jnp.where(own, qp, jnp.zeros_like(qp))
        s_loc = _dot_nt(k_ref[0, pl.ds(starts[blk], nk), cs], qh) + bias_refs[blk][0, h]
        s_ctx = _dot_nt(kc_ref[0, :, cs], qh)
        return s_loc, s_ctx

    def attend(c, s):
        s_loc, s_ctx = s
        blk, h = divmod(c, NA_HEADS)
        start = starts[blk]
        cs = slice((h // 2) * LANES, (h // 2 + 1) * LANES)
        m = jnp.maximum(jnp.max(s_loc, axis=0, keepdims=True), jnp.max(s_ctx, axis=0, keepdims=True))
        p_loc = jnp.exp2(s_loc - m)
        p_ctx = jnp.exp2(s_ctx - m)
        l = jnp.sum(p_loc, axis=0, keepdims=True) + jnp.sum(p_ctx, axis=0, keepdims=True)
        o = _dot(vt_ref[0, cs, pl.ds(start, nk)], p_loc.astype(BF16)) + _dot(vct_ref[0, cs, :], p_ctx.astype(BF16))
        return o / l

    chain = nblk * NA_HEADS
    ahead = [scores(c) for c in range(min(NA_LOOKAHEAD, chain))]
    outs = []
    for c in range(chain):
        if c + NA_LOOKAHEAD < chain:
            ahead.append(scores(c + NA_LOOKAHEAD))
        outs.append(attend(c, ahead.pop(0)))
        if c % 2:
            blk, h = divmod(c, NA_HEADS)
            pair_t = jnp.where(row < NA_DIM, outs[c - 1], outs[c])
            o_ref[0, blk * nq:(blk + 1) * nq, (h // 2) * LANES:(h // 2 + 1) * LANES] = pair_t.T.astype(o_ref.dtype)


def _na_attention(nq, nk, nvt, cnk, cnvt, bias_tiles):
    B, n, W = nq.shape
    L = cnk.shape[1]
    rows = n // GRID_W
    band0, types, _, _ = _na_block_tables(rows)
    qn = NA_QROWS * GRID_W
    kn = NA_BAND * GRID_W
    nblocks = rows // NA_QROWS
    nblk = 2 if nblocks % 2 == 0 else 1

    def bias_spec(k):
        return pl.BlockSpec((1, NA_HEADS, kn, qn), lambda b, i, bd, ty: (ty[i * nblk + k], 0, 0, 0))

    grid_spec = pltpu.PrefetchScalarGridSpec(
        num_scalar_prefetch=2,
        grid=(B, nblocks // nblk),
        in_specs=[pl.BlockSpec((1, nblk * qn, W), lambda b, i, bd, ty: (b, i, 0)),
                  pl.BlockSpec((1, n, W), lambda b, i, bd, ty: (b, 0, 0)),
                  pl.BlockSpec((1, W, n), lambda b, i, bd, ty: (b, 0, 0)),
                  pl.BlockSpec((1, L, W), lambda b, i, bd, ty: (b, 0, 0)),
                  pl.BlockSpec((1, W, L), lambda b, i, bd, ty: (b, 0, 0))] + [bias_spec(k) for k in range(nblk)],
        out_specs=pl.BlockSpec((1, nblk * qn, W), lambda b, i, bd, ty: (b, i, 0)),
    )
    return pl.pallas_call(
        functools.partial(_na_kernel, nblk),
        out_shape=jax.ShapeDtypeStruct((B, n, W), BF16),
        grid_spec=grid_spec,
        compiler_params=_cparams(("parallel", "arbitrary")),
        name="na_attention",
    )(jnp.asarray(band0), jnp.asarray(types), nq, nk, nvt, cnk, cnvt, *([bias_tiles] * nblk))


def _split2(a):
    hi = a.astype(BF16)
    return hi, (a - hi.astype(F32)).astype(BF16)


def _merge_kernel(ym_ref, yn_ref, g_ref, x_ref, g1_ref, sc2_ref, sh2_ref, wpm_ref, wpn_ref, wo_ref,
                  l1g_ref, l1b_ref, wr_ref, xn_ref, up_ref, aff_ref):
    tm, D = x_ref.shape[1], x_ref.shape[2]
    s = D // LANES
    sub = min(MERGE_SUB, tm)
    wh, wl = _split2(wr_ref[...])

    def mix_of(r):
        g = g_ref[0, r, :]
        a = (g[:, :D].astype(F32) * _dot(ym_ref[0, r, :], wpm_ref[...])
             + g[:, D:].astype(F32) * _dot(yn_ref[0, r, :], wpn_ref[...]))
        return _dot(a.astype(BF16), wo_ref[...])

    def finish(r, mix):
        xn = _ln(ALPHA * x_ref[0, r, :] + g1_ref[0] * mix) * l1g_ref[...] + l1b_ref[...]
        xn_ref[0, r, :] = xn
        u2 = _ln(xn) * (1.0 + sc2_ref[0]) + sh2_ref[0]
        _store_rows_as_tiles(up_ref.at[0, r.start * s:r.stop * s], u2)
        uh, ul = _split2(u2)
        logits = _dot(uh, wh) + (_dot(uh, wl) + _dot(ul, wh))
        e = jnp.exp(logits - jnp.max(logits, axis=-1, keepdims=True))
        aff_ref[0, r, :] = e / jnp.sum(e, axis=-1, keepdims=True)

    rows = [slice(i, i + sub) for i in range(0, tm, sub)]
    mix = mix_of(rows[0])
    for i, r in enumerate(rows):
        mix_next = mix_of(rows[i + 1]) if i + 1 < len(rows) else None
        finish(r, mix)
        mix = mix_next


def _merge(y_mla, y_na, gates, x, g1, sc2, sh2, wpm, wpn, wo, l1g, l1b, wr, tm):
    B, n, D = x.shape
    E = wr.shape[1]
    full = lambda a: pl.BlockSpec(a.shape, lambda b, i: (0,) * a.ndim)
    tok = lambda w: pl.BlockSpec((1, tm, w), lambda b, i: (b, i, 0))
    mod = pl.BlockSpec((1, 1, D), lambda b, i: (b, 0, 0))
    return pl.pallas_call(
        _merge_kernel,
        out_shape=[jax.ShapeDtypeStruct((B, n, D), F32),
                   jax.ShapeDtypeStruct((B, n * (D // LANES), LANES), F32),
                   jax.ShapeDtypeStruct((B, n, E), F32)],
        grid=(B, n // tm),
        in_specs=[tok(y_mla.shape[2]), tok(y_na.shape[2]), tok(2 * D), tok(D), mod, mod, mod,
                  full(wpm), full(wpn), full(wo), full(l1g), full(l1b), full(wr)],
        out_specs=[tok(D), pl.BlockSpec((1, tm * (D // LANES), LANES), lambda b, i: (b, i, 0)), tok(E)],
        compiler_params=_cparams(("parallel", "parallel")),
        name="merge_ln_router",
    )(y_mla, y_na, gates, x, g1, sc2, sh2, wpm, wpn, wo, l1g, l1b, wr)


def _topk_kernel(cap, aff_ref, tri_ref, idx_ref, g_ref, pos_ref):
    aff = aff_ref[0]
    E, n = aff.shape

    def count(mask):
        return jnp.sum(mask.astype(F32), axis=-1, keepdims=True)

    def enough(v):
        return count(aff >= v) >= cap

    def search(t, thr):
        b1 = jnp.int32(1) << (30 - 2 * t)
        b0 = b1 >> 1
        c3, c2, c1 = thr | b1 | b0, thr | b1, thr | b0
        ok3, ok2, ok1 = [enough(pltpu.bitcast(c, F32)) for c in (c3, c2, c1)]
        return jnp.where(ok3, c3, jnp.where(ok2, c2, jnp.where(ok1, c1, thr)))

    thr = lax.fori_loop(0, 16, search, jnp.zeros((E, 1), jnp.int32))
    lo = pltpu.bitcast(thr, F32)
    hi = pltpu.bitcast(thr + 1, F32)

    def refine(t, lh):
        lo, hi = lh
        w = hi - lo
        q1, q2, q3 = lo + w * 0.25, lo + w * 0.5, lo + w * 0.75
        ok1, ok2, ok3 = enough(q1), enough(q2), enough(q3)
        new_lo = jnp.where(ok3, q3, jnp.where(ok2, q2, jnp.where(ok1, q1, lo)))
        new_hi = jnp.where(ok3, hi, jnp.where(ok2, q3, jnp.where(ok1, q2, q1)))
        return new_lo, new_hi

    lo, hi = lax.fori_loop(0, TOPK_REFINE // 2, refine, (lo, hi))
    gt = aff >= hi
    eq = (aff >= lo) & ~gt
    need = cap - count(gt)

    tri = tri_ref[...]

    def prefix(mask):
        mf = mask.astype(F32).astype(BF16)
        parts = []
        off = jnp.zeros((E, 1), F32)
        for c in range(n // LANES):
            blk = mf[:, c * LANES:(c + 1) * LANES]
            parts.append(_dot(blk, tri) + off)
            off = off + jnp.sum(blk.astype(F32), axis=-1, keepdims=True)
        return jnp.concatenate(parts, axis=-1)

    sel = gt | (eq & (prefix(eq) < need))
    pos_ref[...] = jnp.where(sel, prefix(sel), -1.0)

    tok = lax.broadcasted_iota(jnp.int32, (1, n), 1)
    digits = [(tok // TOPK_RADIX).astype(F32), (tok % TOPK_RADIX).astype(F32)]
    chunks = cap // TOPK_SLOTS

    def per_expert(e, carry):
        a = aff_ref[0, pl.ds(e, 1), :]
        a1 = a.astype(BF16).astype(F32)
        a2 = (a - a1).astype(BF16).astype(F32)
        a3 = (a - a1) - a2
        rows = digits + [a1, a2, a3]
        table = jnp.concatenate(rows + [jnp.zeros((TOPK_TABLE_ROWS - len(rows), n), F32)], axis=0).astype(BF16)
        pos_e = pos_ref[pl.ds(e, 1), :]

        hits = []
        for ch in range(chunks):
            slot = (lax.broadcasted_iota(jnp.int32, (TOPK_SLOTS, n), 0) + ch * TOPK_SLOTS).astype(F32)
            hits.append(jnp.where(slot == pos_e, 1.0, 0.0).astype(BF16))
        for ch in range(chunks):
            r = _dot_nt(hits[ch], table)
            idx = r[:, 0:1] * TOPK_RADIX + r[:, 1:2]
            idx_ref[0, e, ch * TOPK_SLOTS:(ch + 1) * TOPK_SLOTS, :] = idx.astype(jnp.int32)
            g_ref[0, e, ch * TOPK_SLOTS:(ch + 1) * TOPK_SLOTS, :] = (r[:, 2:3] + r[:, 3:4]) + r[:, 4:5]
        return carry

    lax.fori_loop(0, E, per_expert, 0)


def _topk(aff_t, cap):
    B, E, n = aff_t.shape
    tri = jnp.asarray(np.triu(np.ones((LANES, LANES), np.float32), k=1), BF16)
    return pl.pallas_call(
        functools.partial(_topk_kernel, cap),
        out_shape=[jax.ShapeDtypeStruct((B, E, cap, 1), jnp.int32),
                   jax.ShapeDtypeStruct((B, E, cap, 1), F32)],
        grid=(B,),
        in_specs=[pl.BlockSpec((1, E, n), lambda b: (b, 0, 0)),
                  pl.BlockSpec((LANES, LANES), lambda b: (0, 0))],
        out_specs=[pl.BlockSpec((1, E, cap, 1), lambda b: (b, 0, 0, 0)),
                   pl.BlockSpec((1, E, cap, 1), lambda b: (b, 0, 0, 0))],
        scratch_shapes=[pltpu.VMEM((E, n), F32)],
        compiler_params=_cparams(("parallel",)),
        name="expert_topk",
    )(aff_t, tri)


def _gather_kernel(s, idx_ref, u_ref, o_ref):
    cap = o_ref.shape[2] // s

    def body(i, carry):
        c0 = pl.multiple_of(i * ROW_GROUP, ROW_GROUP)
        for k in range(ROW_GROUP):
            src = pl.multiple_of(idx_ref[0, 0, 0, c0 + k] * s, s)
            o_ref[0, 0, pl.ds((c0 + k) * s, s), :] = u_ref[0, pl.ds(src, s), :]
        return carry

    lax.fori_loop(0, cap // ROW_GROUP, body, 0)


def _gather(idx, u2, n):
    B, E, _, cap = idx.shape
    _, ns, W = u2.shape
    s = ns // n
    return pl.pallas_call(
        functools.partial(_gather_kernel, s),
        out_shape=jax.ShapeDtypeStruct((B, E, cap * s, W), u2.dtype),
        grid=(B, E),
        in_specs=[pl.BlockSpec((1, 1, 1, cap), lambda b, e: (b, e, 0, 0), memory_space=pltpu.SMEM),
                  pl.BlockSpec((1, ns, W), lambda b, e: (b, 0, 0))],
        out_specs=pl.BlockSpec((1, 1, cap * s, W), lambda b, e: (b, e, 0, 0)),
        compiler_params=_cparams(("parallel", "arbitrary")),
        name="expert_gather",
    )(idx, u2)


def _ffn_kernel(xe_ref, g_ref, wg_ref, wu_ref, wd_ref, o_ref, wgb_ref, wub_ref, wdb_ref):
    @pl.when(pl.program_id(1) == 0)
    def _():
        wgb_ref[...] = wg_ref[0].astype(BF16)
        wub_ref[...] = wu_ref[0].astype(BF16)
        wdb_ref[...] = wd_ref[0].astype(BF16)

    xe = _load_tiles_as_rows(xe_ref.at[0, 0], wgb_ref.shape[0]).astype(BF16)
    gate = _dot(xe, wgb_ref[...])
    up = _dot(xe, wub_ref[...])
    h = (gate * jax.nn.sigmoid(gate) * up).astype(BF16)
    _store_rows_as_tiles(o_ref.at[0, 0], _dot(h, wdb_ref[...]) * g_ref[0, 0])


def _ffn(xe, g, w_gate, w_up, w_down):
    B, E, caps, W = xe.shape
    _, D, F = w_gate.shape
    cap = g.shape[2]
    return pl.pallas_call(
        _ffn_kernel,
        out_shape=jax.ShapeDtypeStruct((B, E, caps, W), F32),
        grid=(E, B),
        in_specs=[pl.BlockSpec((1, 1, caps, W), lambda e, b: (b, e, 0, 0)),
                  pl.BlockSpec((1, 1, cap, 1), lambda e, b: (b, e, 0, 0)),
                  pl.BlockSpec((1, D, F), lambda e, b: (e, 0, 0)),
                  pl.BlockSpec((1, D, F), lambda e, b: (e, 0, 0)),
                  pl.BlockSpec((1, F, D), lambda e, b: (e, 0, 0))],
        out_specs=pl.BlockSpec((1, 1, caps, W), lambda e, b: (b, e, 0, 0)),
        scratch_shapes=[pltpu.VMEM((D, F), BF16), pltpu.VMEM((D, F), BF16), pltpu.VMEM((F, D), BF16)],
        compiler_params=_cparams(("arbitrary", "arbitrary")),
        name="expert_ffn",
    )(xe, g, w_gate, w_up, w_down)


def _scatter_kernel(s, idx_ref, y_ref, o_ref):
    cap = y_ref.shape[2] // s

    def tile(r):
        return pl.ds(pl.multiple_of(r * s, s), s)

    @pl.when(pl.program_id(1) == 0)
    def _():
        o_ref[...] = jnp.zeros_like(o_ref)

    def body(i, carry):
        c0 = pl.multiple_of(i * ROW_GROUP, ROW_GROUP)
        rows = [idx_ref[0, 0, 0, c0 + k] for k in range(ROW_GROUP)]
        acc = [o_ref[0, tile(r), :] for r in rows]
        for k in range(ROW_GROUP):
            o_ref[0, tile(rows[k]), :] = acc[k] + y_ref[0, 0, tile(c0 + k), :]
        return carry

    lax.fori_loop(0, cap // ROW_GROUP, body, 0)


def _scatter(idx, ye, n):
    B, E, caps, W = ye.shape
    cap = idx.shape[3]
    s = caps // cap
    return pl.pallas_call(
        functools.partial(_scatter_kernel, s),
        out_shape=jax.ShapeDtypeStruct((B, n * s, W), F32),
        grid=(B, E),
        in_specs=[pl.BlockSpec((1, 1, 1, cap), lambda b, e: (b, e, 0, 0), memory_space=pltpu.SMEM),
                  pl.BlockSpec((1, 1, caps, W), lambda b, e: (b, e, 0, 0))],
        out_specs=pl.BlockSpec((1, n * s, W), lambda b, e: (b, 0, 0)),
        compiler_params=_cparams(("parallel", "arbitrary")),
        name="expert_scatter",
    )(idx, ye)


def _final_kernel(x_ref, moe_ref, g2_ref, lg_ref, lb_ref, o_ref):
    moe = _load_tiles_as_rows(moe_ref.at[0], x_ref.shape[2])
    o_ref[0] = _ln(ALPHA * x_ref[0] + g2_ref[0] * moe) * lg_ref[...] + lb_ref[...]


def _final(xn, moe, g2, lg, lb, tm):
    B, n, D = xn.shape
    tok = pl.BlockSpec((1, tm, D), lambda b, i: (b, i, 0))
    vec = pl.BlockSpec((1, D), lambda b, i: (0, 0))
    return pl.pallas_call(
        _final_kernel,
        out_shape=jax.ShapeDtypeStruct((B, n, D), F32),
        grid=(B, n // tm),
        in_specs=[tok, pl.BlockSpec((1, tm * (D // LANES), LANES), lambda b, i: (b, i, 0)),
                  pl.BlockSpec((1, 1, D), lambda b, i: (b, 0, 0)), vec, vec],
        out_specs=tok,
        compiler_params=_cparams(("parallel", "parallel")),
        name="final_ln",
    )(xn, moe, g2, lg, lb)


def _rope_tables(n):
    t = np.arange(n)
    row = (t // GRID_W).astype(np.float32)
    col = (t % GRID_W).astype(np.float32)
    per_axis = MLA_ROPE // 2
    inv_freq = jnp.asarray(ROPE_THETA, F32) ** (-jnp.arange(0, per_axis, 2, dtype=F32) / per_axis)
    ang = jnp.concatenate([jnp.asarray(row)[:, None] * inv_freq, jnp.asarray(col)[:, None] * inv_freq], axis=-1)
    cos, sin = jnp.cos(ang), jnp.sin(ang)
    pad = HEAD_PAD - MLA_QK
    c_tab = jnp.concatenate([jnp.ones((n, MLA_NOPE), F32), cos, cos, jnp.zeros((n, pad), F32)], axis=-1)
    s_tab = jnp.concatenate([jnp.zeros((n, MLA_NOPE), F32), -sin, sin, jnp.zeros((n, pad), F32)], axis=-1)
    return c_tab, s_tab


def _swap_halves(w):
    half = w.shape[-1] // 2
    return jnp.concatenate([w[..., half:], w[..., :half]], axis=-1)


def _prep_weights(w_in, w_uq, w_ukv):
    D = w_in.shape[0]
    o1 = Q_LORA
    o2 = o1 + KV_LORA
    o3 = o2 + MLA_ROPE
    o4 = o3 + 3 * NA_W
    w_kr = w_in[:, o2:o3]
    z = lambda k: jnp.zeros((D, k), w_in.dtype)
    pad = HEAD_PAD - MLA_QK
    kra = jnp.concatenate([z(MLA_NOPE), w_kr, z(pad)], axis=-1)
    krb = jnp.concatenate([z(MLA_NOPE), _swap_halves(w_kr), z(pad)], axis=-1)
    w_small = jnp.concatenate([w_in[:, :o2], kra, krb], axis=-1).astype(BF16)
    w_na = w_in[:, o3:o3 + 2 * NA_W].astype(BF16)
    w_nvt = w_in[:, o3 + 2 * NA_W:o4].T.astype(BF16)
    w_g = w_in[:, o4:].astype(BF16)

    uq = w_uq.reshape(Q_LORA, MLA_HEADS, MLA_QK)
    zq = lambda k: jnp.zeros((Q_LORA, MLA_HEADS, k), w_uq.dtype)
    wqa = jnp.concatenate([uq, zq(pad)], axis=-1).reshape(Q_LORA, MLA_HEADS * HEAD_PAD).astype(BF16)
    wqb = jnp.concatenate([zq(MLA_NOPE), _swap_halves(uq[..., MLA_NOPE:]), zq(pad)], axis=-1)
    wqb = wqb.reshape(Q_LORA, MLA_HEADS * HEAD_PAD).astype(BF16)

    ukv = w_ukv.reshape(KV_LORA, MLA_HEADS, MLA_NOPE + MLA_V)
    wk = jnp.concatenate([ukv[..., :MLA_NOPE], jnp.zeros((KV_LORA, MLA_HEADS, HEAD_PAD - MLA_NOPE), w_ukv.dtype)], axis=-1)
    wk = wk.reshape(KV_LORA, MLA_HEADS * HEAD_PAD).astype(BF16)
    wv = ukv[..., MLA_NOPE:].reshape(KV_LORA, MLA_HEADS * MLA_V).T.astype(BF16)
    return w_small, w_na, w_nvt, w_g, wqa, wqb, wk, wv


def _tile(n, pref):
    t = min(pref, n)
    while n % t:
        t //= 2
    return t


def kernel(x, c, ctx, c_ctx, w_mod, b_mod, w_in, q_norm_g, w_uq, kv_norm_g, w_ukv, na_rel_bias, w_proj_mla,
           w_proj_na, w_out, ln1_g, ln1_b, w_router, w_exp_gate, w_exp_up, w_exp_down, ln2_g, ln2_b):
    B, n, D = x.shape
    L = ctx.shape[1]
    rows = n // GRID_W
    assert n % (GRID_W * NA_QROWS) == 0 and rows >= NA_BAND
    assert w_mod.shape[0] == DEPTH
    cap = EC_CAPACITY * n // N_EXPERTS

    mod_rows = -(-(B + 1) // 8) * 8
    cc = jnp.concatenate([c, c_ctx[None], jnp.zeros((mod_rows - B - 1, D), F32)], axis=0)
    m = _modulation(cc, w_mod[0], b_mod[0])
    sh1, sc1, g1, sh2, sc2, g2 = [m[:B, k * D:(k + 1) * D].reshape(B, 1, D) for k in range(6)]
    csh1 = m[B:B + 1, :D].reshape(1, 1, D)
    csc1 = m[B:B + 1, D:2 * D].reshape(1, 1, D)

    w_small, w_na, w_nvt, w_g, wqa, wqb, wk, wv = _prep_weights(w_in[0], w_uq[0], w_ukv[0])
    qg = q_norm_g[0].reshape(1, Q_LORA)
    kvg = kv_norm_g[0].reshape(1, KV_LORA)
    c_tab, s_tab = _rope_tables(n)
    pad = HEAD_PAD - MLA_QK
    c_ctx_tab = jnp.concatenate([jnp.ones((L, MLA_QK), F32), jnp.zeros((L, pad), F32)], axis=-1)
    s_ctx_tab = jnp.zeros((L, HEAD_PAD), F32)

    tm = _tile(n, 256)
    q, k_lat, v_lat, nq, nk, nv, gates = _inproj(
        True, x, sc1, sh1, c_tab, s_tab, c_tab * (MLA_SCALE * LOG2E), s_tab * (MLA_SCALE * LOG2E),
        w_small, w_na, w_nvt, w_g, qg, kvg, wqa, wqb, wk, wv, _tile(n, 512))
    k_ctx, v_ctx, cnk, cnv = _inproj(
        False, ctx, csc1, csh1, c_ctx_tab, s_ctx_tab, c_ctx_tab, s_ctx_tab,
        w_small, w_na[:, NA_W:], w_nvt, w_g, qg, kvg, wqa, wqb, wk, wv, _tile(L, 256))

    y_mla = _mla_attention(q, k_lat, k_ctx, v_lat, v_ctx, _tile(n, 1024), _tile(n, 512))
    y_na = _na_attention(nq, nk, nv, cnk, cnv, _na_bias_tiles(na_rel_bias[0], rows))

    xn, u2, aff = _merge(
        y_mla, y_na, gates, x, g1, sc2, sh2,
        w_proj_mla[0].astype(BF16), w_proj_na[0].astype(BF16), w_out[0].astype(BF16),
        ln1_g[0].reshape(1, D), ln1_b[0].reshape(1, D), w_router[0], _tile(n, 2 * MERGE_SUB))

    idx4, g4 = _topk(jnp.swapaxes(aff, 1, 2), cap)
    idx = idx4.reshape(B, N_EXPERTS, 1, cap)
    xe = _gather(idx, u2, n)
    ye = _ffn(xe, g4, w_exp_gate[0], w_exp_up[0], w_exp_down[0])
    moe = _scatter(idx, ye, n)
    return _final(xn, moe, g2, ln2_g[0].reshape(1, D), ln2_b[0].reshape(1, D), tm)
```

```python
import functools
import math

import numpy as np
import jax
import jax.numpy as jnp
from jax import lax
from jax.experimental import pallas as pl
from jax.experimental.pallas import tpu as pltpu

GRID_W = 64
MLA_HEADS = 8
MLA_NOPE = 64
MLA_ROPE = 32
MLA_QK = MLA_NOPE + MLA_ROPE
MLA_V = 64
Q_LORA = 256
KV_LORA = 128
MLA_SCALE = MLA_QK ** -0.5
ROPE_THETA = 10000.0
NA_HEADS = 8
NA_DIM = 64
NA_W = NA_HEADS * NA_DIM
NA_WIN_H = 8
NA_WIN_W = 16
NA_SCALE = NA_DIM ** -0.5
N_EXPERTS = 16
EC_CAPACITY = 2
LN_EPS = 1e-5
RMS_EPS = 1e-6
DEPTH = 1
ALPHA = (2.0 * DEPTH) ** 0.25
LOG2E = math.log2(math.e)

LANES = 128
HEAD_PAD = LANES
VMEM_LIMIT = 56 * 1024 * 1024
NEG_BIG = -1e30

NA_QROWS = 4
NA_BAND = 12
ROW_GROUP = 8
NA_LOOKAHEAD = 2
MERGE_SUB = 256
TOPK_RADIX = 64
TOPK_TABLE_ROWS = 16
TOPK_SLOTS = 128
TOPK_REFINE = 30

BF16 = jnp.bfloat16
F32 = jnp.float32


def _cparams(sem):
    return pltpu.CompilerParams(dimension_semantics=sem, vmem_limit_bytes=VMEM_LIMIT)


def _ln(x):
    mu = jnp.mean(x, axis=-1, keepdims=True)
    xc = x - mu
    var = jnp.mean(xc * xc, axis=-1, keepdims=True)
    return xc * lax.rsqrt(var + LN_EPS)


def _dot(a, b):
    return jnp.dot(a, b, preferred_element_type=F32)


def _dot_nt(a, b):
    return lax.dot_general(a, b, (((1,), (1,)), ((), ())), preferred_element_type=F32)


def _store_rows_as_tiles(ref, val):
    m, d = val.shape
    s = d // LANES
    for j in range(s):
        ref[pl.ds(j, m, stride=s), :] = val[:, j * LANES:(j + 1) * LANES]


def _load_tiles_as_rows(ref, d):
    s = d // LANES
    m = ref.shape[0] // s
    return jnp.concatenate([ref[pl.ds(j, m, stride=s), :] for j in range(s)], axis=-1)


def _mod_kernel(c_ref, w_ref, b_ref, o_ref):
    c = c_ref[...]
    s = c * jax.nn.sigmoid(c)
    o_ref[...] = jnp.dot(s, w_ref[...], preferred_element_type=F32,
                         precision=lax.Precision.HIGHEST) + b_ref[...]


def _modulation(cc, w_mod, b_mod):
    rows, d = cc.shape
    n_out = w_mod.shape[1]
    tn = 1024
    return pl.pallas_call(
        _mod_kernel,
        out_shape=jax.ShapeDtypeStruct((rows, n_out), F32),
        grid=(n_out // tn,),
        in_specs=[pl.BlockSpec((rows, d), lambda j: (0, 0)),
                  pl.BlockSpec((d, tn), lambda j: (0, j)),
                  pl.BlockSpec((1, tn), lambda j: (0, j))],
        out_specs=pl.BlockSpec((rows, tn), lambda j: (0, j)),
        compiler_params=_cparams(("arbitrary",)),
        name="modulation",
    )(cc, w_mod, b_mod.reshape(1, n_out))


def _swap_rope_halves(x):
    half = MLA_ROPE // 2
    lane = lax.broadcasted_iota(jnp.int32, x.shape, 1)
    return jnp.where(lane < MLA_NOPE + half, pltpu.roll(x, HEAD_PAD - half, 1), pltpu.roll(x, half, 1))


def _inproj_kernel(latent, x_ref, sc_ref, sh_ref, ck_ref, sk_ref, cq_ref, sq_ref,
                   w_small_ref, w_na_ref, w_nvt_ref, w_g_ref, qg_ref, kvg_ref, wqa_ref, wk_ref, wv_ref,
                   *out_refs):
    if latent:
        q_ref, k_ref, v_ref, nq_ref, nk_ref, nv_ref, g_ref = out_refs
    else:
        k_ref, v_ref, nk_ref, nv_ref = out_refs
    x = x_ref[0]
    u = (_ln(x) * (1.0 + sc_ref[0]) + sh_ref[0]).astype(BF16)

    small = _dot(u, w_small_ref[...])
    q_c = small[:, :Q_LORA]
    kv_c = small[:, Q_LORA:Q_LORA + KV_LORA]
    k_r = small[:, Q_LORA + KV_LORA:]

    kvn = (kv_c * lax.rsqrt(jnp.mean(kv_c * kv_c, axis=-1, keepdims=True) + RMS_EPS) * kvg_ref[...]).astype(BF16)
    kk = _dot(kvn, wk_ref[...])
    kr = k_r * ck_ref[...] + _swap_rope_halves(k_r) * sk_ref[...]
    for h in range(MLA_HEADS):
        k_ref[0, h] = (kk[:, h * HEAD_PAD:(h + 1) * HEAD_PAD] + kr).astype(BF16)
    v_ref[0] = _dot_nt(wv_ref[...], kvn).astype(BF16)

    na = _dot(u, w_na_ref[...])
    nv_ref[0] = _dot_nt(w_nvt_ref[...], u).astype(BF16)
    if latent:
        qn = (q_c * lax.rsqrt(jnp.mean(q_c * q_c, axis=-1, keepdims=True) + RMS_EPS) * qg_ref[...]).astype(BF16)
        qa = _dot(qn, wqa_ref[...])
        cq = cq_ref[...]
        sq = sq_ref[...]
        for h in range(MLA_HEADS):
            qh = qa[:, h * HEAD_PAD:(h + 1) * HEAD_PAD]
            q_ref[0, h] = (qh * cq + _swap_rope_halves(qh) * sq).astype(BF16)
        nq_ref[0] = (na[:, :NA_W] * (NA_SCALE * LOG2E)).astype(BF16)
        nk_ref[0] = na[:, NA_W:].astype(BF16)
        g_ref[0] = jax.nn.sigmoid(_dot(u, w_g_ref[...])).astype(BF16)
    else:
        nk_ref[0] = na.astype(BF16)


def _inproj(latent, x, sc, sh, ck, sk, cq, sq, w_small, w_na, w_nvt, w_g, qg, kvg, wqa, wk, wv, tm):
    B, n, D = x.shape
    per_batch = sc.shape[0] > 1
    mod_map = (lambda b, i: (b, 0, 0)) if per_batch else (lambda b, i: (0, 0, 0))
    full = lambda a: pl.BlockSpec(a.shape, lambda b, i: (0,) * a.ndim)
    tab = lambda a: pl.BlockSpec((tm, a.shape[1]), lambda b, i: (i, 0))
    in_specs = [pl.BlockSpec((1, tm, D), lambda b, i: (b, i, 0)),
                pl.BlockSpec((1, 1, D), mod_map), pl.BlockSpec((1, 1, D), mod_map),
                tab(ck), tab(sk), tab(cq), tab(sq),
                full(w_small), full(w_na), full(w_nvt), full(w_g), full(qg), full(kvg),
                full(wqa), full(wk), full(wv)]
    hk = jax.ShapeDtypeStruct((B, MLA_HEADS, n, HEAD_PAD), BF16)
    hk_spec = pl.BlockSpec((1, MLA_HEADS, tm, HEAD_PAD), lambda b, i: (b, 0, i, 0))
    tok = lambda w: jax.ShapeDtypeStruct((B, n, w), BF16)
    tok_spec = lambda w: pl.BlockSpec((1, tm, w), lambda b, i: (b, i, 0))
    vt = jax.ShapeDtypeStruct((B, MLA_HEADS * MLA_V, n), BF16)
    vt_spec = pl.BlockSpec((1, MLA_HEADS * MLA_V, tm), lambda b, i: (b, 0, i))
    if latent:
        out_shape = [hk, hk, vt, tok(NA_W), tok(NA_W), vt, tok(2 * D)]
        out_specs = [hk_spec, hk_spec, vt_spec, tok_spec(NA_W), tok_spec(NA_W), vt_spec, tok_spec(2 * D)]
    else:
        out_shape = [hk, vt, tok(NA_W), vt]
        out_specs = [hk_spec, vt_spec, tok_spec(NA_W), vt_spec]
    return pl.pallas_call(
        functools.partial(_inproj_kernel, latent),
        out_shape=out_shape,
        grid=(B, n // tm),
        in_specs=in_specs,
        out_specs=out_specs,
        compiler_params=_cparams(("parallel", "parallel")),
        name="inproj_latent" if latent else "inproj_ctx",
    )(x, sc, sh, ck, sk, cq, sq, w_small, w_na, w_nvt, w_g, qg, kvg, wqa, wk, wv)


def _mla_kernel(tk, q_ref, kc_ref, kl_ref, vc_ref, vl_ref, o_ref, m_ref, l_ref, acc_ref, sa_ref, sb_ref, sc_ref):
    heads = q_ref.shape[1]
    n = kl_ref.shape[2]
    tq = q_ref.shape[2]
    nt = n // tk
    row = lax.broadcasted_iota(jnp.int32, (LANES, tq), 0)

    def scores(h, k):
        return _dot_nt(k, q_ref[0, h])

    def absorb(hh, s, v_t):
        m_old = m_ref[hh]
        m_new = jnp.maximum(m_old, jnp.max(s, axis=0, keepdims=True))
        a = jnp.exp2(m_old - m_new)
        p = jnp.exp2(s - m_new)
        l_ref[hh] = a * l_ref[hh] + jnp.sum(p, axis=0, keepdims=True)
        acc_ref[hh] = a * acc_ref[hh] + _dot(v_t, p.astype(BF16))
        m_ref[hh] = m_new

    def ctx_scores(h0):
        for hh in range(2):
            sc_ref[hh] = scores(h0 + hh, kc_ref[0, h0 + hh])

    def pair_pass(pair, carry):
        h0 = 2 * pair
        rows = pl.ds(pl.multiple_of(pair * LANES, LANES), LANES)

        def k_tile(hh, j):
            return kl_ref[0, h0 + hh, pl.ds(pl.multiple_of(j * tk, tk), tk), :]

        def v_tile(j):
            return vl_ref[0, rows, pl.ds(pl.multiple_of(j * tk, tk), tk)]

        def advance(j, cur_ref, next_ref):
            v_t = v_tile(j)
            for hh in range(2):
                next_ref[hh] = scores(h0 + hh, k_tile(hh, j + 1))
                absorb(hh, cur_ref[hh], v_t)

        m_ref[...] = jnp.full(m_ref.shape, NEG_BIG, F32)
        l_ref[...] = jnp.zeros(l_ref.shape, F32)
        acc_ref[...] = jnp.zeros(acc_ref.shape, F32)
        v_ctx = vc_ref[0, rows, :]
        for hh in range(2):
            sa_ref[hh] = scores(h0 + hh, k_tile(hh, 0))
            absorb(hh, sc_ref[hh], v_ctx)

        def body(jj, c):
            advance(2 * jj, sa_ref, sb_ref)
            advance(2 * jj + 1, sb_ref, sa_ref)
            return c

        lax.fori_loop(0, (nt - 1) // 2, body, 0)
        last_ref = sa_ref
        if (nt - 1) % 2:
            advance(nt - 2, sa_ref, sb_ref)
            last_ref = sb_ref
        ctx_scores(jnp.minimum(h0 + 2, heads - 2))
        v_t = v_tile(nt - 1)
        for hh in range(2):
            absorb(hh, last_ref[hh], v_t)
        out_t = jnp.where(row < MLA_V, acc_ref[0] / l_ref[0], acc_ref[1] / l_ref[1])
        o_ref[0, :, rows] = out_t.T.astype(o_ref.dtype)
        return carry

    ctx_scores(0)
    lax.fori_loop(0, heads // 2, pair_pass, 0)


def _mla_attention(q, k_lat, k_ctx, v_lat, v_ctx, tq, tk):
    B, H, n, _ = q.shape
    L = k_ctx.shape[2]
    W = H * MLA_V
    return pl.pallas_call(
        functools.partial(_mla_kernel, tk),
        out_shape=jax.ShapeDtypeStruct((B, n, W), BF16),
        grid=(B, n // tq),
        in_specs=[pl.BlockSpec((1, H, tq, HEAD_PAD), lambda b, i: (b, 0, i, 0)),
                  pl.BlockSpec((1, H, L, HEAD_PAD), lambda b, i: (b, 0, 0, 0)),
                  pl.BlockSpec((1, H, n, HEAD_PAD), lambda b, i: (b, 0, 0, 0)),
                  pl.BlockSpec((1, W, L), lambda b, i: (b, 0, 0)),
                  pl.BlockSpec((1, W, n), lambda b, i: (b, 0, 0))],
        out_specs=pl.BlockSpec((1, tq, W), lambda b, i: (b, i, 0)),
        scratch_shapes=[pltpu.VMEM((2, 1, tq), F32), pltpu.VMEM((2, 1, tq), F32),
                        pltpu.VMEM((2, LANES, tq), F32),
                        pltpu.VMEM((2, tk, tq), F32), pltpu.VMEM((2, tk, tq), F32),
                        pltpu.VMEM((2, L, tq), F32)],
        compiler_params=_cparams(("parallel", "parallel")),
        name="mla_attention",
    )(q, k_ctx, k_lat, v_ctx, v_lat)


def _na_block_tables(rows):
    nblk = rows // NA_QROWS
    wh = min(NA_WIN_H, rows)
    band0 = np.clip(np.arange(nblk) * NA_QROWS - wh // 2, 0, rows - NA_BAND)
    sigs, types = [], []
    for i in range(nblk):
        r = i * NA_QROWS + np.arange(NA_QROWS)
        r0 = np.clip(r - wh // 2, 0, rows - wh)
        sig = (tuple(r0 - band0[i]), tuple(r - band0[i]))
        if sig not in sigs:
            sigs.append(sig)
        types.append(sigs.index(sig))
    return band0.astype(np.int32), np.asarray(types, np.int32), sigs, wh


def _bias_tile_kernel(dr, ok, tbl_ref, o_ref):
    neg = jnp.full((GRID_W, GRID_W), NEG_BIG, F32)
    for t in range(o_ref.shape[0]):
        for k in range(NA_BAND):
            pieces = [tbl_ref[0, int(dr[t, q, k])] if ok[t, q, k] else neg for q in range(NA_QROWS)]
            o_ref[t, 0, k * GRID_W:(k + 1) * GRID_W, :] = jnp.concatenate(pieces, axis=1)


def _na_bias_tiles(rel_bias, rows):
    _, _, sigs, wh = _na_block_tables(rows)
    ww = NA_WIN_W
    n_dr, n_dc = 2 * NA_WIN_H - 1, 2 * NA_WIN_W - 1
    col = np.arange(GRID_W)
    c0 = np.clip(col - ww // 2, 0, GRID_W - ww)
    col_ok = (col[:, None] >= c0[None, :]) & (col[:, None] < c0[None, :] + ww)
    dc = col[:, None] - col[None, :] + (NA_WIN_W - 1)
    pick_c = ((dc[None] == np.arange(n_dc)[:, None, None]) & col_ok[None]).astype(np.float32)
    by_col = jnp.einsum('hdj,jyx->hdyx', rel_bias.astype(F32), jnp.asarray(pick_c), precision=lax.Precision.HIGHEST)
    by_col = jnp.where(jnp.asarray(col_ok), by_col * LOG2E, NEG_BIG)
    kr = np.arange(NA_BAND)
    ok = np.stack([(kr[None, :] >= np.asarray(r0)[:, None]) & (kr[None, :] < np.asarray(r0)[:, None] + wh)
                   for r0, _ in sigs])
    dr = np.stack([kr[None, :] - np.asarray(r)[:, None] + (NA_WIN_H - 1) for _, r in sigs])
    assert ((dr >= 0) & (dr < n_dr))[ok].all()
    T = len(sigs)
    kn, qn = NA_BAND * GRID_W, NA_QROWS * GRID_W
    return pl.pallas_call(
        functools.partial(_bias_tile_kernel, dr, ok),
        out_shape=jax.ShapeDtypeStruct((T, NA_HEADS, kn, qn), F32),
        grid=(NA_HEADS,),
        in_specs=[pl.BlockSpec((1, n_dr, GRID_W, GRID_W), lambda h: (h, 0, 0, 0))],
        out_specs=pl.BlockSpec((T, 1, kn, qn), lambda h: (0, h, 0, 0)),
        compiler_params=_cparams(("parallel",)),
        name="na_bias_tiles",
    )(by_col)


def _na_kernel(nblk, band_ref, type_ref, q_ref, k_ref, vt_ref, kc_ref, vct_ref, *rest):
    bias_refs, o_ref = rest[:nblk], rest[nblk]
    i = pl.program_id(1)
    nq = NA_QROWS * GRID_W
    nk = NA_BAND * GRID_W
    starts = [pl.multiple_of(band_ref[i * nblk + b] * GRID_W, NA_QROWS * GRID_W) for b in range(nblk)]
    lane = lax.broadcasted_iota(jnp.int32, (nq, LANES), 1)
    row = lax.broadcasted_iota(jnp.int32, (LANES, nq), 0)

    def scores(c):
        blk, h = divmod(c, NA_HEADS)
        cs = slice((h // 2) * LANES, (h // 2 + 1) * LANES)
        qp = q_ref[0, blk * nq:(blk + 1) * nq, cs]
        own = (lane < NA_DIM) if h % 2 == 0 else (lane >= NA_DIM)
        qh = jnp.where(own, qp, jnp.zeros_like(qp))
        s_loc = _dot_nt(k_ref[0, pl.ds(starts[blk], nk), cs], qh) + bias_refs[blk][0, h]
        s_ctx = _dot_nt(kc_ref[0, :, cs], qh)
        return s_loc, s_ctx

    def attend(c, s):
        s_loc, s_ctx = s
        blk, h = divmod(c, NA_HEADS)
        start = starts[blk]
        cs = slice((h // 2) * LANES, (h // 2 + 1) * LANES)
        m = jnp.maximum(jnp.max(s_loc, axis=0, keepdims=True), jnp.max(s_ctx, axis=0, keepdims=True))
        p_loc = jnp.exp2(s_loc - m)
        p_ctx = jnp.exp2(s_ctx - m)
        l = jnp.sum(p_loc, axis=0, keepdims=True) + jnp.sum(p_ctx, axis=0, keepdims=True)
        o = _dot(vt_ref[0, cs, pl.ds(start, nk)], p_loc.astype(BF16)) + _dot(vct_ref[0, cs, :], p_ctx.astype(BF16))
        return o / l

    chain = nblk * NA_HEADS
    ahead = [scores(c) for c in range(min(NA_LOOKAHEAD, chain))]
    outs = []
    for c in range(chain):
        if c + NA_LOOKAHEAD < chain:
            ahead.append(scores(c + NA_LOOKAHEAD))
        outs.append(attend(c, ahead.pop(0)))
        if c % 2:
            blk, h = divmod(c, NA_HEADS)
            pair_t = jnp.where(row < NA_DIM, outs[c - 1], outs[c])
            o_ref[0, blk * nq:(blk + 1) * nq, (h // 2) * LANES:(h // 2 + 1) * LANES] = pair_t.T.astype(o_ref.dtype)


def _na_attention(nq, nk, nvt, cnk, cnvt, bias_tiles):
    B, n, W = nq.shape
    L = cnk.shape[1]
    rows = n // GRID_W
    band0, types, _, _ = _na_block_tables(rows)
    qn = NA_QROWS * GRID_W
    kn = NA_BAND * GRID_W
    nblocks = rows // NA_QROWS
    nblk = 2 if nblocks % 2 == 0 else 1

    def bias_spec(k):
        return pl.BlockSpec((1, NA_HEADS, kn, qn), lambda b, i, bd, ty: (ty[i * nblk + k], 0, 0, 0))

    grid_spec = pltpu.PrefetchScalarGridSpec(
        num_scalar_prefetch=2,
        grid=(B, nblocks // nblk),
        in_specs=[pl.BlockSpec((1, nblk * qn, W), lambda b, i, bd, ty: (b, i, 0)),
                  pl.BlockSpec((1, n, W), lambda b, i, bd, ty: (b, 0, 0)),
                  pl.BlockSpec((1, W, n), lambda b, i, bd, ty: (b, 0, 0)),
                  pl.BlockSpec((1, L, W), lambda b, i, bd, ty: (b, 0, 0)),
                  pl.BlockSpec((1, W, L), lambda b, i, bd, ty: (b, 0, 0))] + [bias_spec(k) for k in range(nblk)],
        out_specs=pl.BlockSpec((1, nblk * qn, W), lambda b, i, bd, ty: (b, i, 0)),
    )
    return pl.pallas_call(
        functools.partial(_na_kernel, nblk),
        out_shape=jax.ShapeDtypeStruct((B, n, W), BF16),
        grid_spec=grid_spec,
        compiler_params=_cparams(("parallel", "arbitrary")),
        name="na_attention",
    )(jnp.asarray(band0), jnp.asarray(types), nq, nk, nvt, cnk, cnvt, *([bias_tiles] * nblk))


def _split2(a):
    hi = a.astype(BF16)
    return hi, (a - hi.astype(F32)).astype(BF16)


def _merge_kernel(ym_ref, yn_ref, g_ref, x_ref, g1_ref, sc2_ref, sh2_ref, wpm_ref, wpn_ref, wo_ref,
                  l1g_ref, l1b_ref, wr_ref, xn_ref, up_ref, aff_ref):
    tm, D = x_ref.shape[1], x_ref.shape[2]
    s = D // LANES
    sub = min(MERGE_SUB, tm)
    wh, wl = _split2(wr_ref[...])

    def mix_of(r):
        g = g_ref[0, r, :]
        a = (g[:, :D].astype(F32) * _dot(ym_ref[0, r, :], wpm_ref[...])
             + g[:, D:].astype(F32) * _dot(yn_ref[0, r, :], wpn_ref[...]))
        return _dot(a.astype(BF16), wo_ref[...])

    def finish(r, mix):
        xn = _ln(ALPHA * x_ref[0, r, :] + g1_ref[0] * mix) * l1g_ref[...] + l1b_ref[...]
        xn_ref[0, r, :] = xn
        u2 = _ln(xn) * (1.0 + sc2_ref[0]) + sh2_ref[0]
        _store_rows_as_tiles(up_ref.at[0, r.start * s:r.stop * s], u2)
        uh, ul = _split2(u2)
        logits = _dot(uh, wh) + (_dot(uh, wl) + _dot(ul, wh))
        e = jnp.exp(logits - jnp.max(logits, axis=-1, keepdims=True))
        aff_ref[0, r, :] = e / jnp.sum(e, axis=-1, keepdims=True)

    rows = [slice(i, i + sub) for i in range(0, tm, sub)]
    mix = mix_of(rows[0])
    for i, r in enumerate(rows):
        mix_next = mix_of(rows[i + 1]) if i + 1 < len(rows) else None
        finish(r, mix)
        mix = mix_next


def _merge(y_mla, y_na, gates, x, g1, sc2, sh2, wpm, wpn, wo, l1g, l1b, wr, tm):
    B, n, D = x.shape
    E = wr.shape[1]
    full = lambda a: pl.BlockSpec(a.shape, lambda b, i: (0,) * a.ndim)
    tok = lambda w: pl.BlockSpec((1, tm, w), lambda b, i: (b, i, 0))
    mod = pl.BlockSpec((1, 1, D), lambda b, i: (b, 0, 0))
    return pl.pallas_call(
        _merge_kernel,
        out_shape=[jax.ShapeDtypeStruct((B, n, D), F32),
                   jax.ShapeDtypeStruct((B, n * (D // LANES), LANES), F32),
                   jax.ShapeDtypeStruct((B, n, E), F32)],
        grid=(B, n // tm),
        in_specs=[tok(y_mla.shape[2]), tok(y_na.shape[2]), tok(2 * D), tok(D), mod, mod, mod,
                  full(wpm), full(wpn), full(wo), full(l1g), full(l1b), full(wr)],
        out_specs=[tok(D), pl.BlockSpec((1, tm * (D // LANES), LANES), lambda b, i: (b, i, 0)), tok(E)],
        compiler_params=_cparams(("parallel", "parallel")),
        name="merge_ln_router",
    )(y_mla, y_na, gates, x, g1, sc2, sh2, wpm, wpn, wo, l1g, l1b, wr)


def _topk_kernel(cap, aff_ref, tri_ref, idx_ref, g_ref, pos_ref):
    aff = aff_ref[0]
    E, n = aff.shape

    def count(mask):
        return jnp.sum(mask.astype(F32), axis=-1, keepdims=True)

    def enough(v):
        return count(aff >= v) >= cap

    def search(t, thr):
        b1 = jnp.int32(1) << (30 - 2 * t)
        b0 = b1 >> 1
        c3, c2, c1 = thr | b1 | b0, thr | b1, thr | b0
        ok3, ok2, ok1 = [enough(pltpu.bitcast(c, F32)) for c in (c3, c2, c1)]
        return jnp.where(ok3, c3, jnp.where(ok2, c2, jnp.where(ok1, c1, thr)))

    thr = lax.fori_loop(0, 16, search, jnp.zeros((E, 1), jnp.int32))
    lo = pltpu.bitcast(thr, F32)
    hi = pltpu.bitcast(thr + 1, F32)

    def refine(t, lh):
        lo, hi = lh
        w = hi - lo
        q1, q2, q3 = lo + w * 0.25, lo + w * 0.5, lo + w * 0.75
        ok1, ok2, ok3 = enough(q1), enough(q2), enough(q3)
        new_lo = jnp.where(ok3, q3, jnp.where(ok2, q2, jnp.where(ok1, q1, lo)))
        new_hi = jnp.where(ok3, hi, jnp.where(ok2, q3, jnp.where(ok1, q2, q1)))
        return new_lo, new_hi

    lo, hi = lax.fori_loop(0, TOPK_REFINE // 2, refine, (lo, hi))
    gt = aff >= hi
    eq = (aff >= lo) & ~gt
    need = cap - count(gt)

    tri = tri_ref[...]

    def prefix(mask):
        mf = mask.astype(F32).astype(BF16)
        parts = []
        off = jnp.zeros((E, 1), F32)
        for c in range(n // LANES):
            blk = mf[:, c * LANES:(c + 1) * LANES]
            parts.append(_dot(blk, tri) + off)
            off = off + jnp.sum(blk.astype(F32), axis=-1, keepdims=True)
        return jnp.concatenate(parts, axis=-1)

    sel = gt | (eq & (prefix(eq) < need))
    pos_ref[...] = jnp.where(sel, prefix(sel), -1.0)

    tok = lax.broadcasted_iota(jnp.int32, (1, n), 1)
    digits = [(tok // TOPK_RADIX).astype(F32), (tok % TOPK_RADIX).astype(F32)]
    chunks = cap // TOPK_SLOTS

    def per_expert(e, carry):
        a = aff_ref[0, pl.ds(e, 1), :]
        a1 = a.astype(BF16).astype(F32)
        a2 = (a - a1).astype(BF16).astype(F32)
        a3 = (a - a1) - a2
        rows = digits + [a1, a2, a3]
        table = jnp.concatenate(rows + [jnp.zeros((TOPK_TABLE_ROWS - len(rows), n), F32)], axis=0).astype(BF16)
        pos_e = pos_ref[pl.ds(e, 1), :]

        hits = []
        for ch in range(chunks):
            slot = (lax.broadcasted_iota(jnp.int32, (TOPK_SLOTS, n), 0) + ch * TOPK_SLOTS).astype(F32)
            hits.append(jnp.where(slot == pos_e, 1.0, 0.0).astype(BF16))
        for ch in range(chunks):
            r = _dot_nt(hits[ch], table)
            idx = r[:, 0:1] * TOPK_RADIX + r[:, 1:2]
            idx_ref[0, e, ch * TOPK_SLOTS:(ch + 1) * TOPK_SLOTS, :] = idx.astype(jnp.int32)
            g_ref[0, e, ch * TOPK_SLOTS:(ch + 1) * TOPK_SLOTS, :] = (r[:, 2:3] + r[:, 3:4]) + r[:, 4:5]
        return carry

    lax.fori_loop(0, E, per_expert, 0)


def _topk(aff_t, cap):
    B, E, n = aff_t.shape
    tri = jnp.asarray(np.triu(np.ones((LANES, LANES), np.float32), k=1), BF16)
    return pl.pallas_call(
        functools.partial(_topk_kernel, cap),
        out_shape=[jax.ShapeDtypeStruct((B, E, cap, 1), jnp.int32),
                   jax.ShapeDtypeStruct((B, E, cap, 1), F32)],
        grid=(B,),
        in_specs=[pl.BlockSpec((1, E, n), lambda b: (b, 0, 0)),
                  pl.BlockSpec((LANES, LANES), lambda b: (0, 0))],
        out_specs=[pl.BlockSpec((1, E, cap, 1), lambda b: (b, 0, 0, 0)),
                   pl.BlockSpec((1, E, cap, 1), lambda b: (b, 0, 0, 0))],
        scratch_shapes=[pltpu.VMEM((E, n), F32)],
        compiler_params=_cparams(("parallel",)),
        name="expert_topk",
    )(aff_t, tri)


def _gather_kernel(s, idx_ref, u_ref, o_ref):
    cap = o_ref.shape[2] // s

    def body(i, carry):
        c0 = pl.multiple_of(i * ROW_GROUP, ROW_GROUP)
        for k in range(ROW_GROUP):
            src = pl.multiple_of(idx_ref[0, 0, 0, c0 + k] * s, s)
            o_ref[0, 0, pl.ds((c0 + k) * s, s), :] = u_ref[0, pl.ds(src, s), :]
        return carry

    lax.fori_loop(0, cap // ROW_GROUP, body, 0)


def _gather(idx, u2, n):
    B, E, _, cap = idx.shape
    _, ns, W = u2.shape
    s = ns // n
    return pl.pallas_call(
        functools.partial(_gather_kernel, s),
        out_shape=jax.ShapeDtypeStruct((B, E, cap * s, W), u2.dtype),
        grid=(B, E),
        in_specs=[pl.BlockSpec((1, 1, 1, cap), lambda b, e: (b, e, 0, 0), memory_space=pltpu.SMEM),
                  pl.BlockSpec((1, ns, W), lambda b, e: (b, 0, 0))],
        out_specs=pl.BlockSpec((1, 1, cap * s, W), lambda b, e: (b, e, 0, 0)),
        compiler_params=_cparams(("parallel", "arbitrary")),
        name="expert_gather",
    )(idx, u2)


def _ffn_kernel(xe_ref, g_ref, wg_ref, wu_ref, wd_ref, o_ref, wgb_ref, wub_ref, wdb_ref):
    @pl.when(pl.program_id(1) == 0)
    def _():
        wgb_ref[...] = wg_ref[0].astype(BF16)
        wub_ref[...] = wu_ref[0].astype(BF16)
        wdb_ref[...] = wd_ref[0].astype(BF16)

    xe = _load_tiles_as_rows(xe_ref.at[0, 0], wgb_ref.shape[0]).astype(BF16)
    gate = _dot(xe, wgb_ref[...])
    up = _dot(xe, wub_ref[...])
    h = (gate * jax.nn.sigmoid(gate) * up).astype(BF16)
    _store_rows_as_tiles(o_ref.at[0, 0], _dot(h, wdb_ref[...]) * g_ref[0, 0])


def _ffn(xe, g, w_gate, w_up, w_down):
    B, E, caps, W = xe.shape
    _, D, F = w_gate.shape
    cap = g.shape[2]
    return pl.pallas_call(
        _ffn_kernel,
        out_shape=jax.ShapeDtypeStruct((B, E, caps, W), F32),
        grid=(E, B),
        in_specs=[pl.BlockSpec((1, 1, caps, W), lambda e, b: (b, e, 0, 0)),
                  pl.BlockSpec((1, 1, cap, 1), lambda e, b: (b, e, 0, 0)),
                  pl.BlockSpec((1, D, F), lambda e, b: (e, 0, 0)),
                  pl.BlockSpec((1, D, F), lambda e, b: (e, 0, 0)),
                  pl.BlockSpec((1, F, D), lambda e, b: (e, 0, 0))],
        out_specs=pl.BlockSpec((1, 1, caps, W), lambda e, b: (b, e, 0, 0)),
        scratch_shapes=[pltpu.VMEM((D, F), BF16), pltpu.VMEM((D, F), BF16), pltpu.VMEM((F, D), BF16)],
        compiler_params=_cparams(("arbitrary", "arbitrary")),
        name="expert_ffn",
    )(xe, g, w_gate, w_up, w_down)


def _scatter_kernel(s, idx_ref, y_ref, o_ref):
    cap = y_ref.shape[2] // s

    def tile(r):
        return pl.ds(pl.multiple_of(r * s, s), s)

    @pl.when(pl.program_id(1) == 0)
    def _():
        o_ref[...] = jnp.zeros_like(o_ref)

    def body(i, carry):
        c0 = pl.multiple_of(i * ROW_GROUP, ROW_GROUP)
        rows = [idx_ref[0, 0, 0, c0 + k] for k in range(ROW_GROUP)]
        acc = [o_ref[0, tile(r), :] for r in rows]
        for k in range(ROW_GROUP):
            o_ref[0, tile(rows[k]), :] = acc[k] + y_ref[0, 0, tile(c0 + k), :]
        return carry

    lax.fori_loop(0, cap // ROW_GROUP, body, 0)


def _scatter(idx, ye, n):
    B, E, caps, W = ye.shape
    cap = idx.shape[3]
    s = caps // cap
    return pl.pallas_call(
        functools.partial(_scatter_kernel, s),
        out_shape=jax.ShapeDtypeStruct((B, n * s, W), F32),
        grid=(B, E),
        in_specs=[pl.BlockSpec((1, 1, 1, cap), lambda b, e: (b, e, 0, 0), memory_space=pltpu.SMEM),
                  pl.BlockSpec((1, 1, caps, W), lambda b, e: (b, e, 0, 0))],
        out_specs=pl.BlockSpec((1, n * s, W), lambda b, e: (b, 0, 0)),
        compiler_params=_cparams(("parallel", "arbitrary")),
        name="expert_scatter",
    )(idx, ye)


def _final_kernel(x_ref, moe_ref, g2_ref, lg_ref, lb_ref, o_ref):
    moe = _load_tiles_as_rows(moe_ref.at[0], x_ref.shape[2])
    o_ref[0] = _ln(ALPHA * x_ref[0] + g2_ref[0] * moe) * lg_ref[...] + lb_ref[...]


def _final(xn, moe, g2, lg, lb, tm):
    B, n, D = xn.shape
    tok = pl.BlockSpec((1, tm, D), lambda b, i: (b, i, 0))
    vec = pl.BlockSpec((1, D), lambda b, i: (0, 0))
    return pl.pallas_call(
        _final_kernel,
        out_shape=jax.ShapeDtypeStruct((B, n, D), F32),
        grid=(B, n // tm),
        in_specs=[tok, pl.BlockSpec((1, tm * (D // LANES), LANES), lambda b, i: (b, i, 0)),
                  pl.BlockSpec((1, 1, D), lambda b, i: (b, 0, 0)), vec, vec],
        out_specs=tok,
        compiler_params=_cparams(("parallel", "parallel")),
        name="final_ln",
    )(xn, moe, g2, lg, lb)


def _rope_tables(n):
    t = np.arange(n)
    row = (t // GRID_W).astype(np.float32)
    col = (t % GRID_W).astype(np.float32)
    per_axis = MLA_ROPE // 2
    inv_freq = jnp.asarray(ROPE_THETA, F32) ** (-jnp.arange(0, per_axis, 2, dtype=F32) / per_axis)
    ang = jnp.concatenate([jnp.asarray(row)[:, None] * inv_freq, jnp.asarray(col)[:, None] * inv_freq], axis=-1)
    cos, sin = jnp.cos(ang), jnp.sin(ang)
    pad = HEAD_PAD - MLA_QK
    c_tab = jnp.concatenate([jnp.ones((n, MLA_NOPE), F32), cos, cos, jnp.zeros((n, pad), F32)], axis=-1)
    s_tab = jnp.concatenate([jnp.zeros((n, MLA_NOPE), F32), -sin, sin, jnp.zeros((n, pad), F32)], axis=-1)
    return c_tab, s_tab


def _prep_weights(w_in, w_uq, w_ukv):
    D = w_in.shape[0]
    o1 = Q_LORA
    o2 = o1 + KV_LORA
    o3 = o2 + MLA_ROPE
    o4 = o3 + 3 * NA_W
    w_kr = w_in[:, o2:o3]
    z = lambda k: jnp.zeros((D, k), w_in.dtype)
    pad = HEAD_PAD - MLA_QK
    w_small = jnp.concatenate([w_in[:, :o2], z(MLA_NOPE), w_kr, z(pad)], axis=-1).astype(BF16)
    w_na = w_in[:, o3:o3 + 2 * NA_W].astype(BF16)
    w_nvt = w_in[:, o3 + 2 * NA_W:o4].T.astype(BF16)
    w_g = w_in[:, o4:].astype(BF16)

    uq = w_uq.reshape(Q_LORA, MLA_HEADS, MLA_QK)
    zq = lambda k: jnp.zeros((Q_LORA, MLA_HEADS, k), w_uq.dtype)
    wqa = jnp.concatenate([uq, zq(pad)], axis=-1).reshape(Q_LORA, MLA_HEADS * HEAD_PAD).astype(BF16)

    ukv = w_ukv.reshape(KV_LORA, MLA_HEADS, MLA_NOPE + MLA_V)
    wk = jnp.concatenate([ukv[..., :MLA_NOPE], jnp.zeros((KV_LORA, MLA_HEADS, HEAD_PAD - MLA_NOPE), w_ukv.dtype)], axis=-1)
    wk = wk.reshape(KV_LORA, MLA_HEADS * HEAD_PAD).astype(BF16)
    wv = ukv[..., MLA_NOPE:].reshape(KV_LORA, MLA_HEADS * MLA_V).T.astype(BF16)
    return w_small, w_na, w_nvt, w_g, wqa, wk, wv


def _tile(n, pref):
    t = min(pref, n)
    while n % t:
        t //= 2
    return t


def kernel(x, c, ctx, c_ctx, w_mod, b_mod, w_in, q_norm_g, w_uq, kv_norm_g, w_ukv, na_rel_bias, w_proj_mla,
           w_proj_na, w_out, ln1_g, ln1_b, w_router, w_exp_gate, w_exp_up, w_exp_down, ln2_g, ln2_b):
    B, n, D = x.shape
    L = ctx.shape[1]
    rows = n // GRID_W
    assert n % (GRID_W * NA_QROWS) == 0 and rows >= NA_BAND
    assert w_mod.shape[0] == DEPTH
    cap = EC_CAPACITY * n // N_EXPERTS

    mod_rows = -(-(B + 1) // 8) * 8
    cc = jnp.concatenate([c, c_ctx[None], jnp.zeros((mod_rows - B - 1, D), F32)], axis=0)
    m = _modulation(cc, w_mod[0], b_mod[0])
    sh1, sc1, g1, sh2, sc2, g2 = [m[:B, k * D:(k + 1) * D].reshape(B, 1, D) for k in range(6)]
    csh1 = m[B:B + 1, :D].reshape(1, 1, D)
    csc1 = m[B:B + 1, D:2 * D].reshape(1, 1, D)

    w_small, w_na, w_nvt, w_g, wqa, wk, wv = _prep_weights(w_in[0], w_uq[0], w_ukv[0])
    qg = q_norm_g[0].reshape(1, Q_LORA)
    kvg = kv_norm_g[0].reshape(1, KV_LORA)
    c_tab, s_tab = _rope_tables(n)
    pad = HEAD_PAD - MLA_QK
    c_ctx_tab = jnp.concatenate([jnp.ones((L, MLA_QK), F32), jnp.zeros((L, pad), F32)], axis=-1)
    s_ctx_tab = jnp.zeros((L, HEAD_PAD), F32)

    q, k_lat, v_lat, nq, nk, nv, gates = _inproj(
        True, x, sc1, sh1, c_tab, s_tab, c_tab * (MLA_SCALE * LOG2E), s_tab * (MLA_SCALE * LOG2E),
        w_small, w_na, w_nvt, w_g, qg, kvg, wqa, wk, wv, _tile(n, 512))
    k_ctx, v_ctx, cnk, cnv = _inproj(
        False, ctx, csc1, csh1, c_ctx_tab, s_ctx_tab, c_ctx_tab, s_ctx_tab,
        w_small, w_na[:, NA_W:], w_nvt, w_g, qg, kvg, wqa, wk, wv, _tile(L, 256))

    y_mla = _mla_attention(q, k_lat, k_ctx, v_lat, v_ctx, _tile(n, 1024), _tile(n, 512))
    y_na = _na_attention(nq, nk, nv, cnk, cnv, _na_bias_tiles(na_rel_bias[0], rows))

    xn, u2, aff = _merge(
        y_mla, y_na, gates, x, g1, sc2, sh2,
        w_proj_mla[0].astype(BF16), w_proj_na[0].astype(BF16), w_out[0].astype(BF16),
        ln1_g[0].reshape(1, D), ln1_b[0].reshape(1, D), w_router[0], _tile(n, 2 * MERGE_SUB))

    idx4, g4 = _topk(jnp.swapaxes(aff, 1, 2), cap)
    idx = idx4.reshape(B, N_EXPERTS, 1, cap)
    xe = _gather(idx, u2, n)
    ye = _ffn(xe, g4, w_exp_gate[0], w_exp_up[0], w_exp_down[0])
    moe = _scatter(idx, ye, n)
    return _final(xn, moe, g2, ln2_g[0].reshape(1, D), ln2_b[0].reshape(1, D), _tile(n, 512))
```

```python
import functools
import math

import numpy as np
import jax
import jax.numpy as jnp
from jax import lax
from jax.experimental import pallas as pl
from jax.experimental.pallas import tpu as pltpu

GRID_W = 64
MLA_HEADS = 8
MLA_NOPE = 64
MLA_ROPE = 32
MLA_QK = MLA_NOPE + MLA_ROPE
MLA_V = 64
Q_LORA = 256
KV_LORA = 128
MLA_SCALE = MLA_QK ** -0.5
ROPE_THETA = 10000.0
NA_HEADS = 8
NA_DIM = 64
NA_W = NA_HEADS * NA_DIM
NA_WIN_H = 8
NA_WIN_W = 16
NA_SCALE = NA_DIM ** -0.5
N_EXPERTS = 16
EC_CAPACITY = 2
LN_EPS = 1e-5
RMS_EPS = 1e-6
DEPTH = 1
ALPHA = (2.0 * DEPTH) ** 0.25
LOG2E = math.log2(math.e)

LANES = 128
HEAD_PAD = LANES
VMEM_LIMIT = 56 * 1024 * 1024
NEG_BIG = -1e30

NA_QROWS = 4
NA_BAND = 12
EXPERT_GROUP = 2
ROW_GROUP = 8
NA_LOOKAHEAD = 2
MERGE_SUB = 256
TOPK_RADIX = 64
TOPK_TABLE_ROWS = 16
TOPK_SLOTS = 128
TOPK_REFINE = 30

BF16 = jnp.bfloat16
F32 = jnp.float32


def _cparams(sem):
    return pltpu.CompilerParams(dimension_semantics=sem, vmem_limit_bytes=VMEM_LIMIT)


def _ln(x):
    mu = jnp.mean(x, axis=-1, keepdims=True)
    xc = x - mu
    var = jnp.mean(xc * xc, axis=-1, keepdims=True)
    return xc * lax.rsqrt(var + LN_EPS)


def _dot(a, b):
    return jnp.dot(a, b, preferred_element_type=F32)


def _dot_nt(a, b):
    return lax.dot_general(a, b, (((1,), (1,)), ((), ())), preferred_element_type=F32)


def _store_rows_as_tiles(ref, val):
    m, d = val.shape
    s = d // LANES
    for j in range(s):
        ref[pl.ds(j, m, stride=s), :] = val[:, j * LANES:(j + 1) * LANES]


def _load_tiles_as_rows(ref, d):
    s = d // LANES
    m = ref.shape[0] // s
    return jnp.concatenate([ref[pl.ds(j, m, stride=s), :] for j in range(s)], axis=-1)


def _mod_kernel(c_ref, w_ref, b_ref, o_ref):
    c = c_ref[...]
    s = c * jax.nn.sigmoid(c)
    o_ref[...] = jnp.dot(s, w_ref[...], preferred_element_type=F32,
                         precision=lax.Precision.HIGHEST) + b_ref[...]


def _modulation(cc, w_mod, b_mod):
    rows, d = cc.shape
    n_out = w_mod.shape[1]
    tn = 1024
    return pl.pallas_call(
        _mod_kernel,
        out_shape=jax.ShapeDtypeStruct((rows, n_out), F32),
        grid=(n_out // tn,),
        in_specs=[pl.BlockSpec((rows, d), lambda j: (0, 0)),
                  pl.BlockSpec((d, tn), lambda j: (0, j)),
                  pl.BlockSpec((1, tn), lambda j: (0, j))],
        out_specs=pl.BlockSpec((rows, tn), lambda j: (0, j)),
        compiler_params=_cparams(("arbitrary",)),
        name="modulation",
    )(cc, w_mod, b_mod.reshape(1, n_out))


def _swap_rope_halves(x):
    half = MLA_ROPE // 2
    lane = lax.broadcasted_iota(jnp.int32, x.shape, 1)
    return jnp.where(lane < MLA_NOPE + half, pltpu.roll(x, HEAD_PAD - half, 1), pltpu.roll(x, half, 1))


def _inproj_kernel(latent, x_ref, sc_ref, sh_ref, ck_ref, sk_ref, cq_ref, sq_ref,
                   w_small_ref, w_na_ref, w_nvt_ref, w_g_ref, qg_ref, kvg_ref, wqa_ref, wk_ref, wv_ref,
                   *out_refs):
    if latent:
        q_ref, k_ref, v_ref, nq_ref, nk_ref, nv_ref, g_ref = out_refs
    else:
        k_ref, v_ref, nk_ref, nv_ref = out_refs
    x = x_ref[0]
    u = (_ln(x) * (1.0 + sc_ref[0]) + sh_ref[0]).astype(BF16)

    small = _dot(u, w_small_ref[...])
    q_c = small[:, :Q_LORA]
    kv_c = small[:, Q_LORA:Q_LORA + KV_LORA]
    k_r = small[:, Q_LORA + KV_LORA:]

    kvn = (kv_c * lax.rsqrt(jnp.mean(kv_c * kv_c, axis=-1, keepdims=True) + RMS_EPS) * kvg_ref[...]).astype(BF16)
    kk = _dot(kvn, wk_ref[...])
    kr = k_r * ck_ref[...] + _swap_rope_halves(k_r) * sk_ref[...]
    for h in range(MLA_HEADS):
        k_ref[0, h] = (kk[:, h * HEAD_PAD:(h + 1) * HEAD_PAD] + kr).astype(BF16)
    v_ref[0] = _dot_nt(wv_ref[...], kvn).astype(BF16)

    na = _dot(u, w_na_ref[...])
    nv_ref[0] = _dot_nt(w_nvt_ref[...], u).astype(BF16)
    if latent:
        qn = (q_c * lax.rsqrt(jnp.mean(q_c * q_c, axis=-1, keepdims=True) + RMS_EPS) * qg_ref[...]).astype(BF16)
        qa = _dot(qn, wqa_ref[...])
        cq = cq_ref[...]
        sq = sq_ref[...]
        for h in range(MLA_HEADS):
            qh = qa[:, h * HEAD_PAD:(h + 1) * HEAD_PAD]
            q_ref[0, h] = (qh * cq + _swap_rope_halves(qh) * sq).astype(BF16)
        nq_ref[0] = (na[:, :NA_W] * (NA_SCALE * LOG2E)).astype(BF16)
        nk_ref[0] = na[:, NA_W:].astype(BF16)
        g_ref[0] = jax.nn.sigmoid(_dot(u, w_g_ref[...])).astype(BF16)
    else:
        nk_ref[0] = na.astype(BF16)


def _inproj(latent, x, sc, sh, ck, sk, cq, sq, w_small, w_na, w_nvt, w_g, qg, kvg, wqa, wk, wv, tm):
    B, n, D = x.shape
    per_batch = sc.shape[0] > 1
    mod_map = (lambda b, i: (b, 0, 0)) if per_batch else (lambda b, i: (0, 0, 0))
    full = lambda a: pl.BlockSpec(a.shape, lambda b, i: (0,) * a.ndim)
    tab = lambda a: pl.BlockSpec((tm, a.shape[1]), lambda b, i: (i, 0))
    in_specs = [pl.BlockSpec((1, tm, D), lambda b, i: (b, i, 0)),
                pl.BlockSpec((1, 1, D), mod_map), pl.BlockSpec((1, 1, D), mod_map),
                tab(ck), tab(sk), tab(cq), tab(sq),
                full(w_small), full(w_na), full(w_nvt), full(w_g), full(qg), full(kvg),
                full(wqa), full(wk), full(wv)]
    hk = jax.ShapeDtypeStruct((B, MLA_HEADS, n, HEAD_PAD), BF16)
    hk_spec = pl.BlockSpec((1, MLA_HEADS, tm, HEAD_PAD), lambda b, i: (b, 0, i, 0))
    tok = lambda w: jax.ShapeDtypeStruct((B, n, w), BF16)
    tok_spec = lambda w: pl.BlockSpec((1, tm, w), lambda b, i: (b, i, 0))
    vt = jax.ShapeDtypeStruct((B, MLA_HEADS * MLA_V, n), BF16)
    vt_spec = pl.BlockSpec((1, MLA_HEADS * MLA_V, tm), lambda b, i: (b, 0, i))
    if latent:
        out_shape = [hk, hk, vt, tok(NA_W), tok(NA_W), vt, tok(2 * D)]
        out_specs = [hk_spec, hk_spec, vt_spec, tok_spec(NA_W), tok_spec(NA_W), vt_spec, tok_spec(2 * D)]
    else:
        out_shape = [hk, vt, tok(NA_W), vt]
        out_specs = [hk_spec, vt_spec, tok_spec(NA_W), vt_spec]
    return pl.pallas_call(
        functools.partial(_inproj_kernel, latent),
        out_shape=out_shape,
        grid=(B, n // tm),
        in_specs=in_specs,
        out_specs=out_specs,
        compiler_params=_cparams(("parallel", "parallel")),
        name="inproj_latent" if latent else "inproj_ctx",
    )(x, sc, sh, ck, sk, cq, sq, w_small, w_na, w_nvt, w_g, qg, kvg, wqa, wk, wv)


def _mla_kernel(tk, q_ref, kc_ref, kl_ref, vc_ref, vl_ref, o_ref, m_ref, l_ref, acc_ref, sa_ref, sb_ref, sc_ref):
    heads = q_ref.shape[1]
    n = kl_ref.shape[2]
    tq = q_ref.shape[2]
    nt = n // tk
    row = lax.broadcasted_iota(jnp.int32, (LANES, tq), 0)

    def scores(h, k):
        return _dot_nt(k, q_ref[0, h])

    def absorb(hh, s, v_t):
        m_old = m_ref[hh]
        m_new = jnp.maximum(m_old, jnp.max(s, axis=0, keepdims=True))
        a = jnp.exp2(m_old - m_new)
        p = jnp.exp2(s - m_new)
        l_ref[hh] = a * l_ref[hh] + jnp.sum(p, axis=0, keepdims=True)
        acc_ref[hh] = a * acc_ref[hh] + _dot(v_t, p.astype(BF16))
        m_ref[hh] = m_new

    def ctx_scores(h0):
        for hh in range(2):
            sc_ref[hh] = scores(h0 + hh, kc_ref[0, h0 + hh])

    def pair_pass(pair, carry):
        h0 = 2 * pair
        rows = pl.ds(pl.multiple_of(pair * LANES, LANES), LANES)

        def k_tile(hh, j):
            return kl_ref[0, h0 + hh, pl.ds(pl.multiple_of(j * tk, tk), tk), :]

        def v_tile(j):
            return vl_ref[0, rows, pl.ds(pl.multiple_of(j * tk, tk), tk)]

        def advance(j, cur_ref, next_ref):
            v_t = v_tile(j)
            for hh in range(2):
                next_ref[hh] = scores(h0 + hh, k_tile(hh, j + 1))
                absorb(hh, cur_ref[hh], v_t)

        m_ref[...] = jnp.full(m_ref.shape, NEG_BIG, F32)
        l_ref[...] = jnp.zeros(l_ref.shape, F32)
        acc_ref[...] = jnp.zeros(acc_ref.shape, F32)
        v_ctx = vc_ref[0, rows, :]
        for hh in range(2):
            sa_ref[hh] = scores(h0 + hh, k_tile(hh, 0))
            absorb(hh, sc_ref[hh], v_ctx)

        def body(jj, c):
            advance(2 * jj, sa_ref, sb_ref)
            advance(2 * jj + 1, sb_ref, sa_ref)
            return c

        lax.fori_loop(0, (nt - 1) // 2, body, 0)
        last_ref = sa_ref
        if (nt - 1) % 2:
            advance(nt - 2, sa_ref, sb_ref)
            last_ref = sb_ref
        ctx_scores(jnp.minimum(h0 + 2, heads - 2))
        v_t = v_tile(nt - 1)
        for hh in range(2):
            absorb(hh, last_ref[hh], v_t)
        out_t = jnp.where(row < MLA_V, acc_ref[0] / l_ref[0], acc_ref[1] / l_ref[1])
        o_ref[0, :, rows] = out_t.T.astype(o_ref.dtype)
        return carry

    ctx_scores(0)
    lax.fori_loop(0, heads // 2, pair_pass, 0)


def _mla_attention(q, k_lat, k_ctx, v_lat, v_ctx, tq, tk):
    B, H, n, _ = q.shape
    L = k_ctx.shape[2]
    W = H * MLA_V
    return pl.pallas_call(
        functools.partial(_mla_kernel, tk),
        out_shape=jax.ShapeDtypeStruct((B, n, W), BF16),
        grid=(B, n // tq),
        in_specs=[pl.BlockSpec((1, H, tq, HEAD_PAD), lambda b, i: (b, 0, i, 0)),
                  pl.BlockSpec((1, H, L, HEAD_PAD), lambda b, i: (b, 0, 0, 0)),
                  pl.BlockSpec((1, H, n, HEAD_PAD), lambda b, i: (b, 0, 0, 0)),
                  pl.BlockSpec((1, W, L), lambda b, i: (b, 0, 0)),
                  pl.BlockSpec((1, W, n), lambda b, i: (b, 0, 0))],
        out_specs=pl.BlockSpec((1, tq, W), lambda b, i: (b, i, 0)),
        scratch_shapes=[pltpu.VMEM((2, 1, tq), F32), pltpu.VMEM((2, 1, tq), F32),
                        pltpu.VMEM((2, LANES, tq), F32),
                        pltpu.VMEM((2, tk, tq), F32), pltpu.VMEM((2, tk, tq), F32),
                        pltpu.VMEM((2, L, tq), F32)],
        compiler_params=_cparams(("parallel", "parallel")),
        name="mla_attention",
    )(q, k_ctx, k_lat, v_ctx, v_lat)


def _na_block_tables(rows):
    nblk = rows // NA_QROWS
    wh = min(NA_WIN_H, rows)
    band0 = np.clip(np.arange(nblk) * NA_QROWS - wh // 2, 0, rows - NA_BAND)
    sigs, types = [], []
    for i in range(nblk):
        r = i * NA_QROWS + np.arange(NA_QROWS)
        r0 = np.clip(r - wh // 2, 0, rows - wh)
        sig = (tuple(r0 - band0[i]), tuple(r - band0[i]))
        if sig not in sigs:
            sigs.append(sig)
        types.append(sigs.index(sig))
    return band0.astype(np.int32), np.asarray(types, np.int32), sigs, wh


def _bias_tile_kernel(dr, ok, tbl_ref, o_ref):
    neg = jnp.full((GRID_W, GRID_W), NEG_BIG, F32)
    for t in range(o_ref.shape[0]):
        for k in range(NA_BAND):
            pieces = [tbl_ref[0, int(dr[t, q, k])] if ok[t, q, k] else neg for q in range(NA_QROWS)]
            o_ref[t, 0, k * GRID_W:(k + 1) * GRID_W, :] = jnp.concatenate(pieces, axis=1)


def _na_bias_tiles(rel_bias, rows):
    _, _, sigs, wh = _na_block_tables(rows)
    ww = NA_WIN_W
    n_dr, n_dc = 2 * NA_WIN_H - 1, 2 * NA_WIN_W - 1
    col = np.arange(GRID_W)
    c0 = np.clip(col - ww // 2, 0, GRID_W - ww)
    col_ok = (col[:, None] >= c0[None, :]) & (col[:, None] < c0[None, :] + ww)
    dc = col[:, None] - col[None, :] + (NA_WIN_W - 1)
    pick_c = ((dc[None] == np.arange(n_dc)[:, None, None]) & col_ok[None]).astype(np.float32)
    by_col = jnp.einsum('hdj,jyx->hdyx', rel_bias.astype(F32), jnp.asarray(pick_c), precision=lax.Precision.HIGHEST)
    by_col = jnp.where(jnp.asarray(col_ok), by_col * LOG2E, NEG_BIG)
    kr = np.arange(NA_BAND)
    ok = np.stack([(kr[None, :] >= np.asarray(r0)[:, None]) & (kr[None, :] < np.asarray(r0)[:, None] + wh)
                   for r0, _ in sigs])
    dr = np.stack([kr[None, :] - np.asarray(r)[:, None] + (NA_WIN_H - 1) for _, r in sigs])
    assert ((dr >= 0) & (dr < n_dr))[ok].all()
    T = len(sigs)
    kn, qn = NA_BAND * GRID_W, NA_QROWS * GRID_W
    return pl.pallas_call(
        functools.partial(_bias_tile_kernel, dr, ok),
        out_shape=jax.ShapeDtypeStruct((T, NA_HEADS, kn, qn), F32),
        grid=(NA_HEADS,),
        in_specs=[pl.BlockSpec((1, n_dr, GRID_W, GRID_W), lambda h: (h, 0, 0, 0))],
        out_specs=pl.BlockSpec((T, 1, kn, qn), lambda h: (0, h, 0, 0)),
        compiler_params=_cparams(("parallel",)),
        name="na_bias_tiles",
    )(by_col)


def _na_kernel(nblk, band_ref, type_ref, q_ref, k_ref, vt_ref, kc_ref, vct_ref, *rest):
    bias_refs, o_ref = rest[:nblk], rest[nblk]
    i = pl.program_id(1)
    nq = NA_QROWS * GRID_W
    nk = NA_BAND * GRID_W
    starts = [pl.multiple_of(band_ref[i * nblk + b] * GRID_W, NA_QROWS * GRID_W) for b in range(nblk)]
    lane = lax.broadcasted_iota(jnp.int32, (nq, LANES), 1)
    row = lax.broadcasted_iota(jnp.int32, (LANES, nq), 0)

    def scores(c):
        blk, h = divmod(c, NA_HEADS)
        cs = slice((h // 2) * LANES, (h // 2 + 1) * LANES)
        qp = q_ref[0, blk * nq:(blk + 1) * nq, cs]
        own = (lane < NA_DIM) if h % 2 == 0 else (lane >= NA_DIM)
        qh = jnp.where(own, qp, jnp.zeros_like(qp))
        s_loc = _dot_nt(k_ref[0, pl.ds(starts[blk], nk), cs], qh) + bias_refs[blk][0, h]
        s_ctx = _dot_nt(kc_ref[0, :, cs], qh)
        return s_loc, s_ctx

    def attend(c, s):
        s_loc, s_ctx = s
        blk, h = divmod(c, NA_HEADS)
        start = starts[blk]
        cs = slice((h // 2) * LANES, (h // 2 + 1) * LANES)
        m = jnp.maximum(jnp.max(s_loc, axis=0, keepdims=True), jnp.max(s_ctx, axis=0, keepdims=True))
        p_loc = jnp.exp2(s_loc - m)
        p_ctx = jnp.exp2(s_ctx - m)
        l = jnp.sum(p_loc, axis=0, keepdims=True) + jnp.sum(p_ctx, axis=0, keepdims=True)
        o = _dot(vt_ref[0, cs, pl.ds(start, nk)], p_loc.astype(BF16)) + _dot(vct_ref[0, cs, :], p_ctx.astype(BF16))
        return o / l

    chain = nblk * NA_HEADS
    ahead = [scores(c) for c in range(min(NA_LOOKAHEAD, chain))]
    outs = []
    for c in range(chain):
        if c + NA_LOOKAHEAD < chain:
            ahead.append(scores(c + NA_LOOKAHEAD))
        outs.append(attend(c, ahead.pop(0)))
        if c % 2:
            blk, h = divmod(c, NA_HEADS)
            pair_t = jnp.where(row < NA_DIM, outs[c - 1], outs[c])
            o_ref[0, blk * nq:(blk + 1) * nq, (h // 2) * LANES:(h // 2 + 1) * LANES] = pair_t.T.astype(o_ref.dtype)


def _na_attention(nq, nk, nvt, cnk, cnvt, bias_tiles):
    B, n, W = nq.shape
    L = cnk.shape[1]
    rows = n // GRID_W
    band0, types, _, _ = _na_block_tables(rows)
    qn = NA_QROWS * GRID_W
    kn = NA_BAND * GRID_W
    nblocks = rows // NA_QROWS
    nblk = 2 if nblocks % 2 == 0 else 1

    def bias_spec(k):
        return pl.BlockSpec((1, NA_HEADS, kn, qn), lambda b, i, bd, ty: (ty[i * nblk + k], 0, 0, 0))

    grid_spec = pltpu.PrefetchScalarGridSpec(
        num_scalar_prefetch=2,
        grid=(B, nblocks // nblk),
        in_specs=[pl.BlockSpec((1, nblk * qn, W), lambda b, i, bd, ty: (b, i, 0)),
                  pl.BlockSpec((1, n, W), lambda b, i, bd, ty: (b, 0, 0)),
                  pl.BlockSpec((1, W, n), lambda b, i, bd, ty: (b, 0, 0)),
                  pl.BlockSpec((1, L, W), lambda b, i, bd, ty: (b, 0, 0)),
                  pl.BlockSpec((1, W, L), lambda b, i, bd, ty: (b, 0, 0))] + [bias_spec(k) for k in range(nblk)],
        out_specs=pl.BlockSpec((1, nblk * qn, W), lambda b, i, bd, ty: (b, i, 0)),
    )
    return pl.pallas_call(
        functools.partial(_na_kernel, nblk),
        out_shape=jax.ShapeDtypeStruct((B, n, W), BF16),
        grid_spec=grid_spec,
        compiler_params=_cparams(("parallel", "arbitrary")),
        name="na_attention",
    )(jnp.asarray(band0), jnp.asarray(types), nq, nk, nvt, cnk, cnvt, *([bias_tiles] * nblk))


def _split2(a):
    hi = a.astype(BF16)
    return hi, (a - hi.astype(F32)).astype(BF16)


def _merge_kernel(ym_ref, yn_ref, g_ref, x_ref, g1_ref, sc2_ref, sh2_ref, wpm_ref, wpn_ref, wo_ref,
                  l1g_ref, l1b_ref, wr_ref, xn_ref, up_ref, aff_ref):
    tm, D = x_ref.shape[1], x_ref.shape[2]
    s = D // LANES
    sub = min(MERGE_SUB, tm)
    wh, wl = _split2(wr_ref[...])

    def mix_of(r):
        g = g_ref[0, r, :]
        a = (g[:, :D].astype(F32) * _dot(ym_ref[0, r, :], wpm_ref[...])
             + g[:, D:].astype(F32) * _dot(yn_ref[0, r, :], wpn_ref[...]))
        return _dot(a.astype(BF16), wo_ref[...])

    def finish(r, mix):
        xn = _ln(ALPHA * x_ref[0, r, :] + g1_ref[0] * mix) * l1g_ref[...] + l1b_ref[...]
        xn_ref[0, r, :] = xn
        u2 = _ln(xn) * (1.0 + sc2_ref[0]) + sh2_ref[0]
        _store_rows_as_tiles(up_ref.at[0, r.start * s:r.stop * s], u2)
        uh, ul = _split2(u2)
        logits = _dot(uh, wh) + (_dot(uh, wl) + _dot(ul, wh))
        e = jnp.exp(logits - jnp.max(logits, axis=-1, keepdims=True))
        aff_ref[0, r, :] = e / jnp.sum(e, axis=-1, keepdims=True)

    rows = [slice(i, i + sub) for i in range(0, tm, sub)]
    mix = mix_of(rows[0])
    for i, r in enumerate(rows):
        mix_next = mix_of(rows[i + 1]) if i + 1 < len(rows) else None
        finish(r, mix)
        mix = mix_next


def _merge(y_mla, y_na, gates, x, g1, sc2, sh2, wpm, wpn, wo, l1g, l1b, wr, tm):
    B, n, D = x.shape
    E = wr.shape[1]
    full = lambda a: pl.BlockSpec(a.shape, lambda b, i: (0,) * a.ndim)
    tok = lambda w: pl.BlockSpec((1, tm, w), lambda b, i: (b, i, 0))
    mod = pl.BlockSpec((1, 1, D), lambda b, i: (b, 0, 0))
    return pl.pallas_call(
        _merge_kernel,
        out_shape=[jax.ShapeDtypeStruct((B, n, D), F32),
                   jax.ShapeDtypeStruct((B, n * (D // LANES), LANES), F32),
                   jax.ShapeDtypeStruct((B, n, E), F32)],
        grid=(B, n // tm),
        in_specs=[tok(y_mla.shape[2]), tok(y_na.shape[2]), tok(2 * D), tok(D), mod, mod, mod,
                  full(wpm), full(wpn), full(wo), full(l1g), full(l1b), full(wr)],
        out_specs=[tok(D), pl.BlockSpec((1, tm * (D // LANES), LANES), lambda b, i: (b, i, 0)), tok(E)],
        compiler_params=_cparams(("parallel", "parallel")),
        name="merge_ln_router",
    )(y_mla, y_na, gates, x, g1, sc2, sh2, wpm, wpn, wo, l1g, l1b, wr)


def _topk_kernel(cap, aff_ref, tri_ref, idx_ref, g_ref, pos_ref):
    aff = aff_ref[0]
    E, n = aff.shape

    def count(mask):
        return jnp.sum(mask.astype(F32), axis=-1, keepdims=True)

    def enough(v):
        return count(aff >= v) >= cap

    def search(t, thr):
        b1 = jnp.int32(1) << (30 - 2 * t)
        b0 = b1 >> 1
        c3, c2, c1 = thr | b1 | b0, thr | b1, thr | b0
        ok3, ok2, ok1 = [enough(pltpu.bitcast(c, F32)) for c in (c3, c2, c1)]
        return jnp.where(ok3, c3, jnp.where(ok2, c2, jnp.where(ok1, c1, thr)))

    thr = lax.fori_loop(0, 16, search, jnp.zeros((E, 1), jnp.int32))
    lo = pltpu.bitcast(thr, F32)
    hi = pltpu.bitcast(thr + 1, F32)

    def refine(t, lh):
        lo, hi = lh
        w = hi - lo
        q1, q2, q3 = lo + w * 0.25, lo + w * 0.5, lo + w * 0.75
        ok1, ok2, ok3 = enough(q1), enough(q2), enough(q3)
        new_lo = jnp.where(ok3, q3, jnp.where(ok2, q2, jnp.where(ok1, q1, lo)))
        new_hi = jnp.where(ok3, hi, jnp.where(ok2, q3, jnp.where(ok1, q2, q1)))
        return new_lo, new_hi

    lo, hi = lax.fori_loop(0, TOPK_REFINE // 2, refine, (lo, hi))
    gt = aff >= hi
    eq = (aff >= lo) & ~gt
    need = cap - count(gt)

    tri = tri_ref[...]

    def prefix(mask):
        mf = mask.astype(F32).astype(BF16)
        parts = []
        off = jnp.zeros((E, 1), F32)
        for c in range(n // LANES):
            blk = mf[:, c * LANES:(c + 1) * LANES]
            parts.append(_dot(blk, tri) + off)
            off = off + jnp.sum(blk.astype(F32), axis=-1, keepdims=True)
        return jnp.concatenate(parts, axis=-1)

    sel = gt | (eq & (prefix(eq) < need))
    pos_ref[...] = jnp.where(sel, prefix(sel), -1.0)

    tok = lax.broadcasted_iota(jnp.int32, (1, n), 1)
    digits = [(tok // TOPK_RADIX).astype(F32), (tok % TOPK_RADIX).astype(F32)]
    chunks = cap // TOPK_SLOTS

    def per_expert(e, carry):
        a = aff_ref[0, pl.ds(e, 1), :]
        a1 = a.astype(BF16).astype(F32)
        a2 = (a - a1).astype(BF16).astype(F32)
        a3 = (a - a1) - a2
        rows = digits + [a1, a2, a3]
        table = jnp.concatenate(rows + [jnp.zeros((TOPK_TABLE_ROWS - len(rows), n), F32)], axis=0).astype(BF16)
        pos_e = pos_ref[pl.ds(e, 1), :]

        hits = []
        for ch in range(chunks):
            slot = (lax.broadcasted_iota(jnp.int32, (TOPK_SLOTS, n), 0) + ch * TOPK_SLOTS).astype(F32)
            hits.append(jnp.where(slot == pos_e, 1.0, 0.0).astype(BF16))
        for ch in range(chunks):
            r = _dot_nt(hits[ch], table)
            idx = r[:, 0:1] * TOPK_RADIX + r[:, 1:2]
            idx_ref[0, e, ch * TOPK_SLOTS:(ch + 1) * TOPK_SLOTS, :] = idx.astype(jnp.int32)
            g_ref[0, e, ch * TOPK_SLOTS:(ch + 1) * TOPK_SLOTS, :] = (r[:, 2:3] + r[:, 3:4]) + r[:, 4:5]
        return carry

    lax.fori_loop(0, E, per_expert, 0)


def _topk(aff_t, cap):
    B, E, n = aff_t.shape
    tri = jnp.asarray(np.triu(np.ones((LANES, LANES), np.float32), k=1), BF16)
    return pl.pallas_call(
        functools.partial(_topk_kernel, cap),
        out_shape=[jax.ShapeDtypeStruct((B, E, cap, 1), jnp.int32),
                   jax.ShapeDtypeStruct((B, E, cap, 1), F32)],
        grid=(B,),
        in_specs=[pl.BlockSpec((1, E, n), lambda b: (b, 0, 0)),
                  pl.BlockSpec((LANES, LANES), lambda b: (0, 0))],
        out_specs=[pl.BlockSpec((1, E, cap, 1), lambda b: (b, 0, 0, 0)),
                   pl.BlockSpec((1, E, cap, 1), lambda b: (b, 0, 0, 0))],
        scratch_shapes=[pltpu.VMEM((E, n), F32)],
        compiler_params=_cparams(("parallel",)),
        name="expert_topk",
    )(aff_t, tri)


def _gather_kernel(s, idx_ref, u_ref, o_ref):
    cap = o_ref.shape[2] // s

    for g in range(o_ref.shape[1]):
        def body(i, carry, g=g):
            c0 = pl.multiple_of(i * ROW_GROUP, ROW_GROUP)
            for k in range(ROW_GROUP):
                src = pl.multiple_of(idx_ref[0, g, 0, c0 + k] * s, s)
                o_ref[0, g, pl.ds((c0 + k) * s, s), :] = u_ref[0, pl.ds(src, s), :]
            return carry

        lax.fori_loop(0, cap // ROW_GROUP, body, 0)


def _gather(idx, u2, n):
    B, E, _, cap = idx.shape
    _, ns, W = u2.shape
    s = ns // n
    eg = _tile(E, EXPERT_GROUP)
    return pl.pallas_call(
        functools.partial(_gather_kernel, s),
        out_shape=jax.ShapeDtypeStruct((B, E, cap * s, W), u2.dtype),
        grid=(B, E // eg),
        in_specs=[pl.BlockSpec((1, eg, 1, cap), lambda b, e: (b, e, 0, 0), memory_space=pltpu.SMEM),
                  pl.BlockSpec((1, ns, W), lambda b, e: (b, 0, 0))],
        out_specs=pl.BlockSpec((1, eg, cap * s, W), lambda b, e: (b, e, 0, 0)),
        compiler_params=_cparams(("parallel", "arbitrary")),
        name="expert_gather",
    )(idx, u2)


def _ffn_kernel(xe_ref, g_ref, wg_ref, wu_ref, wd_ref, o_ref, wgb_ref, wub_ref, wdb_ref):
    @pl.when(pl.program_id(1) == 0)
    def _():
        wgb_ref[...] = wg_ref[0].astype(BF16)
        wub_ref[...] = wu_ref[0].astype(BF16)
        wdb_ref[...] = wd_ref[0].astype(BF16)

    xe = _load_tiles_as_rows(xe_ref.at[0, 0], wgb_ref.shape[0]).astype(BF16)
    gate = _dot(xe, wgb_ref[...])
    up = _dot(xe, wub_ref[...])
    h = (gate * jax.nn.sigmoid(gate) * up).astype(BF16)
    _store_rows_as_tiles(o_ref.at[0, 0], _dot(h, wdb_ref[...]) * g_ref[0, 0])


def _ffn(xe, g, w_gate, w_up, w_down):
    B, E, caps, W = xe.shape
    _, D, F = w_gate.shape
    cap = g.shape[2]
    return pl.pallas_call(
        _ffn_kernel,
        out_shape=jax.ShapeDtypeStruct((B, E, caps, W), F32),
        grid=(E, B),
        in_specs=[pl.BlockSpec((1, 1, caps, W), lambda e, b: (b, e, 0, 0)),
                  pl.BlockSpec((1, 1, cap, 1), lambda e, b: (b, e, 0, 0)),
                  pl.BlockSpec((1, D, F), lambda e, b: (e, 0, 0)),
                  pl.BlockSpec((1, D, F), lambda e, b: (e, 0, 0)),
                  pl.BlockSpec((1, F, D), lambda e, b: (e, 0, 0))],
        out_specs=pl.BlockSpec((1, 1, caps, W), lambda e, b: (b, e, 0, 0)),
        scratch_shapes=[pltpu.VMEM((D, F), BF16), pltpu.VMEM((D, F), BF16), pltpu.VMEM((F, D), BF16)],
        compiler_params=_cparams(("arbitrary", "arbitrary")),
        name="expert_ffn",
    )(xe, g, w_gate, w_up, w_down)


def _scatter_kernel(s, idx_ref, y_ref, o_ref):
    cap = y_ref.shape[2] // s

    def tile(r):
        return pl.ds(pl.multiple_of(r * s, s), s)

    @pl.when(pl.program_id(1) == 0)
    def _():
        o_ref[...] = jnp.zeros_like(o_ref)

    for g in range(y_ref.shape[1]):
        def body(i, carry, g=g):
            c0 = pl.multiple_of(i * ROW_GROUP, ROW_GROUP)
            rows = [idx_ref[0, g, 0, c0 + k] for k in range(ROW_GROUP)]
            acc = [o_ref[0, tile(r), :] for r in rows]
            for k in range(ROW_GROUP):
                o_ref[0, tile(rows[k]), :] = acc[k] + y_ref[0, g, tile(c0 + k), :]
            return carry

        lax.fori_loop(0, cap // ROW_GROUP, body, 0)


def _scatter(idx, ye, n):
    B, E, caps, W = ye.shape
    cap = idx.shape[3]
    s = caps // cap
    eg = _tile(E, EXPERT_GROUP)
    return pl.pallas_call(
        functools.partial(_scatter_kernel, s),
        out_shape=jax.ShapeDtypeStruct((B, n * s, W), F32),
        grid=(B, E // eg),
        in_specs=[pl.BlockSpec((1, eg, 1, cap), lambda b, e: (b, e, 0, 0), memory_space=pltpu.SMEM),
                  pl.BlockSpec((1, eg, caps, W), lambda b, e: (b, e, 0, 0))],
        out_specs=pl.BlockSpec((1, n * s, W), lambda b, e: (b, 0, 0)),
        compiler_params=_cparams(("parallel", "arbitrary")),
        name="expert_scatter",
    )(idx, ye)


def _final_kernel(x_ref, moe_ref, g2_ref, lg_ref, lb_ref, o_ref):
    moe = _load_tiles_as_rows(moe_ref.at[0], x_ref.shape[2])
    o_ref[0] = _ln(ALPHA * x_ref[0] + g2_ref[0] * moe) * lg_ref[...] + lb_ref[...]


def _final(xn, moe, g2, lg, lb, tm):
    B, n, D = xn.shape
    tok = pl.BlockSpec((1, tm, D), lambda b, i: (b, i, 0))
    vec = pl.BlockSpec((1, D), lambda b, i: (0, 0))
    return pl.pallas_call(
        _final_kernel,
        out_shape=jax.ShapeDtypeStruct((B, n, D), F32),
        grid=(B, n // tm),
        in_specs=[tok, pl.BlockSpec((1, tm * (D // LANES), LANES), lambda b, i: (b, i, 0)),
                  pl.BlockSpec((1, 1, D), lambda b, i: (b, 0, 0)), vec, vec],
        out_specs=tok,
        compiler_params=_cparams(("parallel", "parallel")),
        name="final_ln",
    )(xn, moe, g2, lg, lb)


def _rope_tables(n):
    t = np.arange(n)
    row = (t // GRID_W).astype(np.float32)
    col = (t % GRID_W).astype(np.float32)
    per_axis = MLA_ROPE // 2
    inv_freq = jnp.asarray(ROPE_THETA, F32) ** (-jnp.arange(0, per_axis, 2, dtype=F32) / per_axis)
    ang = jnp.concatenate([jnp.asarray(row)[:, None] * inv_freq, jnp.asarray(col)[:, None] * inv_freq], axis=-1)
    cos, sin = jnp.cos(ang), jnp.sin(ang)
    pad = HEAD_PAD - MLA_QK
    c_tab = jnp.concatenate([jnp.ones((n, MLA_NOPE), F32), cos, cos, jnp.zeros((n, pad), F32)], axis=-1)
    s_tab = jnp.concatenate([jnp.zeros((n, MLA_NOPE), F32), -sin, sin, jnp.zeros((n, pad), F32)], axis=-1)
    return c_tab, s_tab


def _prep_weights(w_in, w_uq, w_ukv):
    D = w_in.shape[0]
    o1 = Q_LORA
    o2 = o1 + KV_LORA
    o3 = o2 + MLA_ROPE
    o4 = o3 + 3 * NA_W
    w_kr = w_in[:, o2:o3]
    z = lambda k: jnp.zeros((D, k), w_in.dtype)
    pad = HEAD_PAD - MLA_QK
    w_small = jnp.concatenate([w_in[:, :o2], z(MLA_NOPE), w_kr, z(pad)], axis=-1).astype(BF16)
    w_na = w_in[:, o3:o3 + 2 * NA_W].astype(BF16)
    w_nvt = w_in[:, o3 + 2 * NA_W:o4].T.astype(BF16)
    w_g = w_in[:, o4:].astype(BF16)

    uq = w_uq.reshape(Q_LORA, MLA_HEADS, MLA_QK)
    zq = lambda k: jnp.zeros((Q_LORA, MLA_HEADS, k), w_uq.dtype)
    wqa = jnp.concatenate([uq, zq(pad)], axis=-1).reshape(Q_LORA, MLA_HEADS * HEAD_PAD).astype(BF16)

    ukv = w_ukv.reshape(KV_LORA, MLA_HEADS, MLA_NOPE + MLA_V)
    wk = jnp.concatenate([ukv[..., :MLA_NOPE], jnp.zeros((KV_LORA, MLA_HEADS, HEAD_PAD - MLA_NOPE), w_ukv.dtype)], axis=-1)
    wk = wk.reshape(KV_LORA, MLA_HEADS * HEAD_PAD).astype(BF16)
    wv = ukv[..., MLA_NOPE:].reshape(KV_LORA, MLA_HEADS * MLA_V).T.astype(BF16)
    return w_small, w_na, w_nvt, w_g, wqa, wk, wv


def _tile(n, pref):
    t = min(pref, n)
    while n % t:
        t //= 2
    return t


def kernel(x, c, ctx, c_ctx, w_mod, b_mod, w_in, q_norm_g, w_uq, kv_norm_g, w_ukv, na_rel_bias, w_proj_mla,
           w_proj_na, w_out, ln1_g, ln1_b, w_router, w_exp_gate, w_exp_up, w_exp_down, ln2_g, ln2_b):
    B, n, D = x.shape
    L = ctx.shape[1]
    rows = n // GRID_W
    assert n % (GRID_W * NA_QROWS) == 0 and rows >= NA_BAND
    assert w_mod.shape[0] == DEPTH
    cap = EC_CAPACITY * n // N_EXPERTS

    mod_rows = -(-(B + 1) // 8) * 8
    cc = jnp.concatenate([c, c_ctx[None], jnp.zeros((mod_rows - B - 1, D), F32)], axis=0)
    m = _modulation(cc, w_mod[0], b_mod[0])
    sh1, sc1, g1, sh2, sc2, g2 = [m[:B, k * D:(k + 1) * D].reshape(B, 1, D) for k in range(6)]
    csh1 = m[B:B + 1, :D].reshape(1, 1, D)
    csc1 = m[B:B + 1, D:2 * D].reshape(1, 1, D)

    w_small, w_na, w_nvt, w_g, wqa, wk, wv = _prep_weights(w_in[0], w_uq[0], w_ukv[0])
    qg = q_norm_g[0].reshape(1, Q_LORA)
    kvg = kv_norm_g[0].reshape(1, KV_LORA)
    c_tab, s_tab = _rope_tables(n)
    pad = HEAD_PAD - MLA_QK
    c_ctx_tab = jnp.concatenate([jnp.ones((L, MLA_QK), F32), jnp.zeros((L, pad), F32)], axis=-1)
    s_ctx_tab = jnp.zeros((L, HEAD_PAD), F32)

    q, k_lat, v_lat, nq, nk, nv, gates = _inproj(
        True, x, sc1, sh1, c_tab, s_tab, c_tab * (MLA_SCALE * LOG2E), s_tab * (MLA_SCALE * LOG2E),
        w_small, w_na, w_nvt, w_g, qg, kvg, wqa, wk, wv, _tile(n, 512))
    k_ctx, v_ctx, cnk, cnv = _inproj(
        False, ctx, csc1, csh1, c_ctx_tab, s_ctx_tab, c_ctx_tab, s_ctx_tab,
        w_small, w_na[:, NA_W:], w_nvt, w_g, qg, kvg, wqa, wk, wv, _tile(L, 256))

    y_mla = _mla_attention(q, k_lat, k_ctx, v_lat, v_ctx, _tile(n, 1024), _tile(n, 512))
    y_na = _na_attention(nq, nk, nv, cnk, cnv, _na_bias_tiles(na_rel_bias[0], rows))

    xn, u2, aff = _merge(
        y_mla, y_na, gates, x, g1, sc2, sh2,
        w_proj_mla[0].astype(BF16), w_proj_na[0].astype(BF16), w_out[0].astype(BF16),
        ln1_g[0].reshape(1, D), ln1_b[0].reshape(1, D), w_router[0], _tile(n, 2 * MERGE_SUB))

    idx4, g4 = _topk(jnp.swapaxes(aff, 1, 2), cap)
    idx = idx4.reshape(B, N_EXPERTS, 1, cap)
    xe = _gather(idx, u2, n)
    ye = _ffn(xe, g4, w_exp_gate[0], w_exp_up[0], w_exp_down[0])
    moe = _scatter(idx, ye, n)
    return _final(xn, moe, g2, ln2_g[0].reshape(1, D), ln2_b[0].reshape(1, D), _tile(n, 1024))
```

```python
import functools
import math

import numpy as np
import jax
import jax.numpy as jnp
from jax import lax
from jax.experimental import pallas as pl
from jax.experimental.pallas import tpu as pltpu

GRID_W = 64
MLA_HEADS = 8
MLA_NOPE = 64
MLA_ROPE = 32
MLA_QK = MLA_NOPE + MLA_ROPE
MLA_V = 64
Q_LORA = 256
KV_LORA = 128
MLA_SCALE = MLA_QK ** -0.5
ROPE_THETA = 10000.0
NA_HEADS = 8
NA_DIM = 64
NA_W = NA_HEADS * NA_DIM
NA_WIN_H = 8
NA_WIN_W = 16
NA_SCALE = NA_DIM ** -0.5
N_EXPERTS = 16
EC_CAPACITY = 2
LN_EPS = 1e-5
RMS_EPS = 1e-6
DEPTH = 1
ALPHA = (2.0 * DEPTH) ** 0.25
LOG2E = math.log2(math.e)

LANES = 128
HEAD_PAD = LANES
VMEM_LIMIT = 56 * 1024 * 1024
NEG_BIG = -1e30

NA_QROWS = 4
NA_BAND = 12
ROW_GROUP = 8
NA_LOOKAHEAD = 2
MERGE_SUB = 256
TOPK_RADIX = 64
TOPK_TABLE_ROWS = 16
TOPK_SLOTS = 128
TOPK_REFINE = 30

BF16 = jnp.bfloat16
F32 = jnp.float32


def _cparams(sem):
    return pltpu.CompilerParams(dimension_semantics=sem, vmem_limit_bytes=VMEM_LIMIT)


def _ln(x):
    mu = jnp.mean(x, axis=-1, keepdims=True)
    xc = x - mu
    var = jnp.mean(xc * xc, axis=-1, keepdims=True)
    return xc * lax.rsqrt(var + LN_EPS)


def _dot(a, b):
    return jnp.dot(a, b, preferred_element_type=F32)


def _dot_nt(a, b):
    return lax.dot_general(a, b, (((1,), (1,)), ((), ())), preferred_element_type=F32)


def _store_rows_as_tiles(ref, val):
    m, d = val.shape
    s = d // LANES
    for j in range(s):
        ref[pl.ds(j, m, stride=s), :] = val[:, j * LANES:(j + 1) * LANES]


def _load_tiles_as_rows(ref, d):
    s = d // LANES
    m = ref.shape[0] // s
    return jnp.concatenate([ref[pl.ds(j, m, stride=s), :] for j in range(s)], axis=-1)


def _mod_kernel(c_ref, w_ref, b_ref, o_ref):
    c = c_ref[...]
    s = c * jax.nn.sigmoid(c)
    o_ref[...] = jnp.dot(s, w_ref[...], preferred_element_type=F32,
                         precision=lax.Precision.HIGHEST) + b_ref[...]


def _modulation(cc, w_mod, b_mod):
    rows, d = cc.shape
    n_out = w_mod.shape[1]
    tn = 1024
    return pl.pallas_call(
        _mod_kernel,
        out_shape=jax.ShapeDtypeStruct((rows, n_out), F32),
        grid=(n_out // tn,),
        in_specs=[pl.BlockSpec((rows, d), lambda j: (0, 0)),
                  pl.BlockSpec((d, tn), lambda j: (0, j)),
                  pl.BlockSpec((1, tn), lambda j: (0, j))],
        out_specs=pl.BlockSpec((rows, tn), lambda j: (0, j)),
        compiler_params=_cparams(("arbitrary",)),
        name="modulation",
    )(cc, w_mod, b_mod.reshape(1, n_out))


def _swap_rope_halves(x, axis):
    half = MLA_ROPE // 2
    pos = lax.broadcasted_iota(jnp.int32, x.shape, axis)
    return jnp.where(pos < MLA_NOPE + half, pltpu.roll(x, HEAD_PAD - half, axis), pltpu.roll(x, half, axis))


def _inproj_kernel(latent, x_ref, sc_ref, sh_ref, ck_ref, sk_ref, cq_ref, sq_ref,
                   w_small_ref, w_na_ref, w_nqt_ref, w_nvt_ref, w_g_ref, qg_ref, kvg_ref, wqa_ref, wk_ref, wv_ref,
                   *out_refs):
    if latent:
        q_ref, k_ref, v_ref, nq_ref, nk_ref, nv_ref, g_ref = out_refs
    else:
        k_ref, v_ref, nk_ref, nv_ref = out_refs
    x = x_ref[0]
    u = (_ln(x) * (1.0 + sc_ref[0]) + sh_ref[0]).astype(BF16)

    small = _dot(u, w_small_ref[...])
    q_c = small[:, :Q_LORA]
    kv_c = small[:, Q_LORA:Q_LORA + KV_LORA]
    k_r = small[:, Q_LORA + KV_LORA:]

    kvn = (kv_c * lax.rsqrt(jnp.mean(kv_c * kv_c, axis=-1, keepdims=True) + RMS_EPS) * kvg_ref[...]).astype(BF16)
    kk = _dot(kvn, wk_ref[...])
    kr = k_r * ck_ref[...] + _swap_rope_halves(k_r, 1) * sk_ref[...]
    for h in range(MLA_HEADS):
        k_ref[0, h] = (kk[:, h * HEAD_PAD:(h + 1) * HEAD_PAD] + kr).astype(BF16)
    v_ref[0] = _dot_nt(wv_ref[...], kvn).astype(BF16)

    nk_ref[0] = _dot(u, w_na_ref[...]).astype(BF16)
    nv_ref[0] = _dot_nt(w_nvt_ref[...], u).astype(BF16)
    if latent:
        qn = (q_c * lax.rsqrt(jnp.mean(q_c * q_c, axis=-1, keepdims=True) + RMS_EPS) * qg_ref[...]).astype(BF16)
        qa = _dot_nt(wqa_ref[...], qn)
        cq = cq_ref[...]
        sq = sq_ref[...]
        for h in range(MLA_HEADS):
            qh = qa[h * HEAD_PAD:(h + 1) * HEAD_PAD, :]
            q_ref[0, h] = (qh * cq + _swap_rope_halves(qh, 0) * sq).astype(BF16)
        nq_ref[0] = (_dot_nt(w_nqt_ref[...], u) * (NA_SCALE * LOG2E)).astype(BF16)
        g_ref[0] = jax.nn.sigmoid(_dot(u, w_g_ref[...])).astype(BF16)


def _inproj(latent, x, sc, sh, ck, sk, cq, sq, w_small, w_na, w_nqt, w_nvt, w_g, qg, kvg, wqa, wk, wv, tm):
    B, n, D = x.shape
    per_batch = sc.shape[0] > 1
    mod_map = (lambda b, i: (b, 0, 0)) if per_batch else (lambda b, i: (0, 0, 0))
    full = lambda a: pl.BlockSpec(a.shape, lambda b, i: (0,) * a.ndim)
    tab = lambda a: pl.BlockSpec((tm, a.shape[1]), lambda b, i: (i, 0))
    tab_t = lambda a: pl.BlockSpec((a.shape[0], tm), lambda b, i: (0, i))
    in_specs = [pl.BlockSpec((1, tm, D), lambda b, i: (b, i, 0)),
                pl.BlockSpec((1, 1, D), mod_map), pl.BlockSpec((1, 1, D), mod_map),
                tab(ck), tab(sk), tab_t(cq), tab_t(sq),
                full(w_small), full(w_na), full(w_nqt), full(w_nvt), full(w_g), full(qg), full(kvg),
                full(wqa), full(wk), full(wv)]
    hk = jax.ShapeDtypeStruct((B, MLA_HEADS, n, HEAD_PAD), BF16)
    hk_spec = pl.BlockSpec((1, MLA_HEADS, tm, HEAD_PAD), lambda b, i: (b, 0, i, 0))
    tok = lambda w: jax.ShapeDtypeStruct((B, n, w), BF16)
    tok_spec = lambda w: pl.BlockSpec((1, tm, w), lambda b, i: (b, i, 0))
    vt = jax.ShapeDtypeStruct((B, MLA_HEADS * MLA_V, n), BF16)
    vt_spec = pl.BlockSpec((1, MLA_HEADS * MLA_V, tm), lambda b, i: (b, 0, i))
    if latent:
        qt = jax.ShapeDtypeStruct((B, MLA_HEADS, HEAD_PAD, n), BF16)
        qt_spec = pl.BlockSpec((1, MLA_HEADS, HEAD_PAD, tm), lambda b, i: (b, 0, 0, i))
        out_shape = [qt, hk, vt, vt, tok(NA_W), vt, tok(2 * D)]
        out_specs = [qt_spec, hk_spec, vt_spec, vt_spec, tok_spec(NA_W), vt_spec, tok_spec(2 * D)]
    else:
        out_shape = [hk, vt, tok(NA_W), vt]
        out_specs = [hk_spec, vt_spec, tok_spec(NA_W), vt_spec]
    return pl.pallas_call(
        functools.partial(_inproj_kernel, latent),
        out_shape=out_shape,
        grid=(B, n // tm),
        in_specs=in_specs,
        out_specs=out_specs,
        compiler_params=_cparams(("parallel", "parallel")),
        name="inproj_latent" if latent else "inproj_ctx",
    )(x, sc, sh, ck, sk, cq, sq, w_small, w_na, w_nqt, w_nvt, w_g, qg, kvg, wqa, wk, wv)


def _mla_kernel(tk, q_ref, kc_ref, kl_ref, vc_ref, vl_ref, o_ref, m_ref, l_ref, acc_ref, sa_ref, sb_ref, sc_ref):
    heads = q_ref.shape[1]
    n = kl_ref.shape[2]
    tq = q_ref.shape[3]
    nt = n // tk
    row = lax.broadcasted_iota(jnp.int32, (LANES, tq), 0)

    def scores(h, k):
        return _dot(k, q_ref[0, h])

    def absorb(hh, s, v_t):
        m_old = m_ref[hh]
        m_new = jnp.maximum(m_old, jnp.max(s, axis=0, keepdims=True))
        a = jnp.exp2(m_old - m_new)
        p = jnp.exp2(s - m_new)
        l_ref[hh] = a * l_ref[hh] + jnp.sum(p, axis=0, keepdims=True)
        acc_ref[hh] = a * acc_ref[hh] + _dot(v_t, p.astype(BF16))
        m_ref[hh] = m_new

    def ctx_scores(h0):
        for hh in range(2):
            sc_ref[hh] = scores(h0 + hh, kc_ref[0, h0 + hh])

    def pair_pass(pair, carry):
        h0 = 2 * pair
        rows = pl.ds(pl.multiple_of(pair * LANES, LANES), LANES)

        def k_tile(hh, j):
            return kl_ref[0, h0 + hh, pl.ds(pl.multiple_of(j * tk, tk), tk), :]

        def v_tile(j):
            return vl_ref[0, rows, pl.ds(pl.multiple_of(j * tk, tk), tk)]

        def advance(j, cur_ref, next_ref):
            v_t = v_tile(j)
            for hh in range(2):
                next_ref[hh] = scores(h0 + hh, k_tile(hh, j + 1))
                absorb(hh, cur_ref[hh], v_t)

        m_ref[...] = jnp.full(m_ref.shape, NEG_BIG, F32)
        l_ref[...] = jnp.zeros(l_ref.shape, F32)
        acc_ref[...] = jnp.zeros(acc_ref.shape, F32)
        v_ctx = vc_ref[0, rows, :]
        for hh in range(2):
            sa_ref[hh] = scores(h0 + hh, k_tile(hh, 0))
            absorb(hh, sc_ref[hh], v_ctx)

        def body(jj, c):
            advance(2 * jj, sa_ref, sb_ref)
            advance(2 * jj + 1, sb_ref, sa_ref)
            return c

        lax.fori_loop(0, (nt - 1) // 2, body, 0)
        last_ref = sa_ref
        if (nt - 1) % 2:
            advance(nt - 2, sa_ref, sb_ref)
            last_ref = sb_ref
        ctx_scores(jnp.minimum(h0 + 2, heads - 2))
        v_t = v_tile(nt - 1)
        for hh in range(2):
            absorb(hh, last_ref[hh], v_t)
        out_t = jnp.where(row < MLA_V, acc_ref[0] / l_ref[0], acc_ref[1] / l_ref[1])
        o_ref[0, :, rows] = out_t.T.astype(o_ref.dtype)
        return carry

    ctx_scores(0)
    lax.fori_loop(0, heads // 2, pair_pass, 0)


def _mla_attention(q, k_lat, k_ctx, v_lat, v_ctx, tq, tk):
    B, H, _, n = q.shape
    L = k_ctx.shape[2]
    W = H * MLA_V
    return pl.pallas_call(
        functools.partial(_mla_kernel, tk),
        out_shape=jax.ShapeDtypeStruct((B, n, W), BF16),
        grid=(B, n // tq),
        in_specs=[pl.BlockSpec((1, H, HEAD_PAD, tq), lambda b, i: (b, 0, 0, i)),
                  pl.BlockSpec((1, H, L, HEAD_PAD), lambda b, i: (b, 0, 0, 0)),
                  pl.BlockSpec((1, H, n, HEAD_PAD), lambda b, i: (b, 0, 0, 0)),
                  pl.BlockSpec((1, W, L), lambda b, i: (b, 0, 0)),
                  pl.BlockSpec((1, W, n), lambda b, i: (b, 0, 0))],
        out_specs=pl.BlockSpec((1, tq, W), lambda b, i: (b, i, 0)),
        scratch_shapes=[pltpu.VMEM((2, 1, tq), F32), pltpu.VMEM((2, 1, tq), F32),
                        pltpu.VMEM((2, LANES, tq), F32),
                        pltpu.VMEM((2, tk, tq), F32), pltpu.VMEM((2, tk, tq), F32),
                        pltpu.VMEM((2, L, tq), F32)],
        compiler_params=_cparams(("parallel", "parallel")),
        name="mla_attention",
    )(q, k_ctx, k_lat, v_ctx, v_lat)


def _na_block_tables(rows):
    nblk = rows // NA_QROWS
    wh = min(NA_WIN_H, rows)
    band0 = np.clip(np.arange(nblk) * NA_QROWS - wh // 2, 0, rows - NA_BAND)
    sigs, types = [], []
    for i in range(nblk):
        r = i * NA_QROWS + np.arange(NA_QROWS)
        r0 = np.clip(r - wh // 2, 0, rows - wh)
        sig = (tuple(r0 - band0[i]), tuple(r - band0[i]))
        if sig not in sigs:
            sigs.append(sig)
        types.append(sigs.index(sig))
    return band0.astype(np.int32), np.asarray(types, np.int32), sigs, wh


def _bias_tile_kernel(dr, ok, tbl_ref, o_ref):
    neg = jnp.full((GRID_W, GRID_W), NEG_BIG, F32)
    for t in range(o_ref.shape[0]):
        for k in range(NA_BAND):
            pieces = [tbl_ref[0, int(dr[t, q, k])] if ok[t, q, k] else neg for q in range(NA_QROWS)]
            o_ref[t, 0, k * GRID_W:(k + 1) * GRID_W, :] = jnp.concatenate(pieces, axis=1)


def _na_bias_tiles(rel_bias, rows):
    _, _, sigs, wh = _na_block_tables(rows)
    ww = NA_WIN_W
    n_dr, n_dc = 2 * NA_WIN_H - 1, 2 * NA_WIN_W - 1
    col = np.arange(GRID_W)
    c0 = np.clip(col - ww // 2, 0, GRID_W - ww)
    col_ok = (col[:, None] >= c0[None, :]) & (col[:, None] < c0[None, :] + ww)
    dc = col[:, None] - col[None, :] + (NA_WIN_W - 1)
    pick_c = ((dc[None] == np.arange(n_dc)[:, None, None]) & col_ok[None]).astype(np.float32)
    by_col = jnp.einsum('hdj,jyx->hdyx', rel_bias.astype(F32), jnp.asarray(pick_c), precision=lax.Precision.HIGHEST)
    by_col = jnp.where(jnp.asarray(col_ok), by_col * LOG2E, NEG_BIG)
    kr = np.arange(NA_BAND)
    ok = np.stack([(kr[None, :] >= np.asarray(r0)[:, None]) & (kr[None, :] < np.asarray(r0)[:, None] + wh)
                   for r0, _ in sigs])
    dr = np.stack([kr[None, :] - np.asarray(r)[:, None] + (NA_WIN_H - 1) for _, r in sigs])
    assert ((dr >= 0) & (dr < n_dr))[ok].all()
    T = len(sigs)
    kn, qn = NA_BAND * GRID_W, NA_QROWS * GRID_W
    return pl.pallas_call(
        functools.partial(_bias_tile_kernel, dr, ok),
        out_shape=jax.ShapeDtypeStruct((T, NA_HEADS, kn, qn), F32),
        grid=(NA_HEADS,),
        in_specs=[pl.BlockSpec((1, n_dr, GRID_W, GRID_W), lambda h: (h, 0, 0, 0))],
        out_specs=pl.BlockSpec((T, 1, kn, qn), lambda h: (0, h, 0, 0)),
        compiler_params=_cparams(("parallel",)),
        name="na_bias_tiles",
    )(by_col)


def _na_kernel(nblk, band_ref, type_ref, q_ref, k_ref, vt_ref, kc_ref, vct_ref, *rest):
    bias_refs, o_ref = rest[:nblk], rest[nblk]
    i = pl.program_id(1)
    nq = NA_QROWS * GRID_W
    nk = NA_BAND * GRID_W
    starts = [pl.multiple_of(band_ref[i * nblk + b] * GRID_W, NA_QROWS * GRID_W) for b in range(nblk)]
    row = lax.broadcasted_iota(jnp.int32, (LANES, nq), 0)

    def scores(c):
        blk, h = divmod(c, NA_HEADS)
        cs = slice((h // 2) * LANES, (h // 2 + 1) * LANES)
        qp = q_ref[0, cs, blk * nq:(blk + 1) * nq]
        own = (row < NA_DIM) if h % 2 == 0 else (row >= NA_DIM)
        qh = jnp.where(own, qp, jnp.zeros_like(qp))
        s_loc = _dot(k_ref[0, pl.ds(starts[blk], nk), cs], qh) + bias_refs[blk][0, h]
        s_ctx = _dot(kc_ref[0, :, cs], qh)
        return s_loc, s_ctx

    def attend(c, s):
        s_loc, s_ctx = s
        blk, h = divmod(c, NA_HEADS)
        start = starts[blk]
        cs = slice((h // 2) * LANES, (h // 2 + 1) * LANES)
        m = jnp.maximum(jnp.max(s_loc, axis=0, keepdims=True), jnp.max(s_ctx, axis=0, keepdims=True))
        p_loc = jnp.exp2(s_loc - m)
        p_ctx = jnp.exp2(s_ctx - m)
        l = jnp.sum(p_loc, axis=0, keepdims=True) + jnp.sum(p_ctx, axis=0, keepdims=True)
        o = _dot(vt_ref[0, cs, pl.ds(start, nk)], p_loc.astype(BF16)) + _dot(vct_ref[0, cs, :], p_ctx.astype(BF16))
        return o / l

    chain = nblk * NA_HEADS
    ahead = [scores(c) for c in range(min(NA_LOOKAHEAD, chain))]
    outs = []
    for c in range(chain):
        if c + NA_LOOKAHEAD < chain:
            ahead.append(scores(c + NA_LOOKAHEAD))
        outs.append(attend(c, ahead.pop(0)))
        if c % 2:
            blk, h = divmod(c, NA_HEADS)
            pair_t = jnp.where(row < NA_DIM, outs[c - 1], outs[c])
            o_ref[0, blk * nq:(blk + 1) * nq, (h // 2) * LANES:(h // 2 + 1) * LANES] = pair_t.T.astype(o_ref.dtype)


def _na_attention(nqt, nk, nvt, cnk, cnvt, bias_tiles):
    B, n, W = nk.shape
    L = cnk.shape[1]
    rows = n // GRID_W
    band0, types, _, _ = _na_block_tables(rows)
    qn = NA_QROWS * GRID_W
    kn = NA_BAND * GRID_W
    nblocks = rows // NA_QROWS
    nblk = 2 if nblocks % 2 == 0 else 1

    def bias_spec(k):
        return pl.BlockSpec((1, NA_HEADS, kn, qn), lambda b, i, bd, ty: (ty[i * nblk + k], 0, 0, 0))

    grid_spec = pltpu.PrefetchScalarGridSpec(
        num_scalar_prefetch=2,
        grid=(B, nblocks // nblk),
        in_specs=[pl.BlockSpec((1, W, nblk * qn), lambda b, i, bd, ty: (b, 0, i)),
                  pl.BlockSpec((1, n, W), lambda b, i, bd, ty: (b, 0, 0)),
                  pl.BlockSpec((1, W, n), lambda b, i, bd, ty: (b, 0, 0)),
                  pl.BlockSpec((1, L, W), lambda b, i, bd, ty: (b, 0, 0)),
                  pl.BlockSpec((1, W, L), lambda b, i, bd, ty: (b, 0, 0))] + [bias_spec(k) for k in range(nblk)],
        out_specs=pl.BlockSpec((1, nblk * qn, W), lambda b, i, bd, ty: (b, i, 0)),
    )
    return pl.pallas_call(
        functools.partial(_na_kernel, nblk),
        out_shape=jax.ShapeDtypeStruct((B, n, W), BF16),
        grid_spec=grid_spec,
        compiler_params=_cparams(("parallel", "arbitrary")),
        name="na_attention",
    )(jnp.asarray(band0), jnp.asarray(types), nqt, nk, nvt, cnk, cnvt, *([bias_tiles] * nblk))


def _split2(a):
    hi = a.astype(BF16)
    return hi, (a - hi.astype(F32)).astype(BF16)


def _merge_kernel(ym_ref, yn_ref, g_ref, x_ref, g1_ref, sc2_ref, sh2_ref, wpm_ref, wpn_ref, wo_ref,
                  l1g_ref, l1b_ref, wr_ref, xn_ref, up_ref, aff_ref):
    tm, D = x_ref.shape[1], x_ref.shape[2]
    s = D // LANES
    sub = min(MERGE_SUB, tm)
    wh, wl = _split2(wr_ref[...])

    def mix_of(r):
        g = g_ref[0, r, :]
        a = (g[:, :D].astype(F32) * _dot(ym_ref[0, r, :], wpm_ref[...])
             + g[:, D:].astype(F32) * _dot(yn_ref[0, r, :], wpn_ref[...]))
        return _dot(a.astype(BF16), wo_ref[...])

    def finish(r, mix):
        xn = _ln(ALPHA * x_ref[0, r, :] + g1_ref[0] * mix) * l1g_ref[...] + l1b_ref[...]
        xn_ref[0, r, :] = xn
        u2 = _ln(xn) * (1.0 + sc2_ref[0]) + sh2_ref[0]
        _store_rows_as_tiles(up_ref.at[0, r.start * s:r.stop * s], u2)
        uh, ul = _split2(u2)
        logits = _dot(uh, wh) + (_dot(uh, wl) + _dot(ul, wh))
        e = jnp.exp(logits - jnp.max(logits, axis=-1, keepdims=True))
        aff_ref[0, r, :] = e / jnp.sum(e, axis=-1, keepdims=True)

    rows = [slice(i, i + sub) for i in range(0, tm, sub)]
    mix = mix_of(rows[0])
    for i, r in enumerate(rows):
        mix_next = mix_of(rows[i + 1]) if i + 1 < len(rows) else None
        finish(r, mix)
        mix = mix_next


def _merge(y_mla, y_na, gates, x, g1, sc2, sh2, wpm, wpn, wo, l1g, l1b, wr, tm):
    B, n, D = x.shape
    E = wr.shape[1]
    full = lambda a: pl.BlockSpec(a.shape, lambda b, i: (0,) * a.ndim)
    tok = lambda w: pl.BlockSpec((1, tm, w), lambda b, i: (b, i, 0))
    mod = pl.BlockSpec((1, 1, D), lambda b, i: (b, 0, 0))
    return pl.pallas_call(
        _merge_kernel,
        out_shape=[jax.ShapeDtypeStruct((B, n, D), F32),
                   jax.ShapeDtypeStruct((B, n * (D // LANES), LANES), F32),
                   jax.ShapeDtypeStruct((B, n, E), F32)],
        grid=(B, n // tm),
        in_specs=[tok(y_mla.shape[2]), tok(y_na.shape[2]), tok(2 * D), tok(D), mod, mod, mod,
                  full(wpm), full(wpn), full(wo), full(l1g), full(l1b), full(wr)],
        out_specs=[tok(D), pl.BlockSpec((1, tm * (D // LANES), LANES), lambda b, i: (b, i, 0)), tok(E)],
        compiler_params=_cparams(("parallel", "parallel")),
        name="merge_ln_router",
    )(y_mla, y_na, gates, x, g1, sc2, sh2, wpm, wpn, wo, l1g, l1b, wr)


def _topk_kernel(cap, aff_ref, tri_ref, idx_ref, g_ref, pos_ref):
    aff = aff_ref[0]
    E, n = aff.shape

    def count(mask):
        return jnp.sum(mask.astype(F32), axis=-1, keepdims=True)

    def enough(v):
        return count(aff >= v) >= cap

    def search(t, thr):
        b1 = jnp.int32(1) << (30 - 2 * t)
        b0 = b1 >> 1
        c3, c2, c1 = thr | b1 | b0, thr | b1, thr | b0
        ok3, ok2, ok1 = [enough(pltpu.bitcast(c, F32)) for c in (c3, c2, c1)]
        return jnp.where(ok3, c3, jnp.where(ok2, c2, jnp.where(ok1, c1, thr)))

    thr = lax.fori_loop(0, 16, search, jnp.zeros((E, 1), jnp.int32))
    lo = pltpu.bitcast(thr, F32)
    hi = pltpu.bitcast(thr + 1, F32)

    def refine(t, lh):
        lo, hi = lh
        w = hi - lo
        q1, q2, q3 = lo + w * 0.25, lo + w * 0.5, lo + w * 0.75
        ok1, ok2, ok3 = enough(q1), enough(q2), enough(q3)
        new_lo = jnp.where(ok3, q3, jnp.where(ok2, q2, jnp.where(ok1, q1, lo)))
        new_hi = jnp.where(ok3, hi, jnp.where(ok2, q3, jnp.where(ok1, q2, q1)))
        return new_lo, new_hi

    lo, hi = lax.fori_loop(0, TOPK_REFINE // 2, refine, (lo, hi))
    gt = aff >= hi
    eq = (aff >= lo) & ~gt
    need = cap - count(gt)

    tri = tri_ref[...]

    def prefix(mask):
        mf = mask.astype(F32).astype(BF16)
        parts = []
        off = jnp.zeros((E, 1), F32)
        for c in range(n // LANES):
            blk = mf[:, c * LANES:(c + 1) * LANES]
            parts.append(_dot(blk, tri) + off)
            off = off + jnp.sum(blk.astype(F32), axis=-1, keepdims=True)
        return jnp.concatenate(parts, axis=-1)

    sel = gt | (eq & (prefix(eq) < need))
    pos_ref[...] = jnp.where(sel, prefix(sel), -1.0)

    tok = lax.broadcasted_iota(jnp.int32, (1, n), 1)
    digits = [(tok // TOPK_RADIX).astype(F32), (tok % TOPK_RADIX).astype(F32)]
    chunks = cap // TOPK_SLOTS

    def per_expert(e, carry):
        a = aff_ref[0, pl.ds(e, 1), :]
        a1 = a.astype(BF16).astype(F32)
        a2 = (a - a1).astype(BF16).astype(F32)
        a3 = (a - a1) - a2
        rows = digits + [a1, a2, a3]
        table = jnp.concatenate(rows + [jnp.zeros((TOPK_TABLE_ROWS - len(rows), n), F32)], axis=0).astype(BF16)
        pos_e = pos_ref[pl.ds(e, 1), :]

        hits = []
        for ch in range(chunks):
            slot = (lax.broadcasted_iota(jnp.int32, (TOPK_SLOTS, n), 0) + ch * TOPK_SLOTS).astype(F32)
            hits.append(jnp.where(slot == pos_e, 1.0, 0.0).astype(BF16))
        for ch in range(chunks):
            r = _dot_nt(hits[ch], table)
            idx = r[:, 0:1] * TOPK_RADIX + r[:, 1:2]
            idx_ref[0, e, ch * TOPK_SLOTS:(ch + 1) * TOPK_SLOTS, :] = idx.astype(jnp.int32)
            g_ref[0, e, ch * TOPK_SLOTS:(ch + 1) * TOPK_SLOTS, :] = (r[:, 2:3] + r[:, 3:4]) + r[:, 4:5]
        return carry

    lax.fori_loop(0, E, per_expert, 0)


def _topk(aff_t, cap):
    B, E, n = aff_t.shape
    tri = jnp.asarray(np.triu(np.ones((LANES, LANES), np.float32), k=1), BF16)
    return pl.pallas_call(
        functools.partial(_topk_kernel, cap),
        out_shape=[jax.ShapeDtypeStruct((B, E, cap, 1), jnp.int32),
                   jax.ShapeDtypeStruct((B, E, cap, 1), F32)],
        grid=(B,),
        in_specs=[pl.BlockSpec((1, E, n), lambda b: (b, 0, 0)),
                  pl.BlockSpec((LANES, LANES), lambda b: (0, 0))],
        out_specs=[pl.BlockSpec((1, E, cap, 1), lambda b: (b, 0, 0, 0)),
                   pl.BlockSpec((1, E, cap, 1), lambda b: (b, 0, 0, 0))],
        scratch_shapes=[pltpu.VMEM((E, n), F32)],
        compiler_params=_cparams(("parallel",)),
        name="expert_topk",
    )(aff_t, tri)


def _gather_kernel(s, idx_ref, u_ref, o_ref):
    cap = o_ref.shape[2] // s

    def body(i, carry):
        c0 = pl.multiple_of(i * ROW_GROUP, ROW_GROUP)
        for k in range(ROW_GROUP):
            src = pl.multiple_of(idx_ref[0, 0, 0, c0 + k] * s, s)
            o_ref[0, 0, pl.ds((c0 + k) * s, s), :] = u_ref[0, pl.ds(src, s), :]
        return carry

    lax.fori_loop(0, cap // ROW_GROUP, body, 0)


def _gather(idx, u2, n):
    B, E, _, cap = idx.shape
    _, ns, W = u2.shape
    s = ns // n
    return pl.pallas_call(
        functools.partial(_gather_kernel, s),
        out_shape=jax.ShapeDtypeStruct((B, E, cap * s, W), u2.dtype),
        grid=(B, E),
        in_specs=[pl.BlockSpec((1, 1, 1, cap), lambda b, e: (b, e, 0, 0), memory_space=pltpu.SMEM),
                  pl.BlockSpec((1, ns, W), lambda b, e: (b, 0, 0))],
        out_specs=pl.BlockSpec((1, 1, cap * s, W), lambda b, e: (b, e, 0, 0)),
        compiler_params=_cparams(("parallel", "arbitrary")),
        name="expert_gather",
    )(idx, u2)


def _ffn_kernel(xe_ref, g_ref, wg_ref, wu_ref, wd_ref, o_ref, wgb_ref, wub_ref, wdb_ref):
    @pl.when(pl.program_id(1) == 0)
    def _():
        wgb_ref[...] = wg_ref[0].astype(BF16)
        wub_ref[...] = wu_ref[0].astype(BF16)
        wdb_ref[...] = wd_ref[0].astype(BF16)

    xe = _load_tiles_as_rows(xe_ref.at[0, 0], wgb_ref.shape[0]).astype(BF16)
    gate = _dot(xe, wgb_ref[...])
    up = _dot(xe, wub_ref[...])
    h = (gate * jax.nn.sigmoid(gate) * up).astype(BF16)
    _store_rows_as_tiles(o_ref.at[0, 0], _dot(h, wdb_ref[...]) * g_ref[0, 0])


def _ffn(xe, g, w_gate, w_up, w_down):
    B, E, caps, W = xe.shape
    _, D, F = w_gate.shape
    cap = g.shape[2]
    return pl.pallas_call(
        _ffn_kernel,
        out_shape=jax.ShapeDtypeStruct((B, E, caps, W), F32),
        grid=(E, B),
        in_specs=[pl.BlockSpec((1, 1, caps, W), lambda e, b: (b, e, 0, 0)),
                  pl.BlockSpec((1, 1, cap, 1), lambda e, b: (b, e, 0, 0)),
                  pl.BlockSpec((1, D, F), lambda e, b: (e, 0, 0)),
                  pl.BlockSpec((1, D, F), lambda e, b: (e, 0, 0)),
                  pl.BlockSpec((1, F, D), lambda e, b: (e, 0, 0))],
        out_specs=pl.BlockSpec((1, 1, caps, W), lambda e, b: (b, e, 0, 0)),
        scratch_shapes=[pltpu.VMEM((D, F), BF16), pltpu.VMEM((D, F), BF16), pltpu.VMEM((F, D), BF16)],
        compiler_params=_cparams(("arbitrary", "arbitrary")),
        name="expert_ffn",
    )(xe, g, w_gate, w_up, w_down)


def _scatter_kernel(s, idx_ref, y_ref, o_ref):
    cap = y_ref.shape[2] // s

    def tile(r):
        return pl.ds(pl.multiple_of(r * s, s), s)

    @pl.when(pl.program_id(1) == 0)
    def _():
        o_ref[...] = jnp.zeros_like(o_ref)

    def body(i, carry):
        c0 = pl.multiple_of(i * ROW_GROUP, ROW_GROUP)
        rows = [idx_ref[0, 0, 0, c0 + k] for k in range(ROW_GROUP)]
        acc = [o_ref[0, tile(r), :] for r in rows]
        for k in range(ROW_GROUP):
            o_ref[0, tile(rows[k]), :] = acc[k] + y_ref[0, 0, tile(c0 + k), :]
        return carry

    lax.fori_loop(0, cap // ROW_GROUP, body, 0)


def _scatter(idx, ye, n):
    B, E, caps, W = ye.shape
    cap = idx.shape[3]
    s = caps // cap
    return pl.pallas_call(
        functools.partial(_scatter_kernel, s),
        out_shape=jax.ShapeDtypeStruct((B, n * s, W), F32),
        grid=(B, E),
        in_specs=[pl.BlockSpec((1, 1, 1, cap), lambda b, e: (b, e, 0, 0), memory_space=pltpu.SMEM),
                  pl.BlockSpec((1, 1, caps, W), lambda b, e: (b, e, 0, 0))],
        out_specs=pl.BlockSpec((1, n * s, W), lambda b, e: (b, 0, 0)),
        compiler_params=_cparams(("parallel", "arbitrary")),
        name="expert_scatter",
    )(idx, ye)


def _final_kernel(x_ref, moe_ref, g2_ref, lg_ref, lb_ref, o_ref):
    moe = _load_tiles_as_rows(moe_ref.at[0], x_ref.shape[2])
    o_ref[0] = _ln(ALPHA * x_ref[0] + g2_ref[0] * moe) * lg_ref[...] + lb_ref[...]


def _final(xn, moe, g2, lg, lb, tm):
    B, n, D = xn.shape
    tok = pl.BlockSpec((1, tm, D), lambda b, i: (b, i, 0))
    vec = pl.BlockSpec((1, D), lambda b, i: (0, 0))
    return pl.pallas_call(
        _final_kernel,
        out_shape=jax.ShapeDtypeStruct((B, n, D), F32),
        grid=(B, n // tm),
        in_specs=[tok, pl.BlockSpec((1, tm * (D // LANES), LANES), lambda b, i: (b, i, 0)),
                  pl.BlockSpec((1, 1, D), lambda b, i: (b, 0, 0)), vec, vec],
        out_specs=tok,
        compiler_params=_cparams(("parallel", "parallel")),
        name="final_ln",
    )(xn, moe, g2, lg, lb)


def _rope_tables(n):
    t = np.arange(n)
    row = (t // GRID_W).astype(np.float32)
    col = (t % GRID_W).astype(np.float32)
    per_axis = MLA_ROPE // 2
    inv_freq = jnp.asarray(ROPE_THETA, F32) ** (-jnp.arange(0, per_axis, 2, dtype=F32) / per_axis)
    ang = jnp.concatenate([jnp.asarray(row)[:, None] * inv_freq, jnp.asarray(col)[:, None] * inv_freq], axis=-1)
    cos, sin = jnp.cos(ang), jnp.sin(ang)
    pad = HEAD_PAD - MLA_QK
    c_tab = jnp.concatenate([jnp.ones((n, MLA_NOPE), F32), cos, cos, jnp.zeros((n, pad), F32)], axis=-1)
    s_tab = jnp.concatenate([jnp.zeros((n, MLA_NOPE), F32), -sin, sin, jnp.zeros((n, pad), F32)], axis=-1)
    return c_tab, s_tab


def _prep_weights(w_in, w_uq, w_ukv):
    D = w_in.shape[0]
    o1 = Q_LORA
    o2 = o1 + KV_LORA
    o3 = o2 + MLA_ROPE
    o4 = o3 + 3 * NA_W
    w_kr = w_in[:, o2:o3]
    z = lambda k: jnp.zeros((D, k), w_in.dtype)
    pad = HEAD_PAD - MLA_QK
    w_small = jnp.concatenate([w_in[:, :o2], z(MLA_NOPE), w_kr, z(pad)], axis=-1).astype(BF16)
    w_nqt = w_in[:, o3:o3 + NA_W].T.astype(BF16)
    w_na = w_in[:, o3 + NA_W:o3 + 2 * NA_W].astype(BF16)
    w_nvt = w_in[:, o3 + 2 * NA_W:o4].T.astype(BF16)
    w_g = w_in[:, o4:].astype(BF16)

    uq = w_uq.reshape(Q_LORA, MLA_HEADS, MLA_QK)
    zq = lambda k: jnp.zeros((Q_LORA, MLA_HEADS, k), w_uq.dtype)
    wqa = jnp.concatenate([uq, zq(pad)], axis=-1).reshape(Q_LORA, MLA_HEADS * HEAD_PAD).T.astype(BF16)

    ukv = w_ukv.reshape(KV_LORA, MLA_HEADS, MLA_NOPE + MLA_V)
    wk = jnp.concatenate([ukv[..., :MLA_NOPE], jnp.zeros((KV_LORA, MLA_HEADS, HEAD_PAD - MLA_NOPE), w_ukv.dtype)], axis=-1)
    wk = wk.reshape(KV_LORA, MLA_HEADS * HEAD_PAD).astype(BF16)
    wv = ukv[..., MLA_NOPE:].reshape(KV_LORA, MLA_HEADS * MLA_V).T.astype(BF16)
    return w_small, w_na, w_nqt, w_nvt, w_g, wqa, wk, wv


def _tile(n, pref):
    t = min(pref, n)
    while n % t:
        t //= 2
    return t


def kernel(x, c, ctx, c_ctx, w_mod, b_mod, w_in, q_norm_g, w_uq, kv_norm_g, w_ukv, na_rel_bias, w_proj_mla,
           w_proj_na, w_out, ln1_g, ln1_b, w_router, w_exp_gate, w_exp_up, w_exp_down, ln2_g, ln2_b):
    B, n, D = x.shape
    L = ctx.shape[1]
    rows = n // GRID_W
    assert n % (GRID_W * NA_QROWS) == 0 and rows >= NA_BAND
    assert w_mod.shape[0] == DEPTH
    cap = EC_CAPACITY * n // N_EXPERTS

    mod_rows = -(-(B + 1) // 8) * 8
    cc = jnp.concatenate([c, c_ctx[None], jnp.zeros((mod_rows - B - 1, D), F32)], axis=0)
    m = _modulation(cc, w_mod[0], b_mod[0])
    sh1, sc1, g1, sh2, sc2, g2 = [m[:B, k * D:(k + 1) * D].reshape(B, 1, D) for k in range(6)]
    csh1 = m[B:B + 1, :D].reshape(1, 1, D)
    csc1 = m[B:B + 1, D:2 * D].reshape(1, 1, D)

    w_small, w_na, w_nqt, w_nvt, w_g, wqa, wk, wv = _prep_weights(w_in[0], w_uq[0], w_ukv[0])
    qg = q_norm_g[0].reshape(1, Q_LORA)
    kvg = kv_norm_g[0].reshape(1, KV_LORA)
    c_tab, s_tab = _rope_tables(n)
    pad = HEAD_PAD - MLA_QK
    c_ctx_tab = jnp.concatenate([jnp.ones((L, MLA_QK), F32), jnp.zeros((L, pad), F32)], axis=-1)
    s_ctx_tab = jnp.zeros((L, HEAD_PAD), F32)

    q, k_lat, v_lat, nq, nk, nv, gates = _inproj(
        True, x, sc1, sh1, c_tab, s_tab, c_tab.T * (MLA_SCALE * LOG2E), s_tab.T * (MLA_SCALE * LOG2E),
        w_small, w_na, w_nqt, w_nvt, w_g, qg, kvg, wqa, wk, wv, _tile(n, 512))
    k_ctx, v_ctx, cnk, cnv = _inproj(
        False, ctx, csc1, csh1, c_ctx_tab, s_ctx_tab, c_ctx_tab.T, s_ctx_tab.T,
        w_small, w_na, w_nqt, w_nvt, w_g, qg, kvg, wqa, wk, wv, _tile(L, 256))

    y_mla = _mla_attention(q, k_lat, k_ctx, v_lat, v_ctx, _tile(n, 1024), _tile(n, 512))
    y_na = _na_attention(nq, nk, nv, cnk, cnv, _na_bias_tiles(na_rel_bias[0], rows))

    xn, u2, aff = _merge(
        y_mla, y_na, gates, x, g1, sc2, sh2,
        w_proj_mla[0].astype(BF16), w_proj_na[0].astype(BF16), w_out[0].astype(BF16),
        ln1_g[0].reshape(1, D), ln1_b[0].reshape(1, D), w_router[0], _tile(n, 2 * MERGE_SUB))

    idx4, g4 = _topk(jnp.swapaxes(aff, 1, 2), cap)
    idx = idx4.reshape(B, N_EXPERTS, 1, cap)
    xe = _gather(idx, u2, n)
    ye = _ffn(xe, g4, w_exp_gate[0], w_exp_up[0], w_exp_down[0])
    moe = _scatter(idx, ye, n)
    return _final(xn, moe, g2, ln2_g[0].reshape(1, D), ln2_b[0].reshape(1, D), _tile(n, 512))
```

```python
import functools
import math

import numpy as np
import jax
import jax.numpy as jnp
from jax import lax
from jax.experimental import pallas as pl
from jax.experimental.pallas import tpu as pltpu

GRID_W = 64
MLA_HEADS = 8
MLA_NOPE = 64
MLA_ROPE = 32
MLA_QK = MLA_NOPE + MLA_ROPE
MLA_V = 64
Q_LORA = 256
KV_LORA = 128
MLA_SCALE = MLA_QK ** -0.5
ROPE_THETA = 10000.0
NA_HEADS = 8
NA_DIM = 64
NA_W = NA_HEADS * NA_DIM
NA_WIN_H = 8
NA_WIN_W = 16
NA_SCALE = NA_DIM ** -0.5
N_EXPERTS = 16
EC_CAPACITY = 2
LN_EPS = 1e-5
RMS_EPS = 1e-6
DEPTH = 1
ALPHA = (2.0 * DEPTH) ** 0.25
LOG2E = math.log2(math.e)

LANES = 128
HEAD_PAD = LANES
VMEM_LIMIT = 56 * 1024 * 1024
NEG_BIG = -1e30

NA_QROWS = 4
NA_BAND = 12
ROW_GROUP = 8
NA_LOOKAHEAD = 2
MERGE_SUB = 256
TOPK_RADIX = 64
TOPK_TABLE_ROWS = 16
TOPK_SLOTS = 128
TOPK_REFINE = 30

BF16 = jnp.bfloat16
F32 = jnp.float32


def _cparams(sem):
    return pltpu.CompilerParams(dimension_semantics=sem, vmem_limit_bytes=VMEM_LIMIT)


def _ln(x):
    mu = jnp.mean(x, axis=-1, keepdims=True)
    xc = x - mu
    var = jnp.mean(xc * xc, axis=-1, keepdims=True)
    return xc * lax.rsqrt(var + LN_EPS)


def _dot(a, b):
    return jnp.dot(a, b, preferred_element_type=F32)


def _dot_nt(a, b):
    return lax.dot_general(a, b, (((1,), (1,)), ((), ())), preferred_element_type=F32)


def _store_rows_as_tiles(ref, val):
    m, d = val.shape
    s = d // LANES
    for j in range(s):
        ref[pl.ds(j, m, stride=s), :] = val[:, j * LANES:(j + 1) * LANES]


def _load_tiles_as_rows(ref, d):
    s = d // LANES
    m = ref.shape[0] // s
    return jnp.concatenate([ref[pl.ds(j, m, stride=s), :] for j in range(s)], axis=-1)


def _mod_kernel(c_ref, w_ref, b_ref, o_ref):
    c = c_ref[...]
    s = c * jax.nn.sigmoid(c)
    o_ref[...] = jnp.dot(s, w_ref[...], preferred_element_type=F32,
                         precision=lax.Precision.HIGHEST) + b_ref[...]


def _modulation(cc, w_mod, b_mod):
    rows, d = cc.shape
    n_out = w_mod.shape[1]
    tn = 1024
    return pl.pallas_call(
        _mod_kernel,
        out_shape=jax.ShapeDtypeStruct((rows, n_out), F32),
        grid=(n_out // tn,),
        in_specs=[pl.BlockSpec((rows, d), lambda j: (0, 0)),
                  pl.BlockSpec((d, tn), lambda j: (0, j)),
                  pl.BlockSpec((1, tn), lambda j: (0, j))],
        out_specs=pl.BlockSpec((rows, tn), lambda j: (0, j)),
        compiler_params=_cparams(("arbitrary",)),
        name="modulation",
    )(cc, w_mod, b_mod.reshape(1, n_out))


def _swap_rope_halves(x, axis):
    half = MLA_ROPE // 2
    pos = lax.broadcasted_iota(jnp.int32, x.shape, axis)
    return jnp.where(pos < MLA_NOPE + half, pltpu.roll(x, HEAD_PAD - half, axis), pltpu.roll(x, half, axis))


def _store_value_tiles(ref, vt, dim):
    tm = vt.shape[1]
    tail = jnp.where(lax.broadcasted_iota(jnp.int32, (HEAD_PAD - dim, tm), 0) == 0, 1.0, 0.0).astype(ref.dtype)
    for h in range(vt.shape[0] // dim):
        ref[h * HEAD_PAD:h * HEAD_PAD + dim, :] = vt[h * dim:(h + 1) * dim, :].astype(ref.dtype)
        ref[h * HEAD_PAD + dim:(h + 1) * HEAD_PAD, :] = tail


def _inproj_kernel(latent, x_ref, sc_ref, sh_ref, ck_ref, sk_ref, cq_ref, sq_ref,
                   w_small_ref, w_na_ref, w_nqt_ref, w_nvt_ref, w_g_ref, qg_ref, kvg_ref, wqa_ref, wk_ref, wv_ref,
                   *out_refs):
    if latent:
        q_ref, k_ref, v_ref, nq_ref, nk_ref, nv_ref, g_ref = out_refs
    else:
        k_ref, v_ref, nk_ref, nv_ref = out_refs
    x = x_ref[0]
    u = (_ln(x) * (1.0 + sc_ref[0]) + sh_ref[0]).astype(BF16)

    small = _dot(u, w_small_ref[...])
    q_c = small[:, :Q_LORA]
    kv_c = small[:, Q_LORA:Q_LORA + KV_LORA]
    k_r = small[:, Q_LORA + KV_LORA:]

    kvn = (kv_c * lax.rsqrt(jnp.mean(kv_c * kv_c, axis=-1, keepdims=True) + RMS_EPS) * kvg_ref[...]).astype(BF16)
    kk = _dot(kvn, wk_ref[...])
    kr = k_r * ck_ref[...] + _swap_rope_halves(k_r, 1) * sk_ref[...]
    for h in range(MLA_HEADS):
        k_ref[0, h] = (kk[:, h * HEAD_PAD:(h + 1) * HEAD_PAD] + kr).astype(BF16)
    _store_value_tiles(v_ref.at[0], _dot_nt(wv_ref[...], kvn), MLA_V)

    nk_ref[0] = _dot(u, w_na_ref[...]).astype(BF16)
    _store_value_tiles(nv_ref.at[0], _dot_nt(w_nvt_ref[...], u), NA_DIM)
    if latent:
        qn = (q_c * lax.rsqrt(jnp.mean(q_c * q_c, axis=-1, keepdims=True) + RMS_EPS) * qg_ref[...]).astype(BF16)
        qa = _dot_nt(wqa_ref[...], qn)
        cq = cq_ref[...]
        sq = sq_ref[...]
        for h in range(MLA_HEADS):
            qh = qa[h * HEAD_PAD:(h + 1) * HEAD_PAD, :]
            q_ref[0, h] = (qh * cq + _swap_rope_halves(qh, 0) * sq).astype(BF16)
        nq_ref[0] = (_dot_nt(w_nqt_ref[...], u) * (NA_SCALE * LOG2E)).astype(BF16)
        g_ref[0] = jax.nn.sigmoid(_dot(u, w_g_ref[...])).astype(BF16)


def _inproj(latent, x, sc, sh, ck, sk, cq, sq, w_small, w_na, w_nqt, w_nvt, w_g, qg, kvg, wqa, wk, wv, tm):
    B, n, D = x.shape
    per_batch = sc.shape[0] > 1
    mod_map = (lambda b, i: (b, 0, 0)) if per_batch else (lambda b, i: (0, 0, 0))
    full = lambda a: pl.BlockSpec(a.shape, lambda b, i: (0,) * a.ndim)
    tab = lambda a: pl.BlockSpec((tm, a.shape[1]), lambda b, i: (i, 0))
    tab_t = lambda a: pl.BlockSpec((a.shape[0], tm), lambda b, i: (0, i))
    in_specs = [pl.BlockSpec((1, tm, D), lambda b, i: (b, i, 0)),
                pl.BlockSpec((1, 1, D), mod_map), pl.BlockSpec((1, 1, D), mod_map),
                tab(ck), tab(sk), tab_t(cq), tab_t(sq),
                full(w_small), full(w_na), full(w_nqt), full(w_nvt), full(w_g), full(qg), full(kvg),
                full(wqa), full(wk), full(wv)]
    hk = jax.ShapeDtypeStruct((B, MLA_HEADS, n, HEAD_PAD), BF16)
    hk_spec = pl.BlockSpec((1, MLA_HEADS, tm, HEAD_PAD), lambda b, i: (b, 0, i, 0))
    tok = lambda w: jax.ShapeDtypeStruct((B, n, w), BF16)
    tok_spec = lambda w: pl.BlockSpec((1, tm, w), lambda b, i: (b, i, 0))
    vt = jax.ShapeDtypeStruct((B, MLA_HEADS * HEAD_PAD, n), BF16)
    vt_spec = pl.BlockSpec((1, MLA_HEADS * HEAD_PAD, tm), lambda b, i: (b, 0, i))
    nqt = jax.ShapeDtypeStruct((B, NA_W, n), BF16)
    nqt_spec = pl.BlockSpec((1, NA_W, tm), lambda b, i: (b, 0, i))
    if latent:
        qt = jax.ShapeDtypeStruct((B, MLA_HEADS, HEAD_PAD, n), BF16)
        qt_spec = pl.BlockSpec((1, MLA_HEADS, HEAD_PAD, tm), lambda b, i: (b, 0, 0, i))
        out_shape = [qt, hk, vt, nqt, tok(NA_W), vt, tok(2 * D)]
        out_specs = [qt_spec, hk_spec, vt_spec, nqt_spec, tok_spec(NA_W), vt_spec, tok_spec(2 * D)]
    else:
        out_shape = [hk, vt, tok(NA_W), vt]
        out_specs = [hk_spec, vt_spec, tok_spec(NA_W), vt_spec]
    return pl.pallas_call(
        functools.partial(_inproj_kernel, latent),
        out_shape=out_shape,
        grid=(B, n // tm),
        in_specs=in_specs,
        out_specs=out_specs,
        compiler_params=_cparams(("parallel", "parallel")),
        name="inproj_latent" if latent else "inproj_ctx",
    )(x, sc, sh, ck, sk, cq, sq, w_small, w_na, w_nqt, w_nvt, w_g, qg, kvg, wqa, wk, wv)


def _mla_kernel(tk, q_ref, kc_ref, kl_ref, vc_ref, vl_ref, o_ref, m_ref, acc_ref, sa_ref, sb_ref, sc_ref):
    heads = q_ref.shape[1]
    n = kl_ref.shape[2]
    tq = q_ref.shape[3]
    nt = n // tk

    def scores(h, k):
        return _dot(k, q_ref[0, h])

    def absorb(hh, s, v_t):
        m_old = m_ref[hh]
        m_new = jnp.maximum(m_old, jnp.max(s, axis=0, keepdims=True))
        a = jnp.exp2(m_old - m_new)
        p = jnp.exp2(s - m_new).astype(BF16)
        acc_ref[hh] = a * acc_ref[hh] + _dot(v_t, p)
        m_ref[hh] = m_new

    def ctx_scores(h0):
        for hh in range(2):
            sc_ref[hh] = scores(h0 + hh, kc_ref[0, h0 + hh])

    def pair_pass(pair, carry):
        h0 = 2 * pair

        def head_rows(hh):
            return pl.ds(pl.multiple_of((h0 + hh) * HEAD_PAD, HEAD_PAD), HEAD_PAD)

        def k_tile(hh, j):
            return kl_ref[0, h0 + hh, pl.ds(pl.multiple_of(j * tk, tk), tk), :]

        def v_tile(hh, j):
            return vl_ref[0, head_rows(hh), pl.ds(pl.multiple_of(j * tk, tk), tk)]

        def advance(j, cur_ref, next_ref):
            for hh in range(2):
                next_ref[hh] = scores(h0 + hh, k_tile(hh, j + 1))
                absorb(hh, cur_ref[hh], v_tile(hh, j))

        m_ref[...] = jnp.full(m_ref.shape, NEG_BIG, F32)
        acc_ref[...] = jnp.zeros(acc_ref.shape, F32)
        for hh in range(2):
            sa_ref[hh] = scores(h0 + hh, k_tile(hh, 0))
            absorb(hh, sc_ref[hh], vc_ref[0, head_rows(hh), :])

        def body(jj, c):
            advance(2 * jj, sa_ref, sb_ref)
            advance(2 * jj + 1, sb_ref, sa_ref)
            return c

        lax.fori_loop(0, (nt - 1) // 2, body, 0)
        last_ref = sa_ref
        if (nt - 1) % 2:
            advance(nt - 2, sa_ref, sb_ref)
            last_ref = sb_ref
        ctx_scores(jnp.minimum(h0 + 2, heads - 2))
        outs = []
        for hh in range(2):
            absorb(hh, last_ref[hh], v_tile(hh, nt - 1))
            acc = acc_ref[hh]
            outs.append(acc[:MLA_V] / acc[MLA_V:MLA_V + 1])
        lanes = pl.ds(pl.multiple_of(pair * LANES, LANES), LANES)
        o_ref[0, :, lanes] = jnp.concatenate(outs, axis=0).T.astype(o_ref.dtype)
        return carry

    ctx_scores(0)
    lax.fori_loop(0, heads // 2, pair_pass, 0)


def _mla_attention(q, k_lat, k_ctx, v_lat, v_ctx, tq, tk):
    B, H, _, n = q.shape
    L = k_ctx.shape[2]
    W = H * MLA_V
    return pl.pallas_call(
        functools.partial(_mla_kernel, tk),
        out_shape=jax.ShapeDtypeStruct((B, n, W), BF16),
        grid=(B, n // tq),
        in_specs=[pl.BlockSpec((1, H, HEAD_PAD, tq), lambda b, i: (b, 0, 0, i)),
                  pl.BlockSpec((1, H, L, HEAD_PAD), lambda b, i: (b, 0, 0, 0)),
                  pl.BlockSpec((1, H, n, HEAD_PAD), lambda b, i: (b, 0, 0, 0)),
                  pl.BlockSpec((1, H * HEAD_PAD, L), lambda b, i: (b, 0, 0)),
                  pl.BlockSpec((1, H * HEAD_PAD, n), lambda b, i: (b, 0, 0))],
        out_specs=pl.BlockSpec((1, tq, W), lambda b, i: (b, i, 0)),
        scratch_shapes=[pltpu.VMEM((2, 1, tq), F32),
                        pltpu.VMEM((2, HEAD_PAD, tq), F32),
                        pltpu.VMEM((2, tk, tq), F32), pltpu.VMEM((2, tk, tq), F32),
                        pltpu.VMEM((2, L, tq), F32)],
        compiler_params=_cparams(("parallel", "parallel")),
        name="mla_attention",
    )(q, k_ctx, k_lat, v_ctx, v_lat)


def _na_block_tables(rows):
    nblk = rows // NA_QROWS
    wh = min(NA_WIN_H, rows)
    band0 = np.clip(np.arange(nblk) * NA_QROWS - wh // 2, 0, rows - NA_BAND)
    sigs, types = [], []
    for i in range(nblk):
        r = i * NA_QROWS + np.arange(NA_QROWS)
        r0 = np.clip(r - wh // 2, 0, rows - wh)
        sig = (tuple(r0 - band0[i]), tuple(r - band0[i]))
        if sig not in sigs:
            sigs.append(sig)
        types.append(sigs.index(sig))
    return band0.astype(np.int32), np.asarray(types, np.int32), sigs, wh


def _bias_tile_kernel(dr, ok, tbl_ref, o_ref):
    neg = jnp.full((GRID_W, GRID_W), NEG_BIG, F32)
    for t in range(o_ref.shape[0]):
        for k in range(NA_BAND):
            pieces = [tbl_ref[0, int(dr[t, q, k])] if ok[t, q, k] else neg for q in range(NA_QROWS)]
            o_ref[t, 0, k * GRID_W:(k + 1) * GRID_W, :] = jnp.concatenate(pieces, axis=1)


def _na_bias_tiles(rel_bias, rows):
    _, _, sigs, wh = _na_block_tables(rows)
    ww = NA_WIN_W
    n_dr, n_dc = 2 * NA_WIN_H - 1, 2 * NA_WIN_W - 1
    col = np.arange(GRID_W)
    c0 = np.clip(col - ww // 2, 0, GRID_W - ww)
    col_ok = (col[:, None] >= c0[None, :]) & (col[:, None] < c0[None, :] + ww)
    dc = col[:, None] - col[None, :] + (NA_WIN_W - 1)
    pick_c = ((dc[None] == np.arange(n_dc)[:, None, None]) & col_ok[None]).astype(np.float32)
    by_col = jnp.einsum('hdj,jyx->hdyx', rel_bias.astype(F32), jnp.asarray(pick_c), precision=lax.Precision.HIGHEST)
    by_col = jnp.where(jnp.asarray(col_ok), by_col * LOG2E, NEG_BIG)
    kr = np.arange(NA_BAND)
    ok = np.stack([(kr[None, :] >= np.asarray(r0)[:, None]) & (kr[None, :] < np.asarray(r0)[:, None] + wh)
                   for r0, _ in sigs])
    dr = np.stack([kr[None, :] - np.asarray(r)[:, None] + (NA_WIN_H - 1) for _, r in sigs])
    assert ((dr >= 0) & (dr < n_dr))[ok].all()
    T = len(sigs)
    kn, qn = NA_BAND * GRID_W, NA_QROWS * GRID_W
    return pl.pallas_call(
        functools.partial(_bias_tile_kernel, dr, ok),
        out_shape=jax.ShapeDtypeStruct((T, NA_HEADS, kn, qn), F32),
        grid=(NA_HEADS,),
        in_specs=[pl.BlockSpec((1, n_dr, GRID_W, GRID_W), lambda h: (h, 0, 0, 0))],
        out_specs=pl.BlockSpec((T, 1, kn, qn), lambda h: (0, h, 0, 0)),
        compiler_params=_cparams(("parallel",)),
        name="na_bias_tiles",
    )(by_col)


def _na_kernel(nblk, band_ref, type_ref, q_ref, k_ref, vt_ref, kc_ref, vct_ref, *rest):
    bias_refs, o_ref = rest[:nblk], rest[nblk]
    i = pl.program_id(1)
    nq = NA_QROWS * GRID_W
    nk = NA_BAND * GRID_W
    starts = [pl.multiple_of(band_ref[i * nblk + b] * GRID_W, NA_QROWS * GRID_W) for b in range(nblk)]
    row = lax.broadcasted_iota(jnp.int32, (LANES, nq), 0)

    def scores(c):
        blk, h = divmod(c, NA_HEADS)
        cs = slice((h // 2) * LANES, (h // 2 + 1) * LANES)
        qp = q_ref[0, cs, blk * nq:(blk + 1) * nq]
        own = (row < NA_DIM) if h % 2 == 0 else (row >= NA_DIM)
        qh = jnp.where(own, qp, jnp.zeros_like(qp))
        s_loc = _dot(k_ref[0, pl.ds(starts[blk], nk), cs], qh) + bias_refs[blk][0, h]
        s_ctx = _dot(kc_ref[0, :, cs], qh)
        return s_loc, s_ctx

    def attend(c, s):
        s_loc, s_ctx = s
        blk, h = divmod(c, NA_HEADS)
        start = starts[blk]
        vs = slice(h * HEAD_PAD, (h + 1) * HEAD_PAD)
        m = jnp.maximum(jnp.max(s_loc, axis=0, keepdims=True), jnp.max(s_ctx, axis=0, keepdims=True))
        p_loc = jnp.exp2((s_loc - m).astype(BF16))
        p_ctx = jnp.exp2((s_ctx - m).astype(BF16))
        o = _dot(vt_ref[0, vs, pl.ds(start, nk)], p_loc) + _dot(vct_ref[0, vs, :], p_ctx)
        return o[:NA_DIM] / o[NA_DIM:NA_DIM + 1]

    chain = nblk * NA_HEADS
    ahead = [scores(c) for c in range(min(NA_LOOKAHEAD, chain))]
    outs = []
    for c in range(chain):
        if c + NA_LOOKAHEAD < chain:
            ahead.append(scores(c + NA_LOOKAHEAD))
        outs.append(attend(c, ahead.pop(0)))
        if c % 2:
            blk, h = divmod(c, NA_HEADS)
            pair_t = jnp.concatenate([outs[c - 1], outs[c]], axis=0)
            o_ref[0, blk * nq:(blk + 1) * nq, (h // 2) * LANES:(h // 2 + 1) * LANES] = pair_t.T.astype(o_ref.dtype)


def _na_attention(nqt, nk, nvt, cnk, cnvt, bias_tiles):
    B, n, W = nk.shape
    L = cnk.shape[1]
    rows = n // GRID_W
    band0, types, _, _ = _na_block_tables(rows)
    qn = NA_QROWS * GRID_W
    kn = NA_BAND * GRID_W
    nblocks = rows // NA_QROWS
    nblk = 2 if nblocks % 2 == 0 else 1

    def bias_spec(k):
        return pl.BlockSpec((1, NA_HEADS, kn, qn), lambda b, i, bd, ty: (ty[i * nblk + k], 0, 0, 0))

    grid_spec = pltpu.PrefetchScalarGridSpec(
        num_scalar_prefetch=2,
        grid=(B, nblocks // nblk),
        in_specs=[pl.BlockSpec((1, W, nblk * qn), lambda b, i, bd, ty: (b, 0, i)),
                  pl.BlockSpec((1, n, W), lambda b, i, bd, ty: (b, 0, 0)),
                  pl.BlockSpec((1, NA_HEADS * HEAD_PAD, n), lambda b, i, bd, ty: (b, 0, 0)),
                  pl.BlockSpec((1, L, W), lambda b, i, bd, ty: (b, 0, 0)),
                  pl.BlockSpec((1, NA_HEADS * HEAD_PAD, L), lambda b, i, bd, ty: (b, 0, 0))]
                 + [bias_spec(k) for k in range(nblk)],
        out_specs=pl.BlockSpec((1, nblk * qn, W), lambda b, i, bd, ty: (b, i, 0)),
    )
    return pl.pallas_call(
        functools.partial(_na_kernel, nblk),
        out_shape=jax.ShapeDtypeStruct((B, n, W), BF16),
        grid_spec=grid_spec,
        compiler_params=_cparams(("parallel", "arbitrary")),
        name="na_attention",
    )(jnp.asarray(band0), jnp.asarray(types), nqt, nk, nvt, cnk, cnvt, *([bias_tiles] * nblk))


def _split2(a):
    hi = a.astype(BF16)
    return hi, (a - hi.astype(F32)).astype(BF16)


def _merge_kernel(ym_ref, yn_ref, g_ref, x_ref, g1_ref, sc2_ref, sh2_ref, wpm_ref, wpn_ref, wo_ref,
                  l1g_ref, l1b_ref, wr_ref, xn_ref, up_ref, aff_ref):
    tm, D = x_ref.shape[1], x_ref.shape[2]
    s = D // LANES
    sub = min(MERGE_SUB, tm)
    wh, wl = _split2(wr_ref[...])

    def mix_of(r):
        g = g_ref[0, r, :]
        a = (g[:, :D].astype(F32) * _dot(ym_ref[0, r, :], wpm_ref[...])
             + g[:, D:].astype(F32) * _dot(yn_ref[0, r, :], wpn_ref[...]))
        return _dot(a.astype(BF16), wo_ref[...])

    def finish(r, mix):
        xn = _ln(ALPHA * x_ref[0, r, :] + g1_ref[0] * mix) * l1g_ref[...] + l1b_ref[...]
        xn_ref[0, r, :] = xn
        u2 = _ln(xn) * (1.0 + sc2_ref[0]) + sh2_ref[0]
        _store_rows_as_tiles(up_ref.at[0, r.start * s:r.stop * s], u2)
        uh, ul = _split2(u2)
        logits = _dot(uh, wh) + (_dot(uh, wl) + _dot(ul, wh))
        e = jnp.exp(logits - jnp.max(logits, axis=-1, keepdims=True))
        aff_ref[0, r, :] = e / jnp.sum(e, axis=-1, keepdims=True)

    rows = [slice(i, i + sub) for i in range(0, tm, sub)]
    mix = mix_of(rows[0])
    for i, r in enumerate(rows):
        mix_next = mix_of(rows[i + 1]) if i + 1 < len(rows) else None
        finish(r, mix)
        mix = mix_next


def _merge(y_mla, y_na, gates, x, g1, sc2, sh2, wpm, wpn, wo, l1g, l1b, wr, tm):
    B, n, D = x.shape
    E = wr.shape[1]
    full = lambda a: pl.BlockSpec(a.shape, lambda b, i: (0,) * a.ndim)
    tok = lambda w: pl.BlockSpec((1, tm, w), lambda b, i: (b, i, 0))
    mod = pl.BlockSpec((1, 1, D), lambda b, i: (b, 0, 0))
    return pl.pallas_call(
        _merge_kernel,
        out_shape=[jax.ShapeDtypeStruct((B, n, D), F32),
                   jax.ShapeDtypeStruct((B, n * (D // LANES), LANES), F32),
                   jax.ShapeDtypeStruct((B, n, E), F32)],
        grid=(B, n // tm),
        in_specs=[tok(y_mla.shape[2]), tok(y_na.shape[2]), tok(2 * D), tok(D), mod, mod, mod,
                  full(wpm), full(wpn), full(wo), full(l1g), full(l1b), full(wr)],
        out_specs=[tok(D), pl.BlockSpec((1, tm * (D // LANES), LANES), lambda b, i: (b, i, 0)), tok(E)],
        compiler_params=_cparams(("parallel", "parallel")),
        name="merge_ln_router",
    )(y_mla, y_na, gates, x, g1, sc2, sh2, wpm, wpn, wo, l1g, l1b, wr)


def _topk_kernel(cap, aff_ref, tri_ref, idx_ref, g_ref, pos_ref):
    aff = aff_ref[0]
    E, n = aff.shape

    def count(mask):
        return jnp.sum(mask.astype(F32), axis=-1, keepdims=True)

    def enough(v):
        return count(aff >= v) >= cap

    def search(t, thr):
        b1 = jnp.int32(1) << (30 - 2 * t)
        b0 = b1 >> 1
        c3, c2, c1 = thr | b1 | b0, thr | b1, thr | b0
        ok3, ok2, ok1 = [enough(pltpu.bitcast(c, F32)) for c in (c3, c2, c1)]
        return jnp.where(ok3, c3, jnp.where(ok2, c2, jnp.where(ok1, c1, thr)))

    thr = lax.fori_loop(0, 16, search, jnp.zeros((E, 1), jnp.int32))
    lo = pltpu.bitcast(thr, F32)
    hi = pltpu.bitcast(thr + 1, F32)

    def refine(t, lh):
        lo, hi = lh
        w = hi - lo
        q1, q2, q3 = lo + w * 0.25, lo + w * 0.5, lo + w * 0.75
        ok1, ok2, ok3 = enough(q1), enough(q2), enough(q3)
        new_lo = jnp.where(ok3, q3, jnp.where(ok2, q2, jnp.where(ok1, q1, lo)))
        new_hi = jnp.where(ok3, hi, jnp.where(ok2, q3, jnp.where(ok1, q2, q1)))
        return new_lo, new_hi

    lo, hi = lax.fori_loop(0, TOPK_REFINE // 2, refine, (lo, hi))
    gt = aff >= hi
    eq = (aff >= lo) & ~gt
    need = cap - count(gt)

    tri = tri_ref[...]

    def prefix(mask):
        mf = mask.astype(F32).astype(BF16)
        parts = []
        off = jnp.zeros((E, 1), F32)
        for c in range(n // LANES):
            blk = mf[:, c * LANES:(c + 1) * LANES]
            parts.append(_dot(blk, tri) + off)
            off = off + jnp.sum(blk.astype(F32), axis=-1, keepdims=True)
        return jnp.concatenate(parts, axis=-1)

    sel = gt | (eq & (prefix(eq) < need))
    pos_ref[...] = jnp.where(sel, prefix(sel), -1.0)

    tok = lax.broadcasted_iota(jnp.int32, (1, n), 1)
    digits = [(tok // TOPK_RADIX).astype(F32), (tok % TOPK_RADIX).astype(F32)]
    chunks = cap // TOPK_SLOTS

    def per_expert(e, carry):
        a = aff_ref[0, pl.ds(e, 1), :]
        a1 = a.astype(BF16).astype(F32)
        a2 = (a - a1).astype(BF16).astype(F32)
        a3 = (a - a1) - a2
        rows = digits + [a1, a2, a3]
        table = jnp.concatenate(rows + [jnp.zeros((TOPK_TABLE_ROWS - len(rows), n), F32)], axis=0).astype(BF16)
        pos_e = pos_ref[pl.ds(e, 1), :]

        hits = []
        for ch in range(chunks):
            slot = (lax.broadcasted_iota(jnp.int32, (TOPK_SLOTS, n), 0) + ch * TOPK_SLOTS).astype(F32)
            hits.append(jnp.where(slot == pos_e, 1.0, 0.0).astype(BF16))
        for ch in range(chunks):
            r = _dot_nt(hits[ch], table)
            idx = r[:, 0:1] * TOPK_RADIX + r[:, 1:2]
            idx_ref[0, e, ch * TOPK_SLOTS:(ch + 1) * TOPK_SLOTS, :] = idx.astype(jnp.int32)
            g_ref[0, e, ch * TOPK_SLOTS:(ch + 1) * TOPK_SLOTS, :] = (r[:, 2:3] + r[:, 3:4]) + r[:, 4:5]
        return carry

    lax.fori_loop(0, E, per_expert, 0)


def _topk(aff_t, cap):
    B, E, n = aff_t.shape
    tri = jnp.asarray(np.triu(np.ones((LANES, LANES), np.float32), k=1), BF16)
    return pl.pallas_call(
        functools.partial(_topk_kernel, cap),
        out_shape=[jax.ShapeDtypeStruct((B, E, cap, 1), jnp.int32),
                   jax.ShapeDtypeStruct((B, E, cap, 1), F32)],
        grid=(B,),
        in_specs=[pl.BlockSpec((1, E, n), lambda b: (b, 0, 0)),
                  pl.BlockSpec((LANES, LANES), lambda b: (0, 0))],
        out_specs=[pl.BlockSpec((1, E, cap, 1), lambda b: (b, 0, 0, 0)),
                   pl.BlockSpec((1, E, cap, 1), lambda b: (b, 0, 0, 0))],
        scratch_shapes=[pltpu.VMEM((E, n), F32)],
        compiler_params=_cparams(("parallel",)),
        name="expert_topk",
    )(aff_t, tri)


def _gather_kernel(s, idx_ref, u_ref, o_ref):
    cap = o_ref.shape[2] // s

    def body(i, carry):
        c0 = pl.multiple_of(i * ROW_GROUP, ROW_GROUP)
        for k in range(ROW_GROUP):
            src = pl.multiple_of(idx_ref[0, 0, 0, c0 + k] * s, s)
            o_ref[0, 0, pl.ds((c0 + k) * s, s), :] = u_ref[0, pl.ds(src, s), :]
        return carry

    lax.fori_loop(0, cap // ROW_GROUP, body, 0)


def _gather(idx, u2, n):
    B, E, _, cap = idx.shape
    _, ns, W = u2.shape
    s = ns // n
    return pl.pallas_call(
        functools.partial(_gather_kernel, s),
        out_shape=jax.ShapeDtypeStruct((B, E, cap * s, W), u2.dtype),
        grid=(B, E),
        in_specs=[pl.BlockSpec((1, 1, 1, cap), lambda b, e: (b, e, 0, 0), memory_space=pltpu.SMEM),
                  pl.BlockSpec((1, ns, W), lambda b, e: (b, 0, 0))],
        out_specs=pl.BlockSpec((1, 1, cap * s, W), lambda b, e: (b, e, 0, 0)),
        compiler_params=_cparams(("parallel", "arbitrary")),
        name="expert_gather",
    )(idx, u2)


def _ffn_kernel(xe_ref, g_ref, wg_ref, wu_ref, wd_ref, o_ref, wgb_ref, wub_ref, wdb_ref):
    @pl.when(pl.program_id(1) == 0)
    def _():
        wgb_ref[...] = wg_ref[0].astype(BF16)
        wub_ref[...] = wu_ref[0].astype(BF16)
        wdb_ref[...] = wd_ref[0].astype(BF16)

    xe = _load_tiles_as_rows(xe_ref.at[0, 0], wgb_ref.shape[0]).astype(BF16)
    gate = _dot(xe, wgb_ref[...])
    up = _dot(xe, wub_ref[...])
    h = (gate * jax.nn.sigmoid(gate) * up).astype(BF16)
    _store_rows_as_tiles(o_ref.at[0, 0], _dot(h, wdb_ref[...]) * g_ref[0, 0])


def _ffn(xe, g, w_gate, w_up, w_down):
    B, E, caps, W = xe.shape
    _, D, F = w_gate.shape
    cap = g.shape[2]
    return pl.pallas_call(
        _ffn_kernel,
        out_shape=jax.ShapeDtypeStruct((B, E, caps, W), F32),
        grid=(E, B),
        in_specs=[pl.BlockSpec((1, 1, caps, W), lambda e, b: (b, e, 0, 0)),
                  pl.BlockSpec((1, 1, cap, 1), lambda e, b: (b, e, 0, 0)),
                  pl.BlockSpec((1, D, F), lambda e, b: (e, 0, 0)),
                  pl.BlockSpec((1, D, F), lambda e, b: (e, 0, 0)),
                  pl.BlockSpec((1, F, D), lambda e, b: (e, 0, 0))],
        out_specs=pl.BlockSpec((1, 1, caps, W), lambda e, b: (b, e, 0, 0)),
        scratch_shapes=[pltpu.VMEM((D, F), BF16), pltpu.VMEM((D, F), BF16), pltpu.VMEM((F, D), BF16)],
        compiler_params=_cparams(("arbitrary", "arbitrary")),
        name="expert_ffn",
    )(xe, g, w_gate, w_up, w_down)


def _scatter_kernel(s, idx_ref, y_ref, o_ref):
    cap = y_ref.shape[2] // s

    def tile(r):
        return pl.ds(pl.multiple_of(r * s, s), s)

    @pl.when(pl.program_id(1) == 0)
    def _():
        o_ref[...] = jnp.zeros_like(o_ref)

    def body(i, carry):
        c0 = pl.multiple_of(i * ROW_GROUP, ROW_GROUP)
        rows = [idx_ref[0, 0, 0, c0 + k] for k in range(ROW_GROUP)]
        acc = [o_ref[0, tile(r), :] for r in rows]
        for k in range(ROW_GROUP):
            o_ref[0, tile(rows[k]), :] = acc[k] + y_ref[0, 0, tile(c0 + k), :]
        return carry

    lax.fori_loop(0, cap // ROW_GROUP, body, 0)


def _scatter(idx, ye, n):
    B, E, caps, W = ye.shape
    cap = idx.shape[3]
    s = caps // cap
    return pl.pallas_call(
        functools.partial(_scatter_kernel, s),
        out_shape=jax.ShapeDtypeStruct((B, n * s, W), F32),
        grid=(B, E),
        in_specs=[pl.BlockSpec((1, 1, 1, cap), lambda b, e: (b, e, 0, 0), memory_space=pltpu.SMEM),
                  pl.BlockSpec((1, 1, caps, W), lambda b, e: (b, e, 0, 0))],
        out_specs=pl.BlockSpec((1, n * s, W), lambda b, e: (b, 0, 0)),
        compiler_params=_cparams(("parallel", "arbitrary")),
        name="expert_scatter",
    )(idx, ye)


def _final_kernel(x_ref, moe_ref, g2_ref, lg_ref, lb_ref, o_ref):
    moe = _load_tiles_as_rows(moe_ref.at[0], x_ref.shape[2])
    o_ref[0] = _ln(ALPHA * x_ref[0] + g2_ref[0] * moe) * lg_ref[...] + lb_ref[...]


def _final(xn, moe, g2, lg, lb, tm):
    B, n, D = xn.shape
    tok = pl.BlockSpec((1, tm, D), lambda b, i: (b, i, 0))
    vec = pl.BlockSpec((1, D), lambda b, i: (0, 0))
    return pl.pallas_call(
        _final_kernel,
        out_shape=jax.ShapeDtypeStruct((B, n, D), F32),
        grid=(B, n // tm),
        in_specs=[tok, pl.BlockSpec((1, tm * (D // LANES), LANES), lambda b, i: (b, i, 0)),
                  pl.BlockSpec((1, 1, D), lambda b, i: (b, 0, 0)), vec, vec],
        out_specs=tok,
        compiler_params=_cparams(("parallel", "parallel")),
        name="final_ln",
    )(xn, moe, g2, lg, lb)


def _rope_tables(n):
    t = np.arange(n)
    row = (t // GRID_W).astype(np.float32)
    col = (t % GRID_W).astype(np.float32)
    per_axis = MLA_ROPE // 2
    inv_freq = jnp.asarray(ROPE_THETA, F32) ** (-jnp.arange(0, per_axis, 2, dtype=F32) / per_axis)
    ang = jnp.concatenate([jnp.asarray(row)[:, None] * inv_freq, jnp.asarray(col)[:, None] * inv_freq], axis=-1)
    cos, sin = jnp.cos(ang), jnp.sin(ang)
    pad = HEAD_PAD - MLA_QK
    c_tab = jnp.concatenate([jnp.ones((n, MLA_NOPE), F32), cos, cos, jnp.zeros((n, pad), F32)], axis=-1)
    s_tab = jnp.concatenate([jnp.zeros((n, MLA_NOPE), F32), -sin, sin, jnp.zeros((n, pad), F32)], axis=-1)
    return c_tab, s_tab


def _prep_weights(w_in, w_uq, w_ukv):
    D = w_in.shape[0]
    o1 = Q_LORA
    o2 = o1 + KV_LORA
    o3 = o2 + MLA_ROPE
    o4 = o3 + 3 * NA_W
    w_kr = w_in[:, o2:o3]
    z = lambda k: jnp.zeros((D, k), w_in.dtype)
    pad = HEAD_PAD - MLA_QK
    w_small = jnp.concatenate([w_in[:, :o2], z(MLA_NOPE), w_kr, z(pad)], axis=-1).astype(BF16)
    w_nqt = w_in[:, o3:o3 + NA_W].T.astype(BF16)
    w_na = w_in[:, o3 + NA_W:o3 + 2 * NA_W].astype(BF16)
    w_nvt = w_in[:, o3 + 2 * NA_W:o4].T.astype(BF16)
    w_g = w_in[:, o4:].astype(BF16)

    uq = w_uq.reshape(Q_LORA, MLA_HEADS, MLA_QK)
    zq = lambda k: jnp.zeros((Q_LORA, MLA_HEADS, k), w_uq.dtype)
    wqa = jnp.concatenate([uq, zq(pad)], axis=-1).reshape(Q_LORA, MLA_HEADS * HEAD_PAD).T.astype(BF16)

    ukv = w_ukv.reshape(KV_LORA, MLA_HEADS, MLA_NOPE + MLA_V)
    wk = jnp.concatenate([ukv[..., :MLA_NOPE], jnp.zeros((KV_LORA, MLA_HEADS, HEAD_PAD - MLA_NOPE), w_ukv.dtype)], axis=-1)
    wk = wk.reshape(KV_LORA, MLA_HEADS * HEAD_PAD).astype(BF16)
    wv = ukv[..., MLA_NOPE:].reshape(KV_LORA, MLA_HEADS * MLA_V).T.astype(BF16)
    return w_small, w_na, w_nqt, w_nvt, w_g, wqa, wk, wv


def _tile(n, pref):
    t = min(pref, n)
    while n % t:
        t //= 2
    return t


def kernel(x, c, ctx, c_ctx, w_mod, b_mod, w_in, q_norm_g, w_uq, kv_norm_g, w_ukv, na_rel_bias, w_proj_mla,
           w_proj_na, w_out, ln1_g, ln1_b, w_router, w_exp_gate, w_exp_up, w_exp_down, ln2_g, ln2_b):
    B, n, D = x.shape
    L = ctx.shape[1]
    rows = n // GRID_W
    assert n % (GRID_W * NA_QROWS) == 0 and rows >= NA_BAND
    assert w_mod.shape[0] == DEPTH
    cap = EC_CAPACITY * n // N_EXPERTS

    mod_rows = -(-(B + 1) // 8) * 8
    cc = jnp.concatenate([c, c_ctx[None], jnp.zeros((mod_rows - B - 1, D), F32)], axis=0)
    m = _modulation(cc, w_mod[0], b_mod[0])
    sh1, sc1, g1, sh2, sc2, g2 = [m[:B, k * D:(k + 1) * D].reshape(B, 1, D) for k in range(6)]
    csh1 = m[B:B + 1, :D].reshape(1, 1, D)
    csc1 = m[B:B + 1, D:2 * D].reshape(1, 1, D)

    w_small, w_na, w_nqt, w_nvt, w_g, wqa, wk, wv = _prep_weights(w_in[0], w_uq[0], w_ukv[0])
    qg = q_norm_g[0].reshape(1, Q_LORA)
    kvg = kv_norm_g[0].reshape(1, KV_LORA)
    c_tab, s_tab = _rope_tables(n)
    pad = HEAD_PAD - MLA_QK
    c_ctx_tab = jnp.concatenate([jnp.ones((L, MLA_QK), F32), jnp.zeros((L, pad), F32)], axis=-1)
    s_ctx_tab = jnp.zeros((L, HEAD_PAD), F32)

    q, k_lat, v_lat, nq, nk, nv, gates = _inproj(
        True, x, sc1, sh1, c_tab, s_tab, c_tab.T * (MLA_SCALE * LOG2E), s_tab.T * (MLA_SCALE * LOG2E),
        w_small, w_na, w_nqt, w_nvt, w_g, qg, kvg, wqa, wk, wv, _tile(n, 512))
    k_ctx, v_ctx, cnk, cnv = _inproj(
        False, ctx, csc1, csh1, c_ctx_tab, s_ctx_tab, c_ctx_tab.T, s_ctx_tab.T,
        w_small, w_na, w_nqt, w_nvt, w_g, qg, kvg, wqa, wk, wv, _tile(L, 256))

    y_mla = _mla_attention(q, k_lat, k_ctx, v_lat, v_ctx, _tile(n, 1024), _tile(n, 512))
    y_na = _na_attention(nq, nk, nv, cnk, cnv, _na_bias_tiles(na_rel_bias[0], rows))

    xn, u2, aff = _merge(
        y_mla, y_na, gates, x, g1, sc2, sh2,
        w_proj_mla[0].astype(BF16), w_proj_na[0].astype(BF16), w_out[0].astype(BF16),
        ln1_g[0].reshape(1, D), ln1_b[0].reshape(1, D), w_router[0], _tile(n, 2 * MERGE_SUB))

    idx4, g4 = _topk(jnp.swapaxes(aff, 1, 2), cap)
    idx = idx4.reshape(B, N_EXPERTS, 1, cap)
    xe = _gather(idx, u2, n)
    ye = _ffn(xe, g4, w_exp_gate[0], w_exp_up[0], w_exp_down[0])
    moe = _scatter(idx, ye, n)
    return _final(xn, moe, g2, ln2_g[0].reshape(1, D), ln2_b[0].reshape(1, D), _tile(n, 512))
```

```python
import functools
import math

import numpy as np
import jax
import jax.numpy as jnp
from jax import lax
from jax.experimental import pallas as pl
from jax.experimental.pallas import tpu as pltpu

GRID_W = 64
MLA_HEADS = 8
MLA_NOPE = 64
MLA_ROPE = 32
MLA_QK = MLA_NOPE + MLA_ROPE
MLA_V = 64
Q_LORA = 256
KV_LORA = 128
MLA_SCALE = MLA_QK ** -0.5
ROPE_THETA = 10000.0
NA_HEADS = 8
NA_DIM = 64
NA_W = NA_HEADS * NA_DIM
NA_WIN_H = 8
NA_WIN_W = 16
NA_SCALE = NA_DIM ** -0.5
N_EXPERTS = 16
EC_CAPACITY = 2
LN_EPS = 1e-5
RMS_EPS = 1e-6
DEPTH = 1
ALPHA = (2.0 * DEPTH) ** 0.25
LOG2E = math.log2(math.e)

LANES = 128
HEAD_PAD = LANES
VMEM_LIMIT = 56 * 1024 * 1024
NEG_BIG = -1e30

NA_QROWS = 4
NA_BAND = 12
ROW_GROUP = 8
NA_LOOKAHEAD = 2
MERGE_SUB = 256
TOPK_RADIX = 64
TOPK_TABLE_ROWS = 16
TOPK_SLOTS = 128
TOPK_REFINE = 30

BF16 = jnp.bfloat16
F32 = jnp.float32


def _cparams(sem):
    return pltpu.CompilerParams(dimension_semantics=sem, vmem_limit_bytes=VMEM_LIMIT)


def _ln(x):
    mu = jnp.mean(x, axis=-1, keepdims=True)
    xc = x - mu
    var = jnp.mean(xc * xc, axis=-1, keepdims=True)
    return xc * lax.rsqrt(var + LN_EPS)


def _dot(a, b):
    return jnp.dot(a, b, preferred_element_type=F32)


def _dot_nt(a, b):
    return lax.dot_general(a, b, (((1,), (1,)), ((), ())), preferred_element_type=F32)


def _store_rows_as_tiles(ref, val):
    m, d = val.shape
    s = d // LANES
    for j in range(s):
        ref[pl.ds(j, m, stride=s), :] = val[:, j * LANES:(j + 1) * LANES]


def _load_tiles_as_rows(ref, d):
    s = d // LANES
    m = ref.shape[0] // s
    return jnp.concatenate([ref[pl.ds(j, m, stride=s), :] for j in range(s)], axis=-1)


def _mod_kernel(c_ref, w_ref, b_ref, o_ref):
    c = c_ref[...]
    s = c * jax.nn.sigmoid(c)
    o_ref[...] = jnp.dot(s, w_ref[...], preferred_element_type=F32,
                         precision=lax.Precision.HIGHEST) + b_ref[...]


def _modulation(cc, w_mod, b_mod):
    rows, d = cc.shape
    n_out = w_mod.shape[1]
    tn = 1024
    return pl.pallas_call(
        _mod_kernel,
        out_shape=jax.ShapeDtypeStruct((rows, n_out), F32),
        grid=(n_out // tn,),
        in_specs=[pl.BlockSpec((rows, d), lambda j: (0, 0)),
                  pl.BlockSpec((d, tn), lambda j: (0, j)),
                  pl.BlockSpec((1, tn), lambda j: (0, j))],
        out_specs=pl.BlockSpec((rows, tn), lambda j: (0, j)),
        compiler_params=_cparams(("arbitrary",)),
        name="modulation",
    )(cc, w_mod, b_mod.reshape(1, n_out))


def _swap_rope_halves(x, axis):
    half = MLA_ROPE // 2
    pos = lax.broadcasted_iota(jnp.int32, x.shape, axis)
    return jnp.where(pos < MLA_NOPE + half, pltpu.roll(x, HEAD_PAD - half, axis), pltpu.roll(x, half, axis))


def _store_value_tiles(ref, vt, dim):
    tm = vt.shape[1]
    tail = jnp.where(lax.broadcasted_iota(jnp.int32, (HEAD_PAD - dim, tm), 0) == 0, 1.0, 0.0).astype(ref.dtype)
    for h in range(vt.shape[0] // dim):
        ref[h * HEAD_PAD:h * HEAD_PAD + dim, :] = vt[h * dim:(h + 1) * dim, :].astype(ref.dtype)
        ref[h * HEAD_PAD + dim:(h + 1) * HEAD_PAD, :] = tail


def _inproj_kernel(latent, x_ref, sc_ref, sh_ref, ck_ref, sk_ref, cq_ref, sq_ref,
                   w_small_ref, w_na_ref, w_nqt_ref, w_nvt_ref, w_g_ref, qg_ref, kvg_ref, wqa_ref, wk_ref, wv_ref,
                   *out_refs):
    if latent:
        q_ref, k_ref, v_ref, nq_ref, nk_ref, nv_ref, g_ref = out_refs
    else:
        k_ref, v_ref, nk_ref, nv_ref = out_refs
    x = x_ref[0]
    u = (_ln(x) * (1.0 + sc_ref[0]) + sh_ref[0]).astype(BF16)

    small = _dot(u, w_small_ref[...])
    q_c = small[:, :Q_LORA]
    kv_c = small[:, Q_LORA:Q_LORA + KV_LORA]
    k_r = small[:, Q_LORA + KV_LORA:]

    kvn = (kv_c * lax.rsqrt(jnp.mean(kv_c * kv_c, axis=-1, keepdims=True) + RMS_EPS) * kvg_ref[...]).astype(BF16)
    kk = _dot(kvn, wk_ref[...])
    kr = k_r * ck_ref[...] + _swap_rope_halves(k_r, 1) * sk_ref[...]
    for h in range(MLA_HEADS):
        k_ref[0, h] = (kk[:, h * HEAD_PAD:(h + 1) * HEAD_PAD] + kr).astype(BF16)
    _store_value_tiles(v_ref.at[0], _dot_nt(wv_ref[...], kvn), MLA_V)

    nk_ref[0] = _dot(u, w_na_ref[...]).astype(BF16)
    _store_value_tiles(nv_ref.at[0], _dot_nt(w_nvt_ref[...], u), NA_DIM)
    if latent:
        qn = (q_c * lax.rsqrt(jnp.mean(q_c * q_c, axis=-1, keepdims=True) + RMS_EPS) * qg_ref[...]).astype(BF16)
        qa = _dot_nt(wqa_ref[...], qn)
        cq = cq_ref[...]
        sq = sq_ref[...]
        for h in range(MLA_HEADS):
            qh = qa[h * HEAD_PAD:(h + 1) * HEAD_PAD, :]
            q_ref[0, h] = (qh * cq + _swap_rope_halves(qh, 0) * sq).astype(BF16)
        nq_ref[0] = (_dot_nt(w_nqt_ref[...], u) * (NA_SCALE * LOG2E)).astype(BF16)
        g_ref[0] = jax.nn.sigmoid(_dot(u, w_g_ref[...])).astype(BF16)


def _inproj(latent, x, sc, sh, ck, sk, cq, sq, w_small, w_na, w_nqt, w_nvt, w_g, qg, kvg, wqa, wk, wv, tm):
    B, n, D = x.shape
    per_batch = sc.shape[0] > 1
    mod_map = (lambda b, i: (b, 0, 0)) if per_batch else (lambda b, i: (0, 0, 0))
    full = lambda a: pl.BlockSpec(a.shape, lambda b, i: (0,) * a.ndim)
    tab = lambda a: pl.BlockSpec((tm, a.shape[1]), lambda b, i: (i, 0))
    tab_t = lambda a: pl.BlockSpec((a.shape[0], tm), lambda b, i: (0, i))
    in_specs = [pl.BlockSpec((1, tm, D), lambda b, i: (b, i, 0)),
                pl.BlockSpec((1, 1, D), mod_map), pl.BlockSpec((1, 1, D), mod_map),
                tab(ck), tab(sk), tab_t(cq), tab_t(sq),
                full(w_small), full(w_na), full(w_nqt), full(w_nvt), full(w_g), full(qg), full(kvg),
                full(wqa), full(wk), full(wv)]
    hk = jax.ShapeDtypeStruct((B, MLA_HEADS, n, HEAD_PAD), BF16)
    hk_spec = pl.BlockSpec((1, MLA_HEADS, tm, HEAD_PAD), lambda b, i: (b, 0, i, 0))
    tok = lambda w: jax.ShapeDtypeStruct((B, n, w), BF16)
    tok_spec = lambda w: pl.BlockSpec((1, tm, w), lambda b, i: (b, i, 0))
    vt = jax.ShapeDtypeStruct((B, MLA_HEADS * HEAD_PAD, n), BF16)
    vt_spec = pl.BlockSpec((1, MLA_HEADS * HEAD_PAD, tm), lambda b, i: (b, 0, i))
    nqt = jax.ShapeDtypeStruct((B, NA_W, n), BF16)
    nqt_spec = pl.BlockSpec((1, NA_W, tm), lambda b, i: (b, 0, i))
    if latent:
        qt = jax.ShapeDtypeStruct((B, MLA_HEADS, HEAD_PAD, n), BF16)
        qt_spec = pl.BlockSpec((1, MLA_HEADS, HEAD_PAD, tm), lambda b, i: (b, 0, 0, i))
        out_shape = [qt, hk, vt, nqt, tok(NA_W), vt, tok(2 * D)]
        out_specs = [qt_spec, hk_spec, vt_spec, nqt_spec, tok_spec(NA_W), vt_spec, tok_spec(2 * D)]
    else:
        out_shape = [hk, vt, tok(NA_W), vt]
        out_specs = [hk_spec, vt_spec, tok_spec(NA_W), vt_spec]
    return pl.pallas_call(
        functools.partial(_inproj_kernel, latent),
        out_shape=out_shape,
        grid=(B, n // tm),
        in_specs=in_specs,
        out_specs=out_specs,
        compiler_params=_cparams(("parallel", "parallel")),
        name="inproj_latent" if latent else "inproj_ctx",
    )(x, sc, sh, ck, sk, cq, sq, w_small, w_na, w_nqt, w_nvt, w_g, qg, kvg, wqa, wk, wv)


def _mla_kernel(tk, q_ref, kc_ref, kl_ref, vc_ref, vl_ref, o_ref, m_ref, acc_ref, sa_ref, sb_ref, sc_ref):
    heads = q_ref.shape[1]
    n = kl_ref.shape[2]
    tq = q_ref.shape[3]
    nt = n // tk

    def scores(h, k):
        return _dot(k, q_ref[0, h])

    def absorb(hh, s, v_t):
        m_old = m_ref[hh]
        m_new = jnp.maximum(m_old, jnp.max(s, axis=0, keepdims=True))
        a = jnp.exp2(m_old - m_new)
        p = jnp.exp2(s - m_new).astype(BF16)
        acc_ref[hh] = a * acc_ref[hh] + _dot(v_t, p)
        m_ref[hh] = m_new

    def ctx_scores(h0):
        for hh in range(2):
            sc_ref[hh] = scores(h0 + hh, kc_ref[0, h0 + hh])

    def pair_pass(pair, carry):
        h0 = 2 * pair

        def head_rows(hh):
            return pl.ds(pl.multiple_of((h0 + hh) * HEAD_PAD, HEAD_PAD), HEAD_PAD)

        def k_tile(hh, j):
            return kl_ref[0, h0 + hh, pl.ds(pl.multiple_of(j * tk, tk), tk), :]

        def v_tile(hh, j):
            return vl_ref[0, head_rows(hh), pl.ds(pl.multiple_of(j * tk, tk), tk)]

        def advance(j, cur_ref, next_ref):
            for hh in range(2):
                next_ref[hh] = scores(h0 + hh, k_tile(hh, j + 1))
                absorb(hh, cur_ref[hh], v_tile(hh, j))

        m_ref[...] = jnp.full(m_ref.shape, NEG_BIG, F32)
        acc_ref[...] = jnp.zeros(acc_ref.shape, F32)
        for hh in range(2):
            sa_ref[hh] = scores(h0 + hh, k_tile(hh, 0))
            absorb(hh, sc_ref[hh], vc_ref[0, head_rows(hh), :])

        def body(jj, c):
            advance(2 * jj, sa_ref, sb_ref)
            advance(2 * jj + 1, sb_ref, sa_ref)
            return c

        lax.fori_loop(0, (nt - 1) // 2, body, 0)
        last_ref = sa_ref
        if (nt - 1) % 2:
            advance(nt - 2, sa_ref, sb_ref)
            last_ref = sb_ref
        ctx_scores(jnp.minimum(h0 + 2, heads - 2))
        outs = []
        for hh in range(2):
            absorb(hh, last_ref[hh], v_tile(hh, nt - 1))
            acc = acc_ref[hh]
            outs.append(acc[:MLA_V] / acc[MLA_V:MLA_V + 1])
        lanes = pl.ds(pl.multiple_of(pair * LANES, LANES), LANES)
        o_ref[0, :, lanes] = jnp.concatenate(outs, axis=0).T.astype(o_ref.dtype)
        return carry

    ctx_scores(0)
    lax.fori_loop(0, heads // 2, pair_pass, 0)


def _mla_attention(q, k_lat, k_ctx, v_lat, v_ctx, tq, tk):
    B, H, _, n = q.shape
    L = k_ctx.shape[2]
    W = H * MLA_V
    return pl.pallas_call(
        functools.partial(_mla_kernel, tk),
        out_shape=jax.ShapeDtypeStruct((B, n, W), BF16),
        grid=(B, n // tq),
        in_specs=[pl.BlockSpec((1, H, HEAD_PAD, tq), lambda b, i: (b, 0, 0, i)),
                  pl.BlockSpec((1, H, L, HEAD_PAD), lambda b, i: (b, 0, 0, 0)),
                  pl.BlockSpec((1, H, n, HEAD_PAD), lambda b, i: (b, 0, 0, 0)),
                  pl.BlockSpec((1, H * HEAD_PAD, L), lambda b, i: (b, 0, 0)),
                  pl.BlockSpec((1, H * HEAD_PAD, n), lambda b, i: (b, 0, 0))],
        out_specs=pl.BlockSpec((1, tq, W), lambda b, i: (b, i, 0)),
        scratch_shapes=[pltpu.VMEM((2, 1, tq), F32),
                        pltpu.VMEM((2, HEAD_PAD, tq), F32),
                        pltpu.VMEM((2, tk, tq), F32), pltpu.VMEM((2, tk, tq), F32),
                        pltpu.VMEM((2, L, tq), F32)],
        compiler_params=_cparams(("parallel", "parallel")),
        name="mla_attention",
    )(q, k_ctx, k_lat, v_ctx, v_lat)


def _na_block_tables(rows):
    nblk = rows // NA_QROWS
    wh = min(NA_WIN_H, rows)
    band0 = np.clip(np.arange(nblk) * NA_QROWS - wh // 2, 0, rows - NA_BAND)
    sigs, types = [], []
    for i in range(nblk):
        r = i * NA_QROWS + np.arange(NA_QROWS)
        r0 = np.clip(r - wh // 2, 0, rows - wh)
        sig = (tuple(r0 - band0[i]), tuple(r - band0[i]))
        if sig not in sigs:
            sigs.append(sig)
        types.append(sigs.index(sig))
    return band0.astype(np.int32), np.asarray(types, np.int32), sigs, wh


def _bias_tile_kernel(dr, ok, tbl_ref, o_ref):
    neg = jnp.full((GRID_W, GRID_W), NEG_BIG, F32)
    for t in range(o_ref.shape[0]):
        for k in range(NA_BAND):
            pieces = [tbl_ref[0, int(dr[t, q, k])] if ok[t, q, k] else neg for q in range(NA_QROWS)]
            o_ref[t, 0, k * GRID_W:(k + 1) * GRID_W, :] = jnp.concatenate(pieces, axis=1)


def _na_bias_tiles(rel_bias, rows):
    _, _, sigs, wh = _na_block_tables(rows)
    ww = NA_WIN_W
    n_dr, n_dc = 2 * NA_WIN_H - 1, 2 * NA_WIN_W - 1
    col = np.arange(GRID_W)
    c0 = np.clip(col - ww // 2, 0, GRID_W - ww)
    col_ok = (col[:, None] >= c0[None, :]) & (col[:, None] < c0[None, :] + ww)
    dc = col[:, None] - col[None, :] + (NA_WIN_W - 1)
    pick_c = ((dc[None] == np.arange(n_dc)[:, None, None]) & col_ok[None]).astype(np.float32)
    by_col = jnp.einsum('hdj,jyx->hdyx', rel_bias.astype(F32), jnp.asarray(pick_c), precision=lax.Precision.HIGHEST)
    by_col = jnp.where(jnp.asarray(col_ok), by_col * LOG2E, NEG_BIG)
    kr = np.arange(NA_BAND)
    ok = np.stack([(kr[None, :] >= np.asarray(r0)[:, None]) & (kr[None, :] < np.asarray(r0)[:, None] + wh)
                   for r0, _ in sigs])
    dr = np.stack([kr[None, :] - np.asarray(r)[:, None] + (NA_WIN_H - 1) for _, r in sigs])
    assert ((dr >= 0) & (dr < n_dr))[ok].all()
    T = len(sigs)
    kn, qn = NA_BAND * GRID_W, NA_QROWS * GRID_W
    return pl.pallas_call(
        functools.partial(_bias_tile_kernel, dr, ok),
        out_shape=jax.ShapeDtypeStruct((T, NA_HEADS, kn, qn), F32),
        grid=(NA_HEADS,),
        in_specs=[pl.BlockSpec((1, n_dr, GRID_W, GRID_W), lambda h: (h, 0, 0, 0))],
        out_specs=pl.BlockSpec((T, 1, kn, qn), lambda h: (0, h, 0, 0)),
        compiler_params=_cparams(("parallel",)),
        name="na_bias_tiles",
    )(by_col)


def _na_kernel(nblk, band_ref, type_ref, q_ref, k_ref, vt_ref, kc_ref, vct_ref, *rest):
    bias_refs, o_ref = rest[:nblk], rest[nblk]
    i = pl.program_id(1)
    nq = NA_QROWS * GRID_W
    nk = NA_BAND * GRID_W
    starts = [pl.multiple_of(band_ref[i * nblk + b] * GRID_W, NA_QROWS * GRID_W) for b in range(nblk)]
    row = lax.broadcasted_iota(jnp.int32, (LANES, nq), 0)

    def scores(c):
        blk, h = divmod(c, NA_HEADS)
        cs = slice((h // 2) * LANES, (h // 2 + 1) * LANES)
        qp = q_ref[0, cs, blk * nq:(blk + 1) * nq]
        own = (row < NA_DIM) if h % 2 == 0 else (row >= NA_DIM)
        qh = jnp.where(own, qp, jnp.zeros_like(qp))
        s_loc = _dot(k_ref[0, pl.ds(starts[blk], nk), cs], qh) + bias_refs[blk][0, h]
        s_ctx = _dot(kc_ref[0, :, cs], qh)
        return s_loc, s_ctx

    def attend(c, s):
        s_loc, s_ctx = s
        blk, h = divmod(c, NA_HEADS)
        start = starts[blk]
        vs = slice(h * HEAD_PAD, (h + 1) * HEAD_PAD)
        m = jnp.maximum(jnp.max(s_loc, axis=0, keepdims=True), jnp.max(s_ctx, axis=0, keepdims=True))
        p_loc = jnp.exp2((s_loc - m).astype(BF16))
        p_ctx = jnp.exp2((s_ctx - m).astype(BF16))
        o = _dot(vt_ref[0, vs, pl.ds(start, nk)], p_loc) + _dot(vct_ref[0, vs, :], p_ctx)
        return o[:NA_DIM] / o[NA_DIM:NA_DIM + 1]

    chain = nblk * NA_HEADS
    ahead = [scores(c) for c in range(min(NA_LOOKAHEAD, chain))]
    outs = []
    for c in range(chain):
        if c + NA_LOOKAHEAD < chain:
            ahead.append(scores(c + NA_LOOKAHEAD))
        outs.append(attend(c, ahead.pop(0)))
        if c % 2:
            blk, h = divmod(c, NA_HEADS)
            pair_t = jnp.concatenate([outs[c - 1], outs[c]], axis=0)
            o_ref[0, blk * nq:(blk + 1) * nq, (h // 2) * LANES:(h // 2 + 1) * LANES] = pair_t.T.astype(o_ref.dtype)


def _na_attention(nqt, nk, nvt, cnk, cnvt, bias_tiles):
    B, n, W = nk.shape
    L = cnk.shape[1]
    rows = n // GRID_W
    band0, types, _, _ = _na_block_tables(rows)
    qn = NA_QROWS * GRID_W
    kn = NA_BAND * GRID_W
    nblocks = rows // NA_QROWS
    nblk = 2 if nblocks % 2 == 0 else 1

    def bias_spec(k):
        return pl.BlockSpec((1, NA_HEADS, kn, qn), lambda b, i, bd, ty: (ty[i * nblk + k], 0, 0, 0))

    grid_spec = pltpu.PrefetchScalarGridSpec(
        num_scalar_prefetch=2,
        grid=(B, nblocks // nblk),
        in_specs=[pl.BlockSpec((1, W, nblk * qn), lambda b, i, bd, ty: (b, 0, i)),
                  pl.BlockSpec((1, n, W), lambda b, i, bd, ty: (b, 0, 0)),
                  pl.BlockSpec((1, NA_HEADS * HEAD_PAD, n), lambda b, i, bd, ty: (b, 0, 0)),
                  pl.BlockSpec((1, L, W), lambda b, i, bd, ty: (b, 0, 0)),
                  pl.BlockSpec((1, NA_HEADS * HEAD_PAD, L), lambda b, i, bd, ty: (b, 0, 0))]
                 + [bias_spec(k) for k in range(nblk)],
        out_specs=pl.BlockSpec((1, nblk * qn, W), lambda b, i, bd, ty: (b, i, 0)),
    )
    return pl.pallas_call(
        functools.partial(_na_kernel, nblk),
        out_shape=jax.ShapeDtypeStruct((B, n, W), BF16),
        grid_spec=grid_spec,
        compiler_params=_cparams(("parallel", "arbitrary")),
        name="na_attention",
    )(jnp.asarray(band0), jnp.asarray(types), nqt, nk, nvt, cnk, cnvt, *([bias_tiles] * nblk))


def _split2(a):
    hi = a.astype(BF16)
    return hi, (a - hi.astype(F32)).astype(BF16)


def _merge_kernel(ym_ref, yn_ref, g_ref, x_ref, g1_ref, sc2_ref, sh2_ref, wpm_ref, wpn_ref, wo_ref,
                  l1g_ref, l1b_ref, wr_ref, xn_ref, up_ref, aff_ref):
    tm, D = x_ref.shape[1], x_ref.shape[2]
    s = D // LANES
    sub = min(MERGE_SUB, tm)
    wh, wl = _split2(wr_ref[...])

    def mix_of(r):
        g = g_ref[0, r, :]
        a = (g[:, :D].astype(F32) * _dot(ym_ref[0, r, :], wpm_ref[...])
             + g[:, D:].astype(F32) * _dot(yn_ref[0, r, :], wpn_ref[...]))
        return _dot(a.astype(BF16), wo_ref[...])

    def finish(r, mix):
        xn = _ln(ALPHA * x_ref[0, r, :] + g1_ref[0] * mix) * l1g_ref[...] + l1b_ref[...]
        xn_ref[0, r, :] = xn
        u2 = _ln(xn) * (1.0 + sc2_ref[0]) + sh2_ref[0]
        _store_rows_as_tiles(up_ref.at[0, r.start * s:r.stop * s], u2)
        uh, ul = _split2(u2)
        logits = _dot(uh, wh) + (_dot(uh, wl) + _dot(ul, wh))
        e = jnp.exp(logits - jnp.max(logits, axis=-1, keepdims=True))
        aff_ref[0, r, :] = e / jnp.sum(e, axis=-1, keepdims=True)

    rows = [slice(i, i + sub) for i in range(0, tm, sub)]
    mix = mix_of(rows[0])
    for i, r in enumerate(rows):
        mix_next = mix_of(rows[i + 1]) if i + 1 < len(rows) else None
        finish(r, mix)
        mix = mix_next


def _merge(y_mla, y_na, gates, x, g1, sc2, sh2, wpm, wpn, wo, l1g, l1b, wr, tm):
    B, n, D = x.shape
    E = wr.shape[1]
    full = lambda a: pl.BlockSpec(a.shape, lambda b, i: (0,) * a.ndim)
    tok = lambda w: pl.BlockSpec((1, tm, w), lambda b, i: (b, i, 0))
    mod = pl.BlockSpec((1, 1, D), lambda b, i: (b, 0, 0))
    return pl.pallas_call(
        _merge_kernel,
        out_shape=[jax.ShapeDtypeStruct((B, n, D), F32),
                   jax.ShapeDtypeStruct((B, n * (D // LANES), LANES), F32),
                   jax.ShapeDtypeStruct((B, n, E), F32)],
        grid=(B, n // tm),
        in_specs=[tok(y_mla.shape[2]), tok(y_na.shape[2]), tok(2 * D), tok(D), mod, mod, mod,
                  full(wpm), full(wpn), full(wo), full(l1g), full(l1b), full(wr)],
        out_specs=[tok(D), pl.BlockSpec((1, tm * (D // LANES), LANES), lambda b, i: (b, i, 0)), tok(E)],
        compiler_params=_cparams(("parallel", "parallel")),
        name="merge_ln_router",
    )(y_mla, y_na, gates, x, g1, sc2, sh2, wpm, wpn, wo, l1g, l1b, wr)


def _topk_kernel(cap, aff_ref, tri_ref, idx_ref, g_ref, pos_ref):
    aff = aff_ref[0]
    E, n = aff.shape

    def count(mask):
        return jnp.sum(mask.astype(F32), axis=-1, keepdims=True)

    def enough(v):
        return count(aff >= v) >= cap

    def search(t, thr):
        b1 = jnp.int32(1) << (30 - 2 * t)
        b0 = b1 >> 1
        c3, c2, c1 = thr | b1 | b0, thr | b1, thr | b0
        ok3, ok2, ok1 = [enough(pltpu.bitcast(c, F32)) for c in (c3, c2, c1)]
        return jnp.where(ok3, c3, jnp.where(ok2, c2, jnp.where(ok1, c1, thr)))

    thr = lax.fori_loop(0, 16, search, jnp.zeros((E, 1), jnp.int32))
    lo = pltpu.bitcast(thr, F32)
    hi = pltpu.bitcast(thr + 1, F32)

    def refine(t, lh):
        lo, hi = lh
        w = hi - lo
        q1, q2, q3 = lo + w * 0.25, lo + w * 0.5, lo + w * 0.75
        ok1, ok2, ok3 = enough(q1), enough(q2), enough(q3)
        new_lo = jnp.where(ok3, q3, jnp.where(ok2, q2, jnp.where(ok1, q1, lo)))
        new_hi = jnp.where(ok3, hi, jnp.where(ok2, q3, jnp.where(ok1, q2, q1)))
        return new_lo, new_hi

    lo, hi = lax.fori_loop(0, TOPK_REFINE // 2, refine, (lo, hi))
    gt = aff >= hi
    eq = (aff >= lo) & ~gt
    need = cap - count(gt)

    tri = tri_ref[...]

    def prefix(mask):
        mf = mask.astype(F32).astype(BF16)
        parts = []
        off = jnp.zeros((E, 1), F32)
        for c in range(n // LANES):
            blk = mf[:, c * LANES:(c + 1) * LANES]
            parts.append(_dot(blk, tri) + off)
            off = off + jnp.sum(blk.astype(F32), axis=-1, keepdims=True)
        return jnp.concatenate(parts, axis=-1)

    sel = gt | (eq & (prefix(eq) < need))
    pos_ref[...] = jnp.where(sel, prefix(sel), -1.0)

    tok = lax.broadcasted_iota(jnp.int32, (1, n), 1)
    digits = [(tok // TOPK_RADIX).astype(F32), (tok % TOPK_RADIX).astype(F32)]
    chunks = cap // TOPK_SLOTS

    def per_expert(e, carry):
        a = aff_ref[0, pl.ds(e, 1), :]
        a1 = a.astype(BF16).astype(F32)
        a2 = (a - a1).astype(BF16).astype(F32)
        a3 = (a - a1) - a2
        rows = digits + [a1, a2, a3]
        table = jnp.concatenate(rows + [jnp.zeros((TOPK_TABLE_ROWS - len(rows), n), F32)], axis=0).astype(BF16)
        pos_e = pos_ref[pl.ds(e, 1), :]

        hits = []
        for ch in range(chunks):
            slot = (lax.broadcasted_iota(jnp.int32, (TOPK_SLOTS, n), 0) + ch * TOPK_SLOTS).astype(F32)
            hits.append(jnp.where(slot == pos_e, 1.0, 0.0).astype(BF16))
        for ch in range(chunks):
            r = _dot_nt(hits[ch], table)
            idx = r[:, 0:1] * TOPK_RADIX + r[:, 1:2]
            idx_ref[0, e, ch * TOPK_SLOTS:(ch + 1) * TOPK_SLOTS, :] = idx.astype(jnp.int32)
            g_ref[0, e, ch * TOPK_SLOTS:(ch + 1) * TOPK_SLOTS, :] = (r[:, 2:3] + r[:, 3:4]) + r[:, 4:5]
        return carry

    lax.fori_loop(0, E, per_expert, 0)


def _topk(aff_t, cap):
    B, E, n = aff_t.shape
    tri = jnp.asarray(np.triu(np.ones((LANES, LANES), np.float32), k=1), BF16)
    return pl.pallas_call(
        functools.partial(_topk_kernel, cap),
        out_shape=[jax.ShapeDtypeStruct((B, E, cap, 1), jnp.int32),
                   jax.ShapeDtypeStruct((B, E, cap, 1), F32)],
        grid=(B,),
        in_specs=[pl.BlockSpec((1, E, n), lambda b: (b, 0, 0)),
                  pl.BlockSpec((LANES, LANES), lambda b: (0, 0))],
        out_specs=[pl.BlockSpec((1, E, cap, 1), lambda b: (b, 0, 0, 0)),
                   pl.BlockSpec((1, E, cap, 1), lambda b: (b, 0, 0, 0))],
        scratch_shapes=[pltpu.VMEM((E, n), F32)],
        compiler_params=_cparams(("parallel",)),
        name="expert_topk",
    )(aff_t, tri)


def _gather_kernel(s, idx_ref, u_ref, o_ref):
    cap = o_ref.shape[2] // s

    def body(i, carry):
        c0 = pl.multiple_of(i * ROW_GROUP, ROW_GROUP)
        for k in range(ROW_GROUP):
            src = pl.multiple_of(idx_ref[0, 0, 0, c0 + k] * s, s)
            o_ref[0, 0, pl.ds((c0 + k) * s, s), :] = u_ref[0, pl.ds(src, s), :]
        return carry

    lax.fori_loop(0, cap // ROW_GROUP, body, 0)


def _gather(idx, u2, n):
    B, E, _, cap = idx.shape
    _, ns, W = u2.shape
    s = ns // n
    return pl.pallas_call(
        functools.partial(_gather_kernel, s),
        out_shape=jax.ShapeDtypeStruct((B, E, cap * s, W), u2.dtype),
        grid=(B, E),
        in_specs=[pl.BlockSpec((1, 1, 1, cap), lambda b, e: (b, e, 0, 0), memory_space=pltpu.SMEM),
                  pl.BlockSpec((1, ns, W), lambda b, e: (b, 0, 0))],
        out_specs=pl.BlockSpec((1, 1, cap * s, W), lambda b, e: (b, e, 0, 0)),
        compiler_params=_cparams(("parallel", "arbitrary")),
        name="expert_gather",
    )(idx, u2)


def _ffn_kernel(xe_ref, g_ref, wg_ref, wu_ref, wd_ref, o_ref, wgb_ref, wub_ref, wdb_ref):
    @pl.when(pl.program_id(1) == 0)
    def _():
        wgb_ref[...] = wg_ref[0].astype(BF16)
        wub_ref[...] = wu_ref[0].astype(BF16)
        wdb_ref[...] = wd_ref[0].astype(BF16)

    xe = _load_tiles_as_rows(xe_ref.at[0, 0], wgb_ref.shape[0]).astype(BF16)
    gate = _dot(xe, wgb_ref[...])
    up = _dot(xe, wub_ref[...])
    h = (gate * jax.nn.sigmoid(gate) * up).astype(BF16)
    _store_rows_as_tiles(o_ref.at[0, 0], _dot(h, wdb_ref[...]) * g_ref[0, 0])


def _ffn(xe, g, w_gate, w_up, w_down):
    B, E, caps, W = xe.shape
    _, D, F = w_gate.shape
    cap = g.shape[2]
    return pl.pallas_call(
        _ffn_kernel,
        out_shape=jax.ShapeDtypeStruct((B, E, caps, W), F32),
        grid=(E, B),
        in_specs=[pl.BlockSpec((1, 1, caps, W), lambda e, b: (b, e, 0, 0)),
                  pl.BlockSpec((1, 1, cap, 1), lambda e, b: (b, e, 0, 0)),
                  pl.BlockSpec((1, D, F), lambda e, b: (e, 0, 0)),
                  pl.BlockSpec((1, D, F), lambda e, b: (e, 0, 0)),
                  pl.BlockSpec((1, F, D), lambda e, b: (e, 0, 0))],
        out_specs=pl.BlockSpec((1, 1, caps, W), lambda e, b: (b, e, 0, 0)),
        scratch_shapes=[pltpu.VMEM((D, F), BF16), pltpu.VMEM((D, F), BF16), pltpu.VMEM((F, D), BF16)],
        compiler_params=_cparams(("arbitrary", "arbitrary")),
        name="expert_ffn",
    )(xe, g, w_gate, w_up, w_down)


def _combine_kernel(s, n_exp, x_ref, idx_ref, y_ref, g2_ref, lg_ref, lb_ref, o_ref, acc_ref):
    e = pl.program_id(1)
    cap = y_ref.shape[2] // s
    tm, D = x_ref.shape[1], x_ref.shape[2]

    def tile(r):
        return pl.ds(pl.multiple_of(r * s, s), s)

    @pl.when(e == 0)
    def _():
        acc_ref[...] = jnp.zeros_like(acc_ref)

    @pl.when(e < n_exp)
    def _():
        def body(i, carry):
            c0 = pl.multiple_of(i * ROW_GROUP, ROW_GROUP)
            rows = [idx_ref[0, 0, 0, c0 + k] for k in range(ROW_GROUP)]
            acc = [acc_ref[tile(r), :] for r in rows]
            for k in range(ROW_GROUP):
                acc_ref[tile(rows[k]), :] = acc[k] + y_ref[0, 0, tile(c0 + k), :]
            return carry

        lax.fori_loop(0, cap // ROW_GROUP, body, 0)

    @pl.when(e >= n_exp)
    def _():
        first = pl.multiple_of((e - n_exp) * (tm * s), tm * s)
        moe = _load_tiles_as_rows(acc_ref.at[pl.ds(first, tm * s)], D)
        o_ref[0] = _ln(ALPHA * x_ref[0] + g2_ref[0] * moe) * lg_ref[...] + lb_ref[...]


def _combine(xn, idx, ye, g2, lg, lb, tm):
    B, n, D = xn.shape
    _, E, caps, W = ye.shape
    cap = idx.shape[3]
    s = caps // cap
    blocks = n // tm
    tok = pl.BlockSpec((1, tm, D), lambda b, e: (b, jnp.maximum(e - E, 0), 0))
    vec = pl.BlockSpec((1, D), lambda b, e: (0, 0))
    last = lambda b, e: (b, jnp.minimum(e, E - 1), 0, 0)
    return pl.pallas_call(
        functools.partial(_combine_kernel, s, E),
        out_shape=jax.ShapeDtypeStruct((B, n, D), F32),
        grid=(B, E + blocks),
        in_specs=[tok,
                  pl.BlockSpec((1, 1, 1, cap), last, memory_space=pltpu.SMEM),
                  pl.BlockSpec((1, 1, caps, W), last),
                  pl.BlockSpec((1, 1, D), lambda b, e: (b, 0, 0)), vec, vec],
        out_specs=tok,
        scratch_shapes=[pltpu.VMEM((n * s, W), F32)],
        compiler_params=_cparams(("parallel", "arbitrary")),
        name="expert_combine_ln",
    )(xn, idx, ye, g2, lg, lb)


def _rope_tables(n):
    t = np.arange(n)
    row = (t // GRID_W).astype(np.float32)
    col = (t % GRID_W).astype(np.float32)
    per_axis = MLA_ROPE // 2
    inv_freq = jnp.asarray(ROPE_THETA, F32) ** (-jnp.arange(0, per_axis, 2, dtype=F32) / per_axis)
    ang = jnp.concatenate([jnp.asarray(row)[:, None] * inv_freq, jnp.asarray(col)[:, None] * inv_freq], axis=-1)
    cos, sin = jnp.cos(ang), jnp.sin(ang)
    pad = HEAD_PAD - MLA_QK
    c_tab = jnp.concatenate([jnp.ones((n, MLA_NOPE), F32), cos, cos, jnp.zeros((n, pad), F32)], axis=-1)
    s_tab = jnp.concatenate([jnp.zeros((n, MLA_NOPE), F32), -sin, sin, jnp.zeros((n, pad), F32)], axis=-1)
    return c_tab, s_tab


def _prep_weights(w_in, w_uq, w_ukv):
    D = w_in.shape[0]
    o1 = Q_LORA
    o2 = o1 + KV_LORA
    o3 = o2 + MLA_ROPE
    o4 = o3 + 3 * NA_W
    w_kr = w_in[:, o2:o3]
    z = lambda k: jnp.zeros((D, k), w_in.dtype)
    pad = HEAD_PAD - MLA_QK
    w_small = jnp.concatenate([w_in[:, :o2], z(MLA_NOPE), w_kr, z(pad)], axis=-1).astype(BF16)
    w_nqt = w_in[:, o3:o3 + NA_W].T.astype(BF16)
    w_na = w_in[:, o3 + NA_W:o3 + 2 * NA_W].astype(BF16)
    w_nvt = w_in[:, o3 + 2 * NA_W:o4].T.astype(BF16)
    w_g = w_in[:, o4:].astype(BF16)

    uq = w_uq.reshape(Q_LORA, MLA_HEADS, MLA_QK)
    zq = lambda k: jnp.zeros((Q_LORA, MLA_HEADS, k), w_uq.dtype)
    wqa = jnp.concatenate([uq, zq(pad)], axis=-1).reshape(Q_LORA, MLA_HEADS * HEAD_PAD).T.astype(BF16)

    ukv = w_ukv.reshape(KV_LORA, MLA_HEADS, MLA_NOPE + MLA_V)
    wk = jnp.concatenate([ukv[..., :MLA_NOPE], jnp.zeros((KV_LORA, MLA_HEADS, HEAD_PAD - MLA_NOPE), w_ukv.dtype)], axis=-1)
    wk = wk.reshape(KV_LORA, MLA_HEADS * HEAD_PAD).astype(BF16)
    wv = ukv[..., MLA_NOPE:].reshape(KV_LORA, MLA_HEADS * MLA_V).T.astype(BF16)
    return w_small, w_na, w_nqt, w_nvt, w_g, wqa, wk, wv


def _tile(n, pref):
    t = min(pref, n)
    while n % t:
        t //= 2
    return t


def kernel(x, c, ctx, c_ctx, w_mod, b_mod, w_in, q_norm_g, w_uq, kv_norm_g, w_ukv, na_rel_bias, w_proj_mla,
           w_proj_na, w_out, ln1_g, ln1_b, w_router, w_exp_gate, w_exp_up, w_exp_down, ln2_g, ln2_b):
    B, n, D = x.shape
    L = ctx.shape[1]
    rows = n // GRID_W
    assert n % (GRID_W * NA_QROWS) == 0 and rows >= NA_BAND
    assert w_mod.shape[0] == DEPTH
    cap = EC_CAPACITY * n // N_EXPERTS

    mod_rows = -(-(B + 1) // 8) * 8
    cc = jnp.concatenate([c, c_ctx[None], jnp.zeros((mod_rows - B - 1, D), F32)], axis=0)
    m = _modulation(cc, w_mod[0], b_mod[0])
    sh1, sc1, g1, sh2, sc2, g2 = [m[:B, k * D:(k + 1) * D].reshape(B, 1, D) for k in range(6)]
    csh1 = m[B:B + 1, :D].reshape(1, 1, D)
    csc1 = m[B:B + 1, D:2 * D].reshape(1, 1, D)

    w_small, w_na, w_nqt, w_nvt, w_g, wqa, wk, wv = _prep_weights(w_in[0], w_uq[0], w_ukv[0])
    qg = q_norm_g[0].reshape(1, Q_LORA)
    kvg = kv_norm_g[0].reshape(1, KV_LORA)
    c_tab, s_tab = _rope_tables(n)
    pad = HEAD_PAD - MLA_QK
    c_ctx_tab = jnp.concatenate([jnp.ones((L, MLA_QK), F32), jnp.zeros((L, pad), F32)], axis=-1)
    s_ctx_tab = jnp.zeros((L, HEAD_PAD), F32)

    q, k_lat, v_lat, nq, nk, nv, gates = _inproj(
        True, x, sc1, sh1, c_tab, s_tab, c_tab.T * (MLA_SCALE * LOG2E), s_tab.T * (MLA_SCALE * LOG2E),
        w_small, w_na, w_nqt, w_nvt, w_g, qg, kvg, wqa, wk, wv, _tile(n, 512))
    k_ctx, v_ctx, cnk, cnv = _inproj(
        False, ctx, csc1, csh1, c_ctx_tab, s_ctx_tab, c_ctx_tab.T, s_ctx_tab.T,
        w_small, w_na, w_nqt, w_nvt, w_g, qg, kvg, wqa, wk, wv, _tile(L, 256))

    y_mla = _mla_attention(q, k_lat, k_ctx, v_lat, v_ctx, _tile(n, 1024), _tile(n, 512))
    y_na = _na_attention(nq, nk, nv, cnk, cnv, _na_bias_tiles(na_rel_bias[0], rows))

    xn, u2, aff = _merge(
        y_mla, y_na, gates, x, g1, sc2, sh2,
        w_proj_mla[0].astype(BF16), w_proj_na[0].astype(BF16), w_out[0].astype(BF16),
        ln1_g[0].reshape(1, D), ln1_b[0].reshape(1, D), w_router[0], _tile(n, 2 * MERGE_SUB))

    idx4, g4 = _topk(jnp.swapaxes(aff, 1, 2), cap)
    idx = idx4.reshape(B, N_EXPERTS, 1, cap)
    xe = _gather(idx, u2, n)
    ye = _ffn(xe, g4, w_exp_gate[0], w_exp_up[0], w_exp_down[0])
    return _combine(xn, idx, ye, g2, ln2_g[0].reshape(1, D), ln2_b[0].reshape(1, D), _tile(n, 512))
```

```python
import functools
import math

import numpy as np
import jax
import jax.numpy as jnp
from jax import lax
from jax.experimental import pallas as pl
from jax.experimental.pallas import tpu as pltpu

GRID_W = 64
MLA_HEADS = 8
MLA_NOPE = 64
MLA_ROPE = 32
MLA_QK = MLA_NOPE + MLA_ROPE
MLA_V = 64
Q_LORA = 256
KV_LORA = 128
MLA_SCALE = MLA_QK ** -0.5
ROPE_THETA = 10000.0
NA_HEADS = 8
NA_DIM = 64
NA_W = NA_HEADS * NA_DIM
NA_WIN_H = 8
NA_WIN_W = 16
NA_SCALE = NA_DIM ** -0.5
N_EXPERTS = 16
EC_CAPACITY = 2
LN_EPS = 1e-5
RMS_EPS = 1e-6
DEPTH = 1
ALPHA = (2.0 * DEPTH) ** 0.25
LOG2E = math.log2(math.e)

LANES = 128
HEAD_PAD = LANES
VMEM_LIMIT = 56 * 1024 * 1024
NEG_BIG = -1e30

NA_QROWS = 4
NA_BAND = 12
ROW_GROUP = 8
NA_LOOKAHEAD = 2
MERGE_SUB = 256
TOPK_RADIX = 64
TOPK_TABLE_ROWS = 128
TOPK_SLOTS = 128
TOPK_REFINE = 30

BF16 = jnp.bfloat16
F32 = jnp.float32


def _cparams(sem):
    return pltpu.CompilerParams(dimension_semantics=sem, vmem_limit_bytes=VMEM_LIMIT)


def _ln(x):
    mu = jnp.mean(x, axis=-1, keepdims=True)
    xc = x - mu
    var = jnp.mean(xc * xc, axis=-1, keepdims=True)
    return xc * lax.rsqrt(var + LN_EPS)


def _dot(a, b):
    return jnp.dot(a, b, preferred_element_type=F32)


def _dot_nt(a, b):
    return lax.dot_general(a, b, (((1,), (1,)), ((), ())), preferred_element_type=F32)


def _store_rows_as_tiles(ref, val):
    m, d = val.shape
    s = d // LANES
    for j in range(s):
        ref[pl.ds(j, m, stride=s), :] = val[:, j * LANES:(j + 1) * LANES]


def _load_tiles_as_rows(ref, d):
    s = d // LANES
    m = ref.shape[0] // s
    return jnp.concatenate([ref[pl.ds(j, m, stride=s), :] for j in range(s)], axis=-1)


def _mod_kernel(c_ref, w_ref, b_ref, o_ref):
    c = c_ref[...]
    s = c * jax.nn.sigmoid(c)
    o_ref[...] = jnp.dot(s, w_ref[...], preferred_element_type=F32,
                         precision=lax.Precision.HIGHEST) + b_ref[...]


def _modulation(cc, w_mod, b_mod):
    rows, d = cc.shape
    n_out = w_mod.shape[1]
    tn = 1024
    return pl.pallas_call(
        _mod_kernel,
        out_shape=jax.ShapeDtypeStruct((rows, n_out), F32),
        grid=(n_out // tn,),
        in_specs=[pl.BlockSpec((rows, d), lambda j: (0, 0)),
                  pl.BlockSpec((d, tn), lambda j: (0, j)),
                  pl.BlockSpec((1, tn), lambda j: (0, j))],
        out_specs=pl.BlockSpec((rows, tn), lambda j: (0, j)),
        compiler_params=_cparams(("arbitrary",)),
        name="modulation",
    )(cc, w_mod, b_mod.reshape(1, n_out))


def _swap_rope_halves(x, axis):
    half = MLA_ROPE // 2
    pos = lax.broadcasted_iota(jnp.int32, x.shape, axis)
    return jnp.where(pos < MLA_NOPE + half, pltpu.roll(x, HEAD_PAD - half, axis), pltpu.roll(x, half, axis))


def _store_value_tiles(ref, vt, dim):
    tm = vt.shape[1]
    tail = jnp.where(lax.broadcasted_iota(jnp.int32, (HEAD_PAD - dim, tm), 0) == 0, 1.0, 0.0).astype(ref.dtype)
    for h in range(vt.shape[0] // dim):
        ref[h * HEAD_PAD:h * HEAD_PAD + dim, :] = vt[h * dim:(h + 1) * dim, :].astype(ref.dtype)
        ref[h * HEAD_PAD + dim:(h + 1) * HEAD_PAD, :] = tail


def _inproj_kernel(latent, x_ref, sc_ref, sh_ref, ck_ref, sk_ref, cq_ref, sq_ref,
                   w_small_ref, w_na_ref, w_nqt_ref, w_nvt_ref, w_g_ref, qg_ref, kvg_ref, wqa_ref, wk_ref, wv_ref,
                   *out_refs):
    if latent:
        q_ref, k_ref, v_ref, nq_ref, nk_ref, nv_ref, g_ref = out_refs
    else:
        k_ref, v_ref, nk_ref, nv_ref = out_refs
    x = x_ref[0]
    u = (_ln(x) * (1.0 + sc_ref[0]) + sh_ref[0]).astype(BF16)

    small = _dot(u, w_small_ref[...])
    q_c = small[:, :Q_LORA]
    kv_c = small[:, Q_LORA:Q_LORA + KV_LORA]
    k_r = small[:, Q_LORA + KV_LORA:]

    kvn = (kv_c * lax.rsqrt(jnp.mean(kv_c * kv_c, axis=-1, keepdims=True) + RMS_EPS) * kvg_ref[...]).astype(BF16)
    kk = _dot(kvn, wk_ref[...])
    kr = k_r * ck_ref[...] + _swap_rope_halves(k_r, 1) * sk_ref[...]
    for h in range(MLA_HEADS):
        k_ref[0, h] = (kk[:, h * HEAD_PAD:(h + 1) * HEAD_PAD] + kr).astype(BF16)
    _store_value_tiles(v_ref.at[0], _dot_nt(wv_ref[...], kvn), MLA_V)

    nk_ref[0] = _dot(u, w_na_ref[...]).astype(BF16)
    _store_value_tiles(nv_ref.at[0], _dot_nt(w_nvt_ref[...], u), NA_DIM)
    if latent:
        qn = (q_c * lax.rsqrt(jnp.mean(q_c * q_c, axis=-1, keepdims=True) + RMS_EPS) * qg_ref[...]).astype(BF16)
        qa = _dot_nt(wqa_ref[...], qn)
        cq = cq_ref[...]
        sq = sq_ref[...]
        for h in range(MLA_HEADS):
            qh = qa[h * HEAD_PAD:(h + 1) * HEAD_PAD, :]
            q_ref[0, h] = (qh * cq + _swap_rope_halves(qh, 0) * sq).astype(BF16)
        nq_ref[0] = (_dot_nt(w_nqt_ref[...], u) * (NA_SCALE * LOG2E)).astype(BF16)
        g_ref[0] = jax.nn.sigmoid(_dot(u, w_g_ref[...])).astype(BF16)


def _inproj(latent, x, sc, sh, ck, sk, cq, sq, w_small, w_na, w_nqt, w_nvt, w_g, qg, kvg, wqa, wk, wv, tm):
    B, n, D = x.shape
    per_batch = sc.shape[0] > 1
    mod_map = (lambda b, i: (b, 0, 0)) if per_batch else (lambda b, i: (0, 0, 0))
    full = lambda a: pl.BlockSpec(a.shape, lambda b, i: (0,) * a.ndim)
    tab = lambda a: pl.BlockSpec((tm, a.shape[1]), lambda b, i: (i, 0))
    tab_t = lambda a: pl.BlockSpec((a.shape[0], tm), lambda b, i: (0, i))
    in_specs = [pl.BlockSpec((1, tm, D), lambda b, i: (b, i, 0)),
                pl.BlockSpec((1, 1, D), mod_map), pl.BlockSpec((1, 1, D), mod_map),
                tab(ck), tab(sk), tab_t(cq), tab_t(sq),
                full(w_small), full(w_na), full(w_nqt), full(w_nvt), full(w_g), full(qg), full(kvg),
                full(wqa), full(wk), full(wv)]
    hk = jax.ShapeDtypeStruct((B, MLA_HEADS, n, HEAD_PAD), BF16)
    hk_spec = pl.BlockSpec((1, MLA_HEADS, tm, HEAD_PAD), lambda b, i: (b, 0, i, 0))
    tok = lambda w: jax.ShapeDtypeStruct((B, n, w), BF16)
    tok_spec = lambda w: pl.BlockSpec((1, tm, w), lambda b, i: (b, i, 0))
    vt = jax.ShapeDtypeStruct((B, MLA_HEADS * HEAD_PAD, n), BF16)
    vt_spec = pl.BlockSpec((1, MLA_HEADS * HEAD_PAD, tm), lambda b, i: (b, 0, i))
    nqt = jax.ShapeDtypeStruct((B, NA_W, n), BF16)
    nqt_spec = pl.BlockSpec((1, NA_W, tm), lambda b, i: (b, 0, i))
    if latent:
        qt = jax.ShapeDtypeStruct((B, MLA_HEADS, HEAD_PAD, n), BF16)
        qt_spec = pl.BlockSpec((1, MLA_HEADS, HEAD_PAD, tm), lambda b, i: (b, 0, 0, i))
        out_shape = [qt, hk, vt, nqt, tok(NA_W), vt, tok(2 * D)]
        out_specs = [qt_spec, hk_spec, vt_spec, nqt_spec, tok_spec(NA_W), vt_spec, tok_spec(2 * D)]
    else:
        out_shape = [hk, vt, tok(NA_W), vt]
        out_specs = [hk_spec, vt_spec, tok_spec(NA_W), vt_spec]
    return pl.pallas_call(
        functools.partial(_inproj_kernel, latent),
        out_shape=out_shape,
        grid=(B, n // tm),
        in_specs=in_specs,
        out_specs=out_specs,
        compiler_params=_cparams(("parallel", "parallel")),
        name="inproj_latent" if latent else "inproj_ctx",
    )(x, sc, sh, ck, sk, cq, sq, w_small, w_na, w_nqt, w_nvt, w_g, qg, kvg, wqa, wk, wv)


def _mla_kernel(tk, q_ref, kc_ref, kl_ref, vc_ref, vl_ref, o_ref, m_ref, acc_ref, sa_ref, sb_ref, sc_ref):
    heads = q_ref.shape[1]
    n = kl_ref.shape[2]
    tq = q_ref.shape[3]
    nt = n // tk

    def scores(h, k):
        return _dot(k, q_ref[0, h])

    def absorb(hh, s, v_t):
        m_old = m_ref[hh]
        m_new = jnp.maximum(m_old, jnp.max(s, axis=0, keepdims=True))
        a = jnp.exp2(m_old - m_new)
        p = jnp.exp2(s - m_new).astype(BF16)
        acc_ref[hh] = a * acc_ref[hh] + _dot(v_t, p)
        m_ref[hh] = m_new

    def ctx_scores(h0):
        for hh in range(2):
            sc_ref[hh] = scores(h0 + hh, kc_ref[0, h0 + hh])

    def pair_pass(pair, carry):
        h0 = 2 * pair

        def head_rows(hh):
            return pl.ds(pl.multiple_of((h0 + hh) * HEAD_PAD, HEAD_PAD), HEAD_PAD)

        def k_tile(hh, j):
            return kl_ref[0, h0 + hh, pl.ds(pl.multiple_of(j * tk, tk), tk), :]

        def v_tile(hh, j):
            return vl_ref[0, head_rows(hh), pl.ds(pl.multiple_of(j * tk, tk), tk)]

        def advance(j, cur_ref, next_ref):
            for hh in range(2):
                next_ref[hh] = scores(h0 + hh, k_tile(hh, j + 1))
                absorb(hh, cur_ref[hh], v_tile(hh, j))

        m_ref[...] = jnp.full(m_ref.shape, NEG_BIG, F32)
        acc_ref[...] = jnp.zeros(acc_ref.shape, F32)
        for hh in range(2):
            sa_ref[hh] = scores(h0 + hh, k_tile(hh, 0))
            absorb(hh, sc_ref[hh], vc_ref[0, head_rows(hh), :])

        def body(jj, c):
            advance(2 * jj, sa_ref, sb_ref)
            advance(2 * jj + 1, sb_ref, sa_ref)
            return c

        lax.fori_loop(0, (nt - 1) // 2, body, 0)
        last_ref = sa_ref
        if (nt - 1) % 2:
            advance(nt - 2, sa_ref, sb_ref)
            last_ref = sb_ref
        ctx_scores(jnp.minimum(h0 + 2, heads - 2))
        outs = []
        for hh in range(2):
            absorb(hh, last_ref[hh], v_tile(hh, nt - 1))
            acc = acc_ref[hh]
            outs.append(acc[:MLA_V] / acc[MLA_V:MLA_V + 1])
        lanes = pl.ds(pl.multiple_of(pair * LANES, LANES), LANES)
        o_ref[0, :, lanes] = jnp.concatenate(outs, axis=0).T.astype(o_ref.dtype)
        return carry

    ctx_scores(0)
    lax.fori_loop(0, heads // 2, pair_pass, 0)


def _mla_attention(q, k_lat, k_ctx, v_lat, v_ctx, tq, tk):
    B, H, _, n = q.shape
    L = k_ctx.shape[2]
    W = H * MLA_V
    return pl.pallas_call(
        functools.partial(_mla_kernel, tk),
        out_shape=jax.ShapeDtypeStruct((B, n, W), BF16),
        grid=(B, n // tq),
        in_specs=[pl.BlockSpec((1, H, HEAD_PAD, tq), lambda b, i: (b, 0, 0, i)),
                  pl.BlockSpec((1, H, L, HEAD_PAD), lambda b, i: (b, 0, 0, 0)),
                  pl.BlockSpec((1, H, n, HEAD_PAD), lambda b, i: (b, 0, 0, 0)),
                  pl.BlockSpec((1, H * HEAD_PAD, L), lambda b, i: (b, 0, 0)),
                  pl.BlockSpec((1, H * HEAD_PAD, n), lambda b, i: (b, 0, 0))],
        out_specs=pl.BlockSpec((1, tq, W), lambda b, i: (b, i, 0)),
        scratch_shapes=[pltpu.VMEM((2, 1, tq), F32),
                        pltpu.VMEM((2, HEAD_PAD, tq), F32),
                        pltpu.VMEM((2, tk, tq), F32), pltpu.VMEM((2, tk, tq), F32),
                        pltpu.VMEM((2, L, tq), F32)],
        compiler_params=_cparams(("parallel", "parallel")),
        name="mla_attention",
    )(q, k_ctx, k_lat, v_ctx, v_lat)


def _na_block_tables(rows):
    nblk = rows // NA_QROWS
    wh = min(NA_WIN_H, rows)
    band0 = np.clip(np.arange(nblk) * NA_QROWS - wh // 2, 0, rows - NA_BAND)
    sigs, types = [], []
    for i in range(nblk):
        r = i * NA_QROWS + np.arange(NA_QROWS)
        r0 = np.clip(r - wh // 2, 0, rows - wh)
        sig = (tuple(r0 - band0[i]), tuple(r - band0[i]))
        if sig not in sigs:
            sigs.append(sig)
        types.append(sigs.index(sig))
    return band0.astype(np.int32), np.asarray(types, np.int32), sigs, wh


def _bias_tile_kernel(dr, ok, tbl_ref, o_ref):
    neg = jnp.full((GRID_W, GRID_W), NEG_BIG, F32)
    for t in range(o_ref.shape[0]):
        for k in range(NA_BAND):
            pieces = [tbl_ref[0, int(dr[t, q, k])] if ok[t, q, k] else neg for q in range(NA_QROWS)]
            o_ref[t, 0, k * GRID_W:(k + 1) * GRID_W, :] = jnp.concatenate(pieces, axis=1)


def _na_bias_tiles(rel_bias, rows):
    _, _, sigs, wh = _na_block_tables(rows)
    ww = NA_WIN_W
    n_dr, n_dc = 2 * NA_WIN_H - 1, 2 * NA_WIN_W - 1
    col = np.arange(GRID_W)
    c0 = np.clip(col - ww // 2, 0, GRID_W - ww)
    col_ok = (col[:, None] >= c0[None, :]) & (col[:, None] < c0[None, :] + ww)
    dc = col[:, None] - col[None, :] + (NA_WIN_W - 1)
    pick_c = ((dc[None] == np.arange(n_dc)[:, None, None]) & col_ok[None]).astype(np.float32)
    by_col = jnp.einsum('hdj,jyx->hdyx', rel_bias.astype(F32), jnp.asarray(pick_c), precision=lax.Precision.HIGHEST)
    by_col = jnp.where(jnp.asarray(col_ok), by_col * LOG2E, NEG_BIG)
    kr = np.arange(NA_BAND)
    ok = np.stack([(kr[None, :] >= np.asarray(r0)[:, None]) & (kr[None, :] < np.asarray(r0)[:, None] + wh)
                   for r0, _ in sigs])
    dr = np.stack([kr[None, :] - np.asarray(r)[:, None] + (NA_WIN_H - 1) for _, r in sigs])
    assert ((dr >= 0) & (dr < n_dr))[ok].all()
    T = len(sigs)
    kn, qn = NA_BAND * GRID_W, NA_QROWS * GRID_W
    return pl.pallas_call(
        functools.partial(_bias_tile_kernel, dr, ok),
        out_shape=jax.ShapeDtypeStruct((T, NA_HEADS, kn, qn), F32),
        grid=(NA_HEADS,),
        in_specs=[pl.BlockSpec((1, n_dr, GRID_W, GRID_W), lambda h: (h, 0, 0, 0))],
        out_specs=pl.BlockSpec((T, 1, kn, qn), lambda h: (0, h, 0, 0)),
        compiler_params=_cparams(("parallel",)),
        name="na_bias_tiles",
    )(by_col)


def _na_kernel(nblk, band_ref, type_ref, q_ref, k_ref, vt_ref, kc_ref, vct_ref, *rest):
    bias_refs, o_ref = rest[:nblk], rest[nblk]
    i = pl.program_id(1)
    nq = NA_QROWS * GRID_W
    nk = NA_BAND * GRID_W
    starts = [pl.multiple_of(band_ref[i * nblk + b] * GRID_W, NA_QROWS * GRID_W) for b in range(nblk)]
    row = lax.broadcasted_iota(jnp.int32, (LANES, nq), 0)

    def scores(c):
        blk, h = divmod(c, NA_HEADS)
        cs = slice((h // 2) * LANES, (h // 2 + 1) * LANES)
        qp = q_ref[0, cs, blk * nq:(blk + 1) * nq]
        own = (row < NA_DIM) if h % 2 == 0 else (row >= NA_DIM)
        qh = jnp.where(own, qp, jnp.zeros_like(qp))
        s_loc = _dot(k_ref[0, pl.ds(starts[blk], nk), cs], qh) + bias_refs[blk][0, h]
        s_ctx = _dot(kc_ref[0, :, cs], qh)
        return s_loc, s_ctx

    def attend(c, s):
        s_loc, s_ctx = s
        blk, h = divmod(c, NA_HEADS)
        start = starts[blk]
        vs = slice(h * HEAD_PAD, (h + 1) * HEAD_PAD)
        m = jnp.maximum(jnp.max(s_loc, axis=0, keepdims=True), jnp.max(s_ctx, axis=0, keepdims=True))
        p_loc = jnp.exp2((s_loc - m).astype(BF16))
        p_ctx = jnp.exp2((s_ctx - m).astype(BF16))
        o = _dot(vt_ref[0, vs, pl.ds(start, nk)], p_loc) + _dot(vct_ref[0, vs, :], p_ctx)
        return o[:NA_DIM] / o[NA_DIM:NA_DIM + 1]

    chain = nblk * NA_HEADS
    ahead = [scores(c) for c in range(min(NA_LOOKAHEAD, chain))]
    outs = []
    for c in range(chain):
        if c + NA_LOOKAHEAD < chain:
            ahead.append(scores(c + NA_LOOKAHEAD))
        outs.append(attend(c, ahead.pop(0)))
        if c % 2:
            blk, h = divmod(c, NA_HEADS)
            pair_t = jnp.concatenate([outs[c - 1], outs[c]], axis=0)
            o_ref[0, blk * nq:(blk + 1) * nq, (h // 2) * LANES:(h // 2 + 1) * LANES] = pair_t.T.astype(o_ref.dtype)


def _na_attention(nqt, nk, nvt, cnk, cnvt, bias_tiles):
    B, n, W = nk.shape
    L = cnk.shape[1]
    rows = n // GRID_W
    band0, types, _, _ = _na_block_tables(rows)
    qn = NA_QROWS * GRID_W
    kn = NA_BAND * GRID_W
    nblocks = rows // NA_QROWS
    nblk = 2 if nblocks % 2 == 0 else 1

    def bias_spec(k):
        return pl.BlockSpec((1, NA_HEADS, kn, qn), lambda b, i, bd, ty: (ty[i * nblk + k], 0, 0, 0))

    grid_spec = pltpu.PrefetchScalarGridSpec(
        num_scalar_prefetch=2,
        grid=(B, nblocks // nblk),
        in_specs=[pl.BlockSpec((1, W, nblk * qn), lambda b, i, bd, ty: (b, 0, i)),
                  pl.BlockSpec((1, n, W), lambda b, i, bd, ty: (b, 0, 0)),
                  pl.BlockSpec((1, NA_HEADS * HEAD_PAD, n), lambda b, i, bd, ty: (b, 0, 0)),
                  pl.BlockSpec((1, L, W), lambda b, i, bd, ty: (b, 0, 0)),
                  pl.BlockSpec((1, NA_HEADS * HEAD_PAD, L), lambda b, i, bd, ty: (b, 0, 0))]
                 + [bias_spec(k) for k in range(nblk)],
        out_specs=pl.BlockSpec((1, nblk * qn, W), lambda b, i, bd, ty: (b, i, 0)),
    )
    return pl.pallas_call(
        functools.partial(_na_kernel, nblk),
        out_shape=jax.ShapeDtypeStruct((B, n, W), BF16),
        grid_spec=grid_spec,
        compiler_params=_cparams(("parallel", "arbitrary")),
        name="na_attention",
    )(jnp.asarray(band0), jnp.asarray(types), nqt, nk, nvt, cnk, cnvt, *([bias_tiles] * nblk))


def _split2(a):
    hi = a.astype(BF16)
    return hi, (a - hi.astype(F32)).astype(BF16)


def _merge_kernel(ym_ref, yn_ref, g_ref, x_ref, g1_ref, sc2_ref, sh2_ref, wpm_ref, wpn_ref, wo_ref,
                  l1g_ref, l1b_ref, wr_ref, xn_ref, up_ref, aff_ref):
    tm, D = x_ref.shape[1], x_ref.shape[2]
    s = D // LANES
    sub = min(MERGE_SUB, tm)
    n_exp = aff_ref.shape[1]
    wh, wl = _split2(wr_ref[...])

    def mix_of(r):
        g = g_ref[0, r, :]
        a = (g[:, :D].astype(F32) * _dot(ym_ref[0, r, :], wpm_ref[...])
             + g[:, D:].astype(F32) * _dot(yn_ref[0, r, :], wpn_ref[...]))
        return _dot(a.astype(BF16), wo_ref[...])

    def finish(r, mix):
        xn = _ln(ALPHA * x_ref[0, r, :] + g1_ref[0] * mix) * l1g_ref[...] + l1b_ref[...]
        xn_ref[0, r, :] = xn
        u2 = _ln(xn) * (1.0 + sc2_ref[0]) + sh2_ref[0]
        _store_rows_as_tiles(up_ref.at[0, r.start * s:r.stop * s], u2)
        uh, ul = _split2(u2)
        logits = _dot(uh, wh) + (_dot(uh, wl) + _dot(ul, wh))
        logits = logits.T[:n_exp]
        e = jnp.exp(logits - jnp.max(logits, axis=0, keepdims=True))
        aff_ref[0, :, r] = e / jnp.sum(e, axis=0, keepdims=True)

    rows = [slice(i, i + sub) for i in range(0, tm, sub)]
    mix = mix_of(rows[0])
    for i, r in enumerate(rows):
        mix_next = mix_of(rows[i + 1]) if i + 1 < len(rows) else None
        finish(r, mix)
        mix = mix_next


def _merge(y_mla, y_na, gates, x, g1, sc2, sh2, wpm, wpn, wo, l1g, l1b, wr, tm):
    B, n, D = x.shape
    E = wr.shape[1]
    wr = jnp.concatenate([wr, jnp.zeros((D, LANES - E), wr.dtype)], axis=1)
    full = lambda a: pl.BlockSpec(a.shape, lambda b, i: (0,) * a.ndim)
    tok = lambda w: pl.BlockSpec((1, tm, w), lambda b, i: (b, i, 0))
    mod = pl.BlockSpec((1, 1, D), lambda b, i: (b, 0, 0))
    return pl.pallas_call(
        _merge_kernel,
        out_shape=[jax.ShapeDtypeStruct((B, n, D), F32),
                   jax.ShapeDtypeStruct((B, n * (D // LANES), LANES), F32),
                   jax.ShapeDtypeStruct((B, E, n), F32)],
        grid=(B, n // tm),
        in_specs=[tok(y_mla.shape[2]), tok(y_na.shape[2]), tok(2 * D), tok(D), mod, mod, mod,
                  full(wpm), full(wpn), full(wo), full(l1g), full(l1b), full(wr)],
        out_specs=[tok(D), pl.BlockSpec((1, tm * (D // LANES), LANES), lambda b, i: (b, i, 0)),
                   pl.BlockSpec((1, E, tm), lambda b, i: (b, 0, i))],
        compiler_params=_cparams(("parallel", "parallel")),
        name="merge_ln_router",
    )(y_mla, y_na, gates, x, g1, sc2, sh2, wpm, wpn, wo, l1g, l1b, wr)


def _topk_kernel(cap, aff_ref, tri_ref, idx_ref, g_ref, pos_ref):
    aff = aff_ref[0]
    E, n = aff.shape

    def count(mask):
        return jnp.sum(mask.astype(F32), axis=-1, keepdims=True)

    def enough(v):
        return count(aff >= v) >= cap

    def search(t, thr):
        b1 = jnp.int32(1) << (30 - 2 * t)
        b0 = b1 >> 1
        c3, c2, c1 = thr | b1 | b0, thr | b1, thr | b0
        ok3, ok2, ok1 = [enough(pltpu.bitcast(c, F32)) for c in (c3, c2, c1)]
        return jnp.where(ok3, c3, jnp.where(ok2, c2, jnp.where(ok1, c1, thr)))

    thr = lax.fori_loop(0, 16, search, jnp.zeros((E, 1), jnp.int32))
    lo = pltpu.bitcast(thr, F32)
    hi = pltpu.bitcast(thr + 1, F32)

    def refine(t, lh):
        lo, hi = lh
        w = hi - lo
        q1, q2, q3 = lo + w * 0.25, lo + w * 0.5, lo + w * 0.75
        ok1, ok2, ok3 = enough(q1), enough(q2), enough(q3)
        new_lo = jnp.where(ok3, q3, jnp.where(ok2, q2, jnp.where(ok1, q1, lo)))
        new_hi = jnp.where(ok3, hi, jnp.where(ok2, q3, jnp.where(ok1, q2, q1)))
        return new_lo, new_hi

    lo, hi = lax.fori_loop(0, TOPK_REFINE // 2, refine, (lo, hi))
    gt = aff >= hi
    eq = (aff >= lo) & ~gt
    need = cap - count(gt)

    tri = tri_ref[...]

    def prefix(mask):
        mf = mask.astype(F32).astype(BF16)
        parts = []
        off = jnp.zeros((E, 1), F32)
        for c in range(n // LANES):
            blk = mf[:, c * LANES:(c + 1) * LANES]
            parts.append(_dot(blk, tri) + off)
            off = off + jnp.sum(blk.astype(F32), axis=-1, keepdims=True)
        return jnp.concatenate(parts, axis=-1)

    sel = gt | (eq & (prefix(eq) < need))
    pos_ref[...] = jnp.where(sel, prefix(sel), -1.0)

    tok = lax.broadcasted_iota(jnp.int32, (1, n), 1)
    digits = [(tok // TOPK_RADIX).astype(F32), (tok % TOPK_RADIX).astype(F32)]
    chunks = cap // TOPK_SLOTS

    def per_expert(e, carry):
        a = aff_ref[0, pl.ds(e, 1), :]
        a1 = a.astype(BF16).astype(F32)
        a2 = (a - a1).astype(BF16).astype(F32)
        a3 = (a - a1) - a2
        rows = digits + [a1, a2, a3]
        table = jnp.concatenate(rows + [jnp.zeros((TOPK_TABLE_ROWS - len(rows), n), F32)], axis=0).astype(BF16)
        pos_e = pos_ref[pl.ds(e, 1), :]

        hits = []
        for ch in range(chunks):
            slot = (lax.broadcasted_iota(jnp.int32, (TOPK_SLOTS, n), 0) + ch * TOPK_SLOTS).astype(F32)
            hits.append(jnp.where(slot == pos_e, 1.0, 0.0).astype(BF16))
        for ch in range(chunks):
            r = _dot_nt(hits[ch], table)
            g_ref[0, e, ch * TOPK_SLOTS:(ch + 1) * TOPK_SLOTS, :] = (r[:, 2:3] + r[:, 3:4]) + r[:, 4:5]
            rt = r.T
            idx = rt[0:1] * TOPK_RADIX + rt[1:2]
            idx_ref[0, e, :, ch * TOPK_SLOTS:(ch + 1) * TOPK_SLOTS] = idx.astype(jnp.int32)
        return carry

    lax.fori_loop(0, E, per_expert, 0)


def _topk(aff_t, cap):
    B, E, n = aff_t.shape
    tri = jnp.asarray(np.triu(np.ones((LANES, LANES), np.float32), k=1), BF16)
    return pl.pallas_call(
        functools.partial(_topk_kernel, cap),
        out_shape=[jax.ShapeDtypeStruct((B, E, 1, cap), jnp.int32),
                   jax.ShapeDtypeStruct((B, E, cap, 1), F32)],
        grid=(B,),
        in_specs=[pl.BlockSpec((1, E, n), lambda b: (b, 0, 0)),
                  pl.BlockSpec((LANES, LANES), lambda b: (0, 0))],
        out_specs=[pl.BlockSpec((1, E, 1, cap), lambda b: (b, 0, 0, 0)),
                   pl.BlockSpec((1, E, cap, 1), lambda b: (b, 0, 0, 0))],
        scratch_shapes=[pltpu.VMEM((E, n), F32)],
        compiler_params=_cparams(("parallel",)),
        name="expert_topk",
    )(aff_t, tri)


def _gather_kernel(s, idx_ref, u_ref, o_ref):
    cap = o_ref.shape[2] // s

    def body(i, carry):
        c0 = pl.multiple_of(i * ROW_GROUP, ROW_GROUP)
        for k in range(ROW_GROUP):
            src = pl.multiple_of(idx_ref[0, 0, 0, c0 + k] * s, s)
            o_ref[0, 0, pl.ds((c0 + k) * s, s), :] = u_ref[0, pl.ds(src, s), :]
        return carry

    lax.fori_loop(0, cap // ROW_GROUP, body, 0)


def _gather(idx, u2, n):
    B, E, _, cap = idx.shape
    _, ns, W = u2.shape
    s = ns // n
    return pl.pallas_call(
        functools.partial(_gather_kernel, s),
        out_shape=jax.ShapeDtypeStruct((B, E, cap * s, W), u2.dtype),
        grid=(B, E),
        in_specs=[pl.BlockSpec((1, 1, 1, cap), lambda b, e: (b, e, 0, 0), memory_space=pltpu.SMEM),
                  pl.BlockSpec((1, ns, W), lambda b, e: (b, 0, 0))],
        out_specs=pl.BlockSpec((1, 1, cap * s, W), lambda b, e: (b, e, 0, 0)),
        compiler_params=_cparams(("parallel", "arbitrary")),
        name="expert_gather",
    )(idx, u2)


def _ffn_kernel(xe_ref, g_ref, wg_ref, wu_ref, wd_ref, o_ref, wgb_ref, wub_ref, wdb_ref):
    @pl.when(pl.program_id(1) == 0)
    def _():
        wgb_ref[...] = wg_ref[0].astype(BF16)
        wub_ref[...] = wu_ref[0].astype(BF16)
        wdb_ref[...] = wd_ref[0].astype(BF16)

    xe = _load_tiles_as_rows(xe_ref.at[0, 0], wgb_ref.shape[0]).astype(BF16)
    gate = _dot(xe, wgb_ref[...])
    up = _dot(xe, wub_ref[...])
    h = (gate * jax.nn.sigmoid(gate) * up).astype(BF16)
    _store_rows_as_tiles(o_ref.at[0, 0], _dot(h, wdb_ref[...]) * g_ref[0, 0])


def _ffn(xe, g, w_gate, w_up, w_down):
    B, E, caps, W = xe.shape
    _, D, F = w_gate.shape
    cap = g.shape[2]
    return pl.pallas_call(
        _ffn_kernel,
        out_shape=jax.ShapeDtypeStruct((B, E, caps, W), F32),
        grid=(E, B),
        in_specs=[pl.BlockSpec((1, 1, caps, W), lambda e, b: (b, e, 0, 0)),
                  pl.BlockSpec((1, 1, cap, 1), lambda e, b: (b, e, 0, 0)),
                  pl.BlockSpec((1, D, F), lambda e, b: (e, 0, 0)),
                  pl.BlockSpec((1, D, F), lambda e, b: (e, 0, 0)),
                  pl.BlockSpec((1, F, D), lambda e, b: (e, 0, 0))],
        out_specs=pl.BlockSpec((1, 1, caps, W), lambda e, b: (b, e, 0, 0)),
        scratch_shapes=[pltpu.VMEM((D, F), BF16), pltpu.VMEM((D, F), BF16), pltpu.VMEM((F, D), BF16)],
        compiler_params=_cparams(("arbitrary", "arbitrary")),
        name="expert_ffn",
    )(xe, g, w_gate, w_up, w_down)


def _combine_kernel(s, n_exp, x_ref, idx_ref, y_ref, g2_ref, lg_ref, lb_ref, o_ref, acc_ref):
    e = pl.program_id(1)
    cap = y_ref.shape[2] // s
    tm, D = x_ref.shape[1], x_ref.shape[2]

    def tile(r):
        return pl.ds(pl.multiple_of(r * s, s), s)

    @pl.when(e == 0)
    def _():
        acc_ref[...] = jnp.zeros_like(acc_ref)

    @pl.when(e < n_exp)
    def _():
        def body(i, carry):
            c0 = pl.multiple_of(i * ROW_GROUP, ROW_GROUP)
            rows = [idx_ref[0, 0, 0, c0 + k] for k in range(ROW_GROUP)]
            acc = [acc_ref[tile(r), :] for r in rows]
            for k in range(ROW_GROUP):
                acc_ref[tile(rows[k]), :] = acc[k] + y_ref[0, 0, tile(c0 + k), :]
            return carry

        lax.fori_loop(0, cap // ROW_GROUP, body, 0)

    @pl.when(e >= n_exp)
    def _():
        first = pl.multiple_of((e - n_exp) * (tm * s), tm * s)
        moe = _load_tiles_as_rows(acc_ref.at[pl.ds(first, tm * s)], D)
        o_ref[0] = _ln(ALPHA * x_ref[0] + g2_ref[0] * moe) * lg_ref[...] + lb_ref[...]


def _combine(xn, idx, ye, g2, lg, lb, tm):
    B, n, D = xn.shape
    _, E, caps, W = ye.shape
    cap = idx.shape[3]
    s = caps // cap
    blocks = n // tm
    tok = pl.BlockSpec((1, tm, D), lambda b, e: (b, jnp.maximum(e - E, 0), 0))
    vec = pl.BlockSpec((1, D), lambda b, e: (0, 0))
    last = lambda b, e: (b, jnp.minimum(e, E - 1), 0, 0)
    return pl.pallas_call(
        functools.partial(_combine_kernel, s, E),
        out_shape=jax.ShapeDtypeStruct((B, n, D), F32),
        grid=(B, E + blocks),
        in_specs=[tok,
                  pl.BlockSpec((1, 1, 1, cap), last, memory_space=pltpu.SMEM),
                  pl.BlockSpec((1, 1, caps, W), last),
                  pl.BlockSpec((1, 1, D), lambda b, e: (b, 0, 0)), vec, vec],
        out_specs=tok,
        scratch_shapes=[pltpu.VMEM((n * s, W), F32)],
        compiler_params=_cparams(("parallel", "arbitrary")),
        name="expert_combine_ln",
    )(xn, idx, ye, g2, lg, lb)


def _rope_tables(n):
    t = np.arange(n)
    row = (t // GRID_W).astype(np.float32)
    col = (t % GRID_W).astype(np.float32)
    per_axis = MLA_ROPE // 2
    inv_freq = jnp.asarray(ROPE_THETA, F32) ** (-jnp.arange(0, per_axis, 2, dtype=F32) / per_axis)
    ang = jnp.concatenate([jnp.asarray(row)[:, None] * inv_freq, jnp.asarray(col)[:, None] * inv_freq], axis=-1)
    cos, sin = jnp.cos(ang), jnp.sin(ang)
    pad = HEAD_PAD - MLA_QK
    c_tab = jnp.concatenate([jnp.ones((n, MLA_NOPE), F32), cos, cos, jnp.zeros((n, pad), F32)], axis=-1)
    s_tab = jnp.concatenate([jnp.zeros((n, MLA_NOPE), F32), -sin, sin, jnp.zeros((n, pad), F32)], axis=-1)
    return c_tab, s_tab


def _prep_weights(w_in, w_uq, w_ukv):
    D = w_in.shape[0]
    o1 = Q_LORA
    o2 = o1 + KV_LORA
    o3 = o2 + MLA_ROPE
    o4 = o3 + 3 * NA_W
    w_kr = w_in[:, o2:o3]
    z = lambda k: jnp.zeros((D, k), w_in.dtype)
    pad = HEAD_PAD - MLA_QK
    w_small = jnp.concatenate([w_in[:, :o2], z(MLA_NOPE), w_kr, z(pad)], axis=-1).astype(BF16)
    w_nqt = w_in[:, o3:o3 + NA_W].T.astype(BF16)
    w_na = w_in[:, o3 + NA_W:o3 + 2 * NA_W].astype(BF16)
    w_nvt = w_in[:, o3 + 2 * NA_W:o4].T.astype(BF16)
    w_g = w_in[:, o4:].astype(BF16)

    uq = w_uq.reshape(Q_LORA, MLA_HEADS, MLA_QK)
    zq = lambda k: jnp.zeros((Q_LORA, MLA_HEADS, k), w_uq.dtype)
    wqa = jnp.concatenate([uq, zq(pad)], axis=-1).reshape(Q_LORA, MLA_HEADS * HEAD_PAD).T.astype(BF16)

    ukv = w_ukv.reshape(KV_LORA, MLA_HEADS, MLA_NOPE + MLA_V)
    wk = jnp.concatenate([ukv[..., :MLA_NOPE], jnp.zeros((KV_LORA, MLA_HEADS, HEAD_PAD - MLA_NOPE), w_ukv.dtype)], axis=-1)
    wk = wk.reshape(KV_LORA, MLA_HEADS * HEAD_PAD).astype(BF16)
    wv = ukv[..., MLA_NOPE:].reshape(KV_LORA, MLA_HEADS * MLA_V).T.astype(BF16)
    return w_small, w_na, w_nqt, w_nvt, w_g, wqa, wk, wv


def _tile(n, pref):
    t = min(pref, n)
    while n % t:
        t //= 2
    return t


def kernel(x, c, ctx, c_ctx, w_mod, b_mod, w_in, q_norm_g, w_uq, kv_norm_g, w_ukv, na_rel_bias, w_proj_mla,
           w_proj_na, w_out, ln1_g, ln1_b, w_router, w_exp_gate, w_exp_up, w_exp_down, ln2_g, ln2_b):
    B, n, D = x.shape
    L = ctx.shape[1]
    rows = n // GRID_W
    assert n % (GRID_W * NA_QROWS) == 0 and rows >= NA_BAND
    assert w_mod.shape[0] == DEPTH
    cap = EC_CAPACITY * n // N_EXPERTS

    mod_rows = -(-(B + 1) // 8) * 8
    cc = jnp.concatenate([c, c_ctx[None], jnp.zeros((mod_rows - B - 1, D), F32)], axis=0)
    m = _modulation(cc, w_mod[0], b_mod[0])
    sh1, sc1, g1, sh2, sc2, g2 = [m[:B, k * D:(k + 1) * D].reshape(B, 1, D) for k in range(6)]
    csh1 = m[B:B + 1, :D].reshape(1, 1, D)
    csc1 = m[B:B + 1, D:2 * D].reshape(1, 1, D)

    w_small, w_na, w_nqt, w_nvt, w_g, wqa, wk, wv = _prep_weights(w_in[0], w_uq[0], w_ukv[0])
    qg = q_norm_g[0].reshape(1, Q_LORA)
    kvg = kv_norm_g[0].reshape(1, KV_LORA)
    c_tab, s_tab = _rope_tables(n)
    pad = HEAD_PAD - MLA_QK
    c_ctx_tab = jnp.concatenate([jnp.ones((L, MLA_QK), F32), jnp.zeros((L, pad), F32)], axis=-1)
    s_ctx_tab = jnp.zeros((L, HEAD_PAD), F32)

    q, k_lat, v_lat, nq, nk, nv, gates = _inproj(
        True, x, sc1, sh1, c_tab, s_tab, c_tab.T * (MLA_SCALE * LOG2E), s_tab.T * (MLA_SCALE * LOG2E),
        w_small, w_na, w_nqt, w_nvt, w_g, qg, kvg, wqa, wk, wv, _tile(n, 512))
    k_ctx, v_ctx, cnk, cnv = _inproj(
        False, ctx, csc1, csh1, c_ctx_tab, s_ctx_tab, c_ctx_tab.T, s_ctx_tab.T,
        w_small, w_na, w_nqt, w_nvt, w_g, qg, kvg, wqa, wk, wv, _tile(L, 256))

    y_mla = _mla_attention(q, k_lat, k_ctx, v_lat, v_ctx, _tile(n, 1024), _tile(n, 512))
    y_na = _na_attention(nq, nk, nv, cnk, cnv, _na_bias_tiles(na_rel_bias[0], rows))

    xn, u2, aff = _merge(
        y_mla, y_na, gates, x, g1, sc2, sh2,
        w_proj_mla[0].astype(BF16), w_proj_na[0].astype(BF16), w_out[0].astype(BF16),
        ln1_g[0].reshape(1, D), ln1_b[0].reshape(1, D), w_router[0], _tile(n, 2 * MERGE_SUB))

    idx, g4 = _topk(aff, cap)
    xe = _gather(idx, u2, n)
    ye = _ffn(xe, g4, w_exp_gate[0], w_exp_up[0], w_exp_down[0])
    return _combine(xn, idx, ye, g2, ln2_g[0].reshape(1, D), ln2_b[0].reshape(1, D), _tile(n, 512))
```

```python
import functools
import math

import numpy as np
import jax
import jax.numpy as jnp
from jax import lax
from jax.experimental import pallas as pl
from jax.experimental.pallas import tpu as pltpu

GRID_W = 64
MLA_HEADS = 8
MLA_NOPE = 64
MLA_ROPE = 32
MLA_QK = MLA_NOPE + MLA_ROPE
MLA_V = 64
Q_LORA = 256
KV_LORA = 128
MLA_SCALE = MLA_QK ** -0.5
ROPE_THETA = 10000.0
NA_HEADS = 8
NA_DIM = 64
NA_W = NA_HEADS * NA_DIM
NA_WIN_H = 8
NA_WIN_W = 16
NA_SCALE = NA_DIM ** -0.5
N_EXPERTS = 16
EC_CAPACITY = 2
LN_EPS = 1e-5
RMS_EPS = 1e-6
DEPTH = 1
ALPHA = (2.0 * DEPTH) ** 0.25
LOG2E = math.log2(math.e)

LANES = 128
HEAD_PAD = LANES
VMEM_LIMIT = 56 * 1024 * 1024
NEG_BIG = -1e30

NA_QROWS = 4
NA_BAND = 12
ROW_GROUP = 8
NA_LOOKAHEAD = 2
MERGE_SUB = 256
TOPK_RADIX = 64
TOPK_TABLE_ROWS = 128
TOPK_SLOTS = 128
TOPK_REFINE = 30

BF16 = jnp.bfloat16
F32 = jnp.float32


def _cparams(sem):
    return pltpu.CompilerParams(dimension_semantics=sem, vmem_limit_bytes=VMEM_LIMIT)


def _ln(x):
    mu = jnp.mean(x, axis=-1, keepdims=True)
    xc = x - mu
    var = jnp.mean(xc * xc, axis=-1, keepdims=True)
    return xc * lax.rsqrt(var + LN_EPS)


def _dot(a, b):
    return jnp.dot(a, b, preferred_element_type=F32)


def _dot_nt(a, b):
    return lax.dot_general(a, b, (((1,), (1,)), ((), ())), preferred_element_type=F32)


def _store_rows_as_tiles(ref, val):
    m, d = val.shape
    s = d // LANES
    for j in range(s):
        ref[pl.ds(j, m, stride=s), :] = val[:, j * LANES:(j + 1) * LANES]


def _load_tiles_as_rows(ref, d):
    s = d // LANES
    m = ref.shape[0] // s
    return jnp.concatenate([ref[pl.ds(j, m, stride=s), :] for j in range(s)], axis=-1)


def _mod_kernel(c_ref, w_ref, b_ref, o_ref):
    c = c_ref[...]
    s = c * jax.nn.sigmoid(c)
    o_ref[...] = jnp.dot(s, w_ref[...], preferred_element_type=F32,
                         precision=lax.Precision.HIGHEST) + b_ref[...]


def _modulation(cc, w_mod, b_mod):
    rows, d = cc.shape
    n_out = w_mod.shape[1]
    tn = 1024
    return pl.pallas_call(
        _mod_kernel,
        out_shape=jax.ShapeDtypeStruct((rows, n_out), F32),
        grid=(n_out // tn,),
        in_specs=[pl.BlockSpec((rows, d), lambda j: (0, 0)),
                  pl.BlockSpec((d, tn), lambda j: (0, j)),
                  pl.BlockSpec((1, tn), lambda j: (0, j))],
        out_specs=pl.BlockSpec((rows, tn), lambda j: (0, j)),
        compiler_params=_cparams(("arbitrary",)),
        name="modulation",
    )(cc, w_mod, b_mod.reshape(1, n_out))


def _swap_rope_halves(x, axis):
    half = MLA_ROPE // 2
    pos = lax.broadcasted_iota(jnp.int32, x.shape, axis)
    return jnp.where(pos < MLA_NOPE + half, pltpu.roll(x, HEAD_PAD - half, axis), pltpu.roll(x, half, axis))


def _store_value_tiles(ref, vt, dim):
    tm = vt.shape[1]
    tail = jnp.where(lax.broadcasted_iota(jnp.int32, (HEAD_PAD - dim, tm), 0) == 0, 1.0, 0.0).astype(ref.dtype)
    for h in range(vt.shape[0] // dim):
        ref[h * HEAD_PAD:h * HEAD_PAD + dim, :] = vt[h * dim:(h + 1) * dim, :].astype(ref.dtype)
        ref[h * HEAD_PAD + dim:(h + 1) * HEAD_PAD, :] = tail


def _inproj_kernel(latent, x_ref, sc_ref, sh_ref, ck_ref, sk_ref, cq_ref, sq_ref,
                   w_small_ref, w_na_ref, w_nqt_ref, w_nvt_ref, w_g_ref, qg_ref, kvg_ref, wqa_ref, wk_ref, wv_ref,
                   *out_refs):
    if latent:
        q_ref, k_ref, v_ref, nq_ref, nk_ref, nv_ref, g_ref = out_refs
    else:
        k_ref, v_ref, nk_ref, nv_ref = out_refs
    x = x_ref[0]
    u = (_ln(x) * (1.0 + sc_ref[0]) + sh_ref[0]).astype(BF16)

    small = _dot(u, w_small_ref[...])
    q_c = small[:, :Q_LORA]
    kv_c = small[:, Q_LORA:Q_LORA + KV_LORA]
    k_r = small[:, Q_LORA + KV_LORA:]

    kvn = (kv_c * lax.rsqrt(jnp.mean(kv_c * kv_c, axis=-1, keepdims=True) + RMS_EPS) * kvg_ref[...]).astype(BF16)
    kk = _dot(kvn, wk_ref[...])
    kr = k_r * ck_ref[...] + _swap_rope_halves(k_r, 1) * sk_ref[...]
    for h in range(MLA_HEADS):
        k_ref[0, h] = (kk[:, h * HEAD_PAD:(h + 1) * HEAD_PAD] + kr).astype(BF16)
    _store_value_tiles(v_ref.at[0], _dot_nt(wv_ref[...], kvn), MLA_V)

    nk_ref[0] = _dot(u, w_na_ref[...]).astype(BF16)
    _store_value_tiles(nv_ref.at[0], _dot_nt(w_nvt_ref[...], u), NA_DIM)
    if latent:
        qn = (q_c * lax.rsqrt(jnp.mean(q_c * q_c, axis=-1, keepdims=True) + RMS_EPS) * qg_ref[...]).astype(BF16)
        qa = _dot_nt(wqa_ref[...], qn)
        cq = cq_ref[...]
        sq = sq_ref[...]
        for h in range(MLA_HEADS):
            qh = qa[h * HEAD_PAD:(h + 1) * HEAD_PAD, :]
            q_ref[0, h] = (qh * cq + _swap_rope_halves(qh, 0) * sq).astype(BF16)
        nq_ref[0] = (_dot_nt(w_nqt_ref[...], u) * (NA_SCALE * LOG2E)).astype(BF16)
        g_ref[0] = jax.nn.sigmoid(_dot(u, w_g_ref[...])).astype(BF16)


def _inproj(latent, x, sc, sh, ck, sk, cq, sq, w_small, w_na, w_nqt, w_nvt, w_g, qg, kvg, wqa, wk, wv, tm):
    B, n, D = x.shape
    per_batch = sc.shape[0] > 1
    mod_map = (lambda b, i: (b, 0, 0)) if per_batch else (lambda b, i: (0, 0, 0))
    full = lambda a: pl.BlockSpec(a.shape, lambda b, i: (0,) * a.ndim)
    tab = lambda a: pl.BlockSpec((tm, a.shape[1]), lambda b, i: (i, 0))
    tab_t = lambda a: pl.BlockSpec((a.shape[0], tm), lambda b, i: (0, i))
    in_specs = [pl.BlockSpec((1, tm, D), lambda b, i: (b, i, 0)),
                pl.BlockSpec((1, 1, D), mod_map), pl.BlockSpec((1, 1, D), mod_map),
                tab(ck), tab(sk), tab_t(cq), tab_t(sq),
                full(w_small), full(w_na), full(w_nqt), full(w_nvt), full(w_g), full(qg), full(kvg),
                full(wqa), full(wk), full(wv)]
    hk = jax.ShapeDtypeStruct((B, MLA_HEADS, n, HEAD_PAD), BF16)
    hk_spec = pl.BlockSpec((1, MLA_HEADS, tm, HEAD_PAD), lambda b, i: (b, 0, i, 0))
    tok = lambda w: jax.ShapeDtypeStruct((B, n, w), BF16)
    tok_spec = lambda w: pl.BlockSpec((1, tm, w), lambda b, i: (b, i, 0))
    vt = jax.ShapeDtypeStruct((B, MLA_HEADS * HEAD_PAD, n), BF16)
    vt_spec = pl.BlockSpec((1, MLA_HEADS * HEAD_PAD, tm), lambda b, i: (b, 0, i))
    nqt = jax.ShapeDtypeStruct((B, NA_W, n), BF16)
    nqt_spec = pl.BlockSpec((1, NA_W, tm), lambda b, i: (b, 0, i))
    if latent:
        qt = jax.ShapeDtypeStruct((B, MLA_HEADS, HEAD_PAD, n), BF16)
        qt_spec = pl.BlockSpec((1, MLA_HEADS, HEAD_PAD, tm), lambda b, i: (b, 0, 0, i))
        out_shape = [qt, hk, vt, nqt, tok(NA_W), vt, tok(2 * D)]
        out_specs = [qt_spec, hk_spec, vt_spec, nqt_spec, tok_spec(NA_W), vt_spec, tok_spec(2 * D)]
    else:
        out_shape = [hk, vt, tok(NA_W), vt]
        out_specs = [hk_spec, vt_spec, tok_spec(NA_W), vt_spec]
    return pl.pallas_call(
        functools.partial(_inproj_kernel, latent),
        out_shape=out_shape,
        grid=(B, n // tm),
        in_specs=in_specs,
        out_specs=out_specs,
        compiler_params=_cparams(("parallel", "parallel")),
        name="inproj_latent" if latent else "inproj_ctx",
    )(x, sc, sh, ck, sk, cq, sq, w_small, w_na, w_nqt, w_nvt, w_g, qg, kvg, wqa, wk, wv)


def _mla_kernel(tk, q_ref, kc_ref, kl_ref, vc_ref, vl_ref, o_ref, m_ref, acc_ref, sa_ref, sb_ref, sc_ref):
    heads = q_ref.shape[1]
    n = kl_ref.shape[2]
    tq = q_ref.shape[3]
    nt = n // tk

    def scores(h, k):
        return _dot(k, q_ref[0, h])

    def absorb(hh, s, v_t):
        m_old = m_ref[hh]
        m_new = jnp.maximum(m_old, jnp.max(s, axis=0, keepdims=True))
        a = jnp.exp2(m_old - m_new)
        p = jnp.exp2(s - m_new).astype(BF16)
        acc_ref[hh] = a * acc_ref[hh] + _dot(v_t, p)
        m_ref[hh] = m_new

    def ctx_scores(h0):
        for hh in range(2):
            sc_ref[hh] = scores(h0 + hh, kc_ref[0, h0 + hh])

    def pair_pass(pair, carry):
        h0 = 2 * pair

        def head_rows(hh):
            return pl.ds(pl.multiple_of((h0 + hh) * HEAD_PAD, HEAD_PAD), HEAD_PAD)

        def k_tile(hh, j):
            return kl_ref[0, h0 + hh, pl.ds(pl.multiple_of(j * tk, tk), tk), :]

        def v_tile(hh, j):
            return vl_ref[0, head_rows(hh), pl.ds(pl.multiple_of(j * tk, tk), tk)]

        def advance(j, cur_ref, next_ref):
            for hh in range(2):
                next_ref[hh] = scores(h0 + hh, k_tile(hh, j + 1))
                absorb(hh, cur_ref[hh], v_tile(hh, j))

        m_ref[...] = jnp.full(m_ref.shape, NEG_BIG, F32)
        acc_ref[...] = jnp.zeros(acc_ref.shape, F32)
        for hh in range(2):
            sa_ref[hh] = scores(h0 + hh, k_tile(hh, 0))
            absorb(hh, sc_ref[hh], vc_ref[0, head_rows(hh), :])

        def body(jj, c):
            advance(2 * jj, sa_ref, sb_ref)
            advance(2 * jj + 1, sb_ref, sa_ref)
            return c

        lax.fori_loop(0, (nt - 1) // 2, body, 0)
        last_ref = sa_ref
        if (nt - 1) % 2:
            advance(nt - 2, sa_ref, sb_ref)
            last_ref = sb_ref
        ctx_scores(jnp.minimum(h0 + 2, heads - 2))
        outs = []
        for hh in range(2):
            absorb(hh, last_ref[hh], v_tile(hh, nt - 1))
            acc = acc_ref[hh]
            outs.append(acc[:MLA_V] / acc[MLA_V:MLA_V + 1])
        lanes = pl.ds(pl.multiple_of(pair * LANES, LANES), LANES)
        o_ref[0, :, lanes] = jnp.concatenate(outs, axis=0).T.astype(o_ref.dtype)
        return carry

    ctx_scores(0)
    lax.fori_loop(0, heads // 2, pair_pass, 0)


def _mla_attention(q, k_lat, k_ctx, v_lat, v_ctx, tq, tk):
    B, H, _, n = q.shape
    L = k_ctx.shape[2]
    W = H * MLA_V
    return pl.pallas_call(
        functools.partial(_mla_kernel, tk),
        out_shape=jax.ShapeDtypeStruct((B, n, W), BF16),
        grid=(B, n // tq),
        in_specs=[pl.BlockSpec((1, H, HEAD_PAD, tq), lambda b, i: (b, 0, 0, i)),
                  pl.BlockSpec((1, H, L, HEAD_PAD), lambda b, i: (b, 0, 0, 0)),
                  pl.BlockSpec((1, H, n, HEAD_PAD), lambda b, i: (b, 0, 0, 0)),
                  pl.BlockSpec((1, H * HEAD_PAD, L), lambda b, i: (b, 0, 0)),
                  pl.BlockSpec((1, H * HEAD_PAD, n), lambda b, i: (b, 0, 0))],
        out_specs=pl.BlockSpec((1, tq, W), lambda b, i: (b, i, 0)),
        scratch_shapes=[pltpu.VMEM((2, 1, tq), F32),
                        pltpu.VMEM((2, HEAD_PAD, tq), F32),
                        pltpu.VMEM((2, tk, tq), F32), pltpu.VMEM((2, tk, tq), F32),
                        pltpu.VMEM((2, L, tq), F32)],
        compiler_params=_cparams(("parallel", "parallel")),
        name="mla_attention",
    )(q, k_ctx, k_lat, v_ctx, v_lat)


def _na_block_tables(rows):
    nblk = rows // NA_QROWS
    wh = min(NA_WIN_H, rows)
    band0 = np.clip(np.arange(nblk) * NA_QROWS - wh // 2, 0, rows - NA_BAND)
    sigs, types = [], []
    for i in range(nblk):
        r = i * NA_QROWS + np.arange(NA_QROWS)
        r0 = np.clip(r - wh // 2, 0, rows - wh)
        sig = (tuple(r0 - band0[i]), tuple(r - band0[i]))
        if sig not in sigs:
            sigs.append(sig)
        types.append(sigs.index(sig))
    return band0.astype(np.int32), np.asarray(types, np.int32), sigs, wh


def _bias_tile_kernel(dr, ok, tbl_ref, o_ref):
    neg = jnp.full((GRID_W, GRID_W), NEG_BIG, F32)
    for t in range(o_ref.shape[0]):
        for k in range(NA_BAND):
            pieces = [tbl_ref[0, int(dr[t, q, k])] if ok[t, q, k] else neg for q in range(NA_QROWS)]
            o_ref[t, 0, k * GRID_W:(k + 1) * GRID_W, :] = jnp.concatenate(pieces, axis=1)


def _na_bias_tiles(rel_bias, rows):
    _, _, sigs, wh = _na_block_tables(rows)
    ww = NA_WIN_W
    n_dr, n_dc = 2 * NA_WIN_H - 1, 2 * NA_WIN_W - 1
    col = np.arange(GRID_W)
    c0 = np.clip(col - ww // 2, 0, GRID_W - ww)
    col_ok = (col[:, None] >= c0[None, :]) & (col[:, None] < c0[None, :] + ww)
    dc = col[:, None] - col[None, :] + (NA_WIN_W - 1)
    pick_c = ((dc[None] == np.arange(n_dc)[:, None, None]) & col_ok[None]).astype(np.float32)
    by_col = jnp.einsum('hdj,jyx->hdyx', rel_bias.astype(F32), jnp.asarray(pick_c), precision=lax.Precision.HIGHEST)
    by_col = jnp.where(jnp.asarray(col_ok), by_col * LOG2E, NEG_BIG)
    kr = np.arange(NA_BAND)
    ok = np.stack([(kr[None, :] >= np.asarray(r0)[:, None]) & (kr[None, :] < np.asarray(r0)[:, None] + wh)
                   for r0, _ in sigs])
    dr = np.stack([kr[None, :] - np.asarray(r)[:, None] + (NA_WIN_H - 1) for _, r in sigs])
    assert ((dr >= 0) & (dr < n_dr))[ok].all()
    T = len(sigs)
    kn, qn = NA_BAND * GRID_W, NA_QROWS * GRID_W
    return pl.pallas_call(
        functools.partial(_bias_tile_kernel, dr, ok),
        out_shape=jax.ShapeDtypeStruct((T, NA_HEADS, kn, qn), F32),
        grid=(NA_HEADS,),
        in_specs=[pl.BlockSpec((1, n_dr, GRID_W, GRID_W), lambda h: (h, 0, 0, 0))],
        out_specs=pl.BlockSpec((T, 1, kn, qn), lambda h: (0, h, 0, 0)),
        compiler_params=_cparams(("parallel",)),
        name="na_bias_tiles",
    )(by_col)


def _na_kernel(nblk, band_ref, type_ref, q_ref, k_ref, vt_ref, kc_ref, vct_ref, *rest):
    bias_refs, o_ref = rest[:nblk], rest[nblk]
    i = pl.program_id(1)
    nq = NA_QROWS * GRID_W
    nk = NA_BAND * GRID_W
    starts = [pl.multiple_of(band_ref[i * nblk + b] * GRID_W, NA_QROWS * GRID_W) for b in range(nblk)]
    row = lax.broadcasted_iota(jnp.int32, (LANES, nq), 0)

    def scores(c):
        blk, h = divmod(c, NA_HEADS)
        cs = slice((h // 2) * LANES, (h // 2 + 1) * LANES)
        qp = q_ref[0, cs, blk * nq:(blk + 1) * nq]
        own = (row < NA_DIM) if h % 2 == 0 else (row >= NA_DIM)
        qh = jnp.where(own, qp, jnp.zeros_like(qp))
        s_loc = _dot(k_ref[0, pl.ds(starts[blk], nk), cs], qh) + bias_refs[blk][0, h]
        s_ctx = _dot(kc_ref[0, :, cs], qh)
        return s_loc, s_ctx

    def attend(c, s):
        s_loc, s_ctx = s
        blk, h = divmod(c, NA_HEADS)
        start = starts[blk]
        vs = slice(h * HEAD_PAD, (h + 1) * HEAD_PAD)
        m = jnp.maximum(jnp.max(s_loc, axis=0, keepdims=True), jnp.max(s_ctx, axis=0, keepdims=True))
        p_loc = jnp.exp2((s_loc - m).astype(BF16))
        p_ctx = jnp.exp2((s_ctx - m).astype(BF16))
        o = _dot(vt_ref[0, vs, pl.ds(start, nk)], p_loc) + _dot(vct_ref[0, vs, :], p_ctx)
        return o[:NA_DIM] / o[NA_DIM:NA_DIM + 1]

    chain = nblk * NA_HEADS
    ahead = [scores(c) for c in range(min(NA_LOOKAHEAD, chain))]
    outs = []
    for c in range(chain):
        if c + NA_LOOKAHEAD < chain:
            ahead.append(scores(c + NA_LOOKAHEAD))
        outs.append(attend(c, ahead.pop(0)))
        if c % 2:
            blk, h = divmod(c, NA_HEADS)
            pair_t = jnp.concatenate([outs[c - 1], outs[c]], axis=0)
            o_ref[0, blk * nq:(blk + 1) * nq, (h // 2) * LANES:(h // 2 + 1) * LANES] = pair_t.T.astype(o_ref.dtype)


def _na_attention(nqt, nk, nvt, cnk, cnvt, bias_tiles):
    B, n, W = nk.shape
    L = cnk.shape[1]
    rows = n // GRID_W
    band0, types, _, _ = _na_block_tables(rows)
    qn = NA_QROWS * GRID_W
    kn = NA_BAND * GRID_W
    nblocks = rows // NA_QROWS
    nblk = 2 if nblocks % 2 == 0 else 1

    def bias_spec(k):
        return pl.BlockSpec((1, NA_HEADS, kn, qn), lambda b, i, bd, ty: (ty[i * nblk + k], 0, 0, 0))

    grid_spec = pltpu.PrefetchScalarGridSpec(
        num_scalar_prefetch=2,
        grid=(B, nblocks // nblk),
        in_specs=[pl.BlockSpec((1, W, nblk * qn), lambda b, i, bd, ty: (b, 0, i)),
                  pl.BlockSpec((1, n, W), lambda b, i, bd, ty: (b, 0, 0)),
                  pl.BlockSpec((1, NA_HEADS * HEAD_PAD, n), lambda b, i, bd, ty: (b, 0, 0)),
                  pl.BlockSpec((1, L, W), lambda b, i, bd, ty: (b, 0, 0)),
                  pl.BlockSpec((1, NA_HEADS * HEAD_PAD, L), lambda b, i, bd, ty: (b, 0, 0))]
                 + [bias_spec(k) for k in range(nblk)],
        out_specs=pl.BlockSpec((1, nblk * qn, W), lambda b, i, bd, ty: (b, i, 0)),
    )
    return pl.pallas_call(
        functools.partial(_na_kernel, nblk),
        out_shape=jax.ShapeDtypeStruct((B, n, W), BF16),
        grid_spec=grid_spec,
        compiler_params=_cparams(("parallel", "arbitrary")),
        name="na_attention",
    )(jnp.asarray(band0), jnp.asarray(types), nqt, nk, nvt, cnk, cnvt, *([bias_tiles] * nblk))


def _split2(a):
    hi = a.astype(BF16)
    return hi, (a - hi.astype(F32)).astype(BF16)


def _merge_kernel(ym_ref, yn_ref, g_ref, x_ref, g1_ref, sc2_ref, sh2_ref, wpm_ref, wpn_ref, wo_ref,
                  l1g_ref, l1b_ref, wr_ref, xn_ref, up_ref, aff_ref):
    tm, D = x_ref.shape[1], x_ref.shape[2]
    s = D // LANES
    sub = min(MERGE_SUB, tm)
    n_exp = aff_ref.shape[1]
    wh, wl = _split2(wr_ref[...])

    def mix_of(r):
        g = g_ref[0, r, :]
        a = (g[:, :D].astype(F32) * _dot(ym_ref[0, r, :], wpm_ref[...])
             + g[:, D:].astype(F32) * _dot(yn_ref[0, r, :], wpn_ref[...]))
        return _dot(a.astype(BF16), wo_ref[...])

    def finish(r, mix):
        xn = _ln(ALPHA * x_ref[0, r, :] + g1_ref[0] * mix) * l1g_ref[...] + l1b_ref[...]
        xn_ref[0, r, :] = xn
        u2 = _ln(xn) * (1.0 + sc2_ref[0]) + sh2_ref[0]
        _store_rows_as_tiles(up_ref.at[0, r.start * s:r.stop * s], u2)
        uh, ul = _split2(u2)
        logits = _dot(uh, wh) + (_dot(uh, wl) + _dot(ul, wh))
        logits = logits.T[:n_exp]
        e = jnp.exp(logits - jnp.max(logits, axis=0, keepdims=True))
        aff_ref[0, :, r] = e / jnp.sum(e, axis=0, keepdims=True)

    rows = [slice(i, i + sub) for i in range(0, tm, sub)]
    mix = mix_of(rows[0])
    for i, r in enumerate(rows):
        mix_next = mix_of(rows[i + 1]) if i + 1 < len(rows) else None
        finish(r, mix)
        mix = mix_next


def _merge(y_mla, y_na, gates, x, g1, sc2, sh2, wpm, wpn, wo, l1g, l1b, wr, tm):
    B, n, D = x.shape
    E = wr.shape[1]
    wr = jnp.concatenate([wr, jnp.zeros((D, LANES - E), wr.dtype)], axis=1)
    full = lambda a: pl.BlockSpec(a.shape, lambda b, i: (0,) * a.ndim)
    tok = lambda w: pl.BlockSpec((1, tm, w), lambda b, i: (b, i, 0))
    mod = pl.BlockSpec((1, 1, D), lambda b, i: (b, 0, 0))
    return pl.pallas_call(
        _merge_kernel,
        out_shape=[jax.ShapeDtypeStruct((B, n, D), F32),
                   jax.ShapeDtypeStruct((B, n * (D // LANES), LANES), F32),
                   jax.ShapeDtypeStruct((B, E, n), F32)],
        grid=(B, n // tm),
        in_specs=[tok(y_mla.shape[2]), tok(y_na.shape[2]), tok(2 * D), tok(D), mod, mod, mod,
                  full(wpm), full(wpn), full(wo), full(l1g), full(l1b), full(wr)],
        out_specs=[tok(D), pl.BlockSpec((1, tm * (D // LANES), LANES), lambda b, i: (b, i, 0)),
                   pl.BlockSpec((1, E, tm), lambda b, i: (b, 0, i))],
        compiler_params=_cparams(("parallel", "parallel")),
        name="merge_ln_router",
    )(y_mla, y_na, gates, x, g1, sc2, sh2, wpm, wpn, wo, l1g, l1b, wr)


def _topk_kernel(cap, aff_ref, tri_ref, idx_ref, g_ref, pos_ref):
    aff = aff_ref[0]
    E, n = aff.shape

    def count(mask):
        return jnp.sum(mask.astype(F32), axis=-1, keepdims=True)

    def enough(v):
        return count(aff >= v) >= cap

    def search(t, thr):
        b1 = jnp.int32(1) << (30 - 2 * t)
        b0 = b1 >> 1
        c3, c2, c1 = thr | b1 | b0, thr | b1, thr | b0
        ok3, ok2, ok1 = [enough(pltpu.bitcast(c, F32)) for c in (c3, c2, c1)]
        return jnp.where(ok3, c3, jnp.where(ok2, c2, jnp.where(ok1, c1, thr)))

    thr = lax.fori_loop(0, 16, search, jnp.zeros((E, 1), jnp.int32))
    lo = pltpu.bitcast(thr, F32)
    hi = pltpu.bitcast(thr + 1, F32)

    def refine(t, lh):
        lo, hi = lh
        w = hi - lo
        q1, q2, q3 = lo + w * 0.25, lo + w * 0.5, lo + w * 0.75
        ok1, ok2, ok3 = enough(q1), enough(q2), enough(q3)
        new_lo = jnp.where(ok3, q3, jnp.where(ok2, q2, jnp.where(ok1, q1, lo)))
        new_hi = jnp.where(ok3, hi, jnp.where(ok2, q3, jnp.where(ok1, q2, q1)))
        return new_lo, new_hi

    lo, hi = lax.fori_loop(0, TOPK_REFINE // 2, refine, (lo, hi))
    gt = aff >= hi
    eq = (aff >= lo) & ~gt
    need = cap - count(gt)

    tri = tri_ref[...]

    def prefix(mask):
        mf = mask.astype(F32).astype(BF16)
        parts = []
        off = jnp.zeros((E, 1), F32)
        for c in range(n // LANES):
            blk = mf[:, c * LANES:(c + 1) * LANES]
            parts.append(_dot(blk, tri) + off)
            off = off + jnp.sum(blk.astype(F32), axis=-1, keepdims=True)
        return jnp.concatenate(parts, axis=-1)

    sel = gt | (eq & (prefix(eq) < need))
    pos_ref[...] = jnp.where(sel, prefix(sel), -1.0)

    tok = lax.broadcasted_iota(jnp.int32, (1, n), 1)
    digits = [(tok // TOPK_RADIX).astype(F32), (tok % TOPK_RADIX).astype(F32)]
    chunks = cap // TOPK_SLOTS

    def per_pair(ee, carry):
        work = []
        for e in (2 * ee, 2 * ee + 1):
            a = aff_ref[0, pl.ds(e, 1), :]
            a1 = a.astype(BF16).astype(F32)
            a2 = (a - a1).astype(BF16).astype(F32)
            a3 = (a - a1) - a2
            rows = digits + [a1, a2, a3]
            table = jnp.concatenate(rows + [jnp.zeros((TOPK_TABLE_ROWS - len(rows), n), F32)], axis=0).astype(BF16)
            pos_e = pos_ref[pl.ds(e, 1), :]
            for ch in range(chunks):
                slot = (lax.broadcasted_iota(jnp.int32, (TOPK_SLOTS, n), 0) + ch * TOPK_SLOTS).astype(F32)
                work.append((e, ch, jnp.where(slot == pos_e, 1.0, 0.0).astype(BF16), table))
        for e, ch, hit, table in work:
            r = _dot_nt(hit, table)
            g_ref[0, e, ch * TOPK_SLOTS:(ch + 1) * TOPK_SLOTS, :] = (r[:, 2:3] + r[:, 3:4]) + r[:, 4:5]
            rt = r.T
            idx = rt[0:1] * TOPK_RADIX + rt[1:2]
            idx_ref[0, e, :, ch * TOPK_SLOTS:(ch + 1) * TOPK_SLOTS] = idx.astype(jnp.int32)
        return carry

    lax.fori_loop(0, E // 2, per_pair, 0)


def _topk(aff_t, cap):
    B, E, n = aff_t.shape
    assert E % 2 == 0 and cap % TOPK_SLOTS == 0
    tri = jnp.asarray(np.triu(np.ones((LANES, LANES), np.float32), k=1), BF16)
    return pl.pallas_call(
        functools.partial(_topk_kernel, cap),
        out_shape=[jax.ShapeDtypeStruct((B, E, 1, cap), jnp.int32),
                   jax.ShapeDtypeStruct((B, E, cap, 1), F32)],
        grid=(B,),
        in_specs=[pl.BlockSpec((1, E, n), lambda b: (b, 0, 0)),
                  pl.BlockSpec((LANES, LANES), lambda b: (0, 0))],
        out_specs=[pl.BlockSpec((1, E, 1, cap), lambda b: (b, 0, 0, 0)),
                   pl.BlockSpec((1, E, cap, 1), lambda b: (b, 0, 0, 0))],
        scratch_shapes=[pltpu.VMEM((E, n), F32)],
        compiler_params=_cparams(("parallel",)),
        name="expert_topk",
    )(aff_t, tri)


def _gather_kernel(s, idx_ref, u_ref, o_ref):
    cap = o_ref.shape[2] // s

    def body(i, carry):
        c0 = pl.multiple_of(i * ROW_GROUP, ROW_GROUP)
        for k in range(ROW_GROUP):
            src = pl.multiple_of(idx_ref[0, 0, 0, c0 + k] * s, s)
            o_ref[0, 0, pl.ds((c0 + k) * s, s), :] = u_ref[0, pl.ds(src, s), :]
        return carry

    lax.fori_loop(0, cap // ROW_GROUP, body, 0)


def _gather(idx, u2, n):
    B, E, _, cap = idx.shape
    _, ns, W = u2.shape
    s = ns // n
    return pl.pallas_call(
        functools.partial(_gather_kernel, s),
        out_shape=jax.ShapeDtypeStruct((B, E, cap * s, W), u2.dtype),
        grid=(B, E),
        in_specs=[pl.BlockSpec((1, 1, 1, cap), lambda b, e: (b, e, 0, 0), memory_space=pltpu.SMEM),
                  pl.BlockSpec((1, ns, W), lambda b, e: (b, 0, 0))],
        out_specs=pl.BlockSpec((1, 1, cap * s, W), lambda b, e: (b, e, 0, 0)),
        compiler_params=_cparams(("parallel", "arbitrary")),
        name="expert_gather",
    )(idx, u2)


def _ffn_kernel(xe_ref, g_ref, wg_ref, wu_ref, wd_ref, o_ref, wgb_ref, wub_ref, wdb_ref):
    @pl.when(pl.program_id(1) == 0)
    def _():
        wgb_ref[...] = wg_ref[0].astype(BF16)
        wub_ref[...] = wu_ref[0].astype(BF16)
        wdb_ref[...] = wd_ref[0].astype(BF16)

    xe = _load_tiles_as_rows(xe_ref.at[0, 0], wgb_ref.shape[0]).astype(BF16)
    gate = _dot(xe, wgb_ref[...])
    up = _dot(xe, wub_ref[...])
    h = (gate * jax.nn.sigmoid(gate) * up).astype(BF16)
    _store_rows_as_tiles(o_ref.at[0, 0], _dot(h, wdb_ref[...]) * g_ref[0, 0])


def _ffn(xe, g, w_gate, w_up, w_down):
    B, E, caps, W = xe.shape
    _, D, F = w_gate.shape
    cap = g.shape[2]
    return pl.pallas_call(
        _ffn_kernel,
        out_shape=jax.ShapeDtypeStruct((B, E, caps, W), F32),
        grid=(E, B),
        in_specs=[pl.BlockSpec((1, 1, caps, W), lambda e, b: (b, e, 0, 0)),
                  pl.BlockSpec((1, 1, cap, 1), lambda e, b: (b, e, 0, 0)),
                  pl.BlockSpec((1, D, F), lambda e, b: (e, 0, 0)),
                  pl.BlockSpec((1, D, F), lambda e, b: (e, 0, 0)),
                  pl.BlockSpec((1, F, D), lambda e, b: (e, 0, 0))],
        out_specs=pl.BlockSpec((1, 1, caps, W), lambda e, b: (b, e, 0, 0)),
        scratch_shapes=[pltpu.VMEM((D, F), BF16), pltpu.VMEM((D, F), BF16), pltpu.VMEM((F, D), BF16)],
        compiler_params=_cparams(("arbitrary", "arbitrary")),
        name="expert_ffn",
    )(xe, g, w_gate, w_up, w_down)


def _combine_kernel(s, n_exp, x_ref, idx_ref, y_ref, g2_ref, lg_ref, lb_ref, o_ref, acc_ref):
    e = pl.program_id(1)
    cap = y_ref.shape[2] // s
    tm, D = x_ref.shape[1], x_ref.shape[2]

    def tile(r):
        return pl.ds(pl.multiple_of(r * s, s), s)

    @pl.when(e == 0)
    def _():
        acc_ref[...] = jnp.zeros_like(acc_ref)

    @pl.when(e < n_exp)
    def _():
        def body(i, carry):
            c0 = pl.multiple_of(i * ROW_GROUP, ROW_GROUP)
            rows = [idx_ref[0, 0, 0, c0 + k] for k in range(ROW_GROUP)]
            acc = [acc_ref[tile(r), :] for r in rows]
            for k in range(ROW_GROUP):
                acc_ref[tile(rows[k]), :] = acc[k] + y_ref[0, 0, tile(c0 + k), :]
            return carry

        lax.fori_loop(0, cap // ROW_GROUP, body, 0)

    @pl.when(e >= n_exp)
    def _():
        first = pl.multiple_of((e - n_exp) * (tm * s), tm * s)
        moe = _load_tiles_as_rows(acc_ref.at[pl.ds(first, tm * s)], D)
        o_ref[0] = _ln(ALPHA * x_ref[0] + g2_ref[0] * moe) * lg_ref[...] + lb_ref[...]


def _combine(xn, idx, ye, g2, lg, lb, tm):
    B, n, D = xn.shape
    _, E, caps, W = ye.shape
    cap = idx.shape[3]
    s = caps // cap
    blocks = n // tm
    tok = pl.BlockSpec((1, tm, D), lambda b, e: (b, jnp.maximum(e - E, 0), 0))
    vec = pl.BlockSpec((1, D), lambda b, e: (0, 0))
    last = lambda b, e: (b, jnp.minimum(e, E - 1), 0, 0)
    return pl.pallas_call(
        functools.partial(_combine_kernel, s, E),
        out_shape=jax.ShapeDtypeStruct((B, n, D), F32),
        grid=(B, E + blocks),
        in_specs=[tok,
                  pl.BlockSpec((1, 1, 1, cap), last, memory_space=pltpu.SMEM),
                  pl.BlockSpec((1, 1, caps, W), last),
                  pl.BlockSpec((1, 1, D), lambda b, e: (b, 0, 0)), vec, vec],
        out_specs=tok,
        scratch_shapes=[pltpu.VMEM((n * s, W), F32)],
        compiler_params=_cparams(("parallel", "arbitrary")),
        name="expert_combine_ln",
    )(xn, idx, ye, g2, lg, lb)


def _rope_tables(n):
    t = np.arange(n)
    row = (t // GRID_W).astype(np.float32)
    col = (t % GRID_W).astype(np.float32)
    per_axis = MLA_ROPE // 2
    inv_freq = jnp.asarray(ROPE_THETA, F32) ** (-jnp.arange(0, per_axis, 2, dtype=F32) / per_axis)
    ang = jnp.concatenate([jnp.asarray(row)[:, None] * inv_freq, jnp.asarray(col)[:, None] * inv_freq], axis=-1)
    cos, sin = jnp.cos(ang), jnp.sin(ang)
    pad = HEAD_PAD - MLA_QK
    c_tab = jnp.concatenate([jnp.ones((n, MLA_NOPE), F32), cos, cos, jnp.zeros((n, pad), F32)], axis=-1)
    s_tab = jnp.concatenate([jnp.zeros((n, MLA_NOPE), F32), -sin, sin, jnp.zeros((n, pad), F32)], axis=-1)
    return c_tab, s_tab


def _prep_weights(w_in, w_uq, w_ukv):
    D = w_in.shape[0]
    o1 = Q_LORA
    o2 = o1 + KV_LORA
    o3 = o2 + MLA_ROPE
    o4 = o3 + 3 * NA_W
    w_kr = w_in[:, o2:o3]
    z = lambda k: jnp.zeros((D, k), w_in.dtype)
    pad = HEAD_PAD - MLA_QK
    w_small = jnp.concatenate([w_in[:, :o2], z(MLA_NOPE), w_kr, z(pad)], axis=-1).astype(BF16)
    w_nqt = w_in[:, o3:o3 + NA_W].T.astype(BF16)
    w_na = w_in[:, o3 + NA_W:o3 + 2 * NA_W].astype(BF16)
    w_nvt = w_in[:, o3 + 2 * NA_W:o4].T.astype(BF16)
    w_g = w_in[:, o4:].astype(BF16)

    uq = w_uq.reshape(Q_LORA, MLA_HEADS, MLA_QK)
    zq = lambda k: jnp.zeros((Q_LORA, MLA_HEADS, k), w_uq.dtype)
    wqa = jnp.concatenate([uq, zq(pad)], axis=-1).reshape(Q_LORA, MLA_HEADS * HEAD_PAD).T.astype(BF16)

    ukv = w_ukv.reshape(KV_LORA, MLA_HEADS, MLA_NOPE + MLA_V)
    wk = jnp.concatenate([ukv[..., :MLA_NOPE], jnp.zeros((KV_LORA, MLA_HEADS, HEAD_PAD - MLA_NOPE), w_ukv.dtype)], axis=-1)
    wk = wk.reshape(KV_LORA, MLA_HEADS * HEAD_PAD).astype(BF16)
    wv = ukv[..., MLA_NOPE:].reshape(KV_LORA, MLA_HEADS * MLA_V).T.astype(BF16)
    return w_small, w_na, w_nqt, w_nvt, w_g, wqa, wk, wv


def _tile(n, pref):
    t = min(pref, n)
    while n % t:
        t //= 2
    return t


def kernel(x, c, ctx, c_ctx, w_mod, b_mod, w_in, q_norm_g, w_uq, kv_norm_g, w_ukv, na_rel_bias, w_proj_mla,
           w_proj_na, w_out, ln1_g, ln1_b, w_router, w_exp_gate, w_exp_up, w_exp_down, ln2_g, ln2_b):
    B, n, D = x.shape
    L = ctx.shape[1]
    rows = n // GRID_W
    assert n % (GRID_W * NA_QROWS) == 0 and rows >= NA_BAND
    assert w_mod.shape[0] == DEPTH
    cap = EC_CAPACITY * n // N_EXPERTS

    mod_rows = -(-(B + 1) // 8) * 8
    cc = jnp.concatenate([c, c_ctx[None], jnp.zeros((mod_rows - B - 1, D), F32)], axis=0)
    m = _modulation(cc, w_mod[0], b_mod[0])
    sh1, sc1, g1, sh2, sc2, g2 = [m[:B, k * D:(k + 1) * D].reshape(B, 1, D) for k in range(6)]
    csh1 = m[B:B + 1, :D].reshape(1, 1, D)
    csc1 = m[B:B + 1, D:2 * D].reshape(1, 1, D)

    w_small, w_na, w_nqt, w_nvt, w_g, wqa, wk, wv = _prep_weights(w_in[0], w_uq[0], w_ukv[0])
    qg = q_norm_g[0].reshape(1, Q_LORA)
    kvg = kv_norm_g[0].reshape(1, KV_LORA)
    c_tab, s_tab = _rope_tables(n)
    pad = HEAD_PAD - MLA_QK
    c_ctx_tab = jnp.concatenate([jnp.ones((L, MLA_QK), F32), jnp.zeros((L, pad), F32)], axis=-1)
    s_ctx_tab = jnp.zeros((L, HEAD_PAD), F32)

    q, k_lat, v_lat, nq, nk, nv, gates = _inproj(
        True, x, sc1, sh1, c_tab, s_tab, c_tab.T * (MLA_SCALE * LOG2E), s_tab.T * (MLA_SCALE * LOG2E),
        w_small, w_na, w_nqt, w_nvt, w_g, qg, kvg, wqa, wk, wv, _tile(n, 512))
    k_ctx, v_ctx, cnk, cnv = _inproj(
        False, ctx, csc1, csh1, c_ctx_tab, s_ctx_tab, c_ctx_tab.T, s_ctx_tab.T,
        w_small, w_na, w_nqt, w_nvt, w_g, qg, kvg, wqa, wk, wv, _tile(L, 256))

    y_mla = _mla_attention(q, k_lat, k_ctx, v_lat, v_ctx, _tile(n, 1024), _tile(n, 512))
    y_na = _na_attention(nq, nk, nv, cnk, cnv, _na_bias_tiles(na_rel_bias[0], rows))

    xn, u2, aff = _merge(
        y_mla, y_na, gates, x, g1, sc2, sh2,
        w_proj_mla[0].astype(BF16), w_proj_na[0].astype(BF16), w_out[0].astype(BF16),
        ln1_g[0].reshape(1, D), ln1_b[0].reshape(1, D), w_router[0], _tile(n, 2 * MERGE_SUB))

    idx, g4 = _topk(aff, cap)
    xe = _gather(idx, u2, n)
    ye = _ffn(xe, g4, w_exp_gate[0], w_exp_up[0], w_exp_down[0])
    return _combine(xn, idx, ye, g2, ln2_g[0].reshape(1, D), ln2_b[0].reshape(1, D), _tile(n, 512))
```

```python
import functools
import math

import numpy as np
import jax
import jax.numpy as jnp
from jax import lax
from jax.experimental import pallas as pl
from jax.experimental.pallas import tpu as pltpu

GRID_W = 64
MLA_HEADS = 8
MLA_NOPE = 64
MLA_ROPE = 32
MLA_QK = MLA_NOPE + MLA_ROPE
MLA_V = 64
Q_LORA = 256
KV_LORA = 128
MLA_SCALE = MLA_QK ** -0.5
ROPE_THETA = 10000.0
NA_HEADS = 8
NA_DIM = 64
NA_W = NA_HEADS * NA_DIM
NA_WIN_H = 8
NA_WIN_W = 16
NA_SCALE = NA_DIM ** -0.5
N_EXPERTS = 16
EC_CAPACITY = 2
LN_EPS = 1e-5
RMS_EPS = 1e-6
DEPTH = 1
ALPHA = (2.0 * DEPTH) ** 0.25
LOG2E = math.log2(math.e)

LANES = 128
HEAD_PAD = LANES
VMEM_LIMIT = 56 * 1024 * 1024
NEG_BIG = -1e30

NA_QROWS = 4
NA_BAND = 12
ROW_GROUP = 8
NA_LOOKAHEAD = 2
MERGE_SUB = 256
TOPK_EXPERTS = 4
TOPK_RADIX = 64
TOPK_TABLE_ROWS = 128
TOPK_SLOTS = 128
TOPK_REFINE = 30

BF16 = jnp.bfloat16
F32 = jnp.float32


def _cparams(sem):
    return pltpu.CompilerParams(dimension_semantics=sem, vmem_limit_bytes=VMEM_LIMIT)


def _ln(x):
    mu = jnp.mean(x, axis=-1, keepdims=True)
    xc = x - mu
    var = jnp.mean(xc * xc, axis=-1, keepdims=True)
    return xc * lax.rsqrt(var + LN_EPS)


def _dot(a, b):
    return jnp.dot(a, b, preferred_element_type=F32)


def _dot_nt(a, b):
    return lax.dot_general(a, b, (((1,), (1,)), ((), ())), preferred_element_type=F32)


def _store_rows_as_tiles(ref, val):
    m, d = val.shape
    s = d // LANES
    for j in range(s):
        ref[pl.ds(j, m, stride=s), :] = val[:, j * LANES:(j + 1) * LANES]


def _load_tiles_as_rows(ref, d):
    s = d // LANES
    m = ref.shape[0] // s
    return jnp.concatenate([ref[pl.ds(j, m, stride=s), :] for j in range(s)], axis=-1)


def _mod_kernel(c_ref, w_ref, b_ref, o_ref):
    c = c_ref[...]
    s = c * jax.nn.sigmoid(c)
    o_ref[...] = jnp.dot(s, w_ref[...], preferred_element_type=F32,
                         precision=lax.Precision.HIGHEST) + b_ref[...]


def _modulation(cc, w_mod, b_mod):
    rows, d = cc.shape
    n_out = w_mod.shape[1]
    tn = 1024
    return pl.pallas_call(
        _mod_kernel,
        out_shape=jax.ShapeDtypeStruct((rows, n_out), F32),
        grid=(n_out // tn,),
        in_specs=[pl.BlockSpec((rows, d), lambda j: (0, 0)),
                  pl.BlockSpec((d, tn), lambda j: (0, j)),
                  pl.BlockSpec((1, tn), lambda j: (0, j))],
        out_specs=pl.BlockSpec((rows, tn), lambda j: (0, j)),
        compiler_params=_cparams(("arbitrary",)),
        name="modulation",
    )(cc, w_mod, b_mod.reshape(1, n_out))


def _swap_rope_halves(x, axis):
    half = MLA_ROPE // 2
    pos = lax.broadcasted_iota(jnp.int32, x.shape, axis)
    return jnp.where(pos < MLA_NOPE + half, pltpu.roll(x, HEAD_PAD - half, axis), pltpu.roll(x, half, axis))


def _store_value_tiles(ref, vt, dim):
    tm = vt.shape[1]
    tail = jnp.where(lax.broadcasted_iota(jnp.int32, (HEAD_PAD - dim, tm), 0) == 0, 1.0, 0.0).astype(ref.dtype)
    for h in range(vt.shape[0] // dim):
        ref[h * HEAD_PAD:h * HEAD_PAD + dim, :] = vt[h * dim:(h + 1) * dim, :].astype(ref.dtype)
        ref[h * HEAD_PAD + dim:(h + 1) * HEAD_PAD, :] = tail


def _inproj_kernel(latent, x_ref, sc_ref, sh_ref, ck_ref, sk_ref, cq_ref, sq_ref,
                   w_small_ref, w_na_ref, w_nqt_ref, w_nvt_ref, w_g_ref, qg_ref, kvg_ref, wqa_ref, wk_ref, wv_ref,
                   *out_refs):
    if latent:
        q_ref, k_ref, v_ref, nq_ref, nk_ref, nv_ref, g_ref = out_refs
    else:
        k_ref, v_ref, nk_ref, nv_ref = out_refs
    x = x_ref[0]
    u = (_ln(x) * (1.0 + sc_ref[0]) + sh_ref[0]).astype(BF16)

    small = _dot(u, w_small_ref[...])
    q_c = small[:, :Q_LORA]
    kv_c = small[:, Q_LORA:Q_LORA + KV_LORA]
    k_r = small[:, Q_LORA + KV_LORA:]

    kvn = (kv_c * lax.rsqrt(jnp.mean(kv_c * kv_c, axis=-1, keepdims=True) + RMS_EPS) * kvg_ref[...]).astype(BF16)
    kk = _dot(kvn, wk_ref[...])
    kr = k_r * ck_ref[...] + _swap_rope_halves(k_r, 1) * sk_ref[...]
    for h in range(MLA_HEADS):
        k_ref[0, h] = (kk[:, h * HEAD_PAD:(h + 1) * HEAD_PAD] + kr).astype(BF16)
    _store_value_tiles(v_ref.at[0], _dot_nt(wv_ref[...], kvn), MLA_V)

    nk_ref[0] = _dot(u, w_na_ref[...]).astype(BF16)
    _store_value_tiles(nv_ref.at[0], _dot_nt(w_nvt_ref[...], u), NA_DIM)
    if latent:
        qn = (q_c * lax.rsqrt(jnp.mean(q_c * q_c, axis=-1, keepdims=True) + RMS_EPS) * qg_ref[...]).astype(BF16)
        qa = _dot_nt(wqa_ref[...], qn)
        cq = cq_ref[...]
        sq = sq_ref[...]
        for h in range(MLA_HEADS):
            qh = qa[h * HEAD_PAD:(h + 1) * HEAD_PAD, :]
            q_ref[0, h] = (qh * cq + _swap_rope_halves(qh, 0) * sq).astype(BF16)
        nq_ref[0] = (_dot_nt(w_nqt_ref[...], u) * (NA_SCALE * LOG2E)).astype(BF16)
        g_ref[0] = jax.nn.sigmoid(_dot(u, w_g_ref[...])).astype(BF16)


def _inproj(latent, x, sc, sh, ck, sk, cq, sq, w_small, w_na, w_nqt, w_nvt, w_g, qg, kvg, wqa, wk, wv, tm):
    B, n, D = x.shape
    per_batch = sc.shape[0] > 1
    mod_map = (lambda b, i: (b, 0, 0)) if per_batch else (lambda b, i: (0, 0, 0))
    full = lambda a: pl.BlockSpec(a.shape, lambda b, i: (0,) * a.ndim)
    tab = lambda a: pl.BlockSpec((tm, a.shape[1]), lambda b, i: (i, 0))
    tab_t = lambda a: pl.BlockSpec((a.shape[0], tm), lambda b, i: (0, i))
    in_specs = [pl.BlockSpec((1, tm, D), lambda b, i: (b, i, 0)),
                pl.BlockSpec((1, 1, D), mod_map), pl.BlockSpec((1, 1, D), mod_map),
                tab(ck), tab(sk), tab_t(cq), tab_t(sq),
                full(w_small), full(w_na), full(w_nqt), full(w_nvt), full(w_g), full(qg), full(kvg),
                full(wqa), full(wk), full(wv)]
    hk = jax.ShapeDtypeStruct((B, MLA_HEADS, n, HEAD_PAD), BF16)
    hk_spec = pl.BlockSpec((1, MLA_HEADS, tm, HEAD_PAD), lambda b, i: (b, 0, i, 0))
    tok = lambda w: jax.ShapeDtypeStruct((B, n, w), BF16)
    tok_spec = lambda w: pl.BlockSpec((1, tm, w), lambda b, i: (b, i, 0))
    vt = jax.ShapeDtypeStruct((B, MLA_HEADS * HEAD_PAD, n), BF16)
    vt_spec = pl.BlockSpec((1, MLA_HEADS * HEAD_PAD, tm), lambda b, i: (b, 0, i))
    nqt = jax.ShapeDtypeStruct((B, NA_W, n), BF16)
    nqt_spec = pl.BlockSpec((1, NA_W, tm), lambda b, i: (b, 0, i))
    if latent:
        qt = jax.ShapeDtypeStruct((B, MLA_HEADS, HEAD_PAD, n), BF16)
        qt_spec = pl.BlockSpec((1, MLA_HEADS, HEAD_PAD, tm), lambda b, i: (b, 0, 0, i))
        out_shape = [qt, hk, vt, nqt, tok(NA_W), vt, tok(2 * D)]
        out_specs = [qt_spec, hk_spec, vt_spec, nqt_spec, tok_spec(NA_W), vt_spec, tok_spec(2 * D)]
    else:
        out_shape = [hk, vt, tok(NA_W), vt]
        out_specs = [hk_spec, vt_spec, tok_spec(NA_W), vt_spec]
    return pl.pallas_call(
        functools.partial(_inproj_kernel, latent),
        out_shape=out_shape,
        grid=(B, n // tm),
        in_specs=in_specs,
        out_specs=out_specs,
        compiler_params=_cparams(("parallel", "parallel")),
        name="inproj_latent" if latent else "inproj_ctx",
    )(x, sc, sh, ck, sk, cq, sq, w_small, w_na, w_nqt, w_nvt, w_g, qg, kvg, wqa, wk, wv)


def _mla_kernel(tk, q_ref, kc_ref, kl_ref, vc_ref, vl_ref, o_ref, m_ref, acc_ref, sa_ref, sb_ref, sc_ref):
    heads = q_ref.shape[1]
    n = kl_ref.shape[2]
    tq = q_ref.shape[3]
    nt = n // tk

    def scores(h, k):
        return _dot(k, q_ref[0, h])

    def absorb(hh, s, v_t):
        m_old = m_ref[hh]
        m_new = jnp.maximum(m_old, jnp.max(s, axis=0, keepdims=True))
        a = jnp.exp2(m_old - m_new)
        p = jnp.exp2(s - m_new).astype(BF16)
        acc_ref[hh] = a * acc_ref[hh] + _dot(v_t, p)
        m_ref[hh] = m_new

    def ctx_scores(h0):
        for hh in range(2):
            sc_ref[hh] = scores(h0 + hh, kc_ref[0, h0 + hh])

    def pair_pass(pair, carry):
        h0 = 2 * pair

        def head_rows(hh):
            return pl.ds(pl.multiple_of((h0 + hh) * HEAD_PAD, HEAD_PAD), HEAD_PAD)

        def k_tile(hh, j):
            return kl_ref[0, h0 + hh, pl.ds(pl.multiple_of(j * tk, tk), tk), :]

        def v_tile(hh, j):
            return vl_ref[0, head_rows(hh), pl.ds(pl.multiple_of(j * tk, tk), tk)]

        def advance(j, cur_ref, next_ref):
            for hh in range(2):
                next_ref[hh] = scores(h0 + hh, k_tile(hh, j + 1))
                absorb(hh, cur_ref[hh], v_tile(hh, j))

        m_ref[...] = jnp.full(m_ref.shape, NEG_BIG, F32)
        acc_ref[...] = jnp.zeros(acc_ref.shape, F32)
        for hh in range(2):
            sa_ref[hh] = scores(h0 + hh, k_tile(hh, 0))
            absorb(hh, sc_ref[hh], vc_ref[0, head_rows(hh), :])

        def body(jj, c):
            advance(2 * jj, sa_ref, sb_ref)
            advance(2 * jj + 1, sb_ref, sa_ref)
            return c

        lax.fori_loop(0, (nt - 1) // 2, body, 0)
        last_ref = sa_ref
        if (nt - 1) % 2:
            advance(nt - 2, sa_ref, sb_ref)
            last_ref = sb_ref
        ctx_scores(jnp.minimum(h0 + 2, heads - 2))
        outs = []
        for hh in range(2):
            absorb(hh, last_ref[hh], v_tile(hh, nt - 1))
            acc = acc_ref[hh]
            outs.append(acc[:MLA_V] / acc[MLA_V:MLA_V + 1])
        lanes = pl.ds(pl.multiple_of(pair * LANES, LANES), LANES)
        o_ref[0, :, lanes] = jnp.concatenate(outs, axis=0).T.astype(o_ref.dtype)
        return carry

    ctx_scores(0)
    lax.fori_loop(0, heads // 2, pair_pass, 0)


def _mla_attention(q, k_lat, k_ctx, v_lat, v_ctx, tq, tk):
    B, H, _, n = q.shape
    L = k_ctx.shape[2]
    W = H * MLA_V
    return pl.pallas_call(
        functools.partial(_mla_kernel, tk),
        out_shape=jax.ShapeDtypeStruct((B, n, W), BF16),
        grid=(B, n // tq),
        in_specs=[pl.BlockSpec((1, H, HEAD_PAD, tq), lambda b, i: (b, 0, 0, i)),
                  pl.BlockSpec((1, H, L, HEAD_PAD), lambda b, i: (b, 0, 0, 0)),
                  pl.BlockSpec((1, H, n, HEAD_PAD), lambda b, i: (b, 0, 0, 0)),
                  pl.BlockSpec((1, H * HEAD_PAD, L), lambda b, i: (b, 0, 0)),
                  pl.BlockSpec((1, H * HEAD_PAD, n), lambda b, i: (b, 0, 0))],
        out_specs=pl.BlockSpec((1, tq, W), lambda b, i: (b, i, 0)),
        scratch_shapes=[pltpu.VMEM((2, 1, tq), F32),
                        pltpu.VMEM((2, HEAD_PAD, tq), F32),
                        pltpu.VMEM((2, tk, tq), F32), pltpu.VMEM((2, tk, tq), F32),
                        pltpu.VMEM((2, L, tq), F32)],
        compiler_params=_cparams(("parallel", "parallel")),
        name="mla_attention",
    )(q, k_ctx, k_lat, v_ctx, v_lat)


def _na_block_tables(rows):
    nblk = rows // NA_QROWS
    wh = min(NA_WIN_H, rows)
    band0 = np.clip(np.arange(nblk) * NA_QROWS - wh // 2, 0, rows - NA_BAND)
    sigs, types = [], []
    for i in range(nblk):
        r = i * NA_QROWS + np.arange(NA_QROWS)
        r0 = np.clip(r - wh // 2, 0, rows - wh)
        sig = (tuple(r0 - band0[i]), tuple(r - band0[i]))
        if sig not in sigs:
            sigs.append(sig)
        types.append(sigs.index(sig))
    return band0.astype(np.int32), np.asarray(types, np.int32), sigs, wh


def _bias_tile_kernel(dr, ok, tbl_ref, o_ref):
    neg = jnp.full((GRID_W, GRID_W), NEG_BIG, F32)
    for t in range(o_ref.shape[0]):
        for k in range(NA_BAND):
            pieces = [tbl_ref[0, int(dr[t, q, k])] if ok[t, q, k] else neg for q in range(NA_QROWS)]
            o_ref[t, 0, k * GRID_W:(k + 1) * GRID_W, :] = jnp.concatenate(pieces, axis=1)


def _na_bias_tiles(rel_bias, rows):
    _, _, sigs, wh = _na_block_tables(rows)
    ww = NA_WIN_W
    n_dr, n_dc = 2 * NA_WIN_H - 1, 2 * NA_WIN_W - 1
    col = np.arange(GRID_W)
    c0 = np.clip(col - ww // 2, 0, GRID_W - ww)
    col_ok = (col[:, None] >= c0[None, :]) & (col[:, None] < c0[None, :] + ww)
    dc = col[:, None] - col[None, :] + (NA_WIN_W - 1)
    pick_c = ((dc[None] == np.arange(n_dc)[:, None, None]) & col_ok[None]).astype(np.float32)
    by_col = jnp.einsum('hdj,jyx->hdyx', rel_bias.astype(F32), jnp.asarray(pick_c), precision=lax.Precision.HIGHEST)
    by_col = jnp.where(jnp.asarray(col_ok), by_col * LOG2E, NEG_BIG)
    kr = np.arange(NA_BAND)
    ok = np.stack([(kr[None, :] >= np.asarray(r0)[:, None]) & (kr[None, :] < np.asarray(r0)[:, None] + wh)
                   for r0, _ in sigs])
    dr = np.stack([kr[None, :] - np.asarray(r)[:, None] + (NA_WIN_H - 1) for _, r in sigs])
    assert ((dr >= 0) & (dr < n_dr))[ok].all()
    T = len(sigs)
    kn, qn = NA_BAND * GRID_W, NA_QROWS * GRID_W
    return pl.pallas_call(
        functools.partial(_bias_tile_kernel, dr, ok),
        out_shape=jax.ShapeDtypeStruct((T, NA_HEADS, kn, qn), F32),
        grid=(NA_HEADS,),
        in_specs=[pl.BlockSpec((1, n_dr, GRID_W, GRID_W), lambda h: (h, 0, 0, 0))],
        out_specs=pl.BlockSpec((T, 1, kn, qn), lambda h: (0, h, 0, 0)),
        compiler_params=_cparams(("parallel",)),
        name="na_bias_tiles",
    )(by_col)


def _na_kernel(nblk, band_ref, type_ref, q_ref, k_ref, vt_ref, kc_ref, vct_ref, *rest):
    bias_refs, o_ref = rest[:nblk], rest[nblk]
    i = pl.program_id(1)
    nq = NA_QROWS * GRID_W
    nk = NA_BAND * GRID_W
    starts = [pl.multiple_of(band_ref[i * nblk + b] * GRID_W, NA_QROWS * GRID_W) for b in range(nblk)]
    row = lax.broadcasted_iota(jnp.int32, (LANES, nq), 0)

    def scores(c):
        blk, h = divmod(c, NA_HEADS)
        cs = slice((h // 2) * LANES, (h // 2 + 1) * LANES)
        qp = q_ref[0, cs, blk * nq:(blk + 1) * nq]
        own = (row < NA_DIM) if h % 2 == 0 else (row >= NA_DIM)
        qh = jnp.where(own, qp, jnp.zeros_like(qp))
        s_loc = _dot(k_ref[0, pl.ds(starts[blk], nk), cs], qh) + bias_refs[blk][0, h]
        s_ctx = _dot(kc_ref[0, :, cs], qh)
        return s_loc, s_ctx

    def attend(c, s):
        s_loc, s_ctx = s
        blk, h = divmod(c, NA_HEADS)
        start = starts[blk]
        vs = slice(h * HEAD_PAD, (h + 1) * HEAD_PAD)
        m = jnp.maximum(jnp.max(s_loc, axis=0, keepdims=True), jnp.max(s_ctx, axis=0, keepdims=True))
        p_loc = jnp.exp2((s_loc - m).astype(BF16))
        p_ctx = jnp.exp2((s_ctx - m).astype(BF16))
        o = _dot(vt_ref[0, vs, pl.ds(start, nk)], p_loc) + _dot(vct_ref[0, vs, :], p_ctx)
        return o[:NA_DIM] / o[NA_DIM:NA_DIM + 1]

    chain = nblk * NA_HEADS
    ahead = [scores(c) for c in range(min(NA_LOOKAHEAD, chain))]
    outs = []
    for c in range(chain):
        if c + NA_LOOKAHEAD < chain:
            ahead.append(scores(c + NA_LOOKAHEAD))
        outs.append(attend(c, ahead.pop(0)))
        if c % 2:
            blk, h = divmod(c, NA_HEADS)
            pair_t = jnp.concatenate([outs[c - 1], outs[c]], axis=0)
            o_ref[0, blk * nq:(blk + 1) * nq, (h // 2) * LANES:(h // 2 + 1) * LANES] = pair_t.T.astype(o_ref.dtype)


def _na_attention(nqt, nk, nvt, cnk, cnvt, bias_tiles):
    B, n, W = nk.shape
    L = cnk.shape[1]
    rows = n // GRID_W
    band0, types, _, _ = _na_block_tables(rows)
    qn = NA_QROWS * GRID_W
    kn = NA_BAND * GRID_W
    nblocks = rows // NA_QROWS
    nblk = 2 if nblocks % 2 == 0 else 1

    def bias_spec(k):
        return pl.BlockSpec((1, NA_HEADS, kn, qn), lambda b, i, bd, ty: (ty[i * nblk + k], 0, 0, 0))

    grid_spec = pltpu.PrefetchScalarGridSpec(
        num_scalar_prefetch=2,
        grid=(B, nblocks // nblk),
        in_specs=[pl.BlockSpec((1, W, nblk * qn), lambda b, i, bd, ty: (b, 0, i)),
                  pl.BlockSpec((1, n, W), lambda b, i, bd, ty: (b, 0, 0)),
                  pl.BlockSpec((1, NA_HEADS * HEAD_PAD, n), lambda b, i, bd, ty: (b, 0, 0)),
                  pl.BlockSpec((1, L, W), lambda b, i, bd, ty: (b, 0, 0)),
                  pl.BlockSpec((1, NA_HEADS * HEAD_PAD, L), lambda b, i, bd, ty: (b, 0, 0))]
                 + [bias_spec(k) for k in range(nblk)],
        out_specs=pl.BlockSpec((1, nblk * qn, W), lambda b, i, bd, ty: (b, i, 0)),
    )
    return pl.pallas_call(
        functools.partial(_na_kernel, nblk),
        out_shape=jax.ShapeDtypeStruct((B, n, W), BF16),
        grid_spec=grid_spec,
        compiler_params=_cparams(("parallel", "arbitrary")),
        name="na_attention",
    )(jnp.asarray(band0), jnp.asarray(types), nqt, nk, nvt, cnk, cnvt, *([bias_tiles] * nblk))


def _split2(a):
    hi = a.astype(BF16)
    return hi, (a - hi.astype(F32)).astype(BF16)


def _merge_kernel(ym_ref, yn_ref, g_ref, x_ref, g1_ref, sc2_ref, sh2_ref, wpm_ref, wpn_ref, wo_ref,
                  l1g_ref, l1b_ref, wr_ref, xn_ref, up_ref, aff_ref):
    tm, D = x_ref.shape[1], x_ref.shape[2]
    s = D // LANES
    sub = min(MERGE_SUB, tm)
    n_exp = aff_ref.shape[1]
    wh, wl = _split2(wr_ref[...])

    def mix_of(r):
        g = g_ref[0, r, :]
        a = (g[:, :D].astype(F32) * _dot(ym_ref[0, r, :], wpm_ref[...])
             + g[:, D:].astype(F32) * _dot(yn_ref[0, r, :], wpn_ref[...]))
        return _dot(a.astype(BF16), wo_ref[...])

    def finish(r, mix):
        xn = _ln(ALPHA * x_ref[0, r, :] + g1_ref[0] * mix) * l1g_ref[...] + l1b_ref[...]
        xn_ref[0, r, :] = xn
        u2 = _ln(xn) * (1.0 + sc2_ref[0]) + sh2_ref[0]
        _store_rows_as_tiles(up_ref.at[0, r.start * s:r.stop * s], u2)
        uh, ul = _split2(u2)
        logits = _dot(uh, wh) + (_dot(uh, wl) + _dot(ul, wh))
        logits = logits.T[:n_exp]
        e = jnp.exp(logits - jnp.max(logits, axis=0, keepdims=True))
        aff_ref[0, :, r] = e / jnp.sum(e, axis=0, keepdims=True)

    rows = [slice(i, i + sub) for i in range(0, tm, sub)]
    mix = mix_of(rows[0])
    for i, r in enumerate(rows):
        mix_next = mix_of(rows[i + 1]) if i + 1 < len(rows) else None
        finish(r, mix)
        mix = mix_next


def _merge(y_mla, y_na, gates, x, g1, sc2, sh2, wpm, wpn, wo, l1g, l1b, wr, tm):
    B, n, D = x.shape
    E = wr.shape[1]
    wr = jnp.concatenate([wr, jnp.zeros((D, LANES - E), wr.dtype)], axis=1)
    full = lambda a: pl.BlockSpec(a.shape, lambda b, i: (0,) * a.ndim)
    tok = lambda w: pl.BlockSpec((1, tm, w), lambda b, i: (b, i, 0))
    mod = pl.BlockSpec((1, 1, D), lambda b, i: (b, 0, 0))
    return pl.pallas_call(
        _merge_kernel,
        out_shape=[jax.ShapeDtypeStruct((B, n, D), F32),
                   jax.ShapeDtypeStruct((B, n * (D // LANES), LANES), F32),
                   jax.ShapeDtypeStruct((B, E, n), F32)],
        grid=(B, n // tm),
        in_specs=[tok(y_mla.shape[2]), tok(y_na.shape[2]), tok(2 * D), tok(D), mod, mod, mod,
                  full(wpm), full(wpn), full(wo), full(l1g), full(l1b), full(wr)],
        out_specs=[tok(D), pl.BlockSpec((1, tm * (D // LANES), LANES), lambda b, i: (b, i, 0)),
                   pl.BlockSpec((1, E, tm), lambda b, i: (b, 0, i))],
        compiler_params=_cparams(("parallel", "parallel")),
        name="merge_ln_router",
    )(y_mla, y_na, gates, x, g1, sc2, sh2, wpm, wpn, wo, l1g, l1b, wr)


def _topk_kernel(cap, aff_ref, tri_ref, idx_ref, g_ref, pos_ref):
    aff = aff_ref[0]
    E, n = aff.shape

    def count(mask):
        return jnp.sum(mask.astype(F32), axis=-1, keepdims=True)

    def enough(v):
        return count(aff >= v) >= cap

    def search(t, thr):
        b1 = jnp.int32(1) << (30 - 2 * t)
        b0 = b1 >> 1
        c3, c2, c1 = thr | b1 | b0, thr | b1, thr | b0
        ok3, ok2, ok1 = [enough(pltpu.bitcast(c, F32)) for c in (c3, c2, c1)]
        return jnp.where(ok3, c3, jnp.where(ok2, c2, jnp.where(ok1, c1, thr)))

    thr = lax.fori_loop(0, 16, search, jnp.zeros((E, 1), jnp.int32))
    lo = pltpu.bitcast(thr, F32)
    hi = pltpu.bitcast(thr + 1, F32)

    def refine(t, lh):
        lo, hi = lh
        w = hi - lo
        q1, q2, q3 = lo + w * 0.25, lo + w * 0.5, lo + w * 0.75
        ok1, ok2, ok3 = enough(q1), enough(q2), enough(q3)
        new_lo = jnp.where(ok3, q3, jnp.where(ok2, q2, jnp.where(ok1, q1, lo)))
        new_hi = jnp.where(ok3, hi, jnp.where(ok2, q3, jnp.where(ok1, q2, q1)))
        return new_lo, new_hi

    lo, hi = lax.fori_loop(0, TOPK_REFINE // 2, refine, (lo, hi))
    gt = aff >= hi
    eq = (aff >= lo) & ~gt
    need = cap - count(gt)

    tri = tri_ref[...]

    def prefix(mask):
        mf = mask.astype(F32).astype(BF16)
        parts = []
        off = jnp.zeros((E, 1), F32)
        for c in range(n // LANES):
            blk = mf[:, c * LANES:(c + 1) * LANES]
            parts.append(_dot(blk, tri) + off)
            off = off + jnp.sum(blk.astype(F32), axis=-1, keepdims=True)
        return jnp.concatenate(parts, axis=-1)

    sel = gt | (eq & (prefix(eq) < need))
    pos_ref[...] = jnp.where(sel, prefix(sel), -1.0)

    tok = lax.broadcasted_iota(jnp.int32, (1, n), 1)
    digits = [(tok // TOPK_RADIX).astype(F32), (tok % TOPK_RADIX).astype(F32)]
    chunks = cap // TOPK_SLOTS

    def per_pair(ee, carry):
        work = []
        for e in [TOPK_EXPERTS * ee + k for k in range(TOPK_EXPERTS)]:
            a = aff_ref[0, pl.ds(e, 1), :]
            a1 = a.astype(BF16).astype(F32)
            a2 = (a - a1).astype(BF16).astype(F32)
            a3 = (a - a1) - a2
            rows = digits + [a1, a2, a3]
            table = jnp.concatenate(rows + [jnp.zeros((TOPK_TABLE_ROWS - len(rows), n), F32)], axis=0).astype(BF16)
            pos_e = pos_ref[pl.ds(e, 1), :]
            for ch in range(chunks):
                slot = (lax.broadcasted_iota(jnp.int32, (TOPK_SLOTS, n), 0) + ch * TOPK_SLOTS).astype(F32)
                work.append((e, ch, jnp.where(slot == pos_e, 1.0, 0.0).astype(BF16), table))
        for e, ch, hit, table in work:
            r = _dot_nt(hit, table)
            g_ref[0, e, ch * TOPK_SLOTS:(ch + 1) * TOPK_SLOTS, :] = (r[:, 2:3] + r[:, 3:4]) + r[:, 4:5]
            rt = r.T
            idx = rt[0:1] * TOPK_RADIX + rt[1:2]
            idx_ref[0, e, :, ch * TOPK_SLOTS:(ch + 1) * TOPK_SLOTS] = idx.astype(jnp.int32)
        return carry

    lax.fori_loop(0, E // TOPK_EXPERTS, per_pair, 0)


def _topk(aff_t, cap):
    B, E, n = aff_t.shape
    assert E % TOPK_EXPERTS == 0 and cap % TOPK_SLOTS == 0
    tri = jnp.asarray(np.triu(np.ones((LANES, LANES), np.float32), k=1), BF16)
    return pl.pallas_call(
        functools.partial(_topk_kernel, cap),
        out_shape=[jax.ShapeDtypeStruct((B, E, 1, cap), jnp.int32),
                   jax.ShapeDtypeStruct((B, E, cap, 1), F32)],
        grid=(B,),
        in_specs=[pl.BlockSpec((1, E, n), lambda b: (b, 0, 0)),
                  pl.BlockSpec((LANES, LANES), lambda b: (0, 0))],
        out_specs=[pl.BlockSpec((1, E, 1, cap), lambda b: (b, 0, 0, 0)),
                   pl.BlockSpec((1, E, cap, 1), lambda b: (b, 0, 0, 0))],
        scratch_shapes=[pltpu.VMEM((E, n), F32)],
        compiler_params=_cparams(("parallel",)),
        name="expert_topk",
    )(aff_t, tri)


def _gather_kernel(s, idx_ref, u_ref, o_ref):
    cap = o_ref.shape[2] // s

    def body(i, carry):
        c0 = pl.multiple_of(i * ROW_GROUP, ROW_GROUP)
        for k in range(ROW_GROUP):
            src = pl.multiple_of(idx_ref[0, 0, 0, c0 + k] * s, s)
            o_ref[0, 0, pl.ds((c0 + k) * s, s), :] = u_ref[0, pl.ds(src, s), :]
        return carry

    lax.fori_loop(0, cap // ROW_GROUP, body, 0)


def _gather(idx, u2, n):
    B, E, _, cap = idx.shape
    _, ns, W = u2.shape
    s = ns // n
    return pl.pallas_call(
        functools.partial(_gather_kernel, s),
        out_shape=jax.ShapeDtypeStruct((B, E, cap * s, W), u2.dtype),
        grid=(B, E),
        in_specs=[pl.BlockSpec((1, 1, 1, cap), lambda b, e: (b, e, 0, 0), memory_space=pltpu.SMEM),
                  pl.BlockSpec((1, ns, W), lambda b, e: (b, 0, 0))],
        out_specs=pl.BlockSpec((1, 1, cap * s, W), lambda b, e: (b, e, 0, 0)),
        compiler_params=_cparams(("parallel", "arbitrary")),
        name="expert_gather",
    )(idx, u2)


def _ffn_kernel(xe_ref, g_ref, wg_ref, wu_ref, wd_ref, o_ref, wgb_ref, wub_ref, wdb_ref):
    @pl.when(pl.program_id(1) == 0)
    def _():
        wgb_ref[...] = wg_ref[0].astype(BF16)
        wub_ref[...] = wu_ref[0].astype(BF16)
        wdb_ref[...] = wd_ref[0].astype(BF16)

    xe = _load_tiles_as_rows(xe_ref.at[0, 0], wgb_ref.shape[0]).astype(BF16)
    gate = _dot(xe, wgb_ref[...])
    up = _dot(xe, wub_ref[...])
    h = (gate * jax.nn.sigmoid(gate) * up).astype(BF16)
    _store_rows_as_tiles(o_ref.at[0, 0], _dot(h, wdb_ref[...]) * g_ref[0, 0])


def _ffn(xe, g, w_gate, w_up, w_down):
    B, E, caps, W = xe.shape
    _, D, F = w_gate.shape
    cap = g.shape[2]
    return pl.pallas_call(
        _ffn_kernel,
        out_shape=jax.ShapeDtypeStruct((B, E, caps, W), F32),
        grid=(E, B),
        in_specs=[pl.BlockSpec((1, 1, caps, W), lambda e, b: (b, e, 0, 0)),
                  pl.BlockSpec((1, 1, cap, 1), lambda e, b: (b, e, 0, 0)),
                  pl.BlockSpec((1, D, F), lambda e, b: (e, 0, 0)),
                  pl.BlockSpec((1, D, F), lambda e, b: (e, 0, 0)),
                  pl.BlockSpec((1, F, D), lambda e, b: (e, 0, 0))],
        out_specs=pl.BlockSpec((1, 1, caps, W), lambda e, b: (b, e, 0, 0)),
        scratch_shapes=[pltpu.VMEM((D, F), BF16), pltpu.VMEM((D, F), BF16), pltpu.VMEM((F, D), BF16)],
        compiler_params=_cparams(("arbitrary", "arbitrary")),
        name="expert_ffn",
    )(xe, g, w_gate, w_up, w_down)


def _combine_kernel(s, n_exp, x_ref, idx_ref, y_ref, g2_ref, lg_ref, lb_ref, o_ref, acc_ref):
    e = pl.program_id(1)
    cap = y_ref.shape[2] // s
    tm, D = x_ref.shape[1], x_ref.shape[2]

    def tile(r):
        return pl.ds(pl.multiple_of(r * s, s), s)

    @pl.when(e == 0)
    def _():
        acc_ref[...] = jnp.zeros_like(acc_ref)

    @pl.when(e < n_exp)
    def _():
        def body(i, carry):
            c0 = pl.multiple_of(i * ROW_GROUP, ROW_GROUP)
            rows = [idx_ref[0, 0, 0, c0 + k] for k in range(ROW_GROUP)]
            acc = [acc_ref[tile(r), :] for r in rows]
            for k in range(ROW_GROUP):
                acc_ref[tile(rows[k]), :] = acc[k] + y_ref[0, 0, tile(c0 + k), :]
            return carry

        lax.fori_loop(0, cap // ROW_GROUP, body, 0)

    @pl.when(e >= n_exp)
    def _():
        first = pl.multiple_of((e - n_exp) * (tm * s), tm * s)
        moe = _load_tiles_as_rows(acc_ref.at[pl.ds(first, tm * s)], D)
        o_ref[0] = _ln(ALPHA * x_ref[0] + g2_ref[0] * moe) * lg_ref[...] + lb_ref[...]


def _combine(xn, idx, ye, g2, lg, lb, tm):
    B, n, D = xn.shape
    _, E, caps, W = ye.shape
    cap = idx.shape[3]
    s = caps // cap
    blocks = n // tm
    tok = pl.BlockSpec((1, tm, D), lambda b, e: (b, jnp.maximum(e - E, 0), 0))
    vec = pl.BlockSpec((1, D), lambda b, e: (0, 0))
    last = lambda b, e: (b, jnp.minimum(e, E - 1), 0, 0)
    return pl.pallas_call(
        functools.partial(_combine_kernel, s, E),
        out_shape=jax.ShapeDtypeStruct((B, n, D), F32),
        grid=(B, E + blocks),
        in_specs=[tok,
                  pl.BlockSpec((1, 1, 1, cap), last, memory_space=pltpu.SMEM),
                  pl.BlockSpec((1, 1, caps, W), last),
                  pl.BlockSpec((1, 1, D), lambda b, e: (b, 0, 0)), vec, vec],
        out_specs=tok,
        scratch_shapes=[pltpu.VMEM((n * s, W), F32)],
        compiler_params=_cparams(("parallel", "arbitrary")),
        name="expert_combine_ln",
    )(xn, idx, ye, g2, lg, lb)


def _rope_tables(n):
    t = np.arange(n)
    row = (t // GRID_W).astype(np.float32)
    col = (t % GRID_W).astype(np.float32)
    per_axis = MLA_ROPE // 2
    inv_freq = jnp.asarray(ROPE_THETA, F32) ** (-jnp.arange(0, per_axis, 2, dtype=F32) / per_axis)
    ang = jnp.concatenate([jnp.asarray(row)[:, None] * inv_freq, jnp.asarray(col)[:, None] * inv_freq], axis=-1)
    cos, sin = jnp.cos(ang), jnp.sin(ang)
    pad = HEAD_PAD - MLA_QK
    c_tab = jnp.concatenate([jnp.ones((n, MLA_NOPE), F32), cos, cos, jnp.zeros((n, pad), F32)], axis=-1)
    s_tab = jnp.concatenate([jnp.zeros((n, MLA_NOPE), F32), -sin, sin, jnp.zeros((n, pad), F32)], axis=-1)
    return c_tab, s_tab


def _prep_weights(w_in, w_uq, w_ukv):
    D = w_in.shape[0]
    o1 = Q_LORA
    o2 = o1 + KV_LORA
    o3 = o2 + MLA_ROPE
    o4 = o3 + 3 * NA_W
    w_kr = w_in[:, o2:o3]
    z = lambda k: jnp.zeros((D, k), w_in.dtype)
    pad = HEAD_PAD - MLA_QK
    w_small = jnp.concatenate([w_in[:, :o2], z(MLA_NOPE), w_kr, z(pad)], axis=-1).astype(BF16)
    w_nqt = w_in[:, o3:o3 + NA_W].T.astype(BF16)
    w_na = w_in[:, o3 + NA_W:o3 + 2 * NA_W].astype(BF16)
    w_nvt = w_in[:, o3 + 2 * NA_W:o4].T.astype(BF16)
    w_g = w_in[:, o4:].astype(BF16)

    uq = w_uq.reshape(Q_LORA, MLA_HEADS, MLA_QK)
    zq = lambda k: jnp.zeros((Q_LORA, MLA_HEADS, k), w_uq.dtype)
    wqa = jnp.concatenate([uq, zq(pad)], axis=-1).reshape(Q_LORA, MLA_HEADS * HEAD_PAD).T.astype(BF16)

    ukv = w_ukv.reshape(KV_LORA, MLA_HEADS, MLA_NOPE + MLA_V)
    wk = jnp.concatenate([ukv[..., :MLA_NOPE], jnp.zeros((KV_LORA, MLA_HEADS, HEAD_PAD - MLA_NOPE), w_ukv.dtype)], axis=-1)
    wk = wk.reshape(KV_LORA, MLA_HEADS * HEAD_PAD).astype(BF16)
    wv = ukv[..., MLA_NOPE:].reshape(KV_LORA, MLA_HEADS * MLA_V).T.astype(BF16)
    return w_small, w_na, w_nqt, w_nvt, w_g, wqa, wk, wv


def _tile(n, pref):
    t = min(pref, n)
    while n % t:
        t //= 2
    return t


def kernel(x, c, ctx, c_ctx, w_mod, b_mod, w_in, q_norm_g, w_uq, kv_norm_g, w_ukv, na_rel_bias, w_proj_mla,
           w_proj_na, w_out, ln1_g, ln1_b, w_router, w_exp_gate, w_exp_up, w_exp_down, ln2_g, ln2_b):
    B, n, D = x.shape
    L = ctx.shape[1]
    rows = n // GRID_W
    assert n % (GRID_W * NA_QROWS) == 0 and rows >= NA_BAND
    assert w_mod.shape[0] == DEPTH
    cap = EC_CAPACITY * n // N_EXPERTS

    mod_rows = -(-(B + 1) // 8) * 8
    cc = jnp.concatenate([c, c_ctx[None], jnp.zeros((mod_rows - B - 1, D), F32)], axis=0)
    m = _modulation(cc, w_mod[0], b_mod[0])
    sh1, sc1, g1, sh2, sc2, g2 = [m[:B, k * D:(k + 1) * D].reshape(B, 1, D) for k in range(6)]
    csh1 = m[B:B + 1, :D].reshape(1, 1, D)
    csc1 = m[B:B + 1, D:2 * D].reshape(1, 1, D)

    w_small, w_na, w_nqt, w_nvt, w_g, wqa, wk, wv = _prep_weights(w_in[0], w_uq[0], w_ukv[0])
    qg = q_norm_g[0].reshape(1, Q_LORA)
    kvg = kv_norm_g[0].reshape(1, KV_LORA)
    c_tab, s_tab = _rope_tables(n)
    pad = HEAD_PAD - MLA_QK
    c_ctx_tab = jnp.concatenate([jnp.ones((L, MLA_QK), F32), jnp.zeros((L, pad), F32)], axis=-1)
    s_ctx_tab = jnp.zeros((L, HEAD_PAD), F32)

    q, k_lat, v_lat, nq, nk, nv, gates = _inproj(
        True, x, sc1, sh1, c_tab, s_tab, c_tab.T * (MLA_SCALE * LOG2E), s_tab.T * (MLA_SCALE * LOG2E),
        w_small, w_na, w_nqt, w_nvt, w_g, qg, kvg, wqa, wk, wv, _tile(n, 512))
    k_ctx, v_ctx, cnk, cnv = _inproj(
        False, ctx, csc1, csh1, c_ctx_tab, s_ctx_tab, c_ctx_tab.T, s_ctx_tab.T,
        w_small, w_na, w_nqt, w_nvt, w_g, qg, kvg, wqa, wk, wv, _tile(L, 256))

    y_mla = _mla_attention(q, k_lat, k_ctx, v_lat, v_ctx, _tile(n, 1024), _tile(n, 512))
    y_na = _na_attention(nq, nk, nv, cnk, cnv, _na_bias_tiles(na_rel_bias[0], rows))

    xn, u2, aff = _merge(
        y_mla, y_na, gates, x, g1, sc2, sh2,
        w_proj_mla[0].astype(BF16), w_proj_na[0].astype(BF16), w_out[0].astype(BF16),
        ln1_g[0].reshape(1, D), ln1_b[0].reshape(1, D), w_router[0], _tile(n, 4 * MERGE_SUB))

    idx, g4 = _topk(aff, cap)
    xe = _gather(idx, u2, n)
    ye = _ffn(xe, g4, w_exp_gate[0], w_exp_up[0], w_exp_down[0])
    return _combine(xn, idx, ye, g2, ln2_g[0].reshape(1, D), ln2_b[0].reshape(1, D), _tile(n, 512))
```
